```python
import math
import jax, jax.numpy as jnp
from jax import lax
import numpy as np

D_MODEL = 2048
BATCH = 2
SEQ = 4096
DEPTH = 2

GRID_W = 64
CTX_LEN = 256
N_MIXERS = 2
N_MLA_LAYERS = (DEPTH + N_MIXERS - 1) // N_MIXERS
N_HYENA_LAYERS = DEPTH // N_MIXERS
NORM_EPS = 1e-6
MLA_HEADS = 16
MLA_Q_RANK = 512
MLA_KV_RANK = 256
MLA_NOPE_DIM = 128
MLA_ROPE_DIM = 64
MLA_V_DIM = 128
MLA_IN_DIM = MLA_Q_RANK + MLA_KV_RANK + MLA_ROPE_DIM
ROPE_THETA = 10000.0
Q_BLOCK = 128
HYENA_ORDER = 2
HYENA_SHORT = 3
HYENA_BANDS = 16
HYENA_EMB = 1 + 2 * HYENA_BANDS
HYENA_FILTER_WIDTH = 64
HYENA_TARGET = 1e-2
HYENA_FAST_PCT = 0.3
HYENA_SLOW_PCT = 1.5
N_EXPERTS = 16
EC_CAPACITY_FACTOR = 2
EXPERT_FF = 1024

kernel_name = 'hybrid_mla_hyena_ec_moe_diffusion'

F32 = jnp.float32


def rmsnorm(x, g):
    x32 = x.astype(F32)
    y = x32 * lax.rsqrt(jnp.mean(x32 * x32, axis=-1, keepdims=True) + NORM_EPS)
    return (y * g.astype(F32)).astype(x.dtype)


def ada_modulation(cvec, w, b):
    m = jax.nn.silu(cvec) @ w + b
    return m.reshape(cvec.shape[0], 6, D_MODEL)


def modulate(h, shift, scale):
    return h * (1 + scale[:, None, :]) + shift[:, None, :]


def axial_rope_tables(rows):
    row = jnp.broadcast_to(jnp.arange(rows)[:, None], (rows, GRID_W)).reshape(-1).astype(F32)
    col = jnp.broadcast_to(jnp.arange(GRID_W)[None, :], (rows, GRID_W)).reshape(-1).astype(F32)
    n_freq = MLA_ROPE_DIM // 4
    inv = ROPE_THETA ** (-jnp.arange(n_freq, dtype=F32) / n_freq)
    ang = jnp.concatenate([row[:, None] * inv, col[:, None] * inv], axis=-1)
    return jnp.cos(ang), jnp.sin(ang)


def apply_rope(x, cos, sin):
    half = MLA_ROPE_DIM // 2
    x32 = x.astype(F32)
    x1, x2 = x32[..., :half], x32[..., half:]
    out = jnp.concatenate([x1 * cos - x2 * sin, x2 * cos + x1 * sin], axis=-1)
    return out.astype(x.dtype)


def mla_latents(h, w_in, g_q, g_kv, w_uq, w_ukv, with_queries):
    B, L, _ = h.shape
    if with_queries:
        proj = h @ w_in
        cq, rest = proj[..., :MLA_Q_RANK], proj[..., MLA_Q_RANK:]
    else:
        rest = h @ w_in[:, MLA_Q_RANK:]
    ckv, k_rope = rest[..., :MLA_KV_RANK], rest[..., MLA_KV_RANK:]
    kv = (rmsnorm(ckv, g_kv) @ w_ukv).reshape(B, L, MLA_HEADS, MLA_NOPE_DIM + MLA_V_DIM)
    k_nope, v = kv[..., :MLA_NOPE_DIM], kv[..., MLA_NOPE_DIM:]
    if with_queries:
        q = (rmsnorm(cq, g_q) @ w_uq).reshape(B, L, MLA_HEADS, MLA_NOPE_DIM + MLA_ROPE_DIM)
        return k_nope, k_rope, v, q[..., :MLA_NOPE_DIM], q[..., MLA_NOPE_DIM:]
    return k_nope, k_rope, v


def block_attention(q_nope, q_rope, k_nope, k_rope, v):
    B, L = q_nope.shape[:2]
    nb = L // Q_BLOCK
    scale = (MLA_NOPE_DIM + MLA_ROPE_DIM) ** -0.5

    def split(t):
        return jnp.moveaxis(t.reshape((B, nb, Q_BLOCK) + t.shape[2:]), 1, 0)

    def one_block(args):
        qn, qr = args
        s = jnp.einsum('bqhd,bkhd->bhqk', qn, k_nope) + jnp.einsum('bqhr,bkr->bhqk', qr, k_rope)
        p = jax.nn.softmax(s.astype(F32) * scale, axis=-1).astype(v.dtype)
        return jnp.einsum('bhqk,bkhd->bqhd', p, v)

    o = lax.map(one_block, (split(q_nope), split(q_rope)))
    return jnp.moveaxis(o, 0, 1).reshape(B, L, MLA_HEADS * MLA_V_DIM)


def mla_mixer(h, h_c, rows, ctx_out, w_in, g_q, g_kv, w_uq, w_ukv, w_o):
    cos, sin = axial_rope_tables(rows)
    kn, kr, v, qn, qr = mla_latents(h, w_in, g_q, g_kv, w_uq, w_ukv, True)
    qr = apply_rope(qr, cos[:, None, :], sin[:, None, :])
    kr = apply_rope(kr, cos, sin)
    ctx_side = mla_latents(h_c, w_in, g_q, g_kv, w_uq, w_ukv, ctx_out)
    kn_c, kr_c, v_c = ctx_side[0], ctx_side[1], ctx_side[2]
    o = block_attention(qn, qr,
                        jnp.concatenate([kn, kn_c], axis=1),
                        jnp.concatenate([kr, kr_c], axis=1),
                        jnp.concatenate([v, v_c], axis=1))
    y = o @ w_o
    y_c = None
    if ctx_out:
        y_c = block_attention(ctx_side[3], ctx_side[4], kn_c, kr_c, v_c) @ w_o
    return y, y_c


def hyena_filters(L, w1, b1, w2, b2, w3, b3, freq):
    t = jnp.arange(L, dtype=F32) / L
    w = 2 * math.pi * jnp.arange(L, dtype=F32) / L
    bands = jnp.linspace(1e-4, HYENA_BANDS - 1, HYENA_BANDS, dtype=F32)
    ang = w[:, None] * bands[None, :]
    z = jnp.concatenate([t[:, None], jnp.cos(ang), -jnp.sin(ang)], axis=-1)
    fr = freq.astype(F32)
    a = jnp.sin(fr * (z @ w1.astype(F32) + b1.astype(F32)))
    a = jnp.sin(fr * (a @ w2.astype(F32) + b2.astype(F32)))
    hf = (a @ w3.astype(F32) + b3.astype(F32)).reshape(L, HYENA_ORDER, 2, D_MODEL)
    max_decay = math.log(HYENA_TARGET) / HYENA_FAST_PCT
    min_decay = math.log(HYENA_TARGET) / HYENA_SLOW_PCT
    deltas = jnp.linspace(min_decay, max_decay, D_MODEL, dtype=F32)
    hf = hf * jnp.exp(-t[:, None] * jnp.abs(deltas))[:, None, None, :]
    fwd = hf[:, :, 0]
    bwd = hf[1:, :, 1][::-1]
    k2 = jnp.concatenate([fwd, jnp.zeros((1, HYENA_ORDER, D_MODEL), F32), bwd], axis=0)
    k2 = k2 * lax.rsqrt(jnp.sum(k2 * k2, axis=0, keepdims=True) + NORM_EPS)
    return jnp.fft.rfft(k2, axis=0)


def short_conv(u, w, b):
    L = u.shape[1]
    up = jnp.pad(u, ((0, 0), (1, 1), (0, 0)))
    return up[:, :L] * w[0] + up[:, 1:L + 1] * w[1] + up[:, 2:] * w[2] + b


def long_conv(z, kf, skip):
    L = z.shape[1]
    z32 = z.astype(F32)
    zf = jnp.fft.rfft(z32, n=2 * L, axis=1)
    y = jnp.fft.irfft(zf * kf[None], n=2 * L, axis=1)[:, :L]
    return (y + skip.astype(F32) * z32).astype(z.dtype)


def hyena_mixer(h, w_in, conv_w, conv_b, f_w1, f_b1, f_w2, f_b2, f_w3, f_b3, f_freq, skip, w_out):
    L = h.shape[1]
    u = short_conv(h @ w_in, conv_w, conv_b)
    v, x1, x2 = u[..., :D_MODEL], u[..., D_MODEL:2 * D_MODEL], u[..., 2 * D_MODEL:]
    kf = hyena_filters(L, f_w1, f_b1, f_w2, f_b2, f_w3, f_b3, f_freq)
    z = x1 * long_conv(v, kf[:, 0], skip[0])
    z = x2 * long_conv(z, kf[:, 1], skip[1])
    return z @ w_out


def expert_choice_ffn(h, w_router, w_gate, w_up, w_down):
    B, T, D = h.shape
    cap = EC_CAPACITY_FACTOR * T // N_EXPERTS
    aff = jax.nn.softmax((h @ w_router).astype(F32), axis=-1)
    g, idx = lax.top_k(jnp.swapaxes(aff, 1, 2), cap)
    xg = jax.vmap(lambda hb, ib: hb[ib])(h, idx)
    a = jnp.einsum('becd,edf->becf', xg, w_gate)
    u = jnp.einsum('becd,edf->becf', xg, w_up)
    y = jnp.einsum('becf,efd->becd', jax.nn.silu(a) * u, w_down) * g[..., None].astype(h.dtype)
    return jax.vmap(lambda ib, yb: jnp.zeros((T, D), h.dtype).at[ib.reshape(-1)].add(yb.reshape(-1, D)))(idx, y)


def setup_inputs(seed: int = 0) -> dict:
    key = jax.random.key(seed)
    ks = jax.random.split(key, 32)
    D, E, F = D_MODEL, N_EXPERTS, EXPERT_FF
    nm, nh = N_MLA_LAYERS, N_HYENA_LAYERS

    def nrm(k, shape, scale):
        return jax.random.normal(k, shape, F32) * scale

    return {
        'x': nrm(ks[0], (BATCH, SEQ, D), 1.0),
        'c': nrm(ks[1], (BATCH, D), 1.0),
        'ctx': nrm(ks[2], (BATCH, CTX_LEN, D), 1.0),
        'c_ctx': nrm(ks[3], (D,), 1.0),
        'ada_w': nrm(ks[4], (DEPTH, D, 6 * D), 0.5 * D ** -0.5),
        'ada_b': nrm(ks[5], (DEPTH, 6 * D), 0.01),
        'norm_g': 1.0 + nrm(ks[6], (DEPTH, 2, D), 0.01),
        'final_g': 1.0 + nrm(ks[7], (D,), 0.01),
        'mla_w_in': nrm(ks[8], (nm, D, MLA_IN_DIM), D ** -0.5),
        'mla_g_q': 1.0 + nrm(ks[9], (nm, MLA_Q_RANK), 0.01),
        'mla_g_kv': 1.0 + nrm(ks[10], (nm, MLA_KV_RANK), 0.01),
        'mla_w_uq': nrm(ks[11], (nm, MLA_Q_RANK, MLA_HEADS * (MLA_NOPE_DIM + MLA_ROPE_DIM)), MLA_Q_RANK ** -0.5),
        'mla_w_ukv': nrm(ks[12], (nm, MLA_KV_RANK, MLA_HEADS * (MLA_NOPE_DIM + MLA_V_DIM)), MLA_KV_RANK ** -0.5),
        'mla_w_o': nrm(ks[13], (nm, MLA_HEADS * MLA_V_DIM, D), (MLA_HEADS * MLA_V_DIM) ** -0.5),
        'hy_w_in': nrm(ks[14], (nh, D, 3 * D), D ** -0.5),
        'hy_conv_w': nrm(ks[15], (nh, HYENA_SHORT, 3 * D), 0.6),
        'hy_conv_b': nrm(ks[16], (nh, 3 * D), 0.01),
        'hy_f_w1': nrm(ks[17], (nh, HYENA_EMB, HYENA_FILTER_WIDTH), HYENA_EMB ** -0.5),
        'hy_f_b1': nrm(ks[18], (nh, HYENA_FILTER_WIDTH), 0.1),
        'hy_f_w2': nrm(ks[19], (nh, HYENA_FILTER_WIDTH, HYENA_FILTER_WIDTH), HYENA_FILTER_WIDTH ** -0.5),
        'hy_f_b2': nrm(ks[20], (nh, HYENA_FILTER_WIDTH), 0.1),
        'hy_f_w3': nrm(ks[21], (nh, HYENA_FILTER_WIDTH, HYENA_ORDER * 2 * D), HYENA_FILTER_WIDTH ** -0.5),
        'hy_f_b3': nrm(ks[22], (nh, HYENA_ORDER * 2 * D), 0.01),
        'hy_f_freq': 1.0 + nrm(ks[23], (nh, HYENA_FILTER_WIDTH), 0.1),
        'hy_skip': nrm(ks[24], (nh, HYENA_ORDER, D), 0.5),
        'hy_w_out': nrm(ks[25], (nh, D, D), D ** -0.5),
        'moe_w_router': nrm(ks[26], (DEPTH, D, E), D ** -0.5),
        'moe_w_gate': nrm(ks[27], (DEPTH, E, D, F), D ** -0.5),
        'moe_w_up': nrm(ks[28], (DEPTH, E, D, F), D ** -0.5),
        'moe_w_down': nrm(ks[29], (DEPTH, E, F, D), F ** -0.5),
    }


def reference(x, c, ctx, c_ctx, ada_w, ada_b, norm_g, final_g,
              mla_w_in, mla_g_q, mla_g_kv, mla_w_uq, mla_w_ukv, mla_w_o,
              hy_w_in, hy_conv_w, hy_conv_b, hy_f_w1, hy_f_b1, hy_f_w2, hy_f_b2,
              hy_f_w3, hy_f_b3, hy_f_freq, hy_skip, hy_w_out,
              moe_w_router, moe_w_gate, moe_w_up, moe_w_down):
    ROWS = x.shape[1] // GRID_W
    for i in range(DEPTH):
        kind = i % N_MIXERS
        m = i // N_MIXERS
        ctx_out = any(j % N_MIXERS == 0 for j in range(i + 1, DEPTH))
        ctx_in = ctx_out or kind == 0
        mod = ada_modulation(c, ada_w[i], ada_b[i])
        h = modulate(rmsnorm(x, norm_g[i, 0]), mod[:, 0], mod[:, 1])
        h_c, mod_c = None, None
        if ctx_in:
            mod_c = ada_modulation(c_ctx[None], ada_w[i], ada_b[i])
            h_c = modulate(rmsnorm(ctx, norm_g[i, 0]), mod_c[:, 0], mod_c[:, 1])
        if kind == 0:
            y, y_c = mla_mixer(h, h_c, ROWS, ctx_out, mla_w_in[m], mla_g_q[m], mla_g_kv[m],
                               mla_w_uq[m], mla_w_ukv[m], mla_w_o[m])
        else:
            hp = (hy_w_in[m], hy_conv_w[m], hy_conv_b[m], hy_f_w1[m], hy_f_b1[m], hy_f_w2[m],
                  hy_f_b2[m], hy_f_w3[m], hy_f_b3[m], hy_f_freq[m], hy_skip[m], hy_w_out[m])
            y = hyena_mixer(h, *hp)
            y_c = hyena_mixer(h_c, *hp) if ctx_out else None
        x = x + mod[:, 2][:, None, :] * y
        hm = modulate(rmsnorm(x, norm_g[i, 1]), mod[:, 3], mod[:, 4])
        x = x + mod[:, 5][:, None, :] * expert_choice_ffn(hm, moe_w_router[i], moe_w_gate[i],
                                                          moe_w_up[i], moe_w_down[i])
        if ctx_out:
            ctx = ctx + mod_c[:, 2][:, None, :] * y_c
            hmc = modulate(rmsnorm(ctx, norm_g[i, 1]), mod_c[:, 3], mod_c[:, 4])
            ctx = ctx + mod_c[:, 5][:, None, :] * expert_choice_ffn(hmc, moe_w_router[i], moe_w_gate[i],
                                                                    moe_w_up[i], moe_w_down[i])
    return rmsnorm(x, final_g)
```

```python
import functools
import math

import jax
import jax.numpy as jnp
from jax import lax
from jax.experimental import pallas as pl
from jax.experimental.pallas import tpu as pltpu

F32 = jnp.float32
BF16 = jnp.bfloat16

D_MODEL = 2048
GRID_W = 64
NORM_EPS = 1e-6
MLA_HEADS = 16
MLA_Q_RANK = 512
MLA_KV_RANK = 256
MLA_NOPE_DIM = 128
MLA_ROPE_DIM = 64
MLA_V_DIM = 128
ROPE_THETA = 10000.0
HYENA_ORDER = 2
HYENA_BANDS = 16
HYENA_TARGET = 1e-2
HYENA_FAST_PCT = 0.3
HYENA_SLOW_PCT = 1.5
N_EXPERTS = 16
EC_CAPACITY_FACTOR = 2

V7X_VMEM_LIMIT_BYTES = 56 * 1024 * 1024
LANES = 128
QK_PAD = 256


def _params(*sem):
    return pltpu.CompilerParams(dimension_semantics=sem, vmem_limit_bytes=V7X_VMEM_LIMIT_BYTES)


def _mm_kernel(a_ref, b_ref, *rest, b_is_stacked, has_res):
    o_ref = rest[-1]
    b = b_ref[0] if b_is_stacked else b_ref[...]
    acc = jnp.dot(a_ref[...].astype(BF16), b.astype(BF16), preferred_element_type=F32)
    if has_res:
        res_ref, gate_ref = rest[0], rest[1]
        acc = res_ref[...] + gate_ref[0] * acc
    o_ref[...] = acc.astype(o_ref.dtype)


def matmul(a, b, *, b_index=None, out_dtype=F32, tm=512, tn=512, res=None, gate=None, rows_per_batch=None):
    M, K = a.shape
    N = b.shape[-1]
    tm, tn = min(tm, M), min(tn, N)
    assert M % tm == 0 and N % tn == 0, (M, N, tm, tn)
    stacked = b.ndim == 3
    if stacked:
        b_spec = pl.BlockSpec((1, K, tn), lambda i, j: (b_index, 0, j))
    else:
        b_spec = pl.BlockSpec((K, tn), lambda i, j: (0, j))
    in_specs = [pl.BlockSpec((tm, K), lambda i, j: (i, 0)), b_spec]
    args = [a, b]
    if res is not None:
        assert rows_per_batch % tm == 0
        tpb = rows_per_batch // tm
        in_specs += [pl.BlockSpec((tm, tn), lambda i, j: (i, j)),
                     pl.BlockSpec((1, 1, tn), lambda i, j: (i // tpb, 0, j))]
        args += [res, gate]
    return pl.pallas_call(
        functools.partial(_mm_kernel, b_is_stacked=stacked, has_res=res is not None),
        grid=(M // tm, N // tn),
        in_specs=in_specs,
        out_specs=pl.BlockSpec((tm, tn), lambda i, j: (i, j)),
        out_shape=jax.ShapeDtypeStruct((M, N), out_dtype),
        compiler_params=_params("parallel", "parallel"),
    )(*args)


def _norm_kernel(x_ref, g_ref, *rest, modulated):
    o_ref = rest[-1]
    x = x_ref[...]
    y = x * lax.rsqrt(jnp.mean(x * x, axis=-1, keepdims=True) + NORM_EPS) * g_ref[...]
    if modulated:
        shift_ref, scale_ref = rest[0], rest[1]
        y = y * (1 + scale_ref[0]) + shift_ref[0]
    o_ref[...] = y.astype(o_ref.dtype)


def norm_mod(x2, g, shift=None, scale=None, *, rows_per_batch=None, out_dtype=BF16, tm=256):
    M, D = x2.shape
    tm = min(tm, M)
    in_specs = [pl.BlockSpec((tm, D), lambda i: (i, 0)), pl.BlockSpec((1, D), lambda i: (0, 0))]
    args = [x2, g.reshape(1, D)]
    if shift is not None:
        tpb = rows_per_batch // tm
        in_specs += [pl.BlockSpec((1, 1, D), lambda i: (i // tpb, 0, 0))] * 2
        args += [shift, scale]
    return pl.pallas_call(
        functools.partial(_norm_kernel, modulated=shift is not None),
        grid=(M // tm,),
        in_specs=in_specs,
        out_specs=pl.BlockSpec((tm, D), lambda i: (i, 0)),
        out_shape=jax.ShapeDtypeStruct((M, D), out_dtype),
        compiler_params=_params("parallel"),
    )(*args)


def _attn_kernel(q_ref, kt_ref, v_ref, o_ref, *, chunks, c):
    q = q_ref[0, 0]
    m = l = acc = None
    for lo, hi in chunks:
        s = jnp.dot(q, kt_ref[0, 0, :, lo:hi], preferred_element_type=F32)
        m_chunk = jnp.max(s, axis=-1, keepdims=True)
        m_new = m_chunk if m is None else jnp.maximum(m, m_chunk)
        p = jnp.exp2(s * c - m_new * c)
        pv = jnp.dot(p.astype(BF16), v_ref[0, 0, lo:hi, :], preferred_element_type=F32)
        psum = jnp.sum(p, axis=-1, keepdims=True)
        if m is None:
            l, acc = psum, pv
        else:
            alpha = jnp.exp2((m - m_new) * c)
            l = alpha * l + psum
            acc = alpha * acc + pv
        m = m_new
    o_ref[0] = (acc / l).astype(o_ref.dtype)


def attention(q, kt, v, *, scale, tq=512, tk=512):
    B, H, L, _ = q.shape
    K = kt.shape[-1]
    V = v.shape[-1]
    chunks = tuple((lo, min(lo + tk, K)) for lo in range(0, K, tk))
    return pl.pallas_call(
        functools.partial(_attn_kernel, chunks=chunks, c=scale * math.log2(math.e)),
        grid=(B, H, L // tq),
        in_specs=[pl.BlockSpec((1, 1, tq, QK_PAD), lambda b, h, i: (b, h, i, 0)),
                  pl.BlockSpec((1, 1, QK_PAD, K), lambda b, h, i: (b, h, 0, 0)),
                  pl.BlockSpec((1, 1, K, V), lambda b, h, i: (b, h, 0, 0))],
        out_specs=pl.BlockSpec((1, tq, V), lambda b, h, i: (b, i, h)),
        out_shape=jax.ShapeDtypeStruct((B, L, H * V), BF16),
        compiler_params=_params("parallel", "parallel", "arbitrary"),
    )(q, kt, v)


def _ffn_kernel(x_ref, wg_ref, wu_ref, wd_ref, g_ref, o_ref, *, nf):
    f = pl.program_id(1)
    rows = x_ref.shape[0] * x_ref.shape[2]
    d = x_ref.shape[3]
    x = x_ref[...].reshape(rows, d)
    a = jnp.dot(x, wg_ref[0, 0].astype(BF16), preferred_element_type=F32)
    u = jnp.dot(x, wu_ref[0, 0].astype(BF16), preferred_element_type=F32)
    h1 = (jax.nn.silu(a) * u).astype(BF16)
    y = jnp.dot(h1, wd_ref[0, 0].astype(BF16), preferred_element_type=F32).reshape(o_ref.shape)

    @pl.when(f == 0)
    def _():
        o_ref[...] = y

    @pl.when(f > 0)
    def _():
        o_ref[...] += y

    @pl.when(f == nf - 1)
    def _():
        o_ref[...] = o_ref[...] * g_ref[...]


def expert_ffn(xg, g, w_gate, w_up, w_down, layer, *, tf=256):
    B, E, C, D = xg.shape
    F = w_gate.shape[-1]
    nf = F // tf
    return pl.pallas_call(
        functools.partial(_ffn_kernel, nf=nf),
        grid=(E, nf),
        in_specs=[pl.BlockSpec((B, 1, C, D), lambda e, f: (0, e, 0, 0)),
                  pl.BlockSpec((1, 1, D, tf), lambda e, f: (layer, e, 0, f)),
                  pl.BlockSpec((1, 1, D, tf), lambda e, f: (layer, e, 0, f)),
                  pl.BlockSpec((1, 1, tf, D), lambda e, f: (layer, e, f, 0)),
                  pl.BlockSpec((B, 1, C, 1), lambda e, f: (0, e, 0, 0))],
        out_specs=pl.BlockSpec((B, 1, C, D), lambda e, f: (0, e, 0, 0)),
        out_shape=jax.ShapeDtypeStruct((B, E, C, D), F32),
        compiler_params=_params("parallel", "arbitrary"),
    )(xg, w_gate, w_up, w_down, g)


def _small_rmsnorm(x, g):
    return x * lax.rsqrt(jnp.mean(x * x, axis=-1, keepdims=True) + NORM_EPS) * g


def _rope_tables(rows):
    row = jnp.broadcast_to(jnp.arange(rows)[:, None], (rows, GRID_W)).reshape(-1).astype(F32)
    col = jnp.broadcast_to(jnp.arange(GRID_W)[None, :], (rows, GRID_W)).reshape(-1).astype(F32)
    n_freq = MLA_ROPE_DIM // 4
    inv = ROPE_THETA ** (-jnp.arange(n_freq, dtype=F32) / n_freq)
    ang = jnp.concatenate([row[:, None] * inv, col[:, None] * inv], axis=-1)
    return jnp.cos(ang), jnp.sin(ang)


def _rope(x, cos, sin):
    half = MLA_ROPE_DIM // 2
    x1, x2 = x[..., :half], x[..., half:]
    return jnp.concatenate([x1 * cos - x2 * sin, x2 * cos + x1 * sin], axis=-1)


def _pad_cols(w, n):
    return jnp.pad(w, ((0, 0), (0, n - w.shape[1])))


def _ada(cvec3, ada_w, ada_b, layer):
    R = cvec3.shape[0]
    a = jnp.pad(jax.nn.silu(cvec3), ((0, 8 - R), (0, 0)))
    m = matmul(a, ada_w, b_index=layer, tm=8, tn=512)[:R] + ada_b[layer]
    return m.reshape(R, 6, D_MODEL)


def _mla_layer(h, h_c, B, L, Lc, w_in, g_q, g_kv, w_uq, w_ukv):
    H = MLA_HEADS
    in_dim = w_in.shape[1]
    proj = matmul(h, _pad_cols(w_in, 896).astype(BF16), tn=896)[:, :in_dim]
    rest_c = matmul(h_c, _pad_cols(w_in[:, MLA_Q_RANK:], 384).astype(BF16), tn=384)[:, :in_dim - MLA_Q_RANK]
    cq = proj[:, :MLA_Q_RANK]
    ckv = jnp.concatenate([proj[:, MLA_Q_RANK:MLA_Q_RANK + MLA_KV_RANK].reshape(B, L, -1),
                           rest_c[:, :MLA_KV_RANK].reshape(B, Lc, -1)], axis=1)
    K = L + Lc
    cos, sin = _rope_tables(L // GRID_W)
    kr = _rope(proj[:, MLA_Q_RANK + MLA_KV_RANK:].reshape(B, L, -1), cos, sin)
    kr = jnp.concatenate([kr, rest_c[:, MLA_KV_RANK:].reshape(B, Lc, -1)], axis=1)
    w_ukv_b = w_ukv.astype(BF16)
    kv = matmul(_small_rmsnorm(ckv, g_kv).astype(BF16).reshape(B * K, -1), w_ukv_b, tm=544, tn=1024)
    kv = kv.reshape(B, K, H, MLA_NOPE_DIM + MLA_V_DIM)
    q = matmul(_small_rmsnorm(cq, g_q).astype(BF16), w_uq.astype(BF16), tn=1024)
    q = q.reshape(B, L, H, MLA_NOPE_DIM + MLA_ROPE_DIM)
    qr = _rope(q[..., MLA_NOPE_DIM:], cos[:, None, :], sin[:, None, :])
    pad = QK_PAD - MLA_NOPE_DIM - MLA_ROPE_DIM
    qp = jnp.concatenate([q[..., :MLA_NOPE_DIM], qr, jnp.zeros((B, L, H, pad), F32)], axis=-1)
    qp = qp.astype(BF16).transpose(0, 2, 1, 3)
    kp = jnp.concatenate([kv[..., :MLA_NOPE_DIM],
                          jnp.broadcast_to(kr[:, :, None, :], (B, K, H, MLA_ROPE_DIM)),
                          jnp.zeros((B, K, H, pad), F32)], axis=-1)
    kt = kp.astype(BF16).transpose(0, 2, 3, 1)
    v = kv[..., MLA_NOPE_DIM:].astype(BF16).transpose(0, 2, 1, 3)
    scale = (MLA_NOPE_DIM + MLA_ROPE_DIM) ** -0.5
    return attention(qp, kt, v, scale=scale)


def _hyena_filters(L, w1, b1, w2, b2, w3, b3, freq):
    t = jnp.arange(L, dtype=F32) / L
    w = 2 * math.pi * jnp.arange(L, dtype=F32) / L
    bands = jnp.linspace(1e-4, HYENA_BANDS - 1, HYENA_BANDS, dtype=F32)
    ang = w[:, None] * bands[None, :]
    z = jnp.concatenate([t[:, None], jnp.cos(ang), -jnp.sin(ang)], axis=-1)
    a = jnp.sin(freq * (z @ w1 + b1))
    a = jnp.sin(freq * (a @ w2 + b2))
    hf = (a @ w3 + b3).reshape(L, HYENA_ORDER, 2, D_MODEL)
    max_decay = math.log(HYENA_TARGET) / HYENA_FAST_PCT
    min_decay = math.log(HYENA_TARGET) / HYENA_SLOW_PCT
    deltas = jnp.linspace(min_decay, max_decay, D_MODEL, dtype=F32)
    hf = hf * jnp.exp(-t[:, None] * jnp.abs(deltas))[:, None, None, :]
    fwd = hf[:, :, 0]
    bwd = hf[1:, :, 1][::-1]
    k2 = jnp.concatenate([fwd, jnp.zeros((1, HYENA_ORDER, D_MODEL), F32), bwd], axis=0)
    k2 = k2 * lax.rsqrt(jnp.sum(k2 * k2, axis=0, keepdims=True) + NORM_EPS)
    return jnp.fft.rfft(k2, axis=0)


def _short_conv(u, w, b):
    L = u.shape[1]
    up = jnp.pad(u, ((0, 0), (1, 1), (0, 0)))
    return up[:, :L] * w[0] + up[:, 1:L + 1] * w[1] + up[:, 2:] * w[2] + b


def _long_conv(z, kf, skip):
    L = z.shape[1]
    zf = jnp.fft.rfft(z, n=2 * L, axis=1)
    y = jnp.fft.irfft(zf * kf[None], n=2 * L, axis=1)[:, :L]
    return y + skip * z


def _hyena_layer(h, B, L, w_in, conv_w, conv_b, f_w1, f_b1, f_w2, f_b2, f_w3, f_b3, f_freq, skip):
    D = D_MODEL
    u = matmul(h, w_in.astype(BF16), tn=1024).reshape(B, L, 3 * D)
    u = _short_conv(u, conv_w, conv_b)
    v, x1, x2 = u[..., :D], u[..., D:2 * D], u[..., 2 * D:]
    kf = _hyena_filters(L, f_w1, f_b1, f_w2, f_b2, f_w3, f_b3, f_freq)
    z = x1 * _long_conv(v, kf[:, 0], skip[0])
    z = x2 * _long_conv(z, kf[:, 1], skip[1])
    return z.reshape(B * L, D).astype(BF16)


def _moe_layer(hm, B, T, w_router, w_gate, w_up, w_down, layer):
    D = D_MODEL
    cap = EC_CAPACITY_FACTOR * T // N_EXPERTS
    hm3 = hm.reshape(B, T, D)
    aff = jax.nn.softmax(jnp.dot(hm3, w_router.astype(BF16), preferred_element_type=F32), axis=-1)
    g, idx = lax.top_k(jnp.swapaxes(aff, 1, 2), cap)
    xg = jax.vmap(lambda hb, ib: hb[ib])(hm3, idx)
    y = expert_ffn(xg, g[..., None], w_gate, w_up, w_down, layer)
    out = jax.vmap(lambda ib, yb: jnp.zeros((T, D), F32).at[ib.reshape(-1)].add(yb.reshape(-1, D)))(idx, y)
    return out.reshape(B * T, D)


def kernel(x, c, ctx, c_ctx, ada_w, ada_b, norm_g, final_g, mla_w_in, mla_g_q, mla_g_kv, mla_w_uq, mla_w_ukv, mla_w_o, hy_w_in, hy_conv_w, hy_conv_b, hy_f_w1, hy_f_b1, hy_f_w2, hy_f_b2, hy_f_w3, hy_f_b3, hy_f_freq, hy_skip, hy_w_out, moe_w_router, moe_w_gate, moe_w_up, moe_w_down):
    B, L, D = x.shape
    Lc = ctx.shape[1]
    x2 = x.reshape(B * L, D)
    ctx2 = ctx.reshape(B * Lc, D)

    mod3 = _ada(jnp.concatenate([c, c_ctx[None]], axis=0), ada_w, ada_b, 0)
    mod, mod_c = mod3[:B], jnp.broadcast_to(mod3[B:], (B, 6, D))
    h = norm_mod(x2, norm_g[0, 0], mod[:, 0:1], mod[:, 1:2], rows_per_batch=L)
    h_c = norm_mod(ctx2, norm_g[0, 0], mod_c[:, 0:1], mod_c[:, 1:2], rows_per_batch=Lc)
    o = _mla_layer(h, h_c, B, L, Lc, mla_w_in[0], mla_g_q[0], mla_g_kv[0], mla_w_uq[0], mla_w_ukv[0])
    x2 = matmul(o.reshape(B * L, -1), mla_w_o[0].astype(BF16), tn=1024,
                res=x2, gate=mod[:, 2:3], rows_per_batch=L)
    hm = norm_mod(x2, norm_g[0, 1], mod[:, 3:4], mod[:, 4:5], rows_per_batch=L)
    moe = _moe_layer(hm, B, L, moe_w_router[0], moe_w_gate, moe_w_up, moe_w_down, 0)
    x2 = x2 + jnp.repeat(mod[:, 5], L, axis=0) * moe

    mod = _ada(c, ada_w, ada_b, 1)
    h = norm_mod(x2, norm_g[1, 0], mod[:, 0:1], mod[:, 1:2], rows_per_batch=L)
    z = _hyena_layer(h, B, L, hy_w_in[0], hy_conv_w[0], hy_conv_b[0], hy_f_w1[0], hy_f_b1[0], hy_f_w2[0],
                     hy_f_b2[0], hy_f_w3[0], hy_f_b3[0], hy_f_freq[0], hy_skip[0])
    x2 = matmul(z, hy_w_out[0].astype(BF16), tn=1024, res=x2, gate=mod[:, 2:3], rows_per_batch=L)
    hm = norm_mod(x2, norm_g[1, 1], mod[:, 3:4], mod[:, 4:5], rows_per_batch=L)
    moe = _moe_layer(hm, B, L, moe_w_router[1], moe_w_gate, moe_w_up, moe_w_down, 1)
    x2 = x2 + jnp.repeat(mod[:, 5], L, axis=0) * moe

    return norm_mod(x2, final_g, out_dtype=F32).reshape(B, L, D)
```

```python
import functools
import math

import jax
import jax.numpy as jnp
import numpy as np
from jax import lax
from jax.experimental import pallas as pl
from jax.experimental.pallas import tpu as pltpu

F32 = jnp.float32
BF16 = jnp.bfloat16

D_MODEL = 2048
GRID_W = 64
NORM_EPS = 1e-6
MLA_HEADS = 16
MLA_Q_RANK = 512
MLA_KV_RANK = 256
MLA_NOPE_DIM = 128
MLA_ROPE_DIM = 64
MLA_V_DIM = 128
ROPE_THETA = 10000.0
HYENA_ORDER = 2
HYENA_BANDS = 16
HYENA_TARGET = 1e-2
HYENA_FAST_PCT = 0.3
HYENA_SLOW_PCT = 1.5
N_EXPERTS = 16
EC_CAPACITY_FACTOR = 2
FFT_N1 = 128
FFT_N2 = 64
FFT_K2 = FFT_N2 // 2 + 1
FFT_K2P = 40

V7X_VMEM_LIMIT_BYTES = 56 * 1024 * 1024
LANES = 128
QK_PAD = 256


def _params(*sem):
    return pltpu.CompilerParams(dimension_semantics=sem, vmem_limit_bytes=V7X_VMEM_LIMIT_BYTES)


def _mm_kernel(a_ref, b_ref, *rest, b_is_stacked, has_res):
    o_ref = rest[-1]
    b = b_ref[0] if b_is_stacked else b_ref[...]
    acc = jnp.dot(a_ref[...].astype(BF16), b.astype(BF16), preferred_element_type=F32)
    if has_res:
        res_ref, gate_ref = rest[0], rest[1]
        acc = res_ref[...] + gate_ref[0] * acc
    o_ref[...] = acc.astype(o_ref.dtype)


def matmul(a, b, *, b_index=None, out_dtype=F32, tm=512, tn=512, res=None, gate=None, rows_per_batch=None):
    M, K = a.shape
    N = b.shape[-1]
    tm, tn = min(tm, M), min(tn, N)
    assert M % tm == 0 and N % tn == 0, (M, N, tm, tn)
    stacked = b.ndim == 3
    if stacked:
        b_spec = pl.BlockSpec((1, K, tn), lambda i, j: (b_index, 0, j))
    else:
        b_spec = pl.BlockSpec((K, tn), lambda i, j: (0, j))
    in_specs = [pl.BlockSpec((tm, K), lambda i, j: (i, 0)), b_spec]
    args = [a, b]
    if res is not None:
        assert rows_per_batch % tm == 0
        tpb = rows_per_batch // tm
        in_specs += [pl.BlockSpec((tm, tn), lambda i, j: (i, j)),
                     pl.BlockSpec((1, 1, tn), lambda i, j: (i // tpb, 0, j))]
        args += [res, gate]
    return pl.pallas_call(
        functools.partial(_mm_kernel, b_is_stacked=stacked, has_res=res is not None),
        grid=(M // tm, N // tn),
        in_specs=in_specs,
        out_specs=pl.BlockSpec((tm, tn), lambda i, j: (i, j)),
        out_shape=jax.ShapeDtypeStruct((M, N), out_dtype),
        compiler_params=_params("parallel", "parallel"),
    )(*args)


def _norm_kernel(x_ref, g_ref, *rest, modulated):
    o_ref = rest[-1]
    x = x_ref[...]
    y = x * lax.rsqrt(jnp.mean(x * x, axis=-1, keepdims=True) + NORM_EPS) * g_ref[...]
    if modulated:
        shift_ref, scale_ref = rest[0], rest[1]
        y = y * (1 + scale_ref[0]) + shift_ref[0]
    o_ref[...] = y.astype(o_ref.dtype)


def norm_mod(x2, g, shift=None, scale=None, *, rows_per_batch=None, out_dtype=BF16, tm=256):
    M, D = x2.shape
    tm = min(tm, M)
    in_specs = [pl.BlockSpec((tm, D), lambda i: (i, 0)), pl.BlockSpec((1, D), lambda i: (0, 0))]
    args = [x2, g.reshape(1, D)]
    if shift is not None:
        tpb = rows_per_batch // tm
        in_specs += [pl.BlockSpec((1, 1, D), lambda i: (i // tpb, 0, 0))] * 2
        args += [shift, scale]
    return pl.pallas_call(
        functools.partial(_norm_kernel, modulated=shift is not None),
        grid=(M // tm,),
        in_specs=in_specs,
        out_specs=pl.BlockSpec((tm, D), lambda i: (i, 0)),
        out_shape=jax.ShapeDtypeStruct((M, D), out_dtype),
        compiler_params=_params("parallel"),
    )(*args)


def _attn_kernel(q_ref, kt_ref, v_ref, o_ref, *, chunks, c):
    q = q_ref[0, 0]
    m = l = acc = None
    for lo, hi in chunks:
        s = jnp.dot(q, kt_ref[0, 0, :, lo:hi], preferred_element_type=F32)
        m_chunk = jnp.max(s, axis=-1, keepdims=True)
        m_new = m_chunk if m is None else jnp.maximum(m, m_chunk)
        p = jnp.exp2(s * c - m_new * c)
        pv = jnp.dot(p.astype(BF16), v_ref[0, 0, lo:hi, :], preferred_element_type=F32)
        psum = jnp.sum(p, axis=-1, keepdims=True)
        if m is None:
            l, acc = psum, pv
        else:
            alpha = jnp.exp2((m - m_new) * c)
            l = alpha * l + psum
            acc = alpha * acc + pv
        m = m_new
    o_ref[0] = (acc / l).astype(o_ref.dtype)


def attention(q, kt, v, *, scale, tq=512, tk=512):
    B, H, L, _ = q.shape
    K = kt.shape[-1]
    V = v.shape[-1]
    chunks = tuple((lo, min(lo + tk, K)) for lo in range(0, K, tk))
    return pl.pallas_call(
        functools.partial(_attn_kernel, chunks=chunks, c=scale * math.log2(math.e)),
        grid=(B, H, L // tq),
        in_specs=[pl.BlockSpec((1, 1, tq, QK_PAD), lambda b, h, i: (b, h, i, 0)),
                  pl.BlockSpec((1, 1, QK_PAD, K), lambda b, h, i: (b, h, 0, 0)),
                  pl.BlockSpec((1, 1, K, V), lambda b, h, i: (b, h, 0, 0))],
        out_specs=pl.BlockSpec((1, tq, V), lambda b, h, i: (b, i, h)),
        out_shape=jax.ShapeDtypeStruct((B, L, H * V), BF16),
        compiler_params=_params("parallel", "parallel", "arbitrary"),
    )(q, kt, v)


def _ffn_kernel(x_ref, wg_ref, wu_ref, wd_ref, g_ref, o_ref, *, nf):
    f = pl.program_id(1)
    rows = x_ref.shape[0] * x_ref.shape[2]
    d = x_ref.shape[3]
    x = x_ref[...].reshape(rows, d)
    a = jnp.dot(x, wg_ref[0, 0].astype(BF16), preferred_element_type=F32)
    u = jnp.dot(x, wu_ref[0, 0].astype(BF16), preferred_element_type=F32)
    h1 = (jax.nn.silu(a) * u).astype(BF16)
    y = jnp.dot(h1, wd_ref[0, 0].astype(BF16), preferred_element_type=F32).reshape(o_ref.shape)

    @pl.when(f == 0)
    def _():
        o_ref[...] = y

    @pl.when(f > 0)
    def _():
        o_ref[...] += y

    @pl.when(f == nf - 1)
    def _():
        o_ref[...] = o_ref[...] * g_ref[...]


def expert_ffn(xg, g, w_gate, w_up, w_down, layer, *, tf=256):
    B, E, C, D = xg.shape
    F = w_gate.shape[-1]
    nf = F // tf
    return pl.pallas_call(
        functools.partial(_ffn_kernel, nf=nf),
        grid=(E, nf),
        in_specs=[pl.BlockSpec((B, 1, C, D), lambda e, f: (0, e, 0, 0)),
                  pl.BlockSpec((1, 1, D, tf), lambda e, f: (layer, e, 0, f)),
                  pl.BlockSpec((1, 1, D, tf), lambda e, f: (layer, e, 0, f)),
                  pl.BlockSpec((1, 1, tf, D), lambda e, f: (layer, e, f, 0)),
                  pl.BlockSpec((B, 1, C, 1), lambda e, f: (0, e, 0, 0))],
        out_specs=pl.BlockSpec((B, 1, C, D), lambda e, f: (0, e, 0, 0)),
        out_shape=jax.ShapeDtypeStruct((B, E, C, D), F32),
        compiler_params=_params("parallel", "arbitrary"),
    )(xg, w_gate, w_up, w_down, g)


def _small_rmsnorm(x, g):
    return x * lax.rsqrt(jnp.mean(x * x, axis=-1, keepdims=True) + NORM_EPS) * g


def _rope_tables(rows):
    row = jnp.broadcast_to(jnp.arange(rows)[:, None], (rows, GRID_W)).reshape(-1).astype(F32)
    col = jnp.broadcast_to(jnp.arange(GRID_W)[None, :], (rows, GRID_W)).reshape(-1).astype(F32)
    n_freq = MLA_ROPE_DIM // 4
    inv = ROPE_THETA ** (-jnp.arange(n_freq, dtype=F32) / n_freq)
    ang = jnp.concatenate([row[:, None] * inv, col[:, None] * inv], axis=-1)
    return jnp.cos(ang), jnp.sin(ang)


def _rope(x, cos, sin):
    half = MLA_ROPE_DIM // 2
    x1, x2 = x[..., :half], x[..., half:]
    return jnp.concatenate([x1 * cos - x2 * sin, x2 * cos + x1 * sin], axis=-1)


def _pad_cols(w, n):
    return jnp.pad(w, ((0, 0), (0, n - w.shape[1])))


def _ada(cvec3, ada_w, ada_b, layer):
    R = cvec3.shape[0]
    a = jnp.pad(jax.nn.silu(cvec3), ((0, 8 - R), (0, 0)))
    m = matmul(a, ada_w, b_index=layer, tm=8, tn=512)[:R] + ada_b[layer]
    return m.reshape(R, 6, D_MODEL)


def _mla_layer(h, h_c, B, L, Lc, w_in, g_q, g_kv, w_uq, w_ukv):
    H = MLA_HEADS
    in_dim = w_in.shape[1]
    proj = matmul(h, _pad_cols(w_in, 896).astype(BF16), tn=896)[:, :in_dim]
    rest_c = matmul(h_c, _pad_cols(w_in[:, MLA_Q_RANK:], 384).astype(BF16), tn=384)[:, :in_dim - MLA_Q_RANK]
    cq = proj[:, :MLA_Q_RANK]
    ckv = jnp.concatenate([proj[:, MLA_Q_RANK:MLA_Q_RANK + MLA_KV_RANK].reshape(B, L, -1),
                           rest_c[:, :MLA_KV_RANK].reshape(B, Lc, -1)], axis=1)
    K = L + Lc
    cos, sin = _rope_tables(L // GRID_W)
    kr = _rope(proj[:, MLA_Q_RANK + MLA_KV_RANK:].reshape(B, L, -1), cos, sin)
    kr = jnp.concatenate([kr, rest_c[:, MLA_KV_RANK:].reshape(B, Lc, -1)], axis=1)
    w_ukv_b = w_ukv.astype(BF16)
    kv = matmul(_small_rmsnorm(ckv, g_kv).astype(BF16).reshape(B * K, -1), w_ukv_b, tm=544, tn=1024)
    kv = kv.reshape(B, K, H, MLA_NOPE_DIM + MLA_V_DIM)
    q = matmul(_small_rmsnorm(cq, g_q).astype(BF16), w_uq.astype(BF16), tn=1024)
    q = q.reshape(B, L, H, MLA_NOPE_DIM + MLA_ROPE_DIM)
    qr = _rope(q[..., MLA_NOPE_DIM:], cos[:, None, :], sin[:, None, :])
    pad = QK_PAD - MLA_NOPE_DIM - MLA_ROPE_DIM
    qp = jnp.concatenate([q[..., :MLA_NOPE_DIM], qr, jnp.zeros((B, L, H, pad), F32)], axis=-1)
    qp = qp.astype(BF16).transpose(0, 2, 1, 3)
    kp = jnp.concatenate([kv[..., :MLA_NOPE_DIM],
                          jnp.broadcast_to(kr[:, :, None, :], (B, K, H, MLA_ROPE_DIM)),
                          jnp.zeros((B, K, H, pad), F32)], axis=-1)
    kt = kp.astype(BF16).transpose(0, 2, 3, 1)
    v = kv[..., MLA_NOPE_DIM:].astype(BF16).transpose(0, 2, 1, 3)
    scale = (MLA_NOPE_DIM + MLA_ROPE_DIM) ** -0.5
    return attention(qp, kt, v, scale=scale)


def _fft_tables(L):
    n = 2 * L
    n2_in = L // FFT_N1
    k2 = np.arange(FFT_K2)
    th = 2 * np.pi * ((k2[:, None] * np.arange(n2_in)[None, :]) % FFT_N2) / FFT_N2
    f1 = np.zeros((2 * FFT_K2P, n2_in))
    f1[:FFT_K2] = np.cos(th)
    f1[FFT_K2P:FFT_K2P + FFT_K2] = -np.sin(th)
    wgt = np.where((k2 == 0) | (k2 == FFT_N2 // 2), 1.0, 2.0)
    g = np.zeros((n2_in, 2 * FFT_K2P))
    g[:, :FFT_K2] = (wgt[:, None] * np.cos(th)).T / n
    g[:, FFT_K2P:FFT_K2P + FFT_K2] = -(wgt[:, None] * np.sin(th)).T / n
    k1 = np.arange(FFT_N1)[:, None]
    n1 = np.arange(FFT_N1)[None, :]
    m2 = np.zeros((FFT_K2, 2 * FFT_N1, 2 * FFT_N1))
    for kk in range(FFT_K2):
        phi = 2 * np.pi * ((n1 * k1 * FFT_N2 + n1 * kk) % n) / n
        tr, ti = np.cos(phi), -np.sin(phi)
        m2[kk] = np.block([[tr, -ti], [ti, tr]])
    as_bf16 = lambda a: jnp.asarray(a, dtype=F32).astype(BF16)
    return as_bf16(f1), as_bf16(g), as_bf16(m2), as_bf16(np.transpose(m2, (0, 2, 1)))


def _fft_s1_kernel(f_ref, z_ref, o_ref):
    o_ref[0] = jnp.dot(f_ref[...], z_ref[0].astype(BF16), preferred_element_type=F32).astype(o_ref.dtype)


def fft_stage1(z3, f1, *, lead_off, nb, wl=8192):
    _, n2_in, W = z3.shape
    rows = f1.shape[0]
    return pl.pallas_call(
        _fft_s1_kernel,
        grid=(nb, W // wl),
        in_specs=[pl.BlockSpec((rows, n2_in), lambda b, i: (0, 0)),
                  pl.BlockSpec((1, n2_in, wl), lambda b, i: (lead_off + b, 0, i))],
        out_specs=pl.BlockSpec((1, rows, wl), lambda b, i: (b, 0, i)),
        out_shape=jax.ShapeDtypeStruct((nb, rows, W), BF16),
        compiler_params=_params("parallel", "parallel"),
    )(f1, z3)


def _fft_s2_kernel(a_ref, m_ref, mt_ref, kf_ref, o_ref):
    k2 = pl.program_id(0)

    @pl.when(k2 < FFT_K2)
    def _():
        ct = a_ref.shape[-1]
        x = jnp.dot(m_ref[0], a_ref[...].reshape(2 * FFT_N1, ct), preferred_element_type=F32)
        xr, xi = x[:FFT_N1], x[FFT_N1:]
        kr, ki = kf_ref[0, 0, 0], kf_ref[0, 1, 0]
        p = jnp.concatenate([xr * kr - xi * ki, xr * ki + xi * kr], axis=0).astype(BF16)
        e = jnp.dot(mt_ref[0], p, preferred_element_type=F32)
        o_ref[...] = e.astype(o_ref.dtype).reshape(o_ref.shape)

    @pl.when(k2 >= FFT_K2)
    def _():
        o_ref[...] = jnp.zeros(o_ref.shape, o_ref.dtype)


def fft_stage2(a5, m2, m2t, kf, order):
    B, _, _, _, D = a5.shape
    last = FFT_K2 - 1
    blk = (1, 2, 1, FFT_N1, D)
    return pl.pallas_call(
        _fft_s2_kernel,
        grid=(FFT_K2P, B),
        in_specs=[pl.BlockSpec(blk, lambda k, b: (b, 0, jnp.minimum(k, last), 0, 0)),
                  pl.BlockSpec((1, 2 * FFT_N1, 2 * FFT_N1), lambda k, b: (jnp.minimum(k, last), 0, 0)),
                  pl.BlockSpec((1, 2 * FFT_N1, 2 * FFT_N1), lambda k, b: (jnp.minimum(k, last), 0, 0)),
                  pl.BlockSpec(blk, lambda k, b: (order, 0, jnp.minimum(k, last), 0, 0))],
        out_specs=pl.BlockSpec(blk, lambda k, b: (b, 0, k, 0, 0)),
        out_shape=jax.ShapeDtypeStruct(a5.shape, BF16),
        compiler_params=_params("parallel", "arbitrary"),
    )(a5, m2, m2t, kf)


def _fft_f2_kernel(f_ref, g_ref, m_ref, o_ref):
    k2 = pl.program_id(0)

    @pl.when(k2 < FFT_K2)
    def _():
        ct = f_ref.shape[-1]
        xf = jnp.dot(m_ref[0], f_ref[...].reshape(2 * FFT_N1, ct), preferred_element_type=F32)
        xg = jnp.dot(m_ref[0], g_ref[...].reshape(2 * FFT_N1, ct), preferred_element_type=F32)
        o_ref[0, 0, 0] = xf[:FFT_N1] + xg[:FFT_N1]
        o_ref[0, 1, 0] = xf[FFT_N1:] - xg[FFT_N1:]

    @pl.when(k2 >= FFT_K2)
    def _():
        o_ref[...] = jnp.zeros(o_ref.shape, o_ref.dtype)


def fft_filter_spectrum(a4, m2, D):
    last = FFT_K2 - 1
    blk = (2, 1, FFT_N1, D)
    return pl.pallas_call(
        _fft_f2_kernel,
        grid=(FFT_K2P, HYENA_ORDER),
        in_specs=[pl.BlockSpec(blk, lambda k, o: (0, jnp.minimum(k, last), 0, 2 * o)),
                  pl.BlockSpec(blk, lambda k, o: (0, jnp.minimum(k, last), 0, 2 * o + 1)),
                  pl.BlockSpec((1, 2 * FFT_N1, 2 * FFT_N1), lambda k, o: (jnp.minimum(k, last), 0, 0))],
        out_specs=pl.BlockSpec((1, 2, 1, FFT_N1, D), lambda k, o: (o, 0, k, 0, 0)),
        out_shape=jax.ShapeDtypeStruct((HYENA_ORDER, 2, FFT_K2P, FFT_N1, D), F32),
        compiler_params=_params("arbitrary", "parallel"),
    )(a4, a4, m2)


def _fft_s3_kernel(g_ref, e_ref, z_ref, x_ref, skip_ref, o_ref):
    y = jnp.dot(g_ref[...], e_ref[0], preferred_element_type=F32)
    o_ref[0] = (x_ref[0] * (y + skip_ref[...] * z_ref[0])).astype(o_ref.dtype)


def fft_stage3(e3, g, z3, z_off, x3, x_off, skip_t, *, out_dtype, wl=8192):
    B, rows, W = e3.shape
    n2_in = g.shape[0]
    return pl.pallas_call(
        _fft_s3_kernel,
        grid=(B, W // wl),
        in_specs=[pl.BlockSpec((n2_in, rows), lambda b, i: (0, 0)),
                  pl.BlockSpec((1, rows, wl), lambda b, i: (b, 0, i)),
                  pl.BlockSpec((1, n2_in, wl), lambda b, i: (z_off + b, 0, i)),
                  pl.BlockSpec((1, n2_in, wl), lambda b, i: (x_off + b, 0, i)),
                  pl.BlockSpec((1, wl), lambda b, i: (0, 0))],
        out_specs=pl.BlockSpec((1, n2_in, wl), lambda b, i: (b, 0, i)),
        out_shape=jax.ShapeDtypeStruct((B, n2_in, W), out_dtype),
        compiler_params=_params("parallel", "parallel"),
    )(g, e3, z3, x3, skip_t)


def _hyin_kernel(ap_ref, a_ref, an_ref, b_ref, cw_ref, cb_ref, o_ref, *, tiles_per_seq):
    i = pl.program_id(0)
    b = b_ref[...]
    acc = jnp.dot(a_ref[...], b, preferred_element_type=F32)
    halo = ap_ref.shape[0]
    prev = jnp.dot(ap_ref[...], b, preferred_element_type=F32)[halo - 1:halo]
    nxt = jnp.dot(an_ref[...], b, preferred_element_type=F32)[0:1]
    t = i % tiles_per_seq
    prev = jnp.where(t == 0, 0.0, prev)
    nxt = jnp.where(t == tiles_per_seq - 1, 0.0, nxt)
    tm = acc.shape[0]
    row = lax.broadcasted_iota(jnp.int32, acc.shape, 0)
    up = jnp.where(row == 0, prev, pltpu.roll(acc, 1, axis=0))
    dn = jnp.where(row == tm - 1, nxt, pltpu.roll(acc, tm - 1, axis=0))
    w = cw_ref[...]
    o_ref[0] = up * w[0:1] + acc * w[1:2] + dn * w[2:3] + cb_ref[...]


def hyena_in(h, w_in, conv_w, conv_b, L, *, tm=512, tn=1024, halo=16):
    M, K = h.shape
    N = w_in.shape[1]
    D = N // 3
    per = D // tn
    return pl.pallas_call(
        functools.partial(_hyin_kernel, tiles_per_seq=L // tm),
        grid=(M // tm, N // tn),
        in_specs=[pl.BlockSpec((halo, K), lambda i, j: (jnp.maximum(i * (tm // halo) - 1, 0), 0)),
                  pl.BlockSpec((tm, K), lambda i, j: (i, 0)),
                  pl.BlockSpec((halo, K), lambda i, j: (jnp.minimum((i + 1) * (tm // halo), M // halo - 1), 0)),
                  pl.BlockSpec((K, tn), lambda i, j: (0, j)),
                  pl.BlockSpec((3, tn), lambda i, j: (0, j)),
                  pl.BlockSpec((1, tn), lambda i, j: (0, j))],
        out_specs=pl.BlockSpec((1, tm, tn), lambda i, j: (j // per, i, j % per)),
        out_shape=jax.ShapeDtypeStruct((3, M, D), F32),
        compiler_params=_params("parallel", "parallel"),
    )(h, h, h, w_in, conv_w, conv_b.reshape(1, N))


def _hyena_filter_taps(L, w1, b1, w2, b2, w3, b3, freq):
    D = D_MODEL
    t = jnp.arange(L, dtype=F32) / L
    w = 2 * math.pi * jnp.arange(L, dtype=F32) / L
    bands = jnp.linspace(1e-4, HYENA_BANDS - 1, HYENA_BANDS, dtype=F32)
    ang = w[:, None] * bands[None, :]
    z = jnp.concatenate([t[:, None], jnp.cos(ang), -jnp.sin(ang)], axis=-1)
    a = jnp.sin(freq * (z @ w1 + b1))
    a = jnp.sin(freq * (a @ w2 + b2))
    hf = (matmul(a, w3, tn=1024) + b3).reshape(L, HYENA_ORDER, 2, D)
    max_decay = math.log(HYENA_TARGET) / HYENA_FAST_PCT
    min_decay = math.log(HYENA_TARGET) / HYENA_SLOW_PCT
    deltas = jnp.linspace(min_decay, max_decay, D, dtype=F32)
    hf = hf * jnp.exp(-t[:, None] * jnp.abs(deltas))[:, None, None, :]
    lag0_bwd = (jnp.arange(L)[:, None, None, None] == 0) & (jnp.arange(2)[None, None, :, None] == 1)
    hf = jnp.where(lag0_bwd, 0.0, hf)
    hf = hf * lax.rsqrt(jnp.sum(hf * hf, axis=(0, 2), keepdims=True) + NORM_EPS)
    return hf.reshape(L, HYENA_ORDER * 2 * D)


def _hyena_layer(h, B, L, w_in, conv_w, conv_b, f_w1, f_b1, f_w2, f_b2, f_w3, f_b3, f_freq, skip):
    D = D_MODEL
    n2_in = L // FFT_N1
    W = FFT_N1 * D
    f1, g, m2, m2t = _fft_tables(L)
    taps = _hyena_filter_taps(L, f_w1, f_b1, f_w2, f_b2, f_w3, f_b3, f_freq)
    af = fft_stage1(taps.reshape(1, n2_in, FFT_N1 * taps.shape[1]), f1, lead_off=0, nb=1)
    kf = fft_filter_spectrum(af.reshape(2, FFT_K2P, FFT_N1, taps.shape[1]), m2, D)
    u3 = hyena_in(h, w_in.astype(BF16), conv_w, conv_b, L).reshape(3 * B, n2_in, W)
    z3 = u3
    for order in range(HYENA_ORDER):
        a = fft_stage1(z3, f1, lead_off=0, nb=B)
        e = fft_stage2(a.reshape(B, 2, FFT_K2P, FFT_N1, D), m2, m2t, kf, order)
        last = order == HYENA_ORDER - 1
        z3 = fft_stage3(e.reshape(B, 2 * FFT_K2P, W), g, z3, 0, u3, (order + 1) * B,
                        jnp.tile(skip[order], 8192 // D).reshape(1, 8192), out_dtype=BF16 if last else F32)
    return z3.reshape(B * L, D)


def _moe_layer(hm, B, T, w_router, w_gate, w_up, w_down, layer):
    D = D_MODEL
    cap = EC_CAPACITY_FACTOR * T // N_EXPERTS
    hm3 = hm.reshape(B, T, D)
    aff = jax.nn.softmax(jnp.dot(hm3, w_router.astype(BF16), preferred_element_type=F32), axis=-1)
    g, idx = lax.top_k(jnp.swapaxes(aff, 1, 2), cap)
    xg = jax.vmap(lambda hb, ib: hb[ib])(hm3, idx)
    y = expert_ffn(xg, g[..., None], w_gate, w_up, w_down, layer)
    out = jax.vmap(lambda ib, yb: jnp.zeros((T, D), F32).at[ib.reshape(-1)].add(yb.reshape(-1, D)))(idx, y)
    return out.reshape(B * T, D)


def kernel(x, c, ctx, c_ctx, ada_w, ada_b, norm_g, final_g, mla_w_in, mla_g_q, mla_g_kv, mla_w_uq, mla_w_ukv, mla_w_o, hy_w_in, hy_conv_w, hy_conv_b, hy_f_w1, hy_f_b1, hy_f_w2, hy_f_b2, hy_f_w3, hy_f_b3, hy_f_freq, hy_skip, hy_w_out, moe_w_router, moe_w_gate, moe_w_up, moe_w_down):
    B, L, D = x.shape
    Lc = ctx.shape[1]
    x2 = x.reshape(B * L, D)
    ctx2 = ctx.reshape(B * Lc, D)

    mod3 = _ada(jnp.concatenate([c, c_ctx[None]], axis=0), ada_w, ada_b, 0)
    mod, mod_c = mod3[:B], jnp.broadcast_to(mod3[B:], (B, 6, D))
    h = norm_mod(x2, norm_g[0, 0], mod[:, 0:1], mod[:, 1:2], rows_per_batch=L)
    h_c = norm_mod(ctx2, norm_g[0, 0], mod_c[:, 0:1], mod_c[:, 1:2], rows_per_batch=Lc)
    o = _mla_layer(h, h_c, B, L, Lc, mla_w_in[0], mla_g_q[0], mla_g_kv[0], mla_w_uq[0], mla_w_ukv[0])
    x2 = matmul(o.reshape(B * L, -1), mla_w_o[0].astype(BF16), tn=1024,
                res=x2, gate=mod[:, 2:3], rows_per_batch=L)
    hm = norm_mod(x2, norm_g[0, 1], mod[:, 3:4], mod[:, 4:5], rows_per_batch=L)
    moe = _moe_layer(hm, B, L, moe_w_router[0], moe_w_gate, moe_w_up, moe_w_down, 0)
    x2 = x2 + jnp.repeat(mod[:, 5], L, axis=0) * moe

    mod = _ada(c, ada_w, ada_b, 1)
    h = norm_mod(x2, norm_g[1, 0], mod[:, 0:1], mod[:, 1:2], rows_per_batch=L)
    z = _hyena_layer(h, B, L, hy_w_in[0], hy_conv_w[0], hy_conv_b[0], hy_f_w1[0], hy_f_b1[0], hy_f_w2[0],
                     hy_f_b2[0], hy_f_w3[0], hy_f_b3[0], hy_f_freq[0], hy_skip[0])
    x2 = matmul(z, hy_w_out[0].astype(BF16), tn=1024, res=x2, gate=mod[:, 2:3], rows_per_batch=L)
    hm = norm_mod(x2, norm_g[1, 1], mod[:, 3:4], mod[:, 4:5], rows_per_batch=L)
    moe = _moe_layer(hm, B, L, moe_w_router[1], moe_w_gate, moe_w_up, moe_w_down, 1)
    x2 = x2 + jnp.repeat(mod[:, 5], L, axis=0) * moe

    return norm_mod(x2, final_g, out_dtype=F32).reshape(B, L, D)
```

```python
import functools
import math

import jax
import jax.numpy as jnp
import numpy as np
from jax import lax
from jax.experimental import pallas as pl
from jax.experimental.pallas import tpu as pltpu

F32 = jnp.float32
BF16 = jnp.bfloat16

D_MODEL = 2048
GRID_W = 64
NORM_EPS = 1e-6
MLA_HEADS = 16
MLA_Q_RANK = 512
MLA_KV_RANK = 256
MLA_NOPE_DIM = 128
MLA_ROPE_DIM = 64
MLA_V_DIM = 128
ROPE_THETA = 10000.0
HYENA_ORDER = 2
HYENA_BANDS = 16
HYENA_TARGET = 1e-2
HYENA_FAST_PCT = 0.3
HYENA_SLOW_PCT = 1.5
N_EXPERTS = 16
EC_CAPACITY_FACTOR = 2
FFT_N1 = 128
FFT_N2 = 64
FFT_K2 = FFT_N2 // 2 + 1
FFT_K2P = 40
FFT_GROUP = 16

V7X_VMEM_LIMIT_BYTES = 56 * 1024 * 1024
LANES = 128
QK_PAD = 256


def _params(*sem):
    return pltpu.CompilerParams(dimension_semantics=sem, vmem_limit_bytes=V7X_VMEM_LIMIT_BYTES)


def _mm_kernel(a_ref, b_ref, *rest, b_is_stacked, has_res):
    o_ref = rest[-1]
    b = b_ref[0] if b_is_stacked else b_ref[...]
    acc = jnp.dot(a_ref[...].astype(BF16), b.astype(BF16), preferred_element_type=F32)
    if has_res:
        res_ref, gate_ref = rest[0], rest[1]
        acc = res_ref[...] + gate_ref[0] * acc
    o_ref[...] = acc.astype(o_ref.dtype)


def matmul(a, b, *, b_index=None, out_dtype=F32, tm=512, tn=512, res=None, gate=None, rows_per_batch=None):
    M, K = a.shape
    N = b.shape[-1]
    tm, tn = min(tm, M), min(tn, N)
    assert M % tm == 0 and N % tn == 0, (M, N, tm, tn)
    stacked = b.ndim == 3
    if stacked:
        b_spec = pl.BlockSpec((1, K, tn), lambda i, j: (b_index, 0, j))
    else:
        b_spec = pl.BlockSpec((K, tn), lambda i, j: (0, j))
    in_specs = [pl.BlockSpec((tm, K), lambda i, j: (i, 0)), b_spec]
    args = [a, b]
    if res is not None:
        assert rows_per_batch % tm == 0
        tpb = rows_per_batch // tm
        in_specs += [pl.BlockSpec((tm, tn), lambda i, j: (i, j)),
                     pl.BlockSpec((1, 1, tn), lambda i, j: (i // tpb, 0, j))]
        args += [res, gate]
    return pl.pallas_call(
        functools.partial(_mm_kernel, b_is_stacked=stacked, has_res=res is not None),
        grid=(M // tm, N // tn),
        in_specs=in_specs,
        out_specs=pl.BlockSpec((tm, tn), lambda i, j: (i, j)),
        out_shape=jax.ShapeDtypeStruct((M, N), out_dtype),
        compiler_params=_params("parallel", "parallel"),
    )(*args)


def _norm_kernel(x_ref, g_ref, *rest, modulated):
    o_ref = rest[-1]
    x = x_ref[...]
    y = x * lax.rsqrt(jnp.mean(x * x, axis=-1, keepdims=True) + NORM_EPS) * g_ref[...]
    if modulated:
        shift_ref, scale_ref = rest[0], rest[1]
        y = y * (1 + scale_ref[0]) + shift_ref[0]
    o_ref[...] = y.astype(o_ref.dtype)


def norm_mod(x2, g, shift=None, scale=None, *, rows_per_batch=None, out_dtype=BF16, tm=256):
    M, D = x2.shape
    tm = min(tm, M)
    in_specs = [pl.BlockSpec((tm, D), lambda i: (i, 0)), pl.BlockSpec((1, D), lambda i: (0, 0))]
    args = [x2, g.reshape(1, D)]
    if shift is not None:
        tpb = rows_per_batch // tm
        in_specs += [pl.BlockSpec((1, 1, D), lambda i: (i // tpb, 0, 0))] * 2
        args += [shift, scale]
    return pl.pallas_call(
        functools.partial(_norm_kernel, modulated=shift is not None),
        grid=(M // tm,),
        in_specs=in_specs,
        out_specs=pl.BlockSpec((tm, D), lambda i: (i, 0)),
        out_shape=jax.ShapeDtypeStruct((M, D), out_dtype),
        compiler_params=_params("parallel"),
    )(*args)


def _attn_kernel(q_ref, kt_ref, v_ref, o_ref, *, chunks, c):
    q = q_ref[0, 0]
    m = l = acc = None
    for lo, hi in chunks:
        s = jnp.dot(q, kt_ref[0, 0, :, lo:hi], preferred_element_type=F32)
        m_chunk = jnp.max(s, axis=-1, keepdims=True)
        m_new = m_chunk if m is None else jnp.maximum(m, m_chunk)
        p = jnp.exp2(s * c - m_new * c)
        pv = jnp.dot(p.astype(BF16), v_ref[0, 0, lo:hi, :], preferred_element_type=F32)
        psum = jnp.sum(p, axis=-1, keepdims=True)
        if m is None:
            l, acc = psum, pv
        else:
            alpha = jnp.exp2((m - m_new) * c)
            l = alpha * l + psum
            acc = alpha * acc + pv
        m = m_new
    o_ref[0] = (acc / l).astype(o_ref.dtype)


def attention(q, kt, v, *, scale, tq=512, tk=512):
    B, H, L, _ = q.shape
    K = kt.shape[-1]
    V = v.shape[-1]
    chunks = tuple((lo, min(lo + tk, K)) for lo in range(0, K, tk))
    return pl.pallas_call(
        functools.partial(_attn_kernel, chunks=chunks, c=scale * math.log2(math.e)),
        grid=(B, H, L // tq),
        in_specs=[pl.BlockSpec((1, 1, tq, QK_PAD), lambda b, h, i: (b, h, i, 0)),
                  pl.BlockSpec((1, 1, QK_PAD, K), lambda b, h, i: (b, h, 0, 0)),
                  pl.BlockSpec((1, 1, K, V), lambda b, h, i: (b, h, 0, 0))],
        out_specs=pl.BlockSpec((1, tq, V), lambda b, h, i: (b, i, h)),
        out_shape=jax.ShapeDtypeStruct((B, L, H * V), BF16),
        compiler_params=_params("parallel", "parallel", "arbitrary"),
    )(q, kt, v)


def _ffn_kernel(x_ref, wg_ref, wu_ref, wd_ref, g_ref, o_ref, *, nf):
    f = pl.program_id(1)
    rows = x_ref.shape[0] * x_ref.shape[2]
    d = x_ref.shape[3]
    x = x_ref[...].reshape(rows, d)
    a = jnp.dot(x, wg_ref[0, 0].astype(BF16), preferred_element_type=F32)
    u = jnp.dot(x, wu_ref[0, 0].astype(BF16), preferred_element_type=F32)
    h1 = (jax.nn.silu(a) * u).astype(BF16)
    y = jnp.dot(h1, wd_ref[0, 0].astype(BF16), preferred_element_type=F32).reshape(o_ref.shape)

    @pl.when(f == 0)
    def _():
        o_ref[...] = y

    @pl.when(f > 0)
    def _():
        o_ref[...] += y

    @pl.when(f == nf - 1)
    def _():
        o_ref[...] = o_ref[...] * g_ref[...]


def expert_ffn(xg, g, w_gate, w_up, w_down, layer, *, tf=256):
    B, E, C, D = xg.shape
    F = w_gate.shape[-1]
    nf = F // tf
    return pl.pallas_call(
        functools.partial(_ffn_kernel, nf=nf),
        grid=(E, nf),
        in_specs=[pl.BlockSpec((B, 1, C, D), lambda e, f: (0, e, 0, 0)),
                  pl.BlockSpec((1, 1, D, tf), lambda e, f: (layer, e, 0, f)),
                  pl.BlockSpec((1, 1, D, tf), lambda e, f: (layer, e, 0, f)),
                  pl.BlockSpec((1, 1, tf, D), lambda e, f: (layer, e, f, 0)),
                  pl.BlockSpec((B, 1, C, 1), lambda e, f: (0, e, 0, 0))],
        out_specs=pl.BlockSpec((B, 1, C, D), lambda e, f: (0, e, 0, 0)),
        out_shape=jax.ShapeDtypeStruct((B, E, C, D), F32),
        compiler_params=_params("parallel", "arbitrary"),
    )(xg, w_gate, w_up, w_down, g)


def _small_rmsnorm(x, g):
    return x * lax.rsqrt(jnp.mean(x * x, axis=-1, keepdims=True) + NORM_EPS) * g


def _rope_tables(rows):
    row = jnp.broadcast_to(jnp.arange(rows)[:, None], (rows, GRID_W)).reshape(-1).astype(F32)
    col = jnp.broadcast_to(jnp.arange(GRID_W)[None, :], (rows, GRID_W)).reshape(-1).astype(F32)
    n_freq = MLA_ROPE_DIM // 4
    inv = ROPE_THETA ** (-jnp.arange(n_freq, dtype=F32) / n_freq)
    ang = jnp.concatenate([row[:, None] * inv, col[:, None] * inv], axis=-1)
    return jnp.cos(ang), jnp.sin(ang)


def _rope(x, cos, sin):
    half = MLA_ROPE_DIM // 2
    x1, x2 = x[..., :half], x[..., half:]
    return jnp.concatenate([x1 * cos - x2 * sin, x2 * cos + x1 * sin], axis=-1)


def _pad_cols(w, n):
    return jnp.concatenate([w, jnp.zeros((w.shape[0], n - w.shape[1]), w.dtype)], axis=1)


def _ada(cvec3, ada_w, ada_b, layer):
    R = cvec3.shape[0]
    a = jnp.concatenate([jax.nn.silu(cvec3), jnp.zeros((8 - R, cvec3.shape[1]), F32)], axis=0)
    m = matmul(a, ada_w, b_index=layer, tm=8, tn=512)[:R] + ada_b[layer]
    return m.reshape(R, 6, D_MODEL)


def _mla_layer(h, h_c, B, L, Lc, w_in, g_q, g_kv, w_uq, w_ukv):
    H = MLA_HEADS
    in_dim = w_in.shape[1]
    proj = matmul(h, _pad_cols(w_in, 896).astype(BF16), tn=896)[:, :in_dim]
    rest_c = matmul(h_c, _pad_cols(w_in[:, MLA_Q_RANK:], 384).astype(BF16), tn=384)[:, :in_dim - MLA_Q_RANK]
    cq = proj[:, :MLA_Q_RANK]
    ckv = jnp.concatenate([proj[:, MLA_Q_RANK:MLA_Q_RANK + MLA_KV_RANK].reshape(B, L, -1),
                           rest_c[:, :MLA_KV_RANK].reshape(B, Lc, -1)], axis=1)
    K = L + Lc
    cos, sin = _rope_tables(L // GRID_W)
    kr = _rope(proj[:, MLA_Q_RANK + MLA_KV_RANK:].reshape(B, L, -1), cos, sin)
    kr = jnp.concatenate([kr, rest_c[:, MLA_KV_RANK:].reshape(B, Lc, -1)], axis=1)
    w_ukv_b = w_ukv.astype(BF16)
    kv = matmul(_small_rmsnorm(ckv, g_kv).astype(BF16).reshape(B * K, -1), w_ukv_b, tm=544, tn=1024)
    kv = kv.reshape(B, K, H, MLA_NOPE_DIM + MLA_V_DIM)
    q = matmul(_small_rmsnorm(cq, g_q).astype(BF16), w_uq.astype(BF16), tn=1024)
    q = q.reshape(B, L, H, MLA_NOPE_DIM + MLA_ROPE_DIM)
    qr = _rope(q[..., MLA_NOPE_DIM:], cos[:, None, :], sin[:, None, :])
    pad = QK_PAD - MLA_NOPE_DIM - MLA_ROPE_DIM
    qp = jnp.concatenate([q[..., :MLA_NOPE_DIM], qr, jnp.zeros((B, L, H, pad), F32)], axis=-1)
    qp = qp.astype(BF16).transpose(0, 2, 1, 3)
    kp = jnp.concatenate([kv[..., :MLA_NOPE_DIM],
                          jnp.broadcast_to(kr[:, :, None, :], (B, K, H, MLA_ROPE_DIM)),
                          jnp.zeros((B, K, H, pad), F32)], axis=-1)
    kt = kp.astype(BF16).transpose(0, 2, 3, 1)
    v = kv[..., MLA_NOPE_DIM:].astype(BF16).transpose(0, 2, 1, 3)
    scale = (MLA_NOPE_DIM + MLA_ROPE_DIM) ** -0.5
    return attention(qp, kt, v, scale=scale)


def _fft_tables(L):
    n = 2 * L
    n2_in = L // FFT_N1
    k2 = np.arange(FFT_K2)
    th = 2 * np.pi * ((k2[:, None] * np.arange(n2_in)[None, :]) % FFT_N2) / FFT_N2
    f1 = np.zeros((2 * FFT_K2P, n2_in))
    f1[:FFT_K2] = np.cos(th)
    f1[FFT_K2P:FFT_K2P + FFT_K2] = -np.sin(th)
    wgt = np.where((k2 == 0) | (k2 == FFT_N2 // 2), 1.0, 2.0)
    g = np.zeros((n2_in, 2 * FFT_K2P))
    g[:, :FFT_K2] = (wgt[:, None] * np.cos(th)).T / n
    g[:, FFT_K2P:FFT_K2P + FFT_K2] = -(wgt[:, None] * np.sin(th)).T / n
    k1 = np.arange(FFT_N1)[:, None]
    n1 = np.arange(FFT_N1)[None, :]
    m2 = np.zeros((FFT_K2, 2 * FFT_N1, 2 * FFT_N1))
    for kk in range(FFT_K2):
        phi = 2 * np.pi * ((n1 * k1 * FFT_N2 + n1 * kk) % n) / n
        tr, ti = np.cos(phi), -np.sin(phi)
        m2[kk] = np.block([[tr, -ti], [ti, tr]])
    as_bf16 = lambda a: jnp.asarray(a, dtype=F32).astype(BF16)
    eye = np.eye(FFT_GROUP)
    return (as_bf16(np.kron(f1, eye)), as_bf16(np.kron(g, eye)), as_bf16(m2),
            as_bf16(np.transpose(m2, (0, 2, 1))))


def _fft_s1_kernel(f_ref, z_ref, o_ref):
    _, n2_in, g, ct = z_ref.shape
    z = z_ref[0].reshape(n2_in * g, ct).astype(BF16)
    a = jnp.dot(f_ref[...], z, preferred_element_type=F32)
    o_ref[0] = a.reshape(o_ref.shape[1:]).astype(o_ref.dtype)


def fft_stage1(z4, f1k, *, lead_off, nb, ct=2048):
    _, n2_in, _, C = z4.shape
    g = FFT_GROUP
    return pl.pallas_call(
        _fft_s1_kernel,
        grid=(nb, FFT_N1 // g, C // ct),
        in_specs=[pl.BlockSpec(f1k.shape, lambda b, i, j: (0, 0)),
                  pl.BlockSpec((1, n2_in, g, ct), lambda b, i, j: (lead_off + b, 0, i, j))],
        out_specs=pl.BlockSpec((1, 2, FFT_K2P, g, ct), lambda b, i, j: (b, 0, 0, i, j)),
        out_shape=jax.ShapeDtypeStruct((nb, 2, FFT_K2P, FFT_N1, C), BF16),
        compiler_params=_params("parallel", "parallel", "parallel"),
    )(f1k, z4)


def _fft_s2_kernel(a_ref, m_ref, mt_ref, kf_ref, o_ref):
    k2 = pl.program_id(0)

    @pl.when(k2 < FFT_K2)
    def _():
        ct = a_ref.shape[-1]
        x = jnp.dot(m_ref[0], a_ref[...].reshape(2 * FFT_N1, ct), preferred_element_type=F32)
        xr, xi = x[:FFT_N1], x[FFT_N1:]
        kr, ki = kf_ref[0, 0, 0], kf_ref[0, 1, 0]
        p = jnp.concatenate([xr * kr - xi * ki, xr * ki + xi * kr], axis=0).astype(BF16)
        e = jnp.dot(mt_ref[0], p, preferred_element_type=F32)
        o_ref[...] = e.astype(o_ref.dtype).reshape(o_ref.shape)

    @pl.when(k2 >= FFT_K2)
    def _():
        o_ref[...] = jnp.zeros(o_ref.shape, o_ref.dtype)


def fft_stage2(a5, m2, m2t, kf, order):
    B, _, _, _, D = a5.shape
    last = FFT_K2 - 1
    blk = (1, 2, 1, FFT_N1, D)
    return pl.pallas_call(
        _fft_s2_kernel,
        grid=(FFT_K2P, B),
        in_specs=[pl.BlockSpec(blk, lambda k, b: (b, 0, jnp.minimum(k, last), 0, 0)),
                  pl.BlockSpec((1, 2 * FFT_N1, 2 * FFT_N1), lambda k, b: (jnp.minimum(k, last), 0, 0)),
                  pl.BlockSpec((1, 2 * FFT_N1, 2 * FFT_N1), lambda k, b: (jnp.minimum(k, last), 0, 0)),
                  pl.BlockSpec(blk, lambda k, b: (order, 0, jnp.minimum(k, last), 0, 0))],
        out_specs=pl.BlockSpec(blk, lambda k, b: (b, 0, k, 0, 0)),
        out_shape=jax.ShapeDtypeStruct(a5.shape, BF16),
        compiler_params=_params("parallel", "arbitrary"),
    )(a5, m2, m2t, kf)


def _fft_f2_kernel(f_ref, g_ref, m_ref, o_ref):
    k2 = pl.program_id(0)

    @pl.when(k2 < FFT_K2)
    def _():
        ct = f_ref.shape[-1]
        xf = jnp.dot(m_ref[0], f_ref[...].reshape(2 * FFT_N1, ct), preferred_element_type=F32)
        xg = jnp.dot(m_ref[0], g_ref[...].reshape(2 * FFT_N1, ct), preferred_element_type=F32)
        o_ref[0, 0, 0] = xf[:FFT_N1] + xg[:FFT_N1]
        o_ref[0, 1, 0] = xf[FFT_N1:] - xg[FFT_N1:]

    @pl.when(k2 >= FFT_K2)
    def _():
        o_ref[...] = jnp.zeros(o_ref.shape, o_ref.dtype)


def fft_filter_spectrum(a4, m2, D):
    last = FFT_K2 - 1
    blk = (2, 1, FFT_N1, D)
    return pl.pallas_call(
        _fft_f2_kernel,
        grid=(FFT_K2P, HYENA_ORDER),
        in_specs=[pl.BlockSpec(blk, lambda k, o: (0, jnp.minimum(k, last), 0, 2 * o)),
                  pl.BlockSpec(blk, lambda k, o: (0, jnp.minimum(k, last), 0, 2 * o + 1)),
                  pl.BlockSpec((1, 2 * FFT_N1, 2 * FFT_N1), lambda k, o: (jnp.minimum(k, last), 0, 0))],
        out_specs=pl.BlockSpec((1, 2, 1, FFT_N1, D), lambda k, o: (o, 0, k, 0, 0)),
        out_shape=jax.ShapeDtypeStruct((HYENA_ORDER, 2, FFT_K2P, FFT_N1, D), F32),
        compiler_params=_params("arbitrary", "parallel"),
    )(a4, a4, m2)


def _fft_s3_kernel(g_ref, e_ref, z_ref, x_ref, skip_ref, o_ref):
    _, n2_in, g, ct = z_ref.shape
    e = e_ref[0].reshape(2 * FFT_K2P * g, ct)
    y = jnp.dot(g_ref[...], e, preferred_element_type=F32).reshape(n2_in, g, ct)
    o_ref[0] = (x_ref[0] * (y + skip_ref[...] * z_ref[0])).astype(o_ref.dtype)


def fft_stage3(e5, gk, z4, z_off, x4, x_off, skip, *, out_dtype):
    B, _, _, _, D = e5.shape
    n2_in = z4.shape[1]
    g = FFT_GROUP
    return pl.pallas_call(
        _fft_s3_kernel,
        grid=(B, FFT_N1 // g),
        in_specs=[pl.BlockSpec(gk.shape, lambda b, i: (0, 0)),
                  pl.BlockSpec((1, 2, FFT_K2P, g, D), lambda b, i: (b, 0, 0, i, 0)),
                  pl.BlockSpec((1, n2_in, g, D), lambda b, i: (z_off + b, 0, i, 0)),
                  pl.BlockSpec((1, n2_in, g, D), lambda b, i: (x_off + b, 0, i, 0)),
                  pl.BlockSpec((1, D), lambda b, i: (0, 0))],
        out_specs=pl.BlockSpec((1, n2_in, g, D), lambda b, i: (b, 0, i, 0)),
        out_shape=jax.ShapeDtypeStruct((B, n2_in, FFT_N1, D), out_dtype),
        compiler_params=_params("parallel", "parallel"),
    )(gk, e5, z4, x4, skip.reshape(1, D))


def _hyin_kernel(ap_ref, a_ref, an_ref, b_ref, cw_ref, cb_ref, o_ref, *, tiles_per_seq):
    i = pl.program_id(0)
    b = b_ref[...]
    acc = jnp.dot(a_ref[...], b, preferred_element_type=F32)
    halo = ap_ref.shape[0]
    prev = jnp.dot(ap_ref[...], b, preferred_element_type=F32)[halo - 1:halo]
    nxt = jnp.dot(an_ref[...], b, preferred_element_type=F32)[0:1]
    t = i % tiles_per_seq
    prev = jnp.where(t == 0, 0.0, prev)
    nxt = jnp.where(t == tiles_per_seq - 1, 0.0, nxt)
    tm = acc.shape[0]
    row = lax.broadcasted_iota(jnp.int32, acc.shape, 0)
    up = jnp.where(row == 0, prev, pltpu.roll(acc, 1, axis=0))
    dn = jnp.where(row == tm - 1, nxt, pltpu.roll(acc, tm - 1, axis=0))
    w = cw_ref[...]
    o_ref[0] = up * w[0:1] + acc * w[1:2] + dn * w[2:3] + cb_ref[...]


def hyena_in(h, w_in, conv_w, conv_b, L, *, tm=512, tn=1024, halo=16):
    M, K = h.shape
    N = w_in.shape[1]
    D = N // 3
    per = D // tn
    return pl.pallas_call(
        functools.partial(_hyin_kernel, tiles_per_seq=L // tm),
        grid=(M // tm, N // tn),
        in_specs=[pl.BlockSpec((halo, K), lambda i, j: (jnp.maximum(i * (tm // halo) - 1, 0), 0)),
                  pl.BlockSpec((tm, K), lambda i, j: (i, 0)),
                  pl.BlockSpec((halo, K), lambda i, j: (jnp.minimum((i + 1) * (tm // halo), M // halo - 1), 0)),
                  pl.BlockSpec((K, tn), lambda i, j: (0, j)),
                  pl.BlockSpec((3, tn), lambda i, j: (0, j)),
                  pl.BlockSpec((1, tn), lambda i, j: (0, j))],
        out_specs=pl.BlockSpec((1, tm, tn), lambda i, j: (j // per, i, j % per)),
        out_shape=jax.ShapeDtypeStruct((3, M, D), F32),
        compiler_params=_params("parallel", "parallel"),
    )(h, h, h, w_in, conv_w, conv_b.reshape(1, N))


def _hyena_filter_taps(L, w1, b1, w2, b2, w3, b3, freq):
    D = D_MODEL
    t = jnp.arange(L, dtype=F32) / L
    w = 2 * math.pi * jnp.arange(L, dtype=F32) / L
    bands = jnp.linspace(1e-4, HYENA_BANDS - 1, HYENA_BANDS, dtype=F32)
    ang = w[:, None] * bands[None, :]
    z = jnp.concatenate([t[:, None], jnp.cos(ang), -jnp.sin(ang)], axis=-1)
    a = jnp.sin(freq * (z @ w1 + b1))
    a = jnp.sin(freq * (a @ w2 + b2))
    hf = (matmul(a, w3, tn=1024) + b3).reshape(L, HYENA_ORDER, 2, D)
    max_decay = math.log(HYENA_TARGET) / HYENA_FAST_PCT
    min_decay = math.log(HYENA_TARGET) / HYENA_SLOW_PCT
    deltas = jnp.linspace(min_decay, max_decay, D, dtype=F32)
    hf = hf * jnp.exp(-t[:, None] * jnp.abs(deltas))[:, None, None, :]
    lag0_bwd = (jnp.arange(L)[:, None, None, None] == 0) & (jnp.arange(2)[None, None, :, None] == 1)
    hf = jnp.where(lag0_bwd, 0.0, hf)
    hf = hf * lax.rsqrt(jnp.sum(hf * hf, axis=(0, 2), keepdims=True) + NORM_EPS)
    return hf.reshape(L, HYENA_ORDER * 2 * D)


def _hyena_layer(h, B, L, w_in, conv_w, conv_b, f_w1, f_b1, f_w2, f_b2, f_w3, f_b3, f_freq, skip):
    D = D_MODEL
    n2_in = L // FFT_N1
    f1k, gk, m2, m2t = _fft_tables(L)
    taps = _hyena_filter_taps(L, f_w1, f_b1, f_w2, f_b2, f_w3, f_b3, f_freq)
    af = fft_stage1(taps.reshape(1, n2_in, FFT_N1, taps.shape[1]), f1k, lead_off=0, nb=1)
    kf = fft_filter_spectrum(af[0], m2, D)
    u4 = hyena_in(h, w_in.astype(BF16), conv_w, conv_b, L).reshape(3 * B, n2_in, FFT_N1, D)
    z4 = u4
    for order in range(HYENA_ORDER):
        a = fft_stage1(z4, f1k, lead_off=0, nb=B)
        e = fft_stage2(a, m2, m2t, kf, order)
        last = order == HYENA_ORDER - 1
        z4 = fft_stage3(e, gk, z4, 0, u4, (order + 1) * B, skip[order], out_dtype=BF16 if last else F32)
    return z4.reshape(B * L, D)


def _moe_layer(hm, B, T, w_router, w_gate, w_up, w_down, layer):
    D = D_MODEL
    cap = EC_CAPACITY_FACTOR * T // N_EXPERTS
    hm3 = hm.reshape(B, T, D)
    aff = jax.nn.softmax(jnp.dot(hm3, w_router.astype(BF16), preferred_element_type=F32), axis=-1)
    g, idx = lax.top_k(jnp.swapaxes(aff, 1, 2), cap)
    xg = jax.vmap(lambda hb, ib: hb[ib])(hm3, idx)
    y = expert_ffn(xg, g[..., None], w_gate, w_up, w_down, layer)
    out = jax.vmap(lambda ib, yb: jnp.zeros((T, D), F32).at[ib.reshape(-1)].add(yb.reshape(-1, D)))(idx, y)
    return out.reshape(B * T, D)


def kernel(x, c, ctx, c_ctx, ada_w, ada_b, norm_g, final_g, mla_w_in, mla_g_q, mla_g_kv, mla_w_uq, mla_w_ukv, mla_w_o, hy_w_in, hy_conv_w, hy_conv_b, hy_f_w1, hy_f_b1, hy_f_w2, hy_f_b2, hy_f_w3, hy_f_b3, hy_f_freq, hy_skip, hy_w_out, moe_w_router, moe_w_gate, moe_w_up, moe_w_down):
    B, L, D = x.shape
    Lc = ctx.shape[1]
    x2 = x.reshape(B * L, D)
    ctx2 = ctx.reshape(B * Lc, D)

    mod3 = _ada(jnp.concatenate([c, c_ctx[None]], axis=0), ada_w, ada_b, 0)
    mod, mod_c = mod3[:B], jnp.broadcast_to(mod3[B:], (B, 6, D))
    h = norm_mod(x2, norm_g[0, 0], mod[:, 0:1], mod[:, 1:2], rows_per_batch=L)
    h_c = norm_mod(ctx2, norm_g[0, 0], mod_c[:, 0:1], mod_c[:, 1:2], rows_per_batch=Lc)
    o = _mla_layer(h, h_c, B, L, Lc, mla_w_in[0], mla_g_q[0], mla_g_kv[0], mla_w_uq[0], mla_w_ukv[0])
    x2 = matmul(o.reshape(B * L, -1), mla_w_o[0].astype(BF16), tn=1024,
                res=x2, gate=mod[:, 2:3], rows_per_batch=L)
    hm = norm_mod(x2, norm_g[0, 1], mod[:, 3:4], mod[:, 4:5], rows_per_batch=L)
    moe = _moe_layer(hm, B, L, moe_w_router[0], moe_w_gate, moe_w_up, moe_w_down, 0)
    x2 = x2 + jnp.repeat(mod[:, 5], L, axis=0) * moe

    mod = _ada(c, ada_w, ada_b, 1)
    h = norm_mod(x2, norm_g[1, 0], mod[:, 0:1], mod[:, 1:2], rows_per_batch=L)
    z = _hyena_layer(h, B, L, hy_w_in[0], hy_conv_w[0], hy_conv_b[0], hy_f_w1[0], hy_f_b1[0], hy_f_w2[0],
                     hy_f_b2[0], hy_f_w3[0], hy_f_b3[0], hy_f_freq[0], hy_skip[0])
    x2 = matmul(z, hy_w_out[0].astype(BF16), tn=1024, res=x2, gate=mod[:, 2:3], rows_per_batch=L)
    hm = norm_mod(x2, norm_g[1, 1], mod[:, 3:4], mod[:, 4:5], rows_per_batch=L)
    moe = _moe_layer(hm, B, L, moe_w_router[1], moe_w_gate, moe_w_up, moe_w_down, 1)
    x2 = x2 + jnp.repeat(mod[:, 5], L, axis=0) * moe

    return norm_mod(x2, final_g, out_dtype=F32).reshape(B, L, D)
```

```python
import functools
import math

import jax
import jax.numpy as jnp
import numpy as np
from jax import lax
from jax.experimental import pallas as pl
from jax.experimental.pallas import tpu as pltpu

F32 = jnp.float32
BF16 = jnp.bfloat16

D_MODEL = 2048
GRID_W = 64
NORM_EPS = 1e-6
MLA_HEADS = 16
MLA_Q_RANK = 512
MLA_KV_RANK = 256
MLA_NOPE_DIM = 128
MLA_ROPE_DIM = 64
MLA_V_DIM = 128
ROPE_THETA = 10000.0
HYENA_ORDER = 2
HYENA_BANDS = 16
HYENA_TARGET = 1e-2
HYENA_FAST_PCT = 0.3
HYENA_SLOW_PCT = 1.5
N_EXPERTS = 16
EC_CAPACITY_FACTOR = 2
FFT_N1 = 128
FFT_N2 = 64
FFT_K2 = FFT_N2 // 2 + 1
FFT_K2P = 40
FFT_GROUP = 16

V7X_VMEM_LIMIT_BYTES = 56 * 1024 * 1024
LANES = 128
QK_PAD = 256


def _params(*sem):
    return pltpu.CompilerParams(dimension_semantics=sem, vmem_limit_bytes=V7X_VMEM_LIMIT_BYTES)


def _mm_kernel(a_ref, b_ref, *rest, b_is_stacked, has_res):
    o_ref = rest[-1]
    b = b_ref[0] if b_is_stacked else b_ref[...]
    acc = jnp.dot(a_ref[...].astype(BF16), b.astype(BF16), preferred_element_type=F32)
    if has_res:
        res_ref, gate_ref = rest[0], rest[1]
        acc = res_ref[...] + gate_ref[0] * acc
    o_ref[...] = acc.astype(o_ref.dtype)


def matmul(a, b, *, b_index=None, out_dtype=F32, tm=512, tn=512, res=None, gate=None, rows_per_batch=None):
    M, K = a.shape
    N = b.shape[-1]
    tm, tn = min(tm, M), min(tn, N)
    assert M % tm == 0 and N % tn == 0, (M, N, tm, tn)
    stacked = b.ndim == 3
    if stacked:
        b_spec = pl.BlockSpec((1, K, tn), lambda i, j: (b_index, 0, j))
    else:
        b_spec = pl.BlockSpec((K, tn), lambda i, j: (0, j))
    in_specs = [pl.BlockSpec((tm, K), lambda i, j: (i, 0)), b_spec]
    args = [a, b]
    if res is not None:
        assert rows_per_batch % tm == 0
        tpb = rows_per_batch // tm
        in_specs += [pl.BlockSpec((tm, tn), lambda i, j: (i, j)),
                     pl.BlockSpec((1, 1, tn), lambda i, j: (i // tpb, 0, j))]
        args += [res, gate]
    return pl.pallas_call(
        functools.partial(_mm_kernel, b_is_stacked=stacked, has_res=res is not None),
        grid=(M // tm, N // tn),
        in_specs=in_specs,
        out_specs=pl.BlockSpec((tm, tn), lambda i, j: (i, j)),
        out_shape=jax.ShapeDtypeStruct((M, N), out_dtype),
        compiler_params=_params("parallel", "parallel"),
    )(*args)


def _norm_kernel(x_ref, g_ref, *rest, modulated):
    o_ref = rest[-1]
    x = x_ref[...]
    y = x * lax.rsqrt(jnp.mean(x * x, axis=-1, keepdims=True) + NORM_EPS) * g_ref[...]
    if modulated:
        shift_ref, scale_ref = rest[0], rest[1]
        y = y * (1 + scale_ref[0]) + shift_ref[0]
    o_ref[...] = y.astype(o_ref.dtype)


def norm_mod(x2, g, shift=None, scale=None, *, rows_per_batch=None, out_dtype=BF16, tm=256):
    M, D = x2.shape
    tm = min(tm, M)
    in_specs = [pl.BlockSpec((tm, D), lambda i: (i, 0)), pl.BlockSpec((1, D), lambda i: (0, 0))]
    args = [x2, g.reshape(1, D)]
    if shift is not None:
        tpb = rows_per_batch // tm
        in_specs += [pl.BlockSpec((1, 1, D), lambda i: (i // tpb, 0, 0))] * 2
        args += [shift, scale]
    return pl.pallas_call(
        functools.partial(_norm_kernel, modulated=shift is not None),
        grid=(M // tm,),
        in_specs=in_specs,
        out_specs=pl.BlockSpec((tm, D), lambda i: (i, 0)),
        out_shape=jax.ShapeDtypeStruct((M, D), out_dtype),
        compiler_params=_params("parallel"),
    )(*args)


def _attn_kernel(q_ref, kt_ref, v_ref, o_ref, *, chunks, c):
    q = q_ref[0, 0]
    m = l = acc = None
    for lo, hi in chunks:
        s = jnp.dot(q, kt_ref[0, 0, :, lo:hi], preferred_element_type=F32)
        m_chunk = jnp.max(s, axis=-1, keepdims=True)
        m_new = m_chunk if m is None else jnp.maximum(m, m_chunk)
        p = jnp.exp2(s * c - m_new * c)
        pv = jnp.dot(p.astype(BF16), v_ref[0, 0, lo:hi, :], preferred_element_type=F32)
        psum = jnp.sum(p, axis=-1, keepdims=True)
        if m is None:
            l, acc = psum, pv
        else:
            alpha = jnp.exp2((m - m_new) * c)
            l = alpha * l + psum
            acc = alpha * acc + pv
        m = m_new
    o_ref[0] = (acc / l).astype(o_ref.dtype)


def attention(q, kt, v, *, scale, tq=512, tk=512):
    B, H, L, _ = q.shape
    K = kt.shape[-1]
    V = v.shape[-1]
    chunks = tuple((lo, min(lo + tk, K)) for lo in range(0, K, tk))
    return pl.pallas_call(
        functools.partial(_attn_kernel, chunks=chunks, c=scale * math.log2(math.e)),
        grid=(B, H, L // tq),
        in_specs=[pl.BlockSpec((1, 1, tq, QK_PAD), lambda b, h, i: (b, h, i, 0)),
                  pl.BlockSpec((1, 1, QK_PAD, K), lambda b, h, i: (b, h, 0, 0)),
                  pl.BlockSpec((1, 1, K, V), lambda b, h, i: (b, h, 0, 0))],
        out_specs=pl.BlockSpec((1, tq, V), lambda b, h, i: (b, i, h)),
        out_shape=jax.ShapeDtypeStruct((B, L, H * V), BF16),
        compiler_params=_params("parallel", "parallel", "arbitrary"),
    )(q, kt, v)


def _norm_route_kernel(x_ref, g_ref, shift_ref, scale_ref, wr_ref, h_ref, aff_ref):
    x = x_ref[...]
    y = x * lax.rsqrt(jnp.mean(x * x, axis=-1, keepdims=True) + NORM_EPS) * g_ref[...]
    y = y * (1 + scale_ref[0]) + shift_ref[0]
    h_ref[...] = y
    logits = lax.dot_general(wr_ref[...], y.astype(BF16), (((1,), (1,)), ((), ())),
                             preferred_element_type=F32)
    p = jnp.exp(logits - jnp.max(logits, axis=0, keepdims=True))
    aff_ref[0] = p / jnp.sum(p, axis=0, keepdims=True)


def norm_route(x2, g, shift, scale, w_router_t, *, rows_per_batch, tm=256):
    M, D = x2.shape
    E = w_router_t.shape[0]
    tpb = rows_per_batch // tm
    return pl.pallas_call(
        _norm_route_kernel,
        grid=(M // tm,),
        in_specs=[pl.BlockSpec((tm, D), lambda i: (i, 0)),
                  pl.BlockSpec((1, D), lambda i: (0, 0)),
                  pl.BlockSpec((1, 1, D), lambda i: (i // tpb, 0, 0)),
                  pl.BlockSpec((1, 1, D), lambda i: (i // tpb, 0, 0)),
                  pl.BlockSpec((E, D), lambda i: (0, 0))],
        out_specs=[pl.BlockSpec((tm, D), lambda i: (i, 0)),
                   pl.BlockSpec((1, E, tm), lambda i: (i // tpb, 0, i % tpb))],
        out_shape=[jax.ShapeDtypeStruct((M, D), F32),
                   jax.ShapeDtypeStruct((M // rows_per_batch, E, rows_per_batch), F32)],
        compiler_params=_params("parallel"),
    )(x2, g.reshape(1, D), shift, scale, w_router_t)


def _lane_prefix_sum(x):
    E, T = x.shape
    r = lax.broadcasted_iota(jnp.int32, (LANES, LANES), 0)
    c = lax.broadcasted_iota(jnp.int32, (LANES, LANES), 1)
    tri = jnp.where(r <= c, 1.0, 0.0).astype(BF16)
    outs, carry = [], jnp.zeros((E, 1), F32)
    for j in range(T // LANES):
        seg = jnp.dot(x[:, j * LANES:(j + 1) * LANES].astype(BF16), tri, preferred_element_type=F32) + carry
        outs.append(seg)
        carry = seg[:, LANES - 1:LANES]
    return jnp.concatenate(outs, axis=1)


def _route_select_kernel(aff_ref, pos_ref, idx_ref, gate_ref, cum_s, *, cap, tch):
    aff = aff_ref[0]
    E, T = aff.shape

    def bisect(state):
        lo, hi, _ = state
        mid = 0.5 * (lo + hi)
        cnt = jnp.sum(jnp.where(aff >= mid, 1.0, 0.0), axis=1, keepdims=True)
        ge = cnt >= cap
        open_rows = jnp.where((mid > lo) & (mid < hi), 1.0, 0.0)
        return jnp.where(ge, mid, lo), jnp.where(ge, hi, mid), jnp.max(open_rows)

    lo0 = jnp.min(aff, axis=1, keepdims=True)
    hi0 = 2.0 * jnp.max(aff, axis=1, keepdims=True) + 1e-30
    lo, hi, _ = lax.while_loop(lambda st: st[2] > 0.0, bisect, (lo0, hi0, jnp.float32(1.0)))
    gt = aff >= hi
    eq = (aff >= lo) & (aff < hi)
    need = cap - jnp.sum(jnp.where(gt, 1.0, 0.0), axis=1, keepdims=True)
    eq_rank = _lane_prefix_sum(jnp.where(eq, 1.0, 0.0))
    sel = gt | (eq & (eq_rank <= need))
    cum = _lane_prefix_sum(jnp.where(sel, 1.0, 0.0))
    cumsel = jnp.where(sel, cum, 0.0)
    pos_ref[0] = (cumsel - 1.0).astype(jnp.int32)
    cum_s[0] = cum
    cum_s[1] = cumsel
    slot = lax.broadcasted_iota(jnp.int32, (cap, 1), 0).astype(F32)

    def per_expert(e, carry):
        idx_acc = jnp.zeros((cap, 1), F32)
        g_acc = jnp.zeros((cap, 1), F32)
        for lo in range(0, T, tch):
            cum_e = cum_s[0, pl.ds(e, 1), lo:lo + tch]
            cs_e = cum_s[1, pl.ds(e, 1), lo:lo + tch]
            aff_e = aff_ref[0, pl.ds(e, 1), lo:lo + tch]
            idx_acc = idx_acc + jnp.sum(jnp.where(cum_e <= slot, 1.0, 0.0), axis=1, keepdims=True)
            g_acc = g_acc + jnp.sum(jnp.where(cs_e == slot + 1.0, aff_e, 0.0), axis=1, keepdims=True)
        idx_ref[0, pl.ds(e, 1)] = idx_acc.astype(jnp.int32).reshape(1, cap, 1)
        gate_ref[0, pl.ds(e, 1)] = g_acc.reshape(1, cap, 1)
        return carry

    lax.fori_loop(0, E, per_expert, 0)


def route_select(aff, cap, *, tch=1024):
    B, E, T = aff.shape
    return pl.pallas_call(
        functools.partial(_route_select_kernel, cap=cap, tch=tch),
        grid=(B,),
        in_specs=[pl.BlockSpec((1, E, T), lambda b: (b, 0, 0))],
        out_specs=[pl.BlockSpec((1, E, T), lambda b: (b, 0, 0)),
                   pl.BlockSpec((1, E, cap, 1), lambda b: (b, 0, 0, 0)),
                   pl.BlockSpec((1, E, cap, 1), lambda b: (b, 0, 0, 0))],
        out_shape=[jax.ShapeDtypeStruct((B, E, T), jnp.int32),
                   jax.ShapeDtypeStruct((B, E, cap, 1), jnp.int32),
                   jax.ShapeDtypeStruct((B, E, cap, 1), F32)],
        scratch_shapes=[pltpu.VMEM((2, E, T), F32)],
        compiler_params=_params("parallel"),
    )(aff)


def _moe_ffn_kernel(idx_ref, idxn_ref, h_hbm, wg_ref, wu_ref, wd_ref, g_ref, o_ref, xland, xbf, acc, sem,
                    *, nf, n_exp, cap, tokens_per_batch):
    e = pl.program_id(0)
    f = pl.program_id(1)
    rows = xland.shape[0]

    def issue_gather(ids_ref):
        def body(i, carry):
            src_row = ids_ref[0, 0, i] + (i // cap) * tokens_per_batch
            pltpu.make_async_copy(h_hbm.at[pl.ds(src_row, 1)], xland.at[pl.ds(i, 1)], sem.at[0]).start()
            return carry
        lax.fori_loop(0, rows, body, 0)

    @pl.when((e == 0) & (f == 0))
    def _():
        issue_gather(idx_ref)

    @pl.when(f == 0)
    def _():
        pltpu.make_async_copy(h_hbm.at[pl.ds(0, rows)], xland, sem.at[0]).wait()
        xbf[...] = xland[...].astype(BF16)

        @pl.when(e + 1 < n_exp)
        def _():
            issue_gather(idxn_ref)

    x = xbf[...]
    a = jnp.dot(x, wg_ref[0, 0].astype(BF16), preferred_element_type=F32)
    u = jnp.dot(x, wu_ref[0, 0].astype(BF16), preferred_element_type=F32)
    h1 = (jax.nn.silu(a) * u).astype(BF16)
    wd = wd_ref[0, 0].astype(BF16)
    half = wd.shape[1] // 2
    for c0 in (0, half):
        y = jnp.dot(h1, wd[:, c0:c0 + half], preferred_element_type=F32)

        @pl.when(f == 0)
        def _():
            acc[:, c0:c0 + half] = y

        @pl.when(f > 0)
        def _():
            acc[:, c0:c0 + half] += y

    @pl.when(f == nf - 1)
    def _():
        o_ref[...] = (acc[...].reshape(o_ref.shape) * g_ref[...]).astype(o_ref.dtype)


def moe_ffn(idx3, hp, g, w_gate, w_up, w_down, layer, T, *, tf=256):
    B, E, C = idx3.shape
    D = hp.shape[1]
    F = w_gate.shape[-1]
    nf = F // tf
    ids = idx3.transpose(1, 0, 2).reshape(E, 1, B * C)
    smem_blk = lambda imap: pl.BlockSpec((1, 1, B * C), imap, memory_space=pltpu.SMEM)
    return pl.pallas_call(
        functools.partial(_moe_ffn_kernel, nf=nf, n_exp=E, cap=C, tokens_per_batch=T),
        grid=(E, nf),
        in_specs=[smem_blk(lambda e, f: (e, 0, 0)),
                  smem_blk(lambda e, f: (jnp.minimum(e + 1, E - 1), 0, 0)),
                  pl.BlockSpec(memory_space=pl.ANY),
                  pl.BlockSpec((1, 1, D, tf), lambda e, f: (layer, e, 0, f)),
                  pl.BlockSpec((1, 1, D, tf), lambda e, f: (layer, e, 0, f)),
                  pl.BlockSpec((1, 1, tf, D), lambda e, f: (layer, e, f, 0)),
                  pl.BlockSpec((B, 1, C, 1), lambda e, f: (0, e, 0, 0))],
        out_specs=pl.BlockSpec((B, 1, C, D), lambda e, f: (0, e, 0, 0)),
        out_shape=jax.ShapeDtypeStruct((B, E, C, D), BF16),
        scratch_shapes=[pltpu.VMEM((B * C, D), F32),
                        pltpu.VMEM((B * C, D), BF16),
                        pltpu.VMEM((B * C, D), F32),
                        pltpu.SemaphoreType.DMA((1,))],
        compiler_params=_params("arbitrary", "arbitrary"),
    )(ids, ids, hp, w_gate, w_up, w_down, g)


def _moe_combine_kernel(starts_ref, ntr_ref, y_hbm, post_ref, x_ref, gate_ref, o_ref, ybuf, yext, acc, sem, sem_ext,
                        *, n_exp, cap, win, steps_per_batch):
    s = pl.program_id(0)
    slot = s % 2

    def window(step, e, k):
        nominal = starts_ref[step * n_exp + e] + k * win
        return nominal, pl.multiple_of(jnp.minimum(nominal, cap - win), 16)

    def window_copy(step, e, k, dst, dsem):
        _, actual = window(step, e, k)
        return pltpu.make_async_copy(y_hbm.at[step // steps_per_batch, e, pl.ds(actual, win)],
                                     dst.at[pl.ds(e * win, win)], dsem)

    def onehot(step, k):
        post = post_ref[0]
        lane = lax.broadcasted_iota(jnp.int32, (1, win), 1)
        cols = []
        for e in range(n_exp):
            nominal, actual = window(step, e, k)
            slots = actual + lane
            slots = jnp.where(slots >= nominal, slots, -2)
            cols.append(jnp.where(post[:, e:e + 1] == slots, 1.0, 0.0).astype(BF16))
        return jnp.concatenate(cols, axis=1)

    @pl.when(s == 0)
    def _():
        for e in range(n_exp):
            window_copy(0, e, 0, ybuf.at[0], sem.at[0]).start()

    for e in range(n_exp):
        window_copy(s, e, 0, ybuf.at[slot], sem.at[slot]).wait()

    @pl.when(s + 1 < pl.num_programs(0))
    def _():
        for e in range(n_exp):
            window_copy(s + 1, e, 0, ybuf.at[1 - slot], sem.at[1 - slot]).start()

    acc[...] = jnp.dot(onehot(s, 0), ybuf[slot], preferred_element_type=F32)

    def extra(k, carry):
        for e in range(n_exp):
            window_copy(s, e, k, yext, sem_ext.at[0]).start()
        for e in range(n_exp):
            window_copy(s, e, k, yext, sem_ext.at[0]).wait()
        acc[...] += jnp.dot(onehot(s, k), yext[...], preferred_element_type=F32)
        return carry

    lax.fori_loop(1, ntr_ref[s], extra, 0)
    o_ref[...] = x_ref[...] + gate_ref[0] * acc[...]


def moe_combine(starts, ntr, y, post, x2, gate, *, tj=256, win=128):
    B, E, C, D = y.shape
    M = x2.shape[0]
    T = M // B
    spb = T // tj
    grid_spec = pltpu.PrefetchScalarGridSpec(
        num_scalar_prefetch=2,
        grid=(M // tj,),
        in_specs=[pl.BlockSpec(memory_space=pl.ANY),
                  pl.BlockSpec((1, tj, E), lambda s, *_: (s // spb, s % spb, 0)),
                  pl.BlockSpec((tj, D), lambda s, *_: (s, 0)),
                  pl.BlockSpec((1, 1, D), lambda s, *_: (s // spb, 0, 0))],
        out_specs=pl.BlockSpec((tj, D), lambda s, *_: (s, 0)),
        scratch_shapes=[pltpu.VMEM((2, E * win, D), BF16),
                        pltpu.VMEM((E * win, D), BF16),
                        pltpu.VMEM((tj, D), F32),
                        pltpu.SemaphoreType.DMA((2,)),
                        pltpu.SemaphoreType.DMA((1,))],
    )
    return pl.pallas_call(
        functools.partial(_moe_combine_kernel, n_exp=E, cap=C, win=win, steps_per_batch=spb),
        grid_spec=grid_spec,
        out_shape=jax.ShapeDtypeStruct((M, D), F32),
        compiler_params=_params("arbitrary"),
    )(starts, ntr, y, post, x2, gate)


def _small_rmsnorm(x, g):
    return x * lax.rsqrt(jnp.mean(x * x, axis=-1, keepdims=True) + NORM_EPS) * g


def _rope_tables(rows):
    row = jnp.broadcast_to(jnp.arange(rows)[:, None], (rows, GRID_W)).reshape(-1).astype(F32)
    col = jnp.broadcast_to(jnp.arange(GRID_W)[None, :], (rows, GRID_W)).reshape(-1).astype(F32)
    n_freq = MLA_ROPE_DIM // 4
    inv = ROPE_THETA ** (-jnp.arange(n_freq, dtype=F32) / n_freq)
    ang = jnp.concatenate([row[:, None] * inv, col[:, None] * inv], axis=-1)
    return jnp.cos(ang), jnp.sin(ang)


def _rope(x, cos, sin):
    half = MLA_ROPE_DIM // 2
    x1, x2 = x[..., :half], x[..., half:]
    return jnp.concatenate([x1 * cos - x2 * sin, x2 * cos + x1 * sin], axis=-1)


def _pad_cols(w, n):
    return jnp.concatenate([w, jnp.zeros((w.shape[0], n - w.shape[1]), w.dtype)], axis=1)


def _ada(cvec3, ada_w, ada_b, layer):
    R = cvec3.shape[0]
    a = jnp.concatenate([jax.nn.silu(cvec3), jnp.zeros((8 - R, cvec3.shape[1]), F32)], axis=0)
    m = matmul(a, ada_w, b_index=layer, tm=8, tn=512)[:R] + ada_b[layer]
    return m.reshape(R, 6, D_MODEL)


def _mla_layer(h, h_c, B, L, Lc, w_in, g_q, g_kv, w_uq, w_ukv):
    H = MLA_HEADS
    in_dim = w_in.shape[1]
    proj = matmul(h, _pad_cols(w_in, 896).astype(BF16), tn=896)[:, :in_dim]
    rest_c = matmul(h_c, _pad_cols(w_in[:, MLA_Q_RANK:], 384).astype(BF16), tn=384)[:, :in_dim - MLA_Q_RANK]
    cq = proj[:, :MLA_Q_RANK]
    ckv = jnp.concatenate([proj[:, MLA_Q_RANK:MLA_Q_RANK + MLA_KV_RANK].reshape(B, L, -1),
                           rest_c[:, :MLA_KV_RANK].reshape(B, Lc, -1)], axis=1)
    K = L + Lc
    cos, sin = _rope_tables(L // GRID_W)
    kr = _rope(proj[:, MLA_Q_RANK + MLA_KV_RANK:].reshape(B, L, -1), cos, sin)
    kr = jnp.concatenate([kr, rest_c[:, MLA_KV_RANK:].reshape(B, Lc, -1)], axis=1)
    w_ukv_b = w_ukv.astype(BF16)
    kv = matmul(_small_rmsnorm(ckv, g_kv).astype(BF16).reshape(B * K, -1), w_ukv_b, tm=544, tn=1024)
    kv = kv.reshape(B, K, H, MLA_NOPE_DIM + MLA_V_DIM)
    q = matmul(_small_rmsnorm(cq, g_q).astype(BF16), w_uq.astype(BF16), tn=1024)
    q = q.reshape(B, L, H, MLA_NOPE_DIM + MLA_ROPE_DIM)
    qr = _rope(q[..., MLA_NOPE_DIM:], cos[:, None, :], sin[:, None, :])
    pad = QK_PAD - MLA_NOPE_DIM - MLA_ROPE_DIM
    qp = jnp.concatenate([q[..., :MLA_NOPE_DIM], qr, jnp.zeros((B, L, H, pad), F32)], axis=-1)
    qp = qp.astype(BF16).transpose(0, 2, 1, 3)
    kp = jnp.concatenate([kv[..., :MLA_NOPE_DIM],
                          jnp.broadcast_to(kr[:, :, None, :], (B, K, H, MLA_ROPE_DIM)),
                          jnp.zeros((B, K, H, pad), F32)], axis=-1)
    kt = kp.astype(BF16).transpose(0, 2, 3, 1)
    v = kv[..., MLA_NOPE_DIM:].astype(BF16).transpose(0, 2, 1, 3)
    scale = (MLA_NOPE_DIM + MLA_ROPE_DIM) ** -0.5
    return attention(qp, kt, v, scale=scale)


def _fft_tables(L):
    n = 2 * L
    n2_in = L // FFT_N1
    k2 = np.arange(FFT_K2)
    th = 2 * np.pi * ((k2[:, None] * np.arange(n2_in)[None, :]) % FFT_N2) / FFT_N2
    f1 = np.zeros((2 * FFT_K2P, n2_in))
    f1[:FFT_K2] = np.cos(th)
    f1[FFT_K2P:FFT_K2P + FFT_K2] = -np.sin(th)
    wgt = np.where((k2 == 0) | (k2 == FFT_N2 // 2), 1.0, 2.0)
    g = np.zeros((n2_in, 2 * FFT_K2P))
    g[:, :FFT_K2] = (wgt[:, None] * np.cos(th)).T / n
    g[:, FFT_K2P:FFT_K2P + FFT_K2] = -(wgt[:, None] * np.sin(th)).T / n
    k1 = np.arange(FFT_N1)[:, None]
    n1 = np.arange(FFT_N1)[None, :]
    m2 = np.zeros((FFT_K2, 2 * FFT_N1, 2 * FFT_N1))
    for kk in range(FFT_K2):
        phi = 2 * np.pi * ((n1 * k1 * FFT_N2 + n1 * kk) % n) / n
        tr, ti = np.cos(phi), -np.sin(phi)
        m2[kk] = np.block([[tr, -ti], [ti, tr]])
    as_bf16 = lambda a: jnp.asarray(a, dtype=F32).astype(BF16)
    eye = np.eye(FFT_GROUP)
    return (as_bf16(np.kron(f1, eye)), as_bf16(np.kron(g, eye)), as_bf16(m2),
            as_bf16(np.transpose(m2, (0, 2, 1))))


def _fft_s1_kernel(f_ref, z_ref, o_ref):
    _, n2_in, g, ct = z_ref.shape
    z = z_ref[0].reshape(n2_in * g, ct).astype(BF16)
    a = jnp.dot(f_ref[...], z, preferred_element_type=F32)
    o_ref[0] = a.reshape(o_ref.shape[1:]).astype(o_ref.dtype)


def fft_stage1(z4, f1k, *, lead_off, nb, ct=2048):
    _, n2_in, _, C = z4.shape
    g = FFT_GROUP
    return pl.pallas_call(
        _fft_s1_kernel,
        grid=(nb, FFT_N1 // g, C // ct),
        in_specs=[pl.BlockSpec(f1k.shape, lambda b, i, j: (0, 0)),
                  pl.BlockSpec((1, n2_in, g, ct), lambda b, i, j: (lead_off + b, 0, i, j))],
        out_specs=pl.BlockSpec((1, 2, FFT_K2P, g, ct), lambda b, i, j: (b, 0, 0, i, j)),
        out_shape=jax.ShapeDtypeStruct((nb, 2, FFT_K2P, FFT_N1, C), BF16),
        compiler_params=_params("parallel", "parallel", "parallel"),
    )(f1k, z4)


def _fft_s2_kernel(a_ref, m_ref, mt_ref, kf_ref, o_ref):
    k2 = pl.program_id(0)

    @pl.when(k2 < FFT_K2)
    def _():
        ct = a_ref.shape[-1]
        x = jnp.dot(m_ref[0], a_ref[...].reshape(2 * FFT_N1, ct), preferred_element_type=F32)
        xr, xi = x[:FFT_N1], x[FFT_N1:]
        kr, ki = kf_ref[0, 0, 0], kf_ref[0, 1, 0]
        p = jnp.concatenate([xr * kr - xi * ki, xr * ki + xi * kr], axis=0).astype(BF16)
        e = jnp.dot(mt_ref[0], p, preferred_element_type=F32)
        o_ref[...] = e.astype(o_ref.dtype).reshape(o_ref.shape)

    @pl.when(k2 >= FFT_K2)
    def _():
        o_ref[...] = jnp.zeros(o_ref.shape, o_ref.dtype)


def fft_stage2(a5, m2, m2t, kf, order):
    B, _, _, _, D = a5.shape
    last = FFT_K2 - 1
    blk = (1, 2, 1, FFT_N1, D)
    return pl.pallas_call(
        _fft_s2_kernel,
        grid=(FFT_K2P, B),
        in_specs=[pl.BlockSpec(blk, lambda k, b: (b, 0, jnp.minimum(k, last), 0, 0)),
                  pl.BlockSpec((1, 2 * FFT_N1, 2 * FFT_N1), lambda k, b: (jnp.minimum(k, last), 0, 0)),
                  pl.BlockSpec((1, 2 * FFT_N1, 2 * FFT_N1), lambda k, b: (jnp.minimum(k, last), 0, 0)),
                  pl.BlockSpec(blk, lambda k, b: (order, 0, jnp.minimum(k, last), 0, 0))],
        out_specs=pl.BlockSpec(blk, lambda k, b: (b, 0, k, 0, 0)),
        out_shape=jax.ShapeDtypeStruct(a5.shape, BF16),
        compiler_params=_params("parallel", "arbitrary"),
    )(a5, m2, m2t, kf)


def _fft_f2_kernel(f_ref, g_ref, m_ref, o_ref):
    k2 = pl.program_id(0)

    @pl.when(k2 < FFT_K2)
    def _():
        ct = f_ref.shape[-1]
        xf = jnp.dot(m_ref[0], f_ref[...].reshape(2 * FFT_N1, ct), preferred_element_type=F32)
        xg = jnp.dot(m_ref[0], g_ref[...].reshape(2 * FFT_N1, ct), preferred_element_type=F32)
        o_ref[0, 0, 0] = xf[:FFT_N1] + xg[:FFT_N1]
        o_ref[0, 1, 0] = xf[FFT_N1:] - xg[FFT_N1:]

    @pl.when(k2 >= FFT_K2)
    def _():
        o_ref[...] = jnp.zeros(o_ref.shape, o_ref.dtype)


def fft_filter_spectrum(a4, m2, D):
    last = FFT_K2 - 1
    blk = (2, 1, FFT_N1, D)
    return pl.pallas_call(
        _fft_f2_kernel,
        grid=(FFT_K2P, HYENA_ORDER),
        in_specs=[pl.BlockSpec(blk, lambda k, o: (0, jnp.minimum(k, last), 0, 2 * o)),
                  pl.BlockSpec(blk, lambda k, o: (0, jnp.minimum(k, last), 0, 2 * o + 1)),
                  pl.BlockSpec((1, 2 * FFT_N1, 2 * FFT_N1), lambda k, o: (jnp.minimum(k, last), 0, 0))],
        out_specs=pl.BlockSpec((1, 2, 1, FFT_N1, D), lambda k, o: (o, 0, k, 0, 0)),
        out_shape=jax.ShapeDtypeStruct((HYENA_ORDER, 2, FFT_K2P, FFT_N1, D), F32),
        compiler_params=_params("arbitrary", "parallel"),
    )(a4, a4, m2)


def _fft_s3_kernel(g_ref, e_ref, z_ref, x_ref, skip_ref, o_ref):
    _, n2_in, g, ct = z_ref.shape
    e = e_ref[0].reshape(2 * FFT_K2P * g, ct)
    y = jnp.dot(g_ref[...], e, preferred_element_type=F32).reshape(n2_in, g, ct)
    o_ref[0] = (x_ref[0] * (y + skip_ref[...] * z_ref[0])).astype(o_ref.dtype)


def fft_stage3(e5, gk, z4, z_off, x4, x_off, skip, *, out_dtype):
    B, _, _, _, D = e5.shape
    n2_in = z4.shape[1]
    g = FFT_GROUP
    return pl.pallas_call(
        _fft_s3_kernel,
        grid=(B, FFT_N1 // g),
        in_specs=[pl.BlockSpec(gk.shape, lambda b, i: (0, 0)),
                  pl.BlockSpec((1, 2, FFT_K2P, g, D), lambda b, i: (b, 0, 0, i, 0)),
                  pl.BlockSpec((1, n2_in, g, D), lambda b, i: (z_off + b, 0, i, 0)),
                  pl.BlockSpec((1, n2_in, g, D), lambda b, i: (x_off + b, 0, i, 0)),
                  pl.BlockSpec((1, D), lambda b, i: (0, 0))],
        out_specs=pl.BlockSpec((1, n2_in, g, D), lambda b, i: (b, 0, i, 0)),
        out_shape=jax.ShapeDtypeStruct((B, n2_in, FFT_N1, D), out_dtype),
        compiler_params=_params("parallel", "parallel"),
    )(gk, e5, z4, x4, skip.reshape(1, D))


def _hyin_kernel(ap_ref, a_ref, an_ref, b_ref, cw_ref, cb_ref, o_ref, *, tiles_per_seq):
    i = pl.program_id(0)
    b = b_ref[...]
    acc = jnp.dot(a_ref[...], b, preferred_element_type=F32)
    halo = ap_ref.shape[0]
    prev = jnp.dot(ap_ref[...], b, preferred_element_type=F32)[halo - 1:halo]
    nxt = jnp.dot(an_ref[...], b, preferred_element_type=F32)[0:1]
    t = i % tiles_per_seq
    prev = jnp.where(t == 0, 0.0, prev)
    nxt = jnp.where(t == tiles_per_seq - 1, 0.0, nxt)
    tm = acc.shape[0]
    row = lax.broadcasted_iota(jnp.int32, acc.shape, 0)
    up = jnp.where(row == 0, prev, pltpu.roll(acc, 1, axis=0))
    dn = jnp.where(row == tm - 1, nxt, pltpu.roll(acc, tm - 1, axis=0))
    w = cw_ref[...]
    o_ref[0] = up * w[0:1] + acc * w[1:2] + dn * w[2:3] + cb_ref[...]


def hyena_in(h, w_in, conv_w, conv_b, L, *, tm=512, tn=1024, halo=16):
    M, K = h.shape
    N = w_in.shape[1]
    D = N // 3
    per = D // tn
    return pl.pallas_call(
        functools.partial(_hyin_kernel, tiles_per_seq=L // tm),
        grid=(M // tm, N // tn),
        in_specs=[pl.BlockSpec((halo, K), lambda i, j: (jnp.maximum(i * (tm // halo) - 1, 0), 0)),
                  pl.BlockSpec((tm, K), lambda i, j: (i, 0)),
                  pl.BlockSpec((halo, K), lambda i, j: (jnp.minimum((i + 1) * (tm // halo), M // halo - 1), 0)),
                  pl.BlockSpec((K, tn), lambda i, j: (0, j)),
                  pl.BlockSpec((3, tn), lambda i, j: (0, j)),
                  pl.BlockSpec((1, tn), lambda i, j: (0, j))],
        out_specs=pl.BlockSpec((1, tm, tn), lambda i, j: (j // per, i, j % per)),
        out_shape=jax.ShapeDtypeStruct((3, M, D), F32),
        compiler_params=_params("parallel", "parallel"),
    )(h, h, h, w_in, conv_w, conv_b.reshape(1, N))


def _hyena_filter_taps(L, w1, b1, w2, b2, w3, b3, freq):
    D = D_MODEL
    t = jnp.arange(L, dtype=F32) / L
    w = 2 * math.pi * jnp.arange(L, dtype=F32) / L
    bands = jnp.linspace(1e-4, HYENA_BANDS - 1, HYENA_BANDS, dtype=F32)
    ang = w[:, None] * bands[None, :]
    z = jnp.concatenate([t[:, None], jnp.cos(ang), -jnp.sin(ang)], axis=-1)
    a = jnp.sin(freq * (z @ w1 + b1))
    a = jnp.sin(freq * (a @ w2 + b2))
    hf = matmul(a, w3, tn=1024) + b3
    max_decay = math.log(HYENA_TARGET) / HYENA_FAST_PCT
    min_decay = math.log(HYENA_TARGET) / HYENA_SLOW_PCT
    deltas = jnp.linspace(min_decay, max_decay, D, dtype=F32)
    hf = hf * jnp.tile(jnp.exp(-t[:, None] * jnp.abs(deltas)), (1, 2 * HYENA_ORDER))
    is_bwd = (jnp.arange(2 * HYENA_ORDER * D) // D) % 2 == 1
    hf = jnp.where((jnp.arange(L)[:, None] == 0) & is_bwd[None, :], 0.0, hf)
    ss = jnp.sum(hf * hf, axis=0).reshape(HYENA_ORDER, 2, D).sum(axis=1, keepdims=True)
    inv = jnp.broadcast_to(lax.rsqrt(ss + NORM_EPS), (HYENA_ORDER, 2, D)).reshape(1, -1)
    return hf * inv


def _hyena_layer(h, B, L, w_in, conv_w, conv_b, f_w1, f_b1, f_w2, f_b2, f_w3, f_b3, f_freq, skip):
    D = D_MODEL
    n2_in = L // FFT_N1
    f1k, gk, m2, m2t = _fft_tables(L)
    taps = _hyena_filter_taps(L, f_w1, f_b1, f_w2, f_b2, f_w3, f_b3, f_freq)
    af = fft_stage1(taps.reshape(1, n2_in, FFT_N1, taps.shape[1]), f1k, lead_off=0, nb=1)
    kf = fft_filter_spectrum(af[0], m2, D)
    u4 = hyena_in(h, w_in.astype(BF16), conv_w, conv_b, L).reshape(3 * B, n2_in, FFT_N1, D)
    z4 = u4
    for order in range(HYENA_ORDER):
        a = fft_stage1(z4, f1k, lead_off=0, nb=B)
        e = fft_stage2(a, m2, m2t, kf, order)
        last = order == HYENA_ORDER - 1
        z4 = fft_stage3(e, gk, z4, 0, u4, (order + 1) * B, skip[order], out_dtype=BF16 if last else F32)
    return z4.reshape(B * L, D)


def _moe_layer(x2, norm_gain, mod, B, T, w_router, w_gate, w_up, w_down, layer, *, tj=256, win=128):
    E = N_EXPERTS
    cap = EC_CAPACITY_FACTOR * T // E
    hp, aff = norm_route(x2, norm_gain, mod[:, 3:4], mod[:, 4:5], w_router.T.astype(BF16), rows_per_batch=T)
    pos, idx, g = route_select(aff, cap)
    idx3 = idx.reshape(B, E, cap)
    y = moe_ffn(idx3, hp, g, w_gate, w_up, w_down, layer, T)
    nj = T // tj
    edges = jnp.arange(nj + 1, dtype=jnp.int32) * tj
    first = jnp.sum((idx3[..., None] < edges).astype(jnp.int32), axis=2)
    nominal = (first[..., :nj] // 16) * 16
    ntr = jnp.maximum(jnp.max((first[..., 1:] - nominal + win - 1) // win, axis=1), 1)
    starts = nominal.transpose(0, 2, 1).reshape(-1)
    return moe_combine(starts, ntr.reshape(-1), y, pos.transpose(0, 2, 1), x2, mod[:, 5:6], tj=tj, win=win)


def kernel(x, c, ctx, c_ctx, ada_w, ada_b, norm_g, final_g, mla_w_in, mla_g_q, mla_g_kv, mla_w_uq, mla_w_ukv, mla_w_o, hy_w_in, hy_conv_w, hy_conv_b, hy_f_w1, hy_f_b1, hy_f_w2, hy_f_b2, hy_f_w3, hy_f_b3, hy_f_freq, hy_skip, hy_w_out, moe_w_router, moe_w_gate, moe_w_up, moe_w_down):
    B, L, D = x.shape
    Lc = ctx.shape[1]
    x2 = x.reshape(B * L, D)
    ctx2 = ctx.reshape(B * Lc, D)

    mod3 = _ada(jnp.concatenate([c, c_ctx[None]], axis=0), ada_w, ada_b, 0)
    mod, mod_c = mod3[:B], jnp.broadcast_to(mod3[B:], (B, 6, D))
    h = norm_mod(x2, norm_g[0, 0], mod[:, 0:1], mod[:, 1:2], rows_per_batch=L)
    h_c = norm_mod(ctx2, norm_g[0, 0], mod_c[:, 0:1], mod_c[:, 1:2], rows_per_batch=Lc)
    o = _mla_layer(h, h_c, B, L, Lc, mla_w_in[0], mla_g_q[0], mla_g_kv[0], mla_w_uq[0], mla_w_ukv[0])
    x2 = matmul(o.reshape(B * L, -1), mla_w_o[0].astype(BF16), tn=1024,
                res=x2, gate=mod[:, 2:3], rows_per_batch=L)
    x2 = _moe_layer(x2, norm_g[0, 1], mod, B, L, moe_w_router[0], moe_w_gate, moe_w_up, moe_w_down, 0)

    mod = _ada(c, ada_w, ada_b, 1)
    h = norm_mod(x2, norm_g[1, 0], mod[:, 0:1], mod[:, 1:2], rows_per_batch=L)
    z = _hyena_layer(h, B, L, hy_w_in[0], hy_conv_w[0], hy_conv_b[0], hy_f_w1[0], hy_f_b1[0], hy_f_w2[0],
                     hy_f_b2[0], hy_f_w3[0], hy_f_b3[0], hy_f_freq[0], hy_skip[0])
    x2 = matmul(z, hy_w_out[0].astype(BF16), tn=1024, res=x2, gate=mod[:, 2:3], rows_per_batch=L)
    x2 = _moe_layer(x2, norm_g[1, 1], mod, B, L, moe_w_router[1], moe_w_gate, moe_w_up, moe_w_down, 1)

    return norm_mod(x2, final_g, out_dtype=F32).reshape(B, L, D)
```

```python
import functools
import math

import jax
import jax.numpy as jnp
import numpy as np
from jax import lax
from jax.experimental import pallas as pl
from jax.experimental.pallas import tpu as pltpu

F32 = jnp.float32
BF16 = jnp.bfloat16

D_MODEL = 2048
GRID_W = 64
NORM_EPS = 1e-6
MLA_HEADS = 16
MLA_Q_RANK = 512
MLA_KV_RANK = 256
MLA_NOPE_DIM = 128
MLA_ROPE_DIM = 64
MLA_V_DIM = 128
ROPE_THETA = 10000.0
HYENA_ORDER = 2
HYENA_BANDS = 16
HYENA_TARGET = 1e-2
HYENA_FAST_PCT = 0.3
HYENA_SLOW_PCT = 1.5
N_EXPERTS = 16
EC_CAPACITY_FACTOR = 2
FFT_N1 = 128
FFT_N2 = 64
FFT_K2 = FFT_N2 // 2 + 1
FFT_K2P = 40
FFT_GROUP = 16

V7X_VMEM_LIMIT_BYTES = 56 * 1024 * 1024
LANES = 128
MXU_DIM = 256
QK_PAD = MXU_DIM


def _params(*sem):
    return pltpu.CompilerParams(dimension_semantics=sem, vmem_limit_bytes=V7X_VMEM_LIMIT_BYTES)


def _mm_kernel(a_ref, b_ref, *rest, b_is_stacked, has_res):
    o_ref = rest[-1]
    b = b_ref[0] if b_is_stacked else b_ref[...]
    acc = jnp.dot(a_ref[...].astype(BF16), b.astype(BF16), preferred_element_type=F32)
    if has_res:
        res_ref, gate_ref = rest[0], rest[1]
        acc = res_ref[...] + gate_ref[0] * acc
    o_ref[...] = acc.astype(o_ref.dtype)


def matmul(a, b, *, b_index=None, out_dtype=F32, tm=512, tn=512, res=None, gate=None, rows_per_batch=None):
    M, K = a.shape
    N = b.shape[-1]
    tm, tn = min(tm, M), min(tn, N)
    assert M % tm == 0 and N % tn == 0, (M, N, tm, tn)
    stacked = b.ndim == 3
    if stacked:
        b_spec = pl.BlockSpec((1, K, tn), lambda i, j: (b_index, 0, j))
    else:
        b_spec = pl.BlockSpec((K, tn), lambda i, j: (0, j))
    in_specs = [pl.BlockSpec((tm, K), lambda i, j: (i, 0)), b_spec]
    args = [a, b]
    if res is not None:
        assert rows_per_batch % tm == 0
        tpb = rows_per_batch // tm
        in_specs += [pl.BlockSpec((tm, tn), lambda i, j: (i, j)),
                     pl.BlockSpec((1, 1, tn), lambda i, j: (i // tpb, 0, j))]
        args += [res, gate]
    return pl.pallas_call(
        functools.partial(_mm_kernel, b_is_stacked=stacked, has_res=res is not None),
        grid=(M // tm, N // tn),
        in_specs=in_specs,
        out_specs=pl.BlockSpec((tm, tn), lambda i, j: (i, j)),
        out_shape=jax.ShapeDtypeStruct((M, N), out_dtype),
        compiler_params=_params("parallel", "parallel"),
    )(*args)


def _norm_kernel(x_ref, g_ref, *rest, modulated):
    o_ref = rest[-1]
    x = x_ref[...]
    y = x * lax.rsqrt(jnp.mean(x * x, axis=-1, keepdims=True) + NORM_EPS) * g_ref[...]
    if modulated:
        shift_ref, scale_ref = rest[0], rest[1]
        y = y * (1 + scale_ref[0]) + shift_ref[0]
    o_ref[...] = y.astype(o_ref.dtype)


def norm_mod(x2, g, shift=None, scale=None, *, rows_per_batch=None, out_dtype=BF16, tm=256):
    M, D = x2.shape
    tm = min(tm, M)
    in_specs = [pl.BlockSpec((tm, D), lambda i: (i, 0)), pl.BlockSpec((1, D), lambda i: (0, 0))]
    args = [x2, g.reshape(1, D)]
    if shift is not None:
        tpb = rows_per_batch // tm
        in_specs += [pl.BlockSpec((1, 1, D), lambda i: (i // tpb, 0, 0))] * 2
        args += [shift, scale]
    return pl.pallas_call(
        functools.partial(_norm_kernel, modulated=shift is not None),
        grid=(M // tm,),
        in_specs=in_specs,
        out_specs=pl.BlockSpec((tm, D), lambda i: (i, 0)),
        out_shape=jax.ShapeDtypeStruct((M, D), out_dtype),
        compiler_params=_params("parallel"),
    )(*args)


def _attn_kernel(q_ref, kt_ref, v_ref, o_ref, *, chunks, c):
    q = q_ref[0, 0]
    m = l = acc = None
    for lo, hi in chunks:
        s = jnp.dot(q, kt_ref[0, 0, :, lo:hi], preferred_element_type=F32)
        m_chunk = jnp.max(s, axis=-1, keepdims=True)
        m_new = m_chunk if m is None else jnp.maximum(m, m_chunk)
        p = jnp.exp2(s * c - m_new * c)
        pv = jnp.dot(p.astype(BF16), v_ref[0, 0, lo:hi, :], preferred_element_type=F32)
        psum = jnp.sum(p, axis=-1, keepdims=True)
        if m is None:
            l, acc = psum, pv
        else:
            alpha = jnp.exp2((m - m_new) * c)
            l = alpha * l + psum
            acc = alpha * acc + pv
        m = m_new
    o_ref[0] = (acc / l).astype(o_ref.dtype)


def attention(q, kt, v, *, scale, tq=1024, tk=1024):
    B, H, L, _ = q.shape
    K = kt.shape[-1]
    V = v.shape[-1]
    chunks = tuple((lo, min(lo + tk, K)) for lo in range(0, K, tk))
    return pl.pallas_call(
        functools.partial(_attn_kernel, chunks=chunks, c=scale * math.log2(math.e)),
        grid=(B, H, L // tq),
        in_specs=[pl.BlockSpec((1, 1, tq, QK_PAD), lambda b, h, i: (b, h, i, 0)),
                  pl.BlockSpec((1, 1, QK_PAD, K), lambda b, h, i: (b, h, 0, 0)),
                  pl.BlockSpec((1, 1, K, V), lambda b, h, i: (b, h, 0, 0))],
        out_specs=pl.BlockSpec((1, tq, V), lambda b, h, i: (b, i, h)),
        out_shape=jax.ShapeDtypeStruct((B, L, H * V), BF16),
        compiler_params=_params("parallel", "parallel", "arbitrary"),
    )(q, kt, v)


def _norm_route_kernel(x_ref, g_ref, shift_ref, scale_ref, wr_ref, h_ref, aff_ref):
    x = x_ref[...]
    y = x * lax.rsqrt(jnp.mean(x * x, axis=-1, keepdims=True) + NORM_EPS) * g_ref[...]
    y = y * (1 + scale_ref[0]) + shift_ref[0]
    h_ref[...] = y
    logits = lax.dot_general(wr_ref[...], y.astype(BF16), (((1,), (1,)), ((), ())),
                             preferred_element_type=F32)
    p = jnp.exp(logits - jnp.max(logits, axis=0, keepdims=True))
    aff_ref[0] = p / jnp.sum(p, axis=0, keepdims=True)


def norm_route(x2, g, shift, scale, w_router_t, *, rows_per_batch, tm=256):
    M, D = x2.shape
    E = w_router_t.shape[0]
    tpb = rows_per_batch // tm
    return pl.pallas_call(
        _norm_route_kernel,
        grid=(M // tm,),
        in_specs=[pl.BlockSpec((tm, D), lambda i: (i, 0)),
                  pl.BlockSpec((1, D), lambda i: (0, 0)),
                  pl.BlockSpec((1, 1, D), lambda i: (i // tpb, 0, 0)),
                  pl.BlockSpec((1, 1, D), lambda i: (i // tpb, 0, 0)),
                  pl.BlockSpec((E, D), lambda i: (0, 0))],
        out_specs=[pl.BlockSpec((tm, D), lambda i: (i, 0)),
                   pl.BlockSpec((1, E, tm), lambda i: (i // tpb, 0, i % tpb))],
        out_shape=[jax.ShapeDtypeStruct((M, D), F32),
                   jax.ShapeDtypeStruct((M // rows_per_batch, E, rows_per_batch), F32)],
        compiler_params=_params("parallel"),
    )(x2, g.reshape(1, D), shift, scale, w_router_t)


def _lane_prefix_sum(x):
    E, T = x.shape
    r = lax.broadcasted_iota(jnp.int32, (LANES, LANES), 0)
    c = lax.broadcasted_iota(jnp.int32, (LANES, LANES), 1)
    tri = jnp.where(r <= c, 1.0, 0.0).astype(BF16)
    outs, carry = [], jnp.zeros((E, 1), F32)
    for j in range(T // LANES):
        seg = jnp.dot(x[:, j * LANES:(j + 1) * LANES].astype(BF16), tri, preferred_element_type=F32) + carry
        outs.append(seg)
        carry = seg[:, LANES - 1:LANES]
    return jnp.concatenate(outs, axis=1)


def _route_select_kernel(aff_ref, pos_ref, idx_ref, gate_ref, cum_s, *, cap, tch):
    aff = aff_ref[0]
    E, T = aff.shape

    def bisect(state):
        lo, hi, _ = state
        mid = 0.5 * (lo + hi)
        cnt = jnp.sum(jnp.where(aff >= mid, 1.0, 0.0), axis=1, keepdims=True)
        ge = cnt >= cap
        open_rows = jnp.where((mid > lo) & (mid < hi), 1.0, 0.0)
        return jnp.where(ge, mid, lo), jnp.where(ge, hi, mid), jnp.max(open_rows)

    lo0 = jnp.min(aff, axis=1, keepdims=True)
    hi0 = 2.0 * jnp.max(aff, axis=1, keepdims=True) + 1e-30
    lo, hi, _ = lax.while_loop(lambda st: st[2] > 0.0, bisect, (lo0, hi0, jnp.float32(1.0)))
    gt = aff >= hi
    eq = (aff >= lo) & (aff < hi)
    need = cap - jnp.sum(jnp.where(gt, 1.0, 0.0), axis=1, keepdims=True)
    eq_rank = _lane_prefix_sum(jnp.where(eq, 1.0, 0.0))
    sel = gt | (eq & (eq_rank <= need))
    cum = _lane_prefix_sum(jnp.where(sel, 1.0, 0.0))
    cumsel = jnp.where(sel, cum, 0.0)
    pos_ref[0] = (cumsel - 1.0).astype(jnp.int32)
    cum_s[0] = cum
    cum_s[1] = cumsel
    slot = lax.broadcasted_iota(jnp.int32, (cap, 1), 0).astype(F32)

    def per_expert(e, carry):
        idx_acc = jnp.zeros((cap, 1), F32)
        g_acc = jnp.zeros((cap, 1), F32)
        for lo in range(0, T, tch):
            cum_e = cum_s[0, pl.ds(e, 1), lo:lo + tch]
            cs_e = cum_s[1, pl.ds(e, 1), lo:lo + tch]
            aff_e = aff_ref[0, pl.ds(e, 1), lo:lo + tch]
            idx_acc = idx_acc + jnp.sum(jnp.where(cum_e <= slot, 1.0, 0.0), axis=1, keepdims=True)
            g_acc = g_acc + jnp.sum(jnp.where(cs_e == slot + 1.0, aff_e, 0.0), axis=1, keepdims=True)
        idx_ref[0, pl.ds(e, 1)] = idx_acc.astype(jnp.int32).reshape(1, cap, 1)
        gate_ref[0, pl.ds(e, 1)] = g_acc.reshape(1, cap, 1)
        return carry

    lax.fori_loop(0, E, per_expert, 0)


def route_select(aff, cap, *, tch=1024):
    B, E, T = aff.shape
    return pl.pallas_call(
        functools.partial(_route_select_kernel, cap=cap, tch=tch),
        grid=(B,),
        in_specs=[pl.BlockSpec((1, E, T), lambda b: (b, 0, 0))],
        out_specs=[pl.BlockSpec((1, E, T), lambda b: (b, 0, 0)),
                   pl.BlockSpec((1, E, cap, 1), lambda b: (b, 0, 0, 0)),
                   pl.BlockSpec((1, E, cap, 1), lambda b: (b, 0, 0, 0))],
        out_shape=[jax.ShapeDtypeStruct((B, E, T), jnp.int32),
                   jax.ShapeDtypeStruct((B, E, cap, 1), jnp.int32),
                   jax.ShapeDtypeStruct((B, E, cap, 1), F32)],
        scratch_shapes=[pltpu.VMEM((2, E, T), F32)],
        compiler_params=_params("parallel"),
    )(aff)


def _moe_ffn_kernel(idx_ref, idxn_ref, h_hbm, wg_ref, wu_ref, wd_ref, g_ref, o_ref, xland, xbf, acc, sem,
                    *, nf, n_exp, cap, tokens_per_batch):
    e = pl.program_id(0)
    f = pl.program_id(1)
    rows = xland.shape[0]

    def issue_gather(ids_ref):
        for b in range(rows // cap):
            def body(c, carry, b=b):
                i = b * cap + c
                src_row = ids_ref[0, 0, i] + b * tokens_per_batch
                pltpu.make_async_copy(h_hbm.at[pl.ds(src_row, 1)], xland.at[pl.ds(i, 1)], sem.at[0]).start()
                return carry
            lax.fori_loop(0, cap, body, 0, unroll=8)

    @pl.when((e == 0) & (f == 0))
    def _():
        issue_gather(idx_ref)

    @pl.when(f == 0)
    def _():
        pltpu.make_async_copy(h_hbm.at[pl.ds(0, rows)], xland, sem.at[0]).wait()
        xbf[...] = xland[...].astype(BF16)

        @pl.when(e + 1 < n_exp)
        def _():
            issue_gather(idxn_ref)

    x = xbf[...]
    a = jnp.dot(x, wg_ref[0, 0].astype(BF16), preferred_element_type=F32)
    u = jnp.dot(x, wu_ref[0, 0].astype(BF16), preferred_element_type=F32)
    h1 = (jax.nn.silu(a) * u).astype(BF16)
    wd = wd_ref[0, 0].astype(BF16)
    half = wd.shape[1] // 2
    for c0 in (0, half):
        y = jnp.dot(h1, wd[:, c0:c0 + half], preferred_element_type=F32)

        @pl.when(f == 0)
        def _():
            acc[:, c0:c0 + half] = y

        @pl.when(f > 0)
        def _():
            acc[:, c0:c0 + half] += y

    @pl.when(f == nf - 1)
    def _():
        o_ref[...] = (acc[...].reshape(o_ref.shape) * g_ref[...]).astype(o_ref.dtype)


def moe_ffn(idx3, hp, g, w_gate, w_up, w_down, layer, T, *, tf=256):
    B, E, C = idx3.shape
    D = hp.shape[1]
    F = w_gate.shape[-1]
    nf = F // tf
    ids = idx3.transpose(1, 0, 2).reshape(E, 1, B * C)
    smem_blk = lambda imap: pl.BlockSpec((1, 1, B * C), imap, memory_space=pltpu.SMEM)
    return pl.pallas_call(
        functools.partial(_moe_ffn_kernel, nf=nf, n_exp=E, cap=C, tokens_per_batch=T),
        grid=(E, nf),
        in_specs=[smem_blk(lambda e, f: (e, 0, 0)),
                  smem_blk(lambda e, f: (jnp.minimum(e + 1, E - 1), 0, 0)),
                  pl.BlockSpec(memory_space=pl.ANY),
                  pl.BlockSpec((1, 1, D, tf), lambda e, f: (layer, e, 0, f)),
                  pl.BlockSpec((1, 1, D, tf), lambda e, f: (layer, e, 0, f)),
                  pl.BlockSpec((1, 1, tf, D), lambda e, f: (layer, e, f, 0)),
                  pl.BlockSpec((B, 1, C, 1), lambda e, f: (0, e, 0, 0))],
        out_specs=pl.BlockSpec((B, 1, C, D), lambda e, f: (0, e, 0, 0)),
        out_shape=jax.ShapeDtypeStruct((B, E, C, D), BF16),
        scratch_shapes=[pltpu.VMEM((B * C, D), F32),
                        pltpu.VMEM((B * C, D), BF16),
                        pltpu.VMEM((B * C, D), F32),
                        pltpu.SemaphoreType.DMA((1,))],
        compiler_params=_params("arbitrary", "arbitrary"),
    )(ids, ids, hp, w_gate, w_up, w_down, g)


def _moe_combine_kernel(starts_ref, ntr_ref, y_hbm, post_ref, x_ref, gate_ref, o_ref, ybuf, yext, acc, sem, sem_ext,
                        *, n_exp, cap, win, steps_per_batch):
    s = pl.program_id(0)
    slot = s % 2

    def window(step, e, k):
        nominal = starts_ref[step * n_exp + e] + k * win
        return nominal, pl.multiple_of(jnp.minimum(nominal, cap - win), 16)

    def window_copy(step, e, k, dst, dsem):
        _, actual = window(step, e, k)
        return pltpu.make_async_copy(y_hbm.at[step // steps_per_batch, e, pl.ds(actual, win)],
                                     dst.at[pl.ds(e * win, win)], dsem)

    def onehot(step, k):
        post = post_ref[0]
        lane = lax.broadcasted_iota(jnp.int32, (1, win), 1)
        cols = []
        for e in range(n_exp):
            nominal, actual = window(step, e, k)
            slots = actual + lane
            slots = jnp.where(slots >= nominal, slots, -2)
            cols.append(jnp.where(post[:, e:e + 1] == slots, 1.0, 0.0).astype(BF16))
        return jnp.concatenate(cols, axis=1)

    @pl.when(s == 0)
    def _():
        for e in range(n_exp):
            window_copy(0, e, 0, ybuf.at[0], sem.at[0]).start()

    for e in range(n_exp):
        window_copy(s, e, 0, ybuf.at[slot], sem.at[slot]).wait()

    @pl.when(s + 1 < pl.num_programs(0))
    def _():
        for e in range(n_exp):
            window_copy(s + 1, e, 0, ybuf.at[1 - slot], sem.at[1 - slot]).start()

    acc[...] = jnp.dot(onehot(s, 0), ybuf[slot], preferred_element_type=F32)

    def extra(k, carry):
        for e in range(n_exp):
            window_copy(s, e, k, yext, sem_ext.at[0]).start()
        for e in range(n_exp):
            window_copy(s, e, k, yext, sem_ext.at[0]).wait()
        acc[...] += jnp.dot(onehot(s, k), yext[...], preferred_element_type=F32)
        return carry

    lax.fori_loop(1, ntr_ref[s], extra, 0)
    o_ref[...] = x_ref[...] + gate_ref[0] * acc[...]


def moe_combine(starts, ntr, y, post, x2, gate, *, tj=256, win=128):
    B, E, C, D = y.shape
    M = x2.shape[0]
    T = M // B
    spb = T // tj
    grid_spec = pltpu.PrefetchScalarGridSpec(
        num_scalar_prefetch=2,
        grid=(M // tj,),
        in_specs=[pl.BlockSpec(memory_space=pl.ANY),
                  pl.BlockSpec((1, tj, E), lambda s, *_: (s // spb, s % spb, 0)),
                  pl.BlockSpec((tj, D), lambda s, *_: (s, 0)),
                  pl.BlockSpec((1, 1, D), lambda s, *_: (s // spb, 0, 0))],
        out_specs=pl.BlockSpec((tj, D), lambda s, *_: (s, 0)),
        scratch_shapes=[pltpu.VMEM((2, E * win, D), BF16),
                        pltpu.VMEM((E * win, D), BF16),
                        pltpu.VMEM((tj, D), F32),
                        pltpu.SemaphoreType.DMA((2,)),
                        pltpu.SemaphoreType.DMA((1,))],
    )
    return pl.pallas_call(
        functools.partial(_moe_combine_kernel, n_exp=E, cap=C, win=win, steps_per_batch=spb),
        grid_spec=grid_spec,
        out_shape=jax.ShapeDtypeStruct((M, D), F32),
        compiler_params=_params("arbitrary"),
    )(starts, ntr, y, post, x2, gate)


def _small_rmsnorm(x, g):
    return x * lax.rsqrt(jnp.mean(x * x, axis=-1, keepdims=True) + NORM_EPS) * g


def _rope_tables(rows):
    row = jnp.broadcast_to(jnp.arange(rows)[:, None], (rows, GRID_W)).reshape(-1).astype(F32)
    col = jnp.broadcast_to(jnp.arange(GRID_W)[None, :], (rows, GRID_W)).reshape(-1).astype(F32)
    n_freq = MLA_ROPE_DIM // 4
    inv = ROPE_THETA ** (-jnp.arange(n_freq, dtype=F32) / n_freq)
    ang = jnp.concatenate([row[:, None] * inv, col[:, None] * inv], axis=-1)
    return jnp.cos(ang), jnp.sin(ang)


def _rope(x, cos, sin):
    half = MLA_ROPE_DIM // 2
    x1, x2 = x[..., :half], x[..., half:]
    return jnp.concatenate([x1 * cos - x2 * sin, x2 * cos + x1 * sin], axis=-1)


def _pad_cols(w, n):
    return jnp.concatenate([w, jnp.zeros((w.shape[0], n - w.shape[1]), w.dtype)], axis=1)


def _ada(cvec3, ada_w, ada_b, layer):
    R = cvec3.shape[0]
    a = jnp.concatenate([jax.nn.silu(cvec3), jnp.zeros((8 - R, cvec3.shape[1]), F32)], axis=0)
    m = matmul(a, ada_w, b_index=layer, tm=8, tn=512)[:R] + ada_b[layer]
    return m.reshape(R, 6, D_MODEL)


def _mla_layer(h, h_c, B, L, Lc, w_in, g_q, g_kv, w_uq, w_ukv):
    H = MLA_HEADS
    in_dim = w_in.shape[1]
    proj = matmul(h, _pad_cols(w_in, 896).astype(BF16), tn=896)[:, :in_dim]
    rest_c = matmul(h_c, _pad_cols(w_in[:, MLA_Q_RANK:], 384).astype(BF16), tn=384)[:, :in_dim - MLA_Q_RANK]
    cq = proj[:, :MLA_Q_RANK]
    ckv = jnp.concatenate([proj[:, MLA_Q_RANK:MLA_Q_RANK + MLA_KV_RANK].reshape(B, L, -1),
                           rest_c[:, :MLA_KV_RANK].reshape(B, Lc, -1)], axis=1)
    K = L + Lc
    cos, sin = _rope_tables(L // GRID_W)
    kr = _rope(proj[:, MLA_Q_RANK + MLA_KV_RANK:].reshape(B, L, -1), cos, sin)
    kr = jnp.concatenate([kr, rest_c[:, MLA_KV_RANK:].reshape(B, Lc, -1)], axis=1)
    w_ukv_b = w_ukv.astype(BF16)
    kv = matmul(_small_rmsnorm(ckv, g_kv).astype(BF16).reshape(B * K, -1), w_ukv_b, tm=544, tn=1024)
    kv = kv.reshape(B, K, H, MLA_NOPE_DIM + MLA_V_DIM)
    q = matmul(_small_rmsnorm(cq, g_q).astype(BF16), w_uq.astype(BF16), tn=1024)
    q = q.reshape(B, L, H, MLA_NOPE_DIM + MLA_ROPE_DIM)
    qr = _rope(q[..., MLA_NOPE_DIM:], cos[:, None, :], sin[:, None, :])
    pad = QK_PAD - MLA_NOPE_DIM - MLA_ROPE_DIM
    qp = jnp.concatenate([q[..., :MLA_NOPE_DIM], qr, jnp.zeros((B, L, H, pad), F32)], axis=-1)
    qp = qp.astype(BF16).transpose(0, 2, 1, 3)
    kp = jnp.concatenate([kv[..., :MLA_NOPE_DIM],
                          jnp.broadcast_to(kr[:, :, None, :], (B, K, H, MLA_ROPE_DIM)),
                          jnp.zeros((B, K, H, pad), F32)], axis=-1)
    kt = kp.astype(BF16).transpose(0, 2, 3, 1)
    v = kv[..., MLA_NOPE_DIM:].astype(BF16).transpose(0, 2, 1, 3)
    scale = (MLA_NOPE_DIM + MLA_ROPE_DIM) ** -0.5
    return attention(qp, kt, v, scale=scale)


def _fft_tables(L):
    n = 2 * L
    n2_in = L // FFT_N1
    k2 = np.arange(FFT_K2)
    th = 2 * np.pi * ((k2[:, None] * np.arange(n2_in)[None, :]) % FFT_N2) / FFT_N2
    f1 = np.zeros((2 * FFT_K2P, n2_in))
    f1[:FFT_K2] = np.cos(th)
    f1[FFT_K2P:FFT_K2P + FFT_K2] = -np.sin(th)
    wgt = np.where((k2 == 0) | (k2 == FFT_N2 // 2), 1.0, 2.0)
    g = np.zeros((n2_in, 2 * FFT_K2P))
    g[:, :FFT_K2] = (wgt[:, None] * np.cos(th)).T / n
    g[:, FFT_K2P:FFT_K2P + FFT_K2] = -(wgt[:, None] * np.sin(th)).T / n
    k1 = np.arange(FFT_N1)[:, None]
    n1 = np.arange(FFT_N1)[None, :]
    m2 = np.zeros((FFT_K2, 2 * FFT_N1, 2 * FFT_N1))
    for kk in range(FFT_K2):
        phi = 2 * np.pi * ((n1 * k1 * FFT_N2 + n1 * kk) % n) / n
        tr, ti = np.cos(phi), -np.sin(phi)
        m2[kk] = np.block([[tr, -ti], [ti, tr]])
    as_bf16 = lambda a: jnp.asarray(a, dtype=F32).astype(BF16)
    eye = np.eye(FFT_GROUP)
    return (as_bf16(np.kron(f1, eye)), as_bf16(np.kron(g, eye)), as_bf16(m2),
            as_bf16(np.transpose(m2, (0, 2, 1))))


def _fft_s1_kernel(f_ref, z_ref, o_ref):
    _, n2_in, g, ct = z_ref.shape
    z = z_ref[0].reshape(n2_in * g, ct).astype(BF16)
    a = jnp.dot(f_ref[...], z, preferred_element_type=F32)
    o_ref[0] = a.reshape(o_ref.shape[1:]).astype(o_ref.dtype)


def fft_stage1(z4, f1k, *, lead_off, nb, ct=2048):
    _, n2_in, _, C = z4.shape
    g = FFT_GROUP
    return pl.pallas_call(
        _fft_s1_kernel,
        grid=(nb, FFT_N1 // g, C // ct),
        in_specs=[pl.BlockSpec(f1k.shape, lambda b, i, j: (0, 0)),
                  pl.BlockSpec((1, n2_in, g, ct), lambda b, i, j: (lead_off + b, 0, i, j))],
        out_specs=pl.BlockSpec((1, 2, FFT_K2P, g, ct), lambda b, i, j: (b, 0, 0, i, j)),
        out_shape=jax.ShapeDtypeStruct((nb, 2, FFT_K2P, FFT_N1, C), BF16),
        compiler_params=_params("parallel", "parallel", "parallel"),
    )(f1k, z4)


def _fft_s2_kernel(a_ref, m_ref, mt_ref, kf_ref, o_ref):
    k2 = pl.program_id(0)

    @pl.when(k2 < FFT_K2)
    def _():
        ct = a_ref.shape[-1]
        x = jnp.dot(m_ref[0], a_ref[...].reshape(2 * FFT_N1, ct), preferred_element_type=F32)
        xr, xi = x[:FFT_N1], x[FFT_N1:]
        kr, ki = kf_ref[0, 0, 0], kf_ref[0, 1, 0]
        p = jnp.concatenate([xr * kr - xi * ki, xr * ki + xi * kr], axis=0).astype(BF16)
        e = jnp.dot(mt_ref[0], p, preferred_element_type=F32)
        o_ref[...] = e.astype(o_ref.dtype).reshape(o_ref.shape)

    @pl.when(k2 >= FFT_K2)
    def _():
        o_ref[...] = jnp.zeros(o_ref.shape, o_ref.dtype)


def fft_stage2(a5, m2, m2t, kf, order):
    B, _, _, _, D = a5.shape
    last = FFT_K2 - 1
    blk = (1, 2, 1, FFT_N1, D)
    return pl.pallas_call(
        _fft_s2_kernel,
        grid=(FFT_K2P, B),
        in_specs=[pl.BlockSpec(blk, lambda k, b: (b, 0, jnp.minimum(k, last), 0, 0)),
                  pl.BlockSpec((1, 2 * FFT_N1, 2 * FFT_N1), lambda k, b: (jnp.minimum(k, last), 0, 0)),
                  pl.BlockSpec((1, 2 * FFT_N1, 2 * FFT_N1), lambda k, b: (jnp.minimum(k, last), 0, 0)),
                  pl.BlockSpec(blk, lambda k, b: (order, 0, jnp.minimum(k, last), 0, 0))],
        out_specs=pl.BlockSpec(blk, lambda k, b: (b, 0, k, 0, 0)),
        out_shape=jax.ShapeDtypeStruct(a5.shape, BF16),
        compiler_params=_params("parallel", "arbitrary"),
    )(a5, m2, m2t, kf)


def _fft_f2_kernel(f_ref, g_ref, m_ref, o_ref):
    k2 = pl.program_id(0)

    @pl.when(k2 < FFT_K2)
    def _():
        ct = f_ref.shape[-1]
        xf = jnp.dot(m_ref[0], f_ref[...].reshape(2 * FFT_N1, ct), preferred_element_type=F32)
        xg = jnp.dot(m_ref[0], g_ref[...].reshape(2 * FFT_N1, ct), preferred_element_type=F32)
        o_ref[0, 0, 0] = xf[:FFT_N1] + xg[:FFT_N1]
        o_ref[0, 1, 0] = xf[FFT_N1:] - xg[FFT_N1:]

    @pl.when(k2 >= FFT_K2)
    def _():
        o_ref[...] = jnp.zeros(o_ref.shape, o_ref.dtype)


def fft_filter_spectrum(a4, m2, D):
    last = FFT_K2 - 1
    blk = (2, 1, FFT_N1, D)
    return pl.pallas_call(
        _fft_f2_kernel,
        grid=(FFT_K2P, HYENA_ORDER),
        in_specs=[pl.BlockSpec(blk, lambda k, o: (0, jnp.minimum(k, last), 0, 2 * o)),
                  pl.BlockSpec(blk, lambda k, o: (0, jnp.minimum(k, last), 0, 2 * o + 1)),
                  pl.BlockSpec((1, 2 * FFT_N1, 2 * FFT_N1), lambda k, o: (jnp.minimum(k, last), 0, 0))],
        out_specs=pl.BlockSpec((1, 2, 1, FFT_N1, D), lambda k, o: (o, 0, k, 0, 0)),
        out_shape=jax.ShapeDtypeStruct((HYENA_ORDER, 2, FFT_K2P, FFT_N1, D), F32),
        compiler_params=_params("arbitrary", "parallel"),
    )(a4, a4, m2)


def _fft_s3_kernel(g_ref, e_ref, z_ref, x_ref, skip_ref, o_ref):
    _, n2_in, g, ct = z_ref.shape
    e = e_ref[0].reshape(2 * FFT_K2P * g, ct)
    y = jnp.dot(g_ref[...], e, preferred_element_type=F32).reshape(n2_in, g, ct)
    o_ref[0] = (x_ref[0] * (y + skip_ref[...] * z_ref[0])).astype(o_ref.dtype)


def fft_stage3(e5, gk, z4, z_off, x4, x_off, skip, *, out_dtype):
    B, _, _, _, D = e5.shape
    n2_in = z4.shape[1]
    g = FFT_GROUP
    return pl.pallas_call(
        _fft_s3_kernel,
        grid=(B, FFT_N1 // g),
        in_specs=[pl.BlockSpec(gk.shape, lambda b, i: (0, 0)),
                  pl.BlockSpec((1, 2, FFT_K2P, g, D), lambda b, i: (b, 0, 0, i, 0)),
                  pl.BlockSpec((1, n2_in, g, D), lambda b, i: (z_off + b, 0, i, 0)),
                  pl.BlockSpec((1, n2_in, g, D), lambda b, i: (x_off + b, 0, i, 0)),
                  pl.BlockSpec((1, D), lambda b, i: (0, 0))],
        out_specs=pl.BlockSpec((1, n2_in, g, D), lambda b, i: (b, 0, i, 0)),
        out_shape=jax.ShapeDtypeStruct((B, n2_in, FFT_N1, D), out_dtype),
        compiler_params=_params("parallel", "parallel"),
    )(gk, e5, z4, x4, skip.reshape(1, D))


def _hyin_kernel(ap_ref, a_ref, an_ref, b_ref, cw_ref, cb_ref, o_ref, *, tiles_per_seq):
    i = pl.program_id(0)
    t = i % tiles_per_seq
    halo = ap_ref.shape[0]
    tm, tn = o_ref.shape[1:]
    cw = MXU_DIM
    row = lax.broadcasted_iota(jnp.int32, (tm, cw), 0)
    for c0 in range(0, tn, cw):
        b = b_ref[:, c0:c0 + cw]
        acc = jnp.dot(a_ref[...], b, preferred_element_type=F32)
        prev = jnp.dot(ap_ref[...], b, preferred_element_type=F32)[halo - 1:halo]
        nxt = jnp.dot(an_ref[...], b, preferred_element_type=F32)[0:1]
        prev = jnp.where(t == 0, 0.0, prev)
        nxt = jnp.where(t == tiles_per_seq - 1, 0.0, nxt)
        up = jnp.where(row == 0, prev, pltpu.roll(acc, 1, axis=0))
        dn = jnp.where(row == tm - 1, nxt, pltpu.roll(acc, tm - 1, axis=0))
        w = cw_ref[:, c0:c0 + cw]
        o_ref[0, :, c0:c0 + cw] = up * w[0:1] + acc * w[1:2] + dn * w[2:3] + cb_ref[:, c0:c0 + cw]


def hyena_in(h, w_in, conv_w, conv_b, L, *, tm=512, tn=1024, halo=16):
    M, K = h.shape
    N = w_in.shape[1]
    D = N // 3
    per = D // tn
    return pl.pallas_call(
        functools.partial(_hyin_kernel, tiles_per_seq=L // tm),
        grid=(M // tm, N // tn),
        in_specs=[pl.BlockSpec((halo, K), lambda i, j: (jnp.maximum(i * (tm // halo) - 1, 0), 0)),
                  pl.BlockSpec((tm, K), lambda i, j: (i, 0)),
                  pl.BlockSpec((halo, K), lambda i, j: (jnp.minimum((i + 1) * (tm // halo), M // halo - 1), 0)),
                  pl.BlockSpec((K, tn), lambda i, j: (0, j)),
                  pl.BlockSpec((3, tn), lambda i, j: (0, j)),
                  pl.BlockSpec((1, tn), lambda i, j: (0, j))],
        out_specs=pl.BlockSpec((1, tm, tn), lambda i, j: (j // per, i, j % per)),
        out_shape=jax.ShapeDtypeStruct((3, M, D), F32),
        compiler_params=_params("parallel", "parallel"),
    )(h, h, h, w_in, conv_w, conv_b.reshape(1, N))


def _hyena_filter_taps(L, w1, b1, w2, b2, w3, b3, freq):
    D = D_MODEL
    t = jnp.arange(L, dtype=F32) / L
    w = 2 * math.pi * jnp.arange(L, dtype=F32) / L
    bands = jnp.linspace(1e-4, HYENA_BANDS - 1, HYENA_BANDS, dtype=F32)
    ang = w[:, None] * bands[None, :]
    z = jnp.concatenate([t[:, None], jnp.cos(ang), -jnp.sin(ang)], axis=-1)
    a = jnp.sin(freq * (z @ w1 + b1))
    a = jnp.sin(freq * (a @ w2 + b2))
    hf = matmul(a, w3, tn=1024) + b3
    max_decay = math.log(HYENA_TARGET) / HYENA_FAST_PCT
    min_decay = math.log(HYENA_TARGET) / HYENA_SLOW_PCT
    deltas = jnp.linspace(min_decay, max_decay, D, dtype=F32)
    hf = hf * jnp.tile(jnp.exp(-t[:, None] * jnp.abs(deltas)), (1, 2 * HYENA_ORDER))
    is_bwd = (jnp.arange(2 * HYENA_ORDER * D) // D) % 2 == 1
    hf = jnp.where((jnp.arange(L)[:, None] == 0) & is_bwd[None, :], 0.0, hf)
    ss = jnp.sum(hf * hf, axis=0).reshape(HYENA_ORDER, 2, D).sum(axis=1, keepdims=True)
    inv = jnp.broadcast_to(lax.rsqrt(ss + NORM_EPS), (HYENA_ORDER, 2, D)).reshape(1, -1)
    return hf * inv


def _hyena_layer(h, B, L, w_in, conv_w, conv_b, f_w1, f_b1, f_w2, f_b2, f_w3, f_b3, f_freq, skip):
    D = D_MODEL
    n2_in = L // FFT_N1
    f1k, gk, m2, m2t = _fft_tables(L)
    taps = _hyena_filter_taps(L, f_w1, f_b1, f_w2, f_b2, f_w3, f_b3, f_freq)
    af = fft_stage1(taps.reshape(1, n2_in, FFT_N1, taps.shape[1]), f1k, lead_off=0, nb=1)
    kf = fft_filter_spectrum(af[0], m2, D)
    u4 = hyena_in(h, w_in.astype(BF16), conv_w, conv_b, L).reshape(3 * B, n2_in, FFT_N1, D)
    z4 = u4
    for order in range(HYENA_ORDER):
        a = fft_stage1(z4, f1k, lead_off=0, nb=B)
        e = fft_stage2(a, m2, m2t, kf, order)
        last = order == HYENA_ORDER - 1
        z4 = fft_stage3(e, gk, z4, 0, u4, (order + 1) * B, skip[order], out_dtype=BF16 if last else F32)
    return z4.reshape(B * L, D)


def _moe_layer(x2, norm_gain, mod, B, T, w_router, w_gate, w_up, w_down, layer, *, tj=256, win=128):
    E = N_EXPERTS
    cap = EC_CAPACITY_FACTOR * T // E
    hp, aff = norm_route(x2, norm_gain, mod[:, 3:4], mod[:, 4:5], w_router.T.astype(BF16), rows_per_batch=T)
    pos, idx, g = route_select(aff, cap)
    idx3 = idx.reshape(B, E, cap)
    y = moe_ffn(idx3, hp, g, w_gate, w_up, w_down, layer, T)
    nj = T // tj
    edges = jnp.arange(nj + 1, dtype=jnp.int32) * tj
    first = jnp.sum((idx3[..., None] < edges).astype(jnp.int32), axis=2)
    nominal = (first[..., :nj] // 16) * 16
    ntr = jnp.maximum(jnp.max((first[..., 1:] - nominal + win - 1) // win, axis=1), 1)
    starts = nominal.transpose(0, 2, 1).reshape(-1)
    return moe_combine(starts, ntr.reshape(-1), y, pos.transpose(0, 2, 1), x2, mod[:, 5:6], tj=tj, win=win)


def kernel(x, c, ctx, c_ctx, ada_w, ada_b, norm_g, final_g, mla_w_in, mla_g_q, mla_g_kv, mla_w_uq, mla_w_ukv, mla_w_o, hy_w_in, hy_conv_w, hy_conv_b, hy_f_w1, hy_f_b1, hy_f_w2, hy_f_b2, hy_f_w3, hy_f_b3, hy_f_freq, hy_skip, hy_w_out, moe_w_router, moe_w_gate, moe_w_up, moe_w_down):
    B, L, D = x.shape
    Lc = ctx.shape[1]
    x2 = x.reshape(B * L, D)
    ctx2 = ctx.reshape(B * Lc, D)

    mod3 = _ada(jnp.concatenate([c, c_ctx[None]], axis=0), ada_w, ada_b, 0)
    mod, mod_c = mod3[:B], jnp.broadcast_to(mod3[B:], (B, 6, D))
    h = norm_mod(x2, norm_g[0, 0], mod[:, 0:1], mod[:, 1:2], rows_per_batch=L)
    h_c = norm_mod(ctx2, norm_g[0, 0], mod_c[:, 0:1], mod_c[:, 1:2], rows_per_batch=Lc)
    o = _mla_layer(h, h_c, B, L, Lc, mla_w_in[0], mla_g_q[0], mla_g_kv[0], mla_w_uq[0], mla_w_ukv[0])
    x2 = matmul(o.reshape(B * L, -1), mla_w_o[0].astype(BF16), tn=1024,
                res=x2, gate=mod[:, 2:3], rows_per_batch=L)
    x2 = _moe_layer(x2, norm_g[0, 1], mod, B, L, moe_w_router[0], moe_w_gate, moe_w_up, moe_w_down, 0)

    mod = _ada(c, ada_w, ada_b, 1)
    h = norm_mod(x2, norm_g[1, 0], mod[:, 0:1], mod[:, 1:2], rows_per_batch=L)
    z = _hyena_layer(h, B, L, hy_w_in[0], hy_conv_w[0], hy_conv_b[0], hy_f_w1[0], hy_f_b1[0], hy_f_w2[0],
                     hy_f_b2[0], hy_f_w3[0], hy_f_b3[0], hy_f_freq[0], hy_skip[0])
    x2 = matmul(z, hy_w_out[0].astype(BF16), tn=1024, res=x2, gate=mod[:, 2:3], rows_per_batch=L)
    x2 = _moe_layer(x2, norm_g[1, 1], mod, B, L, moe_w_router[1], moe_w_gate, moe_w_up, moe_w_down, 1)

    return norm_mod(x2, final_g, out_dtype=F32).reshape(B, L, D)
```

```python
import functools
import math

import jax
import jax.numpy as jnp
import numpy as np
from jax import lax
from jax.experimental import pallas as pl
from jax.experimental.pallas import tpu as pltpu

F32 = jnp.float32
BF16 = jnp.bfloat16

D_MODEL = 2048
GRID_W = 64
NORM_EPS = 1e-6
MLA_HEADS = 16
MLA_Q_RANK = 512
MLA_KV_RANK = 256
MLA_NOPE_DIM = 128
MLA_ROPE_DIM = 64
MLA_V_DIM = 128
ROPE_THETA = 10000.0
HYENA_ORDER = 2
HYENA_BANDS = 16
HYENA_TARGET = 1e-2
HYENA_FAST_PCT = 0.3
HYENA_SLOW_PCT = 1.5
N_EXPERTS = 16
EC_CAPACITY_FACTOR = 2
FFT_N1 = 128
FFT_N2 = 64
FFT_K2 = FFT_N2 // 2 + 1
FFT_K2P = 40
FFT_GROUP = 16

V7X_VMEM_LIMIT_BYTES = 56 * 1024 * 1024
LANES = 128
MXU_DIM = 256
QK_PAD = MXU_DIM


def _params(*sem):
    return pltpu.CompilerParams(dimension_semantics=sem, vmem_limit_bytes=V7X_VMEM_LIMIT_BYTES)


def _mm_kernel(a_ref, b_ref, *rest, b_is_stacked, has_res):
    o_ref = rest[-1]
    b = b_ref[0] if b_is_stacked else b_ref[...]
    acc = jnp.dot(a_ref[...].astype(BF16), b.astype(BF16), preferred_element_type=F32)
    if has_res:
        res_ref, gate_ref = rest[0], rest[1]
        acc = res_ref[...] + gate_ref[0] * acc
    o_ref[...] = acc.astype(o_ref.dtype)


def matmul(a, b, *, b_index=None, out_dtype=F32, tm=512, tn=512, res=None, gate=None, rows_per_batch=None):
    M, K = a.shape
    N = b.shape[-1]
    tm, tn = min(tm, M), min(tn, N)
    assert M % tm == 0 and N % tn == 0, (M, N, tm, tn)
    stacked = b.ndim == 3
    if stacked:
        b_spec = pl.BlockSpec((1, K, tn), lambda i, j: (b_index, 0, j))
    else:
        b_spec = pl.BlockSpec((K, tn), lambda i, j: (0, j))
    in_specs = [pl.BlockSpec((tm, K), lambda i, j: (i, 0)), b_spec]
    args = [a, b]
    if res is not None:
        assert rows_per_batch % tm == 0
        tpb = rows_per_batch // tm
        in_specs += [pl.BlockSpec((tm, tn), lambda i, j: (i, j)),
                     pl.BlockSpec((1, 1, tn), lambda i, j: (i // tpb, 0, j))]
        args += [res, gate]
    return pl.pallas_call(
        functools.partial(_mm_kernel, b_is_stacked=stacked, has_res=res is not None),
        grid=(M // tm, N // tn),
        in_specs=in_specs,
        out_specs=pl.BlockSpec((tm, tn), lambda i, j: (i, j)),
        out_shape=jax.ShapeDtypeStruct((M, N), out_dtype),
        compiler_params=_params("parallel", "parallel"),
    )(*args)


def _norm_kernel(x_ref, g_ref, *rest, modulated):
    o_ref = rest[-1]
    x = x_ref[...]
    y = x * lax.rsqrt(jnp.mean(x * x, axis=-1, keepdims=True) + NORM_EPS) * g_ref[...]
    if modulated:
        shift_ref, scale_ref = rest[0], rest[1]
        y = y * (1 + scale_ref[0]) + shift_ref[0]
    o_ref[...] = y.astype(o_ref.dtype)


def norm_mod(x2, g, shift=None, scale=None, *, rows_per_batch=None, out_dtype=BF16, tm=256):
    M, D = x2.shape
    tm = min(tm, M)
    in_specs = [pl.BlockSpec((tm, D), lambda i: (i, 0)), pl.BlockSpec((1, D), lambda i: (0, 0))]
    args = [x2, g.reshape(1, D)]
    if shift is not None:
        tpb = rows_per_batch // tm
        in_specs += [pl.BlockSpec((1, 1, D), lambda i: (i // tpb, 0, 0))] * 2
        args += [shift, scale]
    return pl.pallas_call(
        functools.partial(_norm_kernel, modulated=shift is not None),
        grid=(M // tm,),
        in_specs=in_specs,
        out_specs=pl.BlockSpec((tm, D), lambda i: (i, 0)),
        out_shape=jax.ShapeDtypeStruct((M, D), out_dtype),
        compiler_params=_params("parallel"),
    )(*args)


def _attn_kernel(q_ref, kt_ref, v_ref, o_ref, *, chunks, c):
    q = q_ref[0, 0]
    m = l = acc = None
    for lo, hi in chunks:
        s = jnp.dot(q, kt_ref[0, 0, :, lo:hi], preferred_element_type=F32)
        m_chunk = jnp.max(s, axis=-1, keepdims=True)
        m_new = m_chunk if m is None else jnp.maximum(m, m_chunk)
        p = jnp.exp2(s * c - m_new * c)
        pv = jnp.dot(p.astype(BF16), v_ref[0, 0, lo:hi, :], preferred_element_type=F32)
        psum = jnp.sum(p, axis=-1, keepdims=True)
        if m is None:
            l, acc = psum, pv
        else:
            alpha = jnp.exp2((m - m_new) * c)
            l = alpha * l + psum
            acc = alpha * acc + pv
        m = m_new
    o_ref[0] = (acc / l).astype(o_ref.dtype)


def attention(q, kt, v, *, scale, tq=1024, tk=1024):
    B, H, L, _ = q.shape
    K = kt.shape[-1]
    V = v.shape[-1]
    chunks = tuple((lo, min(lo + tk, K)) for lo in range(0, K, tk))
    return pl.pallas_call(
        functools.partial(_attn_kernel, chunks=chunks, c=scale * math.log2(math.e)),
        grid=(B, H, L // tq),
        in_specs=[pl.BlockSpec((1, 1, tq, QK_PAD), lambda b, h, i: (b, h, i, 0)),
                  pl.BlockSpec((1, 1, QK_PAD, K), lambda b, h, i: (b, h, 0, 0)),
                  pl.BlockSpec((1, 1, K, V), lambda b, h, i: (b, h, 0, 0))],
        out_specs=pl.BlockSpec((1, tq, V), lambda b, h, i: (b, i, h)),
        out_shape=jax.ShapeDtypeStruct((B, L, H * V), BF16),
        compiler_params=_params("parallel", "parallel", "arbitrary"),
    )(q, kt, v)


def _norm_route_kernel(x_ref, g_ref, shift_ref, scale_ref, wr_ref, h_ref, aff_ref):
    x = x_ref[...]
    y = x * lax.rsqrt(jnp.mean(x * x, axis=-1, keepdims=True) + NORM_EPS) * g_ref[...]
    y = y * (1 + scale_ref[0]) + shift_ref[0]
    h_ref[...] = y
    logits = lax.dot_general(wr_ref[...], y.astype(BF16), (((1,), (1,)), ((), ())),
                             preferred_element_type=F32)
    p = jnp.exp(logits - jnp.max(logits, axis=0, keepdims=True))
    aff_ref[0] = p / jnp.sum(p, axis=0, keepdims=True)


def norm_route(x2, g, shift, scale, w_router_t, *, rows_per_batch, tm=256):
    M, D = x2.shape
    E = w_router_t.shape[0]
    tpb = rows_per_batch // tm
    return pl.pallas_call(
        _norm_route_kernel,
        grid=(M // tm,),
        in_specs=[pl.BlockSpec((tm, D), lambda i: (i, 0)),
                  pl.BlockSpec((1, D), lambda i: (0, 0)),
                  pl.BlockSpec((1, 1, D), lambda i: (i // tpb, 0, 0)),
                  pl.BlockSpec((1, 1, D), lambda i: (i // tpb, 0, 0)),
                  pl.BlockSpec((E, D), lambda i: (0, 0))],
        out_specs=[pl.BlockSpec((tm, D), lambda i: (i, 0)),
                   pl.BlockSpec((1, E, tm), lambda i: (i // tpb, 0, i % tpb))],
        out_shape=[jax.ShapeDtypeStruct((M, D), F32),
                   jax.ShapeDtypeStruct((M // rows_per_batch, E, rows_per_batch), F32)],
        compiler_params=_params("parallel"),
    )(x2, g.reshape(1, D), shift, scale, w_router_t)


def _lane_prefix_sum(x):
    E, T = x.shape
    r = lax.broadcasted_iota(jnp.int32, (LANES, LANES), 0)
    c = lax.broadcasted_iota(jnp.int32, (LANES, LANES), 1)
    tri = jnp.where(r <= c, 1.0, 0.0).astype(BF16)
    outs, carry = [], jnp.zeros((E, 1), F32)
    for j in range(T // LANES):
        seg = jnp.dot(x[:, j * LANES:(j + 1) * LANES].astype(BF16), tri, preferred_element_type=F32) + carry
        outs.append(seg)
        carry = seg[:, LANES - 1:LANES]
    return jnp.concatenate(outs, axis=1)


def _route_select_kernel(aff_ref, pos_ref, idx_ref, gate_ref, cum_s, *, cap, tch):
    aff = aff_ref[0]
    E, T = aff.shape

    def bisect(state):
        lo, hi, _ = state
        mid = 0.5 * (lo + hi)
        cnt = jnp.sum(jnp.where(aff >= mid, 1.0, 0.0), axis=1, keepdims=True)
        ge = cnt >= cap
        open_rows = jnp.where((mid > lo) & (mid < hi), 1.0, 0.0)
        return jnp.where(ge, mid, lo), jnp.where(ge, hi, mid), jnp.max(open_rows)

    lo0 = jnp.min(aff, axis=1, keepdims=True)
    hi0 = 2.0 * jnp.max(aff, axis=1, keepdims=True) + 1e-30
    lo, hi, _ = lax.while_loop(lambda st: st[2] > 0.0, bisect, (lo0, hi0, jnp.float32(1.0)))
    gt = aff >= hi
    eq = (aff >= lo) & (aff < hi)
    need = cap - jnp.sum(jnp.where(gt, 1.0, 0.0), axis=1, keepdims=True)
    eq_rank = _lane_prefix_sum(jnp.where(eq, 1.0, 0.0))
    sel = gt | (eq & (eq_rank <= need))
    cum = _lane_prefix_sum(jnp.where(sel, 1.0, 0.0))
    cumsel = jnp.where(sel, cum, 0.0)
    pos_ref[0] = (cumsel - 1.0).astype(jnp.int32)
    cum_s[0] = cum
    cum_s[1] = cumsel
    slot = lax.broadcasted_iota(jnp.int32, (cap, 1), 0).astype(F32)

    def per_expert(e, carry):
        idx_acc = jnp.zeros((cap, 1), F32)
        g_acc = jnp.zeros((cap, 1), F32)
        for lo in range(0, T, tch):
            cum_e = cum_s[0, pl.ds(e, 1), lo:lo + tch]
            cs_e = cum_s[1, pl.ds(e, 1), lo:lo + tch]
            aff_e = aff_ref[0, pl.ds(e, 1), lo:lo + tch]
            idx_acc = idx_acc + jnp.sum(jnp.where(cum_e <= slot, 1.0, 0.0), axis=1, keepdims=True)
            g_acc = g_acc + jnp.sum(jnp.where(cs_e == slot + 1.0, aff_e, 0.0), axis=1, keepdims=True)
        idx_ref[0, pl.ds(e, 1)] = idx_acc.astype(jnp.int32).reshape(1, cap, 1)
        gate_ref[0, pl.ds(e, 1)] = g_acc.reshape(1, cap, 1)
        return carry

    lax.fori_loop(0, E, per_expert, 0)


def route_select(aff, cap, *, tch=1024):
    B, E, T = aff.shape
    return pl.pallas_call(
        functools.partial(_route_select_kernel, cap=cap, tch=tch),
        grid=(B,),
        in_specs=[pl.BlockSpec((1, E, T), lambda b: (b, 0, 0))],
        out_specs=[pl.BlockSpec((1, E, T), lambda b: (b, 0, 0)),
                   pl.BlockSpec((1, E, cap, 1), lambda b: (b, 0, 0, 0)),
                   pl.BlockSpec((1, E, cap, 1), lambda b: (b, 0, 0, 0))],
        out_shape=[jax.ShapeDtypeStruct((B, E, T), jnp.int32),
                   jax.ShapeDtypeStruct((B, E, cap, 1), jnp.int32),
                   jax.ShapeDtypeStruct((B, E, cap, 1), F32)],
        scratch_shapes=[pltpu.VMEM((2, E, T), F32)],
        compiler_params=_params("parallel"),
    )(aff)


def _moe_ffn_kernel(idx_ref, idxn_ref, h_hbm, wg_ref, wu_ref, wd_ref, g_ref, o_ref, xland, xbf, acc, sem,
                    *, nf, n_exp, cap, tokens_per_batch):
    e = pl.program_id(0)
    f = pl.program_id(1)
    rows = xland.shape[0]

    def issue_gather(ids_ref):
        for b in range(rows // cap):
            def body(c, carry, b=b):
                i = b * cap + c
                src_row = ids_ref[0, 0, i] + b * tokens_per_batch
                pltpu.make_async_copy(h_hbm.at[pl.ds(src_row, 1)], xland.at[pl.ds(i, 1)], sem.at[0]).start()
                return carry
            lax.fori_loop(0, cap, body, 0, unroll=8)

    @pl.when((e == 0) & (f == 0))
    def _():
        issue_gather(idx_ref)

    @pl.when(f == 0)
    def _():
        pltpu.make_async_copy(h_hbm.at[pl.ds(0, rows)], xland, sem.at[0]).wait()
        xbf[...] = xland[...].astype(BF16)

        @pl.when(e + 1 < n_exp)
        def _():
            issue_gather(idxn_ref)

    x = xbf[...]
    a = jnp.dot(x, wg_ref[0, 0].astype(BF16), preferred_element_type=F32)
    u = jnp.dot(x, wu_ref[0, 0].astype(BF16), preferred_element_type=F32)
    h1 = (jax.nn.silu(a) * u).astype(BF16)
    wd = wd_ref[0, 0].astype(BF16)
    half = wd.shape[1] // 2
    for c0 in (0, half):
        y = jnp.dot(h1, wd[:, c0:c0 + half], preferred_element_type=F32)

        @pl.when(f == 0)
        def _():
            acc[:, c0:c0 + half] = y

        @pl.when(f > 0)
        def _():
            acc[:, c0:c0 + half] += y

    @pl.when(f == nf - 1)
    def _():
        o_ref[...] = (acc[...].reshape(o_ref.shape) * g_ref[...]).astype(o_ref.dtype)


def moe_ffn(idx3, hp, g, w_gate, w_up, w_down, layer, T, *, tf=256):
    B, E, C = idx3.shape
    D = hp.shape[1]
    F = w_gate.shape[-1]
    nf = F // tf
    ids = idx3.transpose(1, 0, 2).reshape(E, 1, B * C)
    smem_blk = lambda imap: pl.BlockSpec((1, 1, B * C), imap, memory_space=pltpu.SMEM)
    return pl.pallas_call(
        functools.partial(_moe_ffn_kernel, nf=nf, n_exp=E, cap=C, tokens_per_batch=T),
        grid=(E, nf),
        in_specs=[smem_blk(lambda e, f: (e, 0, 0)),
                  smem_blk(lambda e, f: (jnp.minimum(e + 1, E - 1), 0, 0)),
                  pl.BlockSpec(memory_space=pl.ANY),
                  pl.BlockSpec((1, 1, D, tf), lambda e, f: (layer, e, 0, f)),
                  pl.BlockSpec((1, 1, D, tf), lambda e, f: (layer, e, 0, f)),
                  pl.BlockSpec((1, 1, tf, D), lambda e, f: (layer, e, f, 0)),
                  pl.BlockSpec((B, 1, C, 1), lambda e, f: (0, e, 0, 0))],
        out_specs=pl.BlockSpec((B, 1, C, D), lambda e, f: (0, e, 0, 0)),
        out_shape=jax.ShapeDtypeStruct((B, E, C, D), BF16),
        scratch_shapes=[pltpu.VMEM((B * C, D), F32),
                        pltpu.VMEM((B * C, D), BF16),
                        pltpu.VMEM((B * C, D), F32),
                        pltpu.SemaphoreType.DMA((1,))],
        compiler_params=_params("arbitrary", "arbitrary"),
    )(ids, ids, hp, w_gate, w_up, w_down, g)


def _moe_combine_kernel(starts_ref, ntr_ref, y_hbm, post_ref, x_ref, gate_ref, o_ref, ybuf, yext, acc, sem, sem_ext,
                        *, n_exp, cap, win, steps_per_batch):
    s = pl.program_id(0)
    slot = s % 2

    def window(step, e, k):
        nominal = starts_ref[step * n_exp + e] + k * win
        return nominal, pl.multiple_of(jnp.minimum(nominal, cap - win), 16)

    def window_copy(step, e, k, dst, dsem):
        _, actual = window(step, e, k)
        return pltpu.make_async_copy(y_hbm.at[step // steps_per_batch, e, pl.ds(actual, win)],
                                     dst.at[pl.ds(e * win, win)], dsem)

    def onehot(step, k):
        post = post_ref[0]
        lane = lax.broadcasted_iota(jnp.int32, (1, win), 1)
        cols = []
        for e in range(n_exp):
            nominal, actual = window(step, e, k)
            slots = actual + lane
            slots = jnp.where(slots >= nominal, slots, -2)
            cols.append(jnp.where(post[:, e:e + 1] == slots, 1.0, 0.0).astype(BF16))
        return jnp.concatenate(cols, axis=1)

    @pl.when(s == 0)
    def _():
        for e in range(n_exp):
            window_copy(0, e, 0, ybuf.at[0], sem.at[0]).start()

    for e in range(n_exp):
        window_copy(s, e, 0, ybuf.at[slot], sem.at[slot]).wait()

    @pl.when(s + 1 < pl.num_programs(0))
    def _():
        for e in range(n_exp):
            window_copy(s + 1, e, 0, ybuf.at[1 - slot], sem.at[1 - slot]).start()

    acc[...] = jnp.dot(onehot(s, 0), ybuf[slot], preferred_element_type=F32)

    def extra(k, carry):
        for e in range(n_exp):
            window_copy(s, e, k, yext, sem_ext.at[0]).start()
        for e in range(n_exp):
            window_copy(s, e, k, yext, sem_ext.at[0]).wait()
        acc[...] += jnp.dot(onehot(s, k), yext[...], preferred_element_type=F32)
        return carry

    lax.fori_loop(1, ntr_ref[s], extra, 0)
    o_ref[...] = x_ref[...] + gate_ref[0] * acc[...]


def moe_combine(starts, ntr, y, post, x2, gate, *, tj=256, win=128):
    B, E, C, D = y.shape
    M = x2.shape[0]
    T = M // B
    spb = T // tj
    grid_spec = pltpu.PrefetchScalarGridSpec(
        num_scalar_prefetch=2,
        grid=(M // tj,),
        in_specs=[pl.BlockSpec(memory_space=pl.ANY),
                  pl.BlockSpec((1, tj, E), lambda s, *_: (s // spb, s % spb, 0)),
                  pl.BlockSpec((tj, D), lambda s, *_: (s, 0)),
                  pl.BlockSpec((1, 1, D), lambda s, *_: (s // spb, 0, 0))],
        out_specs=pl.BlockSpec((tj, D), lambda s, *_: (s, 0)),
        scratch_shapes=[pltpu.VMEM((2, E * win, D), BF16),
                        pltpu.VMEM((E * win, D), BF16),
                        pltpu.VMEM((tj, D), F32),
                        pltpu.SemaphoreType.DMA((2,)),
                        pltpu.SemaphoreType.DMA((1,))],
    )
    return pl.pallas_call(
        functools.partial(_moe_combine_kernel, n_exp=E, cap=C, win=win, steps_per_batch=spb),
        grid_spec=grid_spec,
        out_shape=jax.ShapeDtypeStruct((M, D), F32),
        compiler_params=_params("arbitrary"),
    )(starts, ntr, y, post, x2, gate)


def _small_rmsnorm(x, g):
    return x * lax.rsqrt(jnp.mean(x * x, axis=-1, keepdims=True) + NORM_EPS) * g


def _rope_tables(rows):
    row = jnp.broadcast_to(jnp.arange(rows)[:, None], (rows, GRID_W)).reshape(-1).astype(F32)
    col = jnp.broadcast_to(jnp.arange(GRID_W)[None, :], (rows, GRID_W)).reshape(-1).astype(F32)
    n_freq = MLA_ROPE_DIM // 4
    inv = ROPE_THETA ** (-jnp.arange(n_freq, dtype=F32) / n_freq)
    ang = jnp.concatenate([row[:, None] * inv, col[:, None] * inv], axis=-1)
    return jnp.cos(ang), jnp.sin(ang)


def _ada(cvec3, ada_w, ada_b, layer):
    R = cvec3.shape[0]
    a = jnp.concatenate([jax.nn.silu(cvec3), jnp.zeros((8 - R, cvec3.shape[1]), F32)], axis=0)
    m = matmul(a, ada_w, b_index=layer, tm=8, tn=512)[:R] + ada_b[layer]
    return m.reshape(R, 6, D_MODEL)


def _q_proj_kernel(cq_ref, w_ref, cos_ref, sin_ref, o_ref):
    acc = jnp.dot(cq_ref[...], w_ref[...], preferred_element_type=F32)
    o_ref[0, 0] = (acc[:, :QK_PAD] * cos_ref[...] + acc[:, QK_PAD:] * sin_ref[...]).astype(o_ref.dtype)


def q_proj(cqn, w_q2, cosq, sinq, B, L, *, tm=512):
    M, R = cqn.shape
    tpb = L // tm
    return pl.pallas_call(
        _q_proj_kernel,
        grid=(M // tm, MLA_HEADS),
        in_specs=[pl.BlockSpec((tm, R), lambda i, h: (i, 0)),
                  pl.BlockSpec((R, 2 * QK_PAD), lambda i, h: (0, h)),
                  pl.BlockSpec((tm, QK_PAD), lambda i, h: (i % tpb, 0)),
                  pl.BlockSpec((tm, QK_PAD), lambda i, h: (i % tpb, 0))],
        out_specs=pl.BlockSpec((1, 1, tm, QK_PAD), lambda i, h: (i // tpb, h, i % tpb, 0)),
        out_shape=jax.ShapeDtypeStruct((B, MLA_HEADS, L, QK_PAD), BF16),
        compiler_params=_params("parallel", "parallel"),
    )(cqn, w_q2, cosq, sinq)


def _q_weights(w_uq):
    R = w_uq.shape[0]
    half = MLA_ROPE_DIM // 2
    w = w_uq.reshape(R, MLA_HEADS, MLA_NOPE_DIM + MLA_ROPE_DIM)
    nope, rope = w[..., :MLA_NOPE_DIM], w[..., MLA_NOPE_DIM:]
    rot = jnp.concatenate([-rope[..., half:], rope[..., :half]], axis=-1)
    pad = jnp.zeros((R, MLA_HEADS, QK_PAD - MLA_NOPE_DIM - MLA_ROPE_DIM), w.dtype)
    parts = [nope, rope, pad, jnp.zeros_like(nope), rot, pad]
    return jnp.concatenate(parts, axis=-1).reshape(R, MLA_HEADS * 2 * QK_PAD).astype(BF16)


def _krope_t_kernel(w_ref, h_ref, *rest, roped):
    o_ref = rest[-1]
    x = lax.dot_general(w_ref[...], h_ref[...], (((1,), (1,)), ((), ())), preferred_element_type=F32)
    if roped:
        half = x.shape[0] // 2
        x1, x2, c, s = x[:half], x[half:], rest[0][...], rest[1][...]
        x = jnp.concatenate([x1 * c - x2 * s, x2 * c + x1 * s], axis=0)
    o_ref[0] = x.astype(o_ref.dtype)


def krope_t(h2, w_kr_t, B, Lx, tt, cos_t=None, sin_t=None):
    D = h2.shape[1]
    R = w_kr_t.shape[0]
    per = Lx // tt
    roped = cos_t is not None
    in_specs = [pl.BlockSpec((R, D), lambda b, i: (0, 0)), pl.BlockSpec((tt, D), lambda b, i: (b * per + i, 0))]
    args = [w_kr_t, h2]
    if roped:
        in_specs += [pl.BlockSpec((R // 2, tt), lambda b, i: (0, i))] * 2
        args += [cos_t, sin_t]
    return pl.pallas_call(
        functools.partial(_krope_t_kernel, roped=roped),
        grid=(B, per),
        in_specs=in_specs,
        out_specs=pl.BlockSpec((1, R, tt), lambda b, i: (b, 0, i)),
        out_shape=jax.ShapeDtypeStruct((B, R, Lx), BF16),
        compiler_params=_params("parallel", "parallel"),
    )(*args)


def _kv_proj_kernel(c_ref, wn_ref, wv_ref, kr_ref, kt_ref, v_ref):
    c = c_ref[0]
    kn_t = lax.dot_general(wn_ref[0], c, (((1,), (1,)), ((), ())), preferred_element_type=F32)
    rope_end = MLA_NOPE_DIM + MLA_ROPE_DIM
    kt_ref[0, 0, :MLA_NOPE_DIM] = kn_t.astype(kt_ref.dtype)
    kt_ref[0, 0, MLA_NOPE_DIM:rope_end] = kr_ref[0]
    kt_ref[0, 0, rope_end:] = jnp.zeros((QK_PAD - rope_end, c.shape[0]), kt_ref.dtype)
    v_ref[0, 0] = jnp.dot(c, wv_ref[0], preferred_element_type=F32).astype(v_ref.dtype)


def kv_proj(ckvn, wn_t, wv, kr_t, *, tk=2176):
    B, K, R = ckvn.shape
    H, V = wv.shape[0], wv.shape[2]
    return pl.pallas_call(
        _kv_proj_kernel,
        grid=(B, H, K // tk),
        in_specs=[pl.BlockSpec((1, tk, R), lambda b, h, i: (b, i, 0)),
                  pl.BlockSpec((1, MLA_NOPE_DIM, R), lambda b, h, i: (h, 0, 0)),
                  pl.BlockSpec((1, R, V), lambda b, h, i: (h, 0, 0)),
                  pl.BlockSpec((1, MLA_ROPE_DIM, tk), lambda b, h, i: (b, 0, i))],
        out_specs=[pl.BlockSpec((1, 1, QK_PAD, tk), lambda b, h, i: (b, h, 0, i)),
                   pl.BlockSpec((1, 1, tk, V), lambda b, h, i: (b, h, i, 0))],
        out_shape=[jax.ShapeDtypeStruct((B, H, QK_PAD, K), BF16), jax.ShapeDtypeStruct((B, H, K, V), BF16)],
        compiler_params=_params("parallel", "parallel", "parallel"),
    )(ckvn, wn_t, wv, kr_t)


def _mla_layer(h, h_c, B, L, Lc, w_in, g_q, g_kv, w_uq, w_ukv):
    H = MLA_HEADS
    lat = MLA_Q_RANK + MLA_KV_RANK
    w_in_b = w_in.astype(BF16)
    proj = matmul(h, w_in_b[:, :lat], tn=lat)
    ckv_c = matmul(h_c, w_in_b[:, MLA_Q_RANK:lat], tn=MLA_KV_RANK)
    ckv = jnp.concatenate([proj[:, MLA_Q_RANK:].reshape(B, L, -1), ckv_c.reshape(B, Lc, -1)], axis=1)
    cos, sin = _rope_tables(L // GRID_W)
    ones, zeros = jnp.ones((L, MLA_NOPE_DIM), F32), jnp.zeros((L, MLA_NOPE_DIM), F32)
    pad = jnp.zeros((L, QK_PAD - MLA_NOPE_DIM - MLA_ROPE_DIM), F32)
    cosq = jnp.concatenate([ones, cos, cos, pad], axis=1)
    sinq = jnp.concatenate([zeros, sin, sin, pad], axis=1)
    qp = q_proj(_small_rmsnorm(proj[:, :MLA_Q_RANK], g_q).astype(BF16), _q_weights(w_uq), cosq, sinq, B, L)
    w_kr_t = w_in_b[:, lat:].T
    kr_t = jnp.concatenate([krope_t(h, w_kr_t, B, L, 1024, cos.T, sin.T), krope_t(h_c, w_kr_t, B, Lc, Lc)], axis=2)
    w_kv = w_ukv.astype(BF16).reshape(MLA_KV_RANK, H, MLA_NOPE_DIM + MLA_V_DIM)
    wn_t = w_kv[..., :MLA_NOPE_DIM].transpose(1, 2, 0)
    wv = w_kv[..., MLA_NOPE_DIM:].transpose(1, 0, 2)
    kt, v = kv_proj(_small_rmsnorm(ckv, g_kv).astype(BF16), wn_t, wv, kr_t)
    scale = (MLA_NOPE_DIM + MLA_ROPE_DIM) ** -0.5
    return attention(qp, kt, v, scale=scale)


def _fft_tables(L):
    n = 2 * L
    n2_in = L // FFT_N1
    k2 = np.arange(FFT_K2)
    th = 2 * np.pi * ((k2[:, None] * np.arange(n2_in)[None, :]) % FFT_N2) / FFT_N2
    f1 = np.zeros((2 * FFT_K2P, n2_in))
    f1[:FFT_K2] = np.cos(th)
    f1[FFT_K2P:FFT_K2P + FFT_K2] = -np.sin(th)
    wgt = np.where((k2 == 0) | (k2 == FFT_N2 // 2), 1.0, 2.0)
    g = np.zeros((n2_in, 2 * FFT_K2P))
    g[:, :FFT_K2] = (wgt[:, None] * np.cos(th)).T / n
    g[:, FFT_K2P:FFT_K2P + FFT_K2] = -(wgt[:, None] * np.sin(th)).T / n
    k1 = np.arange(FFT_N1)[:, None]
    n1 = np.arange(FFT_N1)[None, :]
    m2 = np.zeros((FFT_K2, 2 * FFT_N1, 2 * FFT_N1))
    for kk in range(FFT_K2):
        phi = 2 * np.pi * ((n1 * k1 * FFT_N2 + n1 * kk) % n) / n
        tr, ti = np.cos(phi), -np.sin(phi)
        m2[kk] = np.block([[tr, -ti], [ti, tr]])
    as_bf16 = lambda a: jnp.asarray(a, dtype=F32).astype(BF16)
    eye = np.eye(FFT_GROUP)
    return (as_bf16(np.kron(f1, eye)), as_bf16(np.kron(g, eye)), as_bf16(m2),
            as_bf16(np.transpose(m2, (0, 2, 1))))


def _fft_s1_kernel(f_ref, z_ref, o_ref):
    _, n2_in, g, ct = z_ref.shape
    z = z_ref[0].reshape(n2_in * g, ct).astype(BF16)
    a = jnp.dot(f_ref[...], z, preferred_element_type=F32)
    o_ref[0] = a.reshape(o_ref.shape[1:]).astype(o_ref.dtype)


def fft_stage1(z4, f1k, *, lead_off, nb, ct=2048):
    _, n2_in, _, C = z4.shape
    g = FFT_GROUP
    return pl.pallas_call(
        _fft_s1_kernel,
        grid=(nb, FFT_N1 // g, C // ct),
        in_specs=[pl.BlockSpec(f1k.shape, lambda b, i, j: (0, 0)),
                  pl.BlockSpec((1, n2_in, g, ct), lambda b, i, j: (lead_off + b, 0, i, j))],
        out_specs=pl.BlockSpec((1, 2, FFT_K2P, g, ct), lambda b, i, j: (b, 0, 0, i, j)),
        out_shape=jax.ShapeDtypeStruct((nb, 2, FFT_K2P, FFT_N1, C), BF16),
        compiler_params=_params("parallel", "parallel", "parallel"),
    )(f1k, z4)


def _fft_s2_kernel(a_ref, m_ref, mt_ref, kf_ref, o_ref):
    k2 = pl.program_id(0)

    @pl.when(k2 < FFT_K2)
    def _():
        ct = a_ref.shape[-1]
        x = jnp.dot(m_ref[0], a_ref[...].reshape(2 * FFT_N1, ct), preferred_element_type=F32)
        xr, xi = x[:FFT_N1], x[FFT_N1:]
        kr, ki = kf_ref[0, 0, 0], kf_ref[0, 1, 0]
        p = jnp.concatenate([xr * kr - xi * ki, xr * ki + xi * kr], axis=0).astype(BF16)
        e = jnp.dot(mt_ref[0], p, preferred_element_type=F32)
        o_ref[...] = e.astype(o_ref.dtype).reshape(o_ref.shape)

    @pl.when(k2 >= FFT_K2)
    def _():
        o_ref[...] = jnp.zeros(o_ref.shape, o_ref.dtype)


def fft_stage2(a5, m2, m2t, kf, order):
    B, _, _, _, D = a5.shape
    last = FFT_K2 - 1
    blk = (1, 2, 1, FFT_N1, D)
    return pl.pallas_call(
        _fft_s2_kernel,
        grid=(FFT_K2P, B),
        in_specs=[pl.BlockSpec(blk, lambda k, b: (b, 0, jnp.minimum(k, last), 0, 0)),
                  pl.BlockSpec((1, 2 * FFT_N1, 2 * FFT_N1), lambda k, b: (jnp.minimum(k, last), 0, 0)),
                  pl.BlockSpec((1, 2 * FFT_N1, 2 * FFT_N1), lambda k, b: (jnp.minimum(k, last), 0, 0)),
                  pl.BlockSpec(blk, lambda k, b: (order, 0, jnp.minimum(k, last), 0, 0))],
        out_specs=pl.BlockSpec(blk, lambda k, b: (b, 0, k, 0, 0)),
        out_shape=jax.ShapeDtypeStruct(a5.shape, BF16),
        compiler_params=_params("parallel", "arbitrary"),
    )(a5, m2, m2t, kf)


def _fft_f2_kernel(f_ref, g_ref, m_ref, inv_ref, o_ref):
    k2 = pl.program_id(0)

    @pl.when(k2 < FFT_K2)
    def _():
        ct = f_ref.shape[-1]
        xf = jnp.dot(m_ref[0], f_ref[...].reshape(2 * FFT_N1, ct), preferred_element_type=F32)
        xg = jnp.dot(m_ref[0], g_ref[...].reshape(2 * FFT_N1, ct), preferred_element_type=F32)
        inv = inv_ref[0]
        o_ref[0, 0, 0] = (xf[:FFT_N1] + xg[:FFT_N1]) * inv
        o_ref[0, 1, 0] = (xf[FFT_N1:] - xg[FFT_N1:]) * inv

    @pl.when(k2 >= FFT_K2)
    def _():
        o_ref[...] = jnp.zeros(o_ref.shape, o_ref.dtype)


def fft_filter_spectrum(a4, m2, inv, D):
    last = FFT_K2 - 1
    blk = (2, 1, FFT_N1, D)
    return pl.pallas_call(
        _fft_f2_kernel,
        grid=(FFT_K2P, HYENA_ORDER),
        in_specs=[pl.BlockSpec(blk, lambda k, o: (0, jnp.minimum(k, last), 0, 2 * o)),
                  pl.BlockSpec(blk, lambda k, o: (0, jnp.minimum(k, last), 0, 2 * o + 1)),
                  pl.BlockSpec((1, 2 * FFT_N1, 2 * FFT_N1), lambda k, o: (jnp.minimum(k, last), 0, 0)),
                  pl.BlockSpec((1, 1, D), lambda k, o: (o, 0, 0))],
        out_specs=pl.BlockSpec((1, 2, 1, FFT_N1, D), lambda k, o: (o, 0, k, 0, 0)),
        out_shape=jax.ShapeDtypeStruct((HYENA_ORDER, 2, FFT_K2P, FFT_N1, D), F32),
        compiler_params=_params("arbitrary", "parallel"),
    )(a4, a4, m2, inv)


def _fft_s3_kernel(g_ref, e_ref, z_ref, x_ref, skip_ref, o_ref):
    _, n2_in, g, ct = z_ref.shape
    e = e_ref[0].reshape(2 * FFT_K2P * g, ct)
    y = jnp.dot(g_ref[...], e, preferred_element_type=F32).reshape(n2_in, g, ct)
    o_ref[0] = (x_ref[0] * (y + skip_ref[...] * z_ref[0])).astype(o_ref.dtype)


def fft_stage3(e5, gk, z4, z_off, x4, x_off, skip, *, out_dtype):
    B, _, _, _, D = e5.shape
    n2_in = z4.shape[1]
    g = FFT_GROUP
    return pl.pallas_call(
        _fft_s3_kernel,
        grid=(B, FFT_N1 // g),
        in_specs=[pl.BlockSpec(gk.shape, lambda b, i: (0, 0)),
                  pl.BlockSpec((1, 2, FFT_K2P, g, D), lambda b, i: (b, 0, 0, i, 0)),
                  pl.BlockSpec((1, n2_in, g, D), lambda b, i: (z_off + b, 0, i, 0)),
                  pl.BlockSpec((1, n2_in, g, D), lambda b, i: (x_off + b, 0, i, 0)),
                  pl.BlockSpec((1, D), lambda b, i: (0, 0))],
        out_specs=pl.BlockSpec((1, n2_in, g, D), lambda b, i: (b, 0, i, 0)),
        out_shape=jax.ShapeDtypeStruct((B, n2_in, FFT_N1, D), out_dtype),
        compiler_params=_params("parallel", "parallel"),
    )(gk, e5, z4, x4, skip.reshape(1, D))


def _hyin_kernel(ap_ref, a_ref, an_ref, b_ref, cw_ref, cb_ref, o_ref, *, tiles_per_seq):
    i = pl.program_id(0)
    t = i % tiles_per_seq
    halo = ap_ref.shape[0]
    tm, tn = o_ref.shape[1:]
    cw = MXU_DIM
    row = lax.broadcasted_iota(jnp.int32, (tm, cw), 0)
    for c0 in range(0, tn, cw):
        b = b_ref[:, c0:c0 + cw]
        acc = jnp.dot(a_ref[...], b, preferred_element_type=F32)
        prev = jnp.dot(ap_ref[...], b, preferred_element_type=F32)[halo - 1:halo]
        nxt = jnp.dot(an_ref[...], b, preferred_element_type=F32)[0:1]
        prev = jnp.where(t == 0, 0.0, prev)
        nxt = jnp.where(t == tiles_per_seq - 1, 0.0, nxt)
        up = jnp.where(row == 0, prev, pltpu.roll(acc, 1, axis=0))
        dn = jnp.where(row == tm - 1, nxt, pltpu.roll(acc, tm - 1, axis=0))
        w = cw_ref[:, c0:c0 + cw]
        o_ref[0, :, c0:c0 + cw] = up * w[0:1] + acc * w[1:2] + dn * w[2:3] + cb_ref[:, c0:c0 + cw]


def hyena_in(h, w_in, conv_w, conv_b, L, *, tm=512, tn=1024, halo=16):
    M, K = h.shape
    N = w_in.shape[1]
    D = N // 3
    per = D // tn
    return pl.pallas_call(
        functools.partial(_hyin_kernel, tiles_per_seq=L // tm),
        grid=(M // tm, N // tn),
        in_specs=[pl.BlockSpec((halo, K), lambda i, j: (jnp.maximum(i * (tm // halo) - 1, 0), 0)),
                  pl.BlockSpec((tm, K), lambda i, j: (i, 0)),
                  pl.BlockSpec((halo, K), lambda i, j: (jnp.minimum((i + 1) * (tm // halo), M // halo - 1), 0)),
                  pl.BlockSpec((K, tn), lambda i, j: (0, j)),
                  pl.BlockSpec((3, tn), lambda i, j: (0, j)),
                  pl.BlockSpec((1, tn), lambda i, j: (0, j))],
        out_specs=pl.BlockSpec((1, tm, tn), lambda i, j: (j // per, i, j % per)),
        out_shape=jax.ShapeDtypeStruct((3, M, D), F32),
        compiler_params=_params("parallel", "parallel"),
    )(h, h, h, w_in, conv_w, conv_b.reshape(1, N))


def _taps_kernel(a_ref, w_ref, b_ref, dl_ref, keep_ref, o_ref, ss_ref, *, seq_len):
    i = pl.program_id(1)
    tm = a_ref.shape[0]
    acc = jnp.dot(a_ref[...].astype(BF16), w_ref[...].astype(BF16), preferred_element_type=F32) + b_ref[...]
    row = i * tm + lax.broadcasted_iota(jnp.int32, acc.shape, 0)
    t = row.astype(F32) / seq_len
    hf = acc * jnp.exp(-t * dl_ref[...])
    hf = jnp.where(row == 0, hf * keep_ref[...], hf)
    o_ref[...] = hf.astype(o_ref.dtype)
    part = jnp.sum(hf * hf, axis=0, keepdims=True)

    @pl.when(i == 0)
    def _():
        ss_ref[...] = part

    @pl.when(i > 0)
    def _():
        ss_ref[...] += part


def _hyena_filter_taps(L, w1, b1, w2, b2, w3, b3, freq, *, tm=512, tn=1024):
    D = D_MODEL
    N = 2 * HYENA_ORDER * D
    t = jnp.arange(L, dtype=F32) / L
    w = 2 * math.pi * jnp.arange(L, dtype=F32) / L
    bands = jnp.linspace(1e-4, HYENA_BANDS - 1, HYENA_BANDS, dtype=F32)
    ang = w[:, None] * bands[None, :]
    z = jnp.concatenate([t[:, None], jnp.cos(ang), -jnp.sin(ang)], axis=-1)
    a = jnp.sin(freq * (z @ w1 + b1))
    a = jnp.sin(freq * (a @ w2 + b2))
    max_decay = math.log(HYENA_TARGET) / HYENA_FAST_PCT
    min_decay = math.log(HYENA_TARGET) / HYENA_SLOW_PCT
    decay = jnp.tile(jnp.abs(jnp.linspace(min_decay, max_decay, D, dtype=F32)), 2 * HYENA_ORDER).reshape(1, N)
    keep0 = jnp.where((jnp.arange(N) // D) % 2 == 1, 0.0, 1.0).astype(F32).reshape(1, N)
    K = a.shape[1]
    taps, ss = pl.pallas_call(
        functools.partial(_taps_kernel, seq_len=L),
        grid=(N // tn, L // tm),
        in_specs=[pl.BlockSpec((tm, K), lambda j, i: (i, 0)),
                  pl.BlockSpec((K, tn), lambda j, i: (0, j)),
                  pl.BlockSpec((1, tn), lambda j, i: (0, j)),
                  pl.BlockSpec((1, tn), lambda j, i: (0, j)),
                  pl.BlockSpec((1, tn), lambda j, i: (0, j))],
        out_specs=[pl.BlockSpec((tm, tn), lambda j, i: (i, j)), pl.BlockSpec((1, tn), lambda j, i: (0, j))],
        out_shape=[jax.ShapeDtypeStruct((L, N), BF16), jax.ShapeDtypeStruct((1, N), F32)],
        compiler_params=_params("parallel", "arbitrary"),
    )(a, w3, b3.reshape(1, N), decay, keep0)
    ss = ss.reshape(HYENA_ORDER, 2, D).sum(axis=1, keepdims=True)
    return taps, lax.rsqrt(ss + NORM_EPS)


def _hyena_layer(h, B, L, w_in, conv_w, conv_b, f_w1, f_b1, f_w2, f_b2, f_w3, f_b3, f_freq, skip):
    D = D_MODEL
    n2_in = L // FFT_N1
    f1k, gk, m2, m2t = _fft_tables(L)
    taps, inv = _hyena_filter_taps(L, f_w1, f_b1, f_w2, f_b2, f_w3, f_b3, f_freq)
    af = fft_stage1(taps.reshape(1, n2_in, FFT_N1, taps.shape[1]), f1k, lead_off=0, nb=1)
    kf = fft_filter_spectrum(af[0], m2, inv, D)
    u4 = hyena_in(h, w_in.astype(BF16), conv_w, conv_b, L).reshape(3 * B, n2_in, FFT_N1, D)
    z4 = u4
    for order in range(HYENA_ORDER):
        a = fft_stage1(z4, f1k, lead_off=0, nb=B)
        e = fft_stage2(a, m2, m2t, kf, order)
        last = order == HYENA_ORDER - 1
        z4 = fft_stage3(e, gk, z4, 0, u4, (order + 1) * B, skip[order], out_dtype=BF16 if last else F32)
    return z4.reshape(B * L, D)


def _moe_layer(x2, norm_gain, mod, B, T, w_router, w_gate, w_up, w_down, layer, *, tj=256, win=128):
    E = N_EXPERTS
    cap = EC_CAPACITY_FACTOR * T // E
    hp, aff = norm_route(x2, norm_gain, mod[:, 3:4], mod[:, 4:5], w_router.T.astype(BF16), rows_per_batch=T)
    pos, idx, g = route_select(aff, cap)
    idx3 = idx.reshape(B, E, cap)
    y = moe_ffn(idx3, hp, g, w_gate, w_up, w_down, layer, T)
    nj = T // tj
    edges = jnp.arange(nj + 1, dtype=jnp.int32) * tj
    first = jnp.sum((idx3[..., None] < edges).astype(jnp.int32), axis=2)
    nominal = (first[..., :nj] // 16) * 16
    ntr = jnp.maximum(jnp.max((first[..., 1:] - nominal + win - 1) // win, axis=1), 1)
    starts = nominal.transpose(0, 2, 1).reshape(-1)
    return moe_combine(starts, ntr.reshape(-1), y, pos.transpose(0, 2, 1), x2, mod[:, 5:6], tj=tj, win=win)


def kernel(x, c, ctx, c_ctx, ada_w, ada_b, norm_g, final_g, mla_w_in, mla_g_q, mla_g_kv, mla_w_uq, mla_w_ukv, mla_w_o, hy_w_in, hy_conv_w, hy_conv_b, hy_f_w1, hy_f_b1, hy_f_w2, hy_f_b2, hy_f_w3, hy_f_b3, hy_f_freq, hy_skip, hy_w_out, moe_w_router, moe_w_gate, moe_w_up, moe_w_down):
    B, L, D = x.shape
    Lc = ctx.shape[1]
    x2 = x.reshape(B * L, D)
    ctx2 = ctx.reshape(B * Lc, D)

    mod3 = _ada(jnp.concatenate([c, c_ctx[None]], axis=0), ada_w, ada_b, 0)
    mod, mod_c = mod3[:B], jnp.broadcast_to(mod3[B:], (B, 6, D))
    h = norm_mod(x2, norm_g[0, 0], mod[:, 0:1], mod[:, 1:2], rows_per_batch=L)
    h_c = norm_mod(ctx2, norm_g[0, 0], mod_c[:, 0:1], mod_c[:, 1:2], rows_per_batch=Lc)
    o = _mla_layer(h, h_c, B, L, Lc, mla_w_in[0], mla_g_q[0], mla_g_kv[0], mla_w_uq[0], mla_w_ukv[0])
    x2 = matmul(o.reshape(B * L, -1), mla_w_o[0].astype(BF16), tn=1024,
                res=x2, gate=mod[:, 2:3], rows_per_batch=L)
    x2 = _moe_layer(x2, norm_g[0, 1], mod, B, L, moe_w_router[0], moe_w_gate, moe_w_up, moe_w_down, 0)

    mod = _ada(c, ada_w, ada_b, 1)
    h = norm_mod(x2, norm_g[1, 0], mod[:, 0:1], mod[:, 1:2], rows_per_batch=L)
    z = _hyena_layer(h, B, L, hy_w_in[0], hy_conv_w[0], hy_conv_b[0], hy_f_w1[0], hy_f_b1[0], hy_f_w2[0],
                     hy_f_b2[0], hy_f_w3[0], hy_f_b3[0], hy_f_freq[0], hy_skip[0])
    x2 = matmul(z, hy_w_out[0].astype(BF16), tn=1024, res=x2, gate=mod[:, 2:3], rows_per_batch=L)
    x2 = _moe_layer(x2, norm_g[1, 1], mod, B, L, moe_w_router[1], moe_w_gate, moe_w_up, moe_w_down, 1)

    return norm_mod(x2, final_g, out_dtype=F32).reshape(B, L, D)
```

```python
import functools
import math

import jax
import jax.numpy as jnp
import numpy as np
from jax import lax
from jax.experimental import pallas as pl
from jax.experimental.pallas import tpu as pltpu

F32 = jnp.float32
BF16 = jnp.bfloat16

D_MODEL = 2048
GRID_W = 64
NORM_EPS = 1e-6
MLA_HEADS = 16
MLA_Q_RANK = 512
MLA_KV_RANK = 256
MLA_NOPE_DIM = 128
MLA_ROPE_DIM = 64
MLA_V_DIM = 128
ROPE_THETA = 10000.0
HYENA_ORDER = 2
HYENA_BANDS = 16
HYENA_TARGET = 1e-2
HYENA_FAST_PCT = 0.3
HYENA_SLOW_PCT = 1.5
N_EXPERTS = 16
EC_CAPACITY_FACTOR = 2
FFT_N1 = 128
FFT_N2 = 64
FFT_K2 = FFT_N2 // 2 + 1
FFT_K2P = 40
FFT_GROUP = 16

V7X_VMEM_LIMIT_BYTES = 56 * 1024 * 1024
LANES = 128
MXU_DIM = 256
QK_PAD = MXU_DIM


def _params(*sem):
    return pltpu.CompilerParams(dimension_semantics=sem, vmem_limit_bytes=V7X_VMEM_LIMIT_BYTES)


def _mm_kernel(a_ref, b_ref, *rest, b_is_stacked, has_res):
    o_ref = rest[-1]
    b = b_ref[0] if b_is_stacked else b_ref[...]
    acc = jnp.dot(a_ref[...].astype(BF16), b.astype(BF16), preferred_element_type=F32)
    if has_res:
        res_ref, gate_ref = rest[0], rest[1]
        acc = res_ref[...] + gate_ref[0] * acc
    o_ref[...] = acc.astype(o_ref.dtype)


def matmul(a, b, *, b_index=None, out_dtype=F32, tm=512, tn=512, res=None, gate=None, rows_per_batch=None):
    M, K = a.shape
    N = b.shape[-1]
    tm, tn = min(tm, M), min(tn, N)
    assert M % tm == 0 and N % tn == 0, (M, N, tm, tn)
    stacked = b.ndim == 3
    if stacked:
        b_spec = pl.BlockSpec((1, K, tn), lambda i, j: (b_index, 0, j))
    else:
        b_spec = pl.BlockSpec((K, tn), lambda i, j: (0, j))
    in_specs = [pl.BlockSpec((tm, K), lambda i, j: (i, 0)), b_spec]
    args = [a, b]
    if res is not None:
        assert rows_per_batch % tm == 0
        tpb = rows_per_batch // tm
        in_specs += [pl.BlockSpec((tm, tn), lambda i, j: (i, j)),
                     pl.BlockSpec((1, 1, tn), lambda i, j: (i // tpb, 0, j))]
        args += [res, gate]
    return pl.pallas_call(
        functools.partial(_mm_kernel, b_is_stacked=stacked, has_res=res is not None),
        grid=(M // tm, N // tn),
        in_specs=in_specs,
        out_specs=pl.BlockSpec((tm, tn), lambda i, j: (i, j)),
        out_shape=jax.ShapeDtypeStruct((M, N), out_dtype),
        compiler_params=_params("parallel", "parallel"),
    )(*args)


def _norm_kernel(x_ref, g_ref, *rest, modulated):
    o_ref = rest[-1]
    x = x_ref[...]
    y = x * lax.rsqrt(jnp.mean(x * x, axis=-1, keepdims=True) + NORM_EPS) * g_ref[...]
    if modulated:
        shift_ref, scale_ref = rest[0], rest[1]
        y = y * (1 + scale_ref[0]) + shift_ref[0]
    o_ref[...] = y.astype(o_ref.dtype)


def norm_mod(x2, g, shift=None, scale=None, *, rows_per_batch=None, out_dtype=BF16, tm=256):
    M, D = x2.shape
    tm = min(tm, M)
    in_specs = [pl.BlockSpec((tm, D), lambda i: (i, 0)), pl.BlockSpec((1, D), lambda i: (0, 0))]
    args = [x2, g.reshape(1, D)]
    if shift is not None:
        tpb = rows_per_batch // tm
        in_specs += [pl.BlockSpec((1, 1, D), lambda i: (i // tpb, 0, 0))] * 2
        args += [shift, scale]
    return pl.pallas_call(
        functools.partial(_norm_kernel, modulated=shift is not None),
        grid=(M // tm,),
        in_specs=in_specs,
        out_specs=pl.BlockSpec((tm, D), lambda i: (i, 0)),
        out_shape=jax.ShapeDtypeStruct((M, D), out_dtype),
        compiler_params=_params("parallel"),
    )(*args)


def _attn_kernel(q_ref, kt_ref, v_ref, o_ref, *, chunks, c):
    q = q_ref[0, 0]
    m = l = acc = None
    for lo, hi in chunks:
        s = jnp.dot(q, kt_ref[0, 0, :, lo:hi], preferred_element_type=F32)
        m_chunk = jnp.max(s, axis=-1, keepdims=True)
        m_new = m_chunk if m is None else jnp.maximum(m, m_chunk)
        p = jnp.exp2(s * c - m_new * c)
        pv = jnp.dot(p.astype(BF16), v_ref[0, 0, lo:hi, :], preferred_element_type=F32)
        psum = jnp.sum(p, axis=-1, keepdims=True)
        if m is None:
            l, acc = psum, pv
        else:
            alpha = jnp.exp2((m - m_new) * c)
            l = alpha * l + psum
            acc = alpha * acc + pv
        m = m_new
    o_ref[0] = (acc / l).astype(o_ref.dtype)


def attention(q, kt, v, *, scale, tq=1024, tk=1024):
    B, H, L, _ = q.shape
    K = kt.shape[-1]
    V = v.shape[-1]
    chunks = tuple((lo, min(lo + tk, K)) for lo in range(0, K, tk))
    return pl.pallas_call(
        functools.partial(_attn_kernel, chunks=chunks, c=scale * math.log2(math.e)),
        grid=(B, H, L // tq),
        in_specs=[pl.BlockSpec((1, 1, tq, QK_PAD), lambda b, h, i: (b, h, i, 0)),
                  pl.BlockSpec((1, 1, QK_PAD, K), lambda b, h, i: (b, h, 0, 0)),
                  pl.BlockSpec((1, 1, K, V), lambda b, h, i: (b, h, 0, 0))],
        out_specs=pl.BlockSpec((1, tq, V), lambda b, h, i: (b, i, h)),
        out_shape=jax.ShapeDtypeStruct((B, L, H * V), BF16),
        compiler_params=_params("parallel", "parallel", "arbitrary"),
    )(q, kt, v)


def _norm_route_kernel(x_ref, g_ref, shift_ref, scale_ref, wr_ref, h_ref, aff_ref):
    x = x_ref[...]
    y = x * lax.rsqrt(jnp.mean(x * x, axis=-1, keepdims=True) + NORM_EPS) * g_ref[...]
    y = y * (1 + scale_ref[0]) + shift_ref[0]
    h_ref[...] = y
    logits = lax.dot_general(wr_ref[...], y.astype(BF16), (((1,), (1,)), ((), ())),
                             preferred_element_type=F32)
    p = jnp.exp(logits - jnp.max(logits, axis=0, keepdims=True))
    aff_ref[0] = p / jnp.sum(p, axis=0, keepdims=True)


def norm_route(x2, g, shift, scale, w_router_t, *, rows_per_batch, tm=256):
    M, D = x2.shape
    E = w_router_t.shape[0]
    tpb = rows_per_batch // tm
    return pl.pallas_call(
        _norm_route_kernel,
        grid=(M // tm,),
        in_specs=[pl.BlockSpec((tm, D), lambda i: (i, 0)),
                  pl.BlockSpec((1, D), lambda i: (0, 0)),
                  pl.BlockSpec((1, 1, D), lambda i: (i // tpb, 0, 0)),
                  pl.BlockSpec((1, 1, D), lambda i: (i // tpb, 0, 0)),
                  pl.BlockSpec((E, D), lambda i: (0, 0))],
        out_specs=[pl.BlockSpec((tm, D), lambda i: (i, 0)),
                   pl.BlockSpec((1, E, tm), lambda i: (i // tpb, 0, i % tpb))],
        out_shape=[jax.ShapeDtypeStruct((M, D), F32),
                   jax.ShapeDtypeStruct((M // rows_per_batch, E, rows_per_batch), F32)],
        compiler_params=_params("parallel"),
    )(x2, g.reshape(1, D), shift, scale, w_router_t)


def _lane_prefix_sum(x):
    E, T = x.shape
    r = lax.broadcasted_iota(jnp.int32, (LANES, LANES), 0)
    c = lax.broadcasted_iota(jnp.int32, (LANES, LANES), 1)
    tri = jnp.where(r <= c, 1.0, 0.0).astype(BF16)
    outs, carry = [], jnp.zeros((E, 1), F32)
    for j in range(T // LANES):
        seg = jnp.dot(x[:, j * LANES:(j + 1) * LANES].astype(BF16), tri, preferred_element_type=F32) + carry
        outs.append(seg)
        carry = seg[:, LANES - 1:LANES]
    return jnp.concatenate(outs, axis=1)


def _route_select_kernel(aff_ref, pos_ref, idx_ref, gate_ref, cum_s, *, cap, tch):
    aff = aff_ref[0]
    E, T = aff.shape

    def bisect(state):
        lo, hi, _ = state
        mid = 0.5 * (lo + hi)
        cnt = jnp.sum(jnp.where(aff >= mid, 1.0, 0.0), axis=1, keepdims=True)
        ge = cnt >= cap
        open_rows = jnp.where((mid > lo) & (mid < hi), 1.0, 0.0)
        return jnp.where(ge, mid, lo), jnp.where(ge, hi, mid), jnp.max(open_rows)

    lo0 = jnp.min(aff, axis=1, keepdims=True)
    hi0 = 2.0 * jnp.max(aff, axis=1, keepdims=True) + 1e-30
    lo, hi, _ = lax.while_loop(lambda st: st[2] > 0.0, bisect, (lo0, hi0, jnp.float32(1.0)))
    gt = aff >= hi
    eq = (aff >= lo) & (aff < hi)
    need = cap - jnp.sum(jnp.where(gt, 1.0, 0.0), axis=1, keepdims=True)
    eq_rank = _lane_prefix_sum(jnp.where(eq, 1.0, 0.0))
    sel = gt | (eq & (eq_rank <= need))
    cum = _lane_prefix_sum(jnp.where(sel, 1.0, 0.0))
    cumsel = jnp.where(sel, cum, 0.0)
    pos_ref[0] = (cumsel - 1.0).astype(jnp.int32)
    cum_s[0] = cum
    cum_s[1] = cumsel
    slot = lax.broadcasted_iota(jnp.int32, (cap, 1), 0).astype(F32)

    def per_expert(e, carry):
        idx_acc = jnp.zeros((cap, 1), F32)
        g_acc = jnp.zeros((cap, 1), F32)
        for lo in range(0, T, tch):
            cum_e = cum_s[0, pl.ds(e, 1), lo:lo + tch]
            cs_e = cum_s[1, pl.ds(e, 1), lo:lo + tch]
            aff_e = aff_ref[0, pl.ds(e, 1), lo:lo + tch]
            idx_acc = idx_acc + jnp.sum(jnp.where(cum_e <= slot, 1.0, 0.0), axis=1, keepdims=True)
            g_acc = g_acc + jnp.sum(jnp.where(cs_e == slot + 1.0, aff_e, 0.0), axis=1, keepdims=True)
        idx_ref[0, pl.ds(e, 1)] = idx_acc.astype(jnp.int32).reshape(1, cap, 1)
        gate_ref[0, pl.ds(e, 1)] = g_acc.reshape(1, cap, 1)
        return carry

    lax.fori_loop(0, E, per_expert, 0)


def route_select(aff, cap, *, tch=1024):
    B, E, T = aff.shape
    return pl.pallas_call(
        functools.partial(_route_select_kernel, cap=cap, tch=tch),
        grid=(B,),
        in_specs=[pl.BlockSpec((1, E, T), lambda b: (b, 0, 0))],
        out_specs=[pl.BlockSpec((1, E, T), lambda b: (b, 0, 0)),
                   pl.BlockSpec((1, E, cap, 1), lambda b: (b, 0, 0, 0)),
                   pl.BlockSpec((1, E, cap, 1), lambda b: (b, 0, 0, 0))],
        out_shape=[jax.ShapeDtypeStruct((B, E, T), jnp.int32),
                   jax.ShapeDtypeStruct((B, E, cap, 1), jnp.int32),
                   jax.ShapeDtypeStruct((B, E, cap, 1), F32)],
        scratch_shapes=[pltpu.VMEM((2, E, T), F32)],
        compiler_params=_params("parallel"),
    )(aff)


def _moe_ffn_kernel(idx_ref, idxn_ref, h_hbm, wg_ref, wu_ref, wd_ref, g_ref, o_ref, xland, xbf, h1, sem,
                    *, nf, n_exp, cap, tokens_per_batch):
    e = pl.program_id(0)
    f = pl.program_id(1)
    rows = xland.shape[0]

    def issue_gather(ids_ref):
        for b in range(rows // cap):
            def body(c, carry, b=b):
                i = b * cap + c
                src_row = ids_ref[0, 0, i] + b * tokens_per_batch
                pltpu.make_async_copy(h_hbm.at[pl.ds(src_row, 1)], xland.at[pl.ds(i, 1)], sem.at[0]).start()
                return carry
            lax.fori_loop(0, cap, body, 0, unroll=8)

    @pl.when((e == 0) & (f == 0))
    def _():
        issue_gather(idx_ref)

    @pl.when(f == 0)
    def _():
        pltpu.make_async_copy(h_hbm.at[pl.ds(0, rows)], xland, sem.at[0]).wait()
        xbf[...] = xland[...].astype(BF16)

        @pl.when(e + 1 < n_exp)
        def _():
            issue_gather(idxn_ref)

    @pl.when(f < nf)
    def _():
        x = xbf[...]
        a = jnp.dot(x, wg_ref[0, 0].astype(BF16), preferred_element_type=F32)
        u = jnp.dot(x, wu_ref[0, 0].astype(BF16), preferred_element_type=F32)
        h1[jnp.minimum(f, nf - 1)] = (jax.nn.silu(a) * u).astype(BF16)

    @pl.when(f >= nf)
    def _():
        hid = jnp.concatenate([h1[k] for k in range(nf)], axis=1)
        y = jnp.dot(hid, wd_ref[0, 0].astype(BF16), preferred_element_type=F32)
        o_ref[...] = (y.reshape(o_ref.shape) * g_ref[...]).astype(o_ref.dtype)


def moe_ffn(idx3, hp, g, w_gate, w_up, w_down, layer, T, *, tf=256, tc=512):
    B, E, C = idx3.shape
    D = hp.shape[1]
    F = w_gate.shape[-1]
    nf = F // tf
    tc = min(tc, D)
    ids = idx3.transpose(1, 0, 2).reshape(E, 1, B * C)
    smem_blk = lambda imap: pl.BlockSpec((1, 1, B * C), imap, memory_space=pltpu.SMEM)
    up_blk = pl.BlockSpec((1, 1, D, tf), lambda e, s: (layer, e, 0, jnp.minimum(s, nf - 1)))
    return pl.pallas_call(
        functools.partial(_moe_ffn_kernel, nf=nf, n_exp=E, cap=C, tokens_per_batch=T),
        grid=(E, nf + D // tc),
        in_specs=[smem_blk(lambda e, s: (e, 0, 0)),
                  smem_blk(lambda e, s: (jnp.minimum(e + 1, E - 1), 0, 0)),
                  pl.BlockSpec(memory_space=pl.ANY),
                  up_blk,
                  up_blk,
                  pl.BlockSpec((1, 1, F, tc), lambda e, s: (layer, e, 0, jnp.maximum(s - nf, 0))),
                  pl.BlockSpec((B, 1, C, 1), lambda e, s: (0, e, 0, 0))],
        out_specs=pl.BlockSpec((B, 1, C, tc), lambda e, s: (0, e, 0, jnp.maximum(s - nf, 0))),
        out_shape=jax.ShapeDtypeStruct((B, E, C, D), BF16),
        scratch_shapes=[pltpu.VMEM((B * C, D), F32),
                        pltpu.VMEM((B * C, D), BF16),
                        pltpu.VMEM((nf, B * C, tf), BF16),
                        pltpu.SemaphoreType.DMA((1,))],
        compiler_params=_params("arbitrary", "arbitrary"),
    )(ids, ids, hp, w_gate, w_up, w_down, g)


def _moe_combine_kernel(starts_ref, ntr_ref, y_hbm, post_ref, x_ref, gate_ref, o_ref, ybuf, yext, acc, sem, sem_ext,
                        *, n_exp, cap, win, steps_per_batch):
    s = pl.program_id(0)
    slot = s % 2

    def window(step, e, k):
        nominal = starts_ref[step * n_exp + e] + k * win
        return nominal, pl.multiple_of(jnp.minimum(nominal, cap - win), 16)

    def window_copy(step, e, k, dst, dsem):
        _, actual = window(step, e, k)
        return pltpu.make_async_copy(y_hbm.at[step // steps_per_batch, e, pl.ds(actual, win)],
                                     dst.at[pl.ds(e * win, win)], dsem)

    def onehot(step, k):
        post = post_ref[0]
        per = max(LANES // win, 1)
        lane = lax.broadcasted_iota(jnp.int32, (1, per * win), 1)
        cols = []
        for e0 in range(0, n_exp, per):
            slots = tok = None
            for q in range(per):
                nominal, actual = window(step, e0 + q, k)
                s_q = actual + lane - q * win
                s_q = jnp.where(s_q >= nominal, s_q, -2)
                t_q = post[:, e0 + q:e0 + q + 1]
                mine = lane >= q * win
                slots = s_q if q == 0 else jnp.where(mine, s_q, slots)
                tok = t_q if q == 0 else jnp.where(mine, t_q, tok)
            cols.append(jnp.where(tok == slots, 1.0, 0.0).astype(BF16))
        return jnp.concatenate(cols, axis=1)

    @pl.when(s == 0)
    def _():
        for e in range(n_exp):
            window_copy(0, e, 0, ybuf.at[0], sem.at[0]).start()

    for e in range(n_exp):
        window_copy(s, e, 0, ybuf.at[slot], sem.at[slot]).wait()

    @pl.when(s + 1 < pl.num_programs(0))
    def _():
        for e in range(n_exp):
            window_copy(s + 1, e, 0, ybuf.at[1 - slot], sem.at[1 - slot]).start()

    acc[...] = jnp.dot(onehot(s, 0), ybuf[slot], preferred_element_type=F32)

    def extra(k, carry):
        for e in range(n_exp):
            window_copy(s, e, k, yext, sem_ext.at[0]).start()
        for e in range(n_exp):
            window_copy(s, e, k, yext, sem_ext.at[0]).wait()
        acc[...] += jnp.dot(onehot(s, k), yext[...], preferred_element_type=F32)
        return carry

    lax.fori_loop(1, ntr_ref[s], extra, 0)
    o_ref[...] = x_ref[...] + gate_ref[0] * acc[...]


def moe_combine(starts, ntr, y, post, x2, gate, *, tj=256, win=64):
    B, E, C, D = y.shape
    M = x2.shape[0]
    T = M // B
    spb = T // tj
    grid_spec = pltpu.PrefetchScalarGridSpec(
        num_scalar_prefetch=2,
        grid=(M // tj,),
        in_specs=[pl.BlockSpec(memory_space=pl.ANY),
                  pl.BlockSpec((1, tj, E), lambda s, *_: (s // spb, s % spb, 0)),
                  pl.BlockSpec((tj, D), lambda s, *_: (s, 0)),
                  pl.BlockSpec((1, 1, D), lambda s, *_: (s // spb, 0, 0))],
        out_specs=pl.BlockSpec((tj, D), lambda s, *_: (s, 0)),
        scratch_shapes=[pltpu.VMEM((2, E * win, D), BF16),
                        pltpu.VMEM((E * win, D), BF16),
                        pltpu.VMEM((tj, D), F32),
                        pltpu.SemaphoreType.DMA((2,)),
                        pltpu.SemaphoreType.DMA((1,))],
    )
    return pl.pallas_call(
        functools.partial(_moe_combine_kernel, n_exp=E, cap=C, win=win, steps_per_batch=spb),
        grid_spec=grid_spec,
        out_shape=jax.ShapeDtypeStruct((M, D), F32),
        compiler_params=_params("arbitrary"),
    )(starts, ntr, y, post, x2, gate)


def _small_rmsnorm(x, g):
    return x * lax.rsqrt(jnp.mean(x * x, axis=-1, keepdims=True) + NORM_EPS) * g


def _rope_tables(rows):
    row = jnp.broadcast_to(jnp.arange(rows)[:, None], (rows, GRID_W)).reshape(-1).astype(F32)
    col = jnp.broadcast_to(jnp.arange(GRID_W)[None, :], (rows, GRID_W)).reshape(-1).astype(F32)
    n_freq = MLA_ROPE_DIM // 4
    inv = ROPE_THETA ** (-jnp.arange(n_freq, dtype=F32) / n_freq)
    ang = jnp.concatenate([row[:, None] * inv, col[:, None] * inv], axis=-1)
    return jnp.cos(ang), jnp.sin(ang)


def _ada(cvec3, ada_w, ada_b, layer):
    R = cvec3.shape[0]
    a = jnp.concatenate([jax.nn.silu(cvec3), jnp.zeros((8 - R, cvec3.shape[1]), F32)], axis=0)
    m = matmul(a, ada_w, b_index=layer, tm=8, tn=512)[:R] + ada_b[layer]
    return m.reshape(R, 6, D_MODEL)


def _q_proj_kernel(cq_ref, w_ref, cos_ref, sin_ref, o_ref):
    cq, cos, sin = cq_ref[...], cos_ref[...], sin_ref[...]
    for hh in range(o_ref.shape[1]):
        w = w_ref[:, hh * 2 * QK_PAD:(hh + 1) * 2 * QK_PAD]
        acc = jnp.dot(cq, w, preferred_element_type=F32)
        o_ref[0, hh] = (acc[:, :QK_PAD] * cos + acc[:, QK_PAD:] * sin).astype(o_ref.dtype)


def q_proj(cqn, w_q2, cosq, sinq, B, L, *, tm=1024, hg=4):
    M, R = cqn.shape
    tpb = L // tm
    return pl.pallas_call(
        _q_proj_kernel,
        grid=(M // tm, MLA_HEADS // hg),
        in_specs=[pl.BlockSpec((tm, R), lambda i, h: (i, 0)),
                  pl.BlockSpec((R, hg * 2 * QK_PAD), lambda i, h: (0, h)),
                  pl.BlockSpec((tm, QK_PAD), lambda i, h: (i % tpb, 0)),
                  pl.BlockSpec((tm, QK_PAD), lambda i, h: (i % tpb, 0))],
        out_specs=pl.BlockSpec((1, hg, tm, QK_PAD), lambda i, h: (i // tpb, h, i % tpb, 0)),
        out_shape=jax.ShapeDtypeStruct((B, MLA_HEADS, L, QK_PAD), BF16),
        compiler_params=_params("parallel", "parallel"),
    )(cqn, w_q2, cosq, sinq)


def _q_weights(w_uq):
    R = w_uq.shape[0]
    half = MLA_ROPE_DIM // 2
    w = w_uq.reshape(R, MLA_HEADS, MLA_NOPE_DIM + MLA_ROPE_DIM)
    nope, rope = w[..., :MLA_NOPE_DIM], w[..., MLA_NOPE_DIM:]
    rot = jnp.concatenate([-rope[..., half:], rope[..., :half]], axis=-1)
    pad = jnp.zeros((R, MLA_HEADS, QK_PAD - MLA_NOPE_DIM - MLA_ROPE_DIM), w.dtype)
    parts = [nope, rope, pad, jnp.zeros_like(nope), rot, pad]
    return jnp.concatenate(parts, axis=-1).reshape(R, MLA_HEADS * 2 * QK_PAD).astype(BF16)


def _krope_t_kernel(w_ref, h_ref, *rest, roped):
    o_ref = rest[-1]
    x = lax.dot_general(w_ref[...], h_ref[...], (((1,), (1,)), ((), ())), preferred_element_type=F32)
    if roped:
        half = x.shape[0] // 2
        x1, x2, c, s = x[:half], x[half:], rest[0][...], rest[1][...]
        x = jnp.concatenate([x1 * c - x2 * s, x2 * c + x1 * s], axis=0)
    o_ref[0] = x.astype(o_ref.dtype)


def krope_t(h2, w_kr_t, B, Lx, tt, cos_t=None, sin_t=None):
    D = h2.shape[1]
    R = w_kr_t.shape[0]
    per = Lx // tt
    roped = cos_t is not None
    in_specs = [pl.BlockSpec((R, D), lambda b, i: (0, 0)), pl.BlockSpec((tt, D), lambda b, i: (b * per + i, 0))]
    args = [w_kr_t, h2]
    if roped:
        in_specs += [pl.BlockSpec((R // 2, tt), lambda b, i: (0, i))] * 2
        args += [cos_t, sin_t]
    return pl.pallas_call(
        functools.partial(_krope_t_kernel, roped=roped),
        grid=(B, per),
        in_specs=in_specs,
        out_specs=pl.BlockSpec((1, R, tt), lambda b, i: (b, 0, i)),
        out_shape=jax.ShapeDtypeStruct((B, R, Lx), BF16),
        compiler_params=_params("parallel", "parallel"),
    )(*args)


def _kv_proj_kernel(c_ref, wn_ref, wv_ref, kr_ref, kt_ref, v_ref):
    c = c_ref[0]
    kn_t = lax.dot_general(wn_ref[0], c, (((1,), (1,)), ((), ())), preferred_element_type=F32)
    rope_end = MLA_NOPE_DIM + MLA_ROPE_DIM
    kt_ref[0, 0, :MLA_NOPE_DIM] = kn_t.astype(kt_ref.dtype)
    kt_ref[0, 0, MLA_NOPE_DIM:rope_end] = kr_ref[0]
    kt_ref[0, 0, rope_end:] = jnp.zeros((QK_PAD - rope_end, c.shape[0]), kt_ref.dtype)
    v_ref[0, 0] = jnp.dot(c, wv_ref[0], preferred_element_type=F32).astype(v_ref.dtype)


def kv_proj(ckvn, wn_t, wv, kr_t, *, tk=2176):
    B, K, R = ckvn.shape
    H, V = wv.shape[0], wv.shape[2]
    return pl.pallas_call(
        _kv_proj_kernel,
        grid=(B, H, K // tk),
        in_specs=[pl.BlockSpec((1, tk, R), lambda b, h, i: (b, i, 0)),
                  pl.BlockSpec((1, MLA_NOPE_DIM, R), lambda b, h, i: (h, 0, 0)),
                  pl.BlockSpec((1, R, V), lambda b, h, i: (h, 0, 0)),
                  pl.BlockSpec((1, MLA_ROPE_DIM, tk), lambda b, h, i: (b, 0, i))],
        out_specs=[pl.BlockSpec((1, 1, QK_PAD, tk), lambda b, h, i: (b, h, 0, i)),
                   pl.BlockSpec((1, 1, tk, V), lambda b, h, i: (b, h, i, 0))],
        out_shape=[jax.ShapeDtypeStruct((B, H, QK_PAD, K), BF16), jax.ShapeDtypeStruct((B, H, K, V), BF16)],
        compiler_params=_params("parallel", "parallel", "parallel"),
    )(ckvn, wn_t, wv, kr_t)


def _mla_layer(h, h_c, B, L, Lc, w_in, g_q, g_kv, w_uq, w_ukv):
    H = MLA_HEADS
    lat = MLA_Q_RANK + MLA_KV_RANK
    w_in_b = w_in.astype(BF16)
    proj = matmul(h, w_in_b[:, :lat], tn=lat)
    ckv_c = matmul(h_c, w_in_b[:, MLA_Q_RANK:lat], tn=MLA_KV_RANK)
    ckv = jnp.concatenate([proj[:, MLA_Q_RANK:].reshape(B, L, -1), ckv_c.reshape(B, Lc, -1)], axis=1)
    cos, sin = _rope_tables(L // GRID_W)
    ones, zeros = jnp.ones((L, MLA_NOPE_DIM), F32), jnp.zeros((L, MLA_NOPE_DIM), F32)
    pad = jnp.zeros((L, QK_PAD - MLA_NOPE_DIM - MLA_ROPE_DIM), F32)
    cosq = jnp.concatenate([ones, cos, cos, pad], axis=1)
    sinq = jnp.concatenate([zeros, sin, sin, pad], axis=1)
    qp = q_proj(_small_rmsnorm(proj[:, :MLA_Q_RANK], g_q).astype(BF16), _q_weights(w_uq), cosq, sinq, B, L)
    w_kr_t = w_in_b[:, lat:].T
    kr_t = jnp.concatenate([krope_t(h, w_kr_t, B, L, 1024, cos.T, sin.T), krope_t(h_c, w_kr_t, B, Lc, Lc)], axis=2)
    w_kv = w_ukv.astype(BF16).reshape(MLA_KV_RANK, H, MLA_NOPE_DIM + MLA_V_DIM)
    wn_t = w_kv[..., :MLA_NOPE_DIM].transpose(1, 2, 0)
    wv = w_kv[..., MLA_NOPE_DIM:].transpose(1, 0, 2)
    kt, v = kv_proj(_small_rmsnorm(ckv, g_kv).astype(BF16), wn_t, wv, kr_t)
    scale = (MLA_NOPE_DIM + MLA_ROPE_DIM) ** -0.5
    return attention(qp, kt, v, scale=scale)


def _fft_tables(L):
    n = 2 * L
    n2_in = L // FFT_N1
    k2 = np.arange(FFT_K2)
    th = 2 * np.pi * ((k2[:, None] * np.arange(n2_in)[None, :]) % FFT_N2) / FFT_N2
    f1 = np.zeros((2 * FFT_K2P, n2_in))
    f1[:FFT_K2] = np.cos(th)
    f1[FFT_K2P:FFT_K2P + FFT_K2] = -np.sin(th)
    wgt = np.where((k2 == 0) | (k2 == FFT_N2 // 2), 1.0, 2.0)
    g = np.zeros((n2_in, 2 * FFT_K2P))
    g[:, :FFT_K2] = (wgt[:, None] * np.cos(th)).T / n
    g[:, FFT_K2P:FFT_K2P + FFT_K2] = -(wgt[:, None] * np.sin(th)).T / n
    k1 = np.arange(FFT_N1)[:, None]
    n1 = np.arange(FFT_N1)[None, :]
    m2 = np.zeros((FFT_K2, 2 * FFT_N1, 2 * FFT_N1))
    for kk in range(FFT_K2):
        phi = 2 * np.pi * ((n1 * k1 * FFT_N2 + n1 * kk) % n) / n
        tr, ti = np.cos(phi), -np.sin(phi)
        m2[kk] = np.block([[tr, -ti], [ti, tr]])
    as_bf16 = lambda a: jnp.asarray(a, dtype=F32).astype(BF16)
    eye = np.eye(FFT_GROUP)
    return (as_bf16(np.kron(f1, eye)), as_bf16(np.kron(g, eye)), as_bf16(m2),
            as_bf16(np.transpose(m2, (0, 2, 1))))


def _fft_s1_kernel(f_ref, z_ref, o_ref):
    _, n2_in, g, ct = z_ref.shape
    z = z_ref[0].reshape(n2_in * g, ct).astype(BF16)
    a = jnp.dot(f_ref[...], z, preferred_element_type=F32)
    o_ref[0] = a.reshape(o_ref.shape[1:]).astype(o_ref.dtype)


def fft_stage1(z4, f1k, *, lead_off, nb, ct=2048):
    _, n2_in, _, C = z4.shape
    g = FFT_GROUP
    return pl.pallas_call(
        _fft_s1_kernel,
        grid=(nb, FFT_N1 // g, C // ct),
        in_specs=[pl.BlockSpec(f1k.shape, lambda b, i, j: (0, 0)),
                  pl.BlockSpec((1, n2_in, g, ct), lambda b, i, j: (lead_off + b, 0, i, j))],
        out_specs=pl.BlockSpec((1, 2, FFT_K2P, g, ct), lambda b, i, j: (b, 0, 0, i, j)),
        out_shape=jax.ShapeDtypeStruct((nb, 2, FFT_K2P, FFT_N1, C), BF16),
        compiler_params=_params("parallel", "parallel", "parallel"),
    )(f1k, z4)


def _fft_s2_kernel(a_ref, m_ref, mt_ref, kf_ref, o_ref):
    k2 = pl.program_id(0)

    @pl.when(k2 < FFT_K2)
    def _():
        ct = a_ref.shape[-1]
        x = jnp.dot(m_ref[0], a_ref[...].reshape(2 * FFT_N1, ct), preferred_element_type=F32)
        xr, xi = x[:FFT_N1], x[FFT_N1:]
        kr, ki = kf_ref[0, 0, 0].astype(F32), kf_ref[0, 1, 0].astype(F32)
        p = jnp.concatenate([xr * kr - xi * ki, xr * ki + xi * kr], axis=0).astype(BF16)
        e = jnp.dot(mt_ref[0], p, preferred_element_type=F32)
        o_ref[...] = e.astype(o_ref.dtype).reshape(o_ref.shape)

    @pl.when(k2 >= FFT_K2)
    def _():
        o_ref[...] = jnp.zeros(o_ref.shape, o_ref.dtype)


def fft_stage2(a5, m2, m2t, kf, order):
    B, _, _, _, D = a5.shape
    last = FFT_K2 - 1
    blk = (1, 2, 1, FFT_N1, D)
    return pl.pallas_call(
        _fft_s2_kernel,
        grid=(FFT_K2P, B),
        in_specs=[pl.BlockSpec(blk, lambda k, b: (b, 0, jnp.minimum(k, last), 0, 0)),
                  pl.BlockSpec((1, 2 * FFT_N1, 2 * FFT_N1), lambda k, b: (jnp.minimum(k, last), 0, 0)),
                  pl.BlockSpec((1, 2 * FFT_N1, 2 * FFT_N1), lambda k, b: (jnp.minimum(k, last), 0, 0)),
                  pl.BlockSpec(blk, lambda k, b: (order, 0, jnp.minimum(k, last), 0, 0))],
        out_specs=pl.BlockSpec(blk, lambda k, b: (b, 0, k, 0, 0)),
        out_shape=jax.ShapeDtypeStruct(a5.shape, BF16),
        compiler_params=_params("parallel", "arbitrary"),
    )(a5, m2, m2t, kf)


def _fft_f2_kernel(f_ref, g_ref, m_ref, inv_ref, o_ref):
    k2 = pl.program_id(0)

    @pl.when(k2 < FFT_K2)
    def _():
        ct = f_ref.shape[-1]
        xf = jnp.dot(m_ref[0], f_ref[...].reshape(2 * FFT_N1, ct), preferred_element_type=F32)
        xg = jnp.dot(m_ref[0], g_ref[...].reshape(2 * FFT_N1, ct), preferred_element_type=F32)
        inv = inv_ref[0]
        o_ref[0, 0, 0] = ((xf[:FFT_N1] + xg[:FFT_N1]) * inv).astype(o_ref.dtype)
        o_ref[0, 1, 0] = ((xf[FFT_N1:] - xg[FFT_N1:]) * inv).astype(o_ref.dtype)

    @pl.when(k2 >= FFT_K2)
    def _():
        o_ref[...] = jnp.zeros(o_ref.shape, o_ref.dtype)


def fft_filter_spectrum(a4, m2, inv, D):
    last = FFT_K2 - 1
    blk = (2, 1, FFT_N1, D)
    return pl.pallas_call(
        _fft_f2_kernel,
        grid=(FFT_K2P, HYENA_ORDER),
        in_specs=[pl.BlockSpec(blk, lambda k, o: (0, jnp.minimum(k, last), 0, 2 * o)),
                  pl.BlockSpec(blk, lambda k, o: (0, jnp.minimum(k, last), 0, 2 * o + 1)),
                  pl.BlockSpec((1, 2 * FFT_N1, 2 * FFT_N1), lambda k, o: (jnp.minimum(k, last), 0, 0)),
                  pl.BlockSpec((1, 1, D), lambda k, o: (o, 0, 0))],
        out_specs=pl.BlockSpec((1, 2, 1, FFT_N1, D), lambda k, o: (o, 0, k, 0, 0)),
        out_shape=jax.ShapeDtypeStruct((HYENA_ORDER, 2, FFT_K2P, FFT_N1, D), BF16),
        compiler_params=_params("arbitrary", "parallel"),
    )(a4, a4, m2, inv)


def _fft_s3_kernel(g_ref, e_ref, z_ref, x_ref, skip_ref, o_ref):
    _, n2_in, g, ct = z_ref.shape
    e = e_ref[0].reshape(2 * FFT_K2P * g, ct)
    y = jnp.dot(g_ref[...], e, preferred_element_type=F32).reshape(n2_in, g, ct)
    o_ref[0] = (x_ref[0] * (y + skip_ref[...] * z_ref[0])).astype(o_ref.dtype)


def fft_stage3(e5, gk, z4, z_off, x4, x_off, skip, *, out_dtype):
    B, _, _, _, D = e5.shape
    n2_in = z4.shape[1]
    g = FFT_GROUP
    return pl.pallas_call(
        _fft_s3_kernel,
        grid=(B, FFT_N1 // g),
        in_specs=[pl.BlockSpec(gk.shape, lambda b, i: (0, 0)),
                  pl.BlockSpec((1, 2, FFT_K2P, g, D), lambda b, i: (b, 0, 0, i, 0)),
                  pl.BlockSpec((1, n2_in, g, D), lambda b, i: (z_off + b, 0, i, 0)),
                  pl.BlockSpec((1, n2_in, g, D), lambda b, i: (x_off + b, 0, i, 0)),
                  pl.BlockSpec((1, D), lambda b, i: (0, 0))],
        out_specs=pl.BlockSpec((1, n2_in, g, D), lambda b, i: (b, 0, i, 0)),
        out_shape=jax.ShapeDtypeStruct((B, n2_in, FFT_N1, D), out_dtype),
        compiler_params=_params("parallel", "parallel"),
    )(gk, e5, z4, x4, skip.reshape(1, D))


def _hyin_kernel(ap_ref, a_ref, an_ref, b_ref, cw_ref, cb_ref, o_ref, *, tiles_per_seq):
    i = pl.program_id(0)
    t = i % tiles_per_seq
    halo = ap_ref.shape[0]
    tm, tn = o_ref.shape[1:]
    cw = MXU_DIM
    row = lax.broadcasted_iota(jnp.int32, (tm, cw), 0)
    at_start = jnp.logical_and(row == 0, t == 0)
    at_end = jnp.logical_and(row == tm - 1, t == tiles_per_seq - 1)
    a_ext = jnp.concatenate([ap_ref[...], a_ref[...], an_ref[...]], axis=0)
    for c0 in range(0, tn, cw):
        acc = jnp.dot(a_ext, b_ref[:, c0:c0 + cw], preferred_element_type=F32)
        up = jnp.where(at_start, 0.0, acc[halo - 1:halo - 1 + tm])
        dn = jnp.where(at_end, 0.0, acc[halo + 1:halo + 1 + tm])
        w = cw_ref[:, c0:c0 + cw]
        o_ref[0, :, c0:c0 + cw] = up * w[0:1] + acc[halo:halo + tm] * w[1:2] + dn * w[2:3] + cb_ref[:, c0:c0 + cw]


def hyena_in(h, w_in, conv_w, conv_b, L, *, tm=512, tn=1024, halo=16):
    M, K = h.shape
    N = w_in.shape[1]
    D = N // 3
    per = D // tn
    return pl.pallas_call(
        functools.partial(_hyin_kernel, tiles_per_seq=L // tm),
        grid=(M // tm, N // tn),
        in_specs=[pl.BlockSpec((halo, K), lambda i, j: (jnp.maximum(i * (tm // halo) - 1, 0), 0)),
                  pl.BlockSpec((tm, K), lambda i, j: (i, 0)),
                  pl.BlockSpec((halo, K), lambda i, j: (jnp.minimum((i + 1) * (tm // halo), M // halo - 1), 0)),
                  pl.BlockSpec((K, tn), lambda i, j: (0, j)),
                  pl.BlockSpec((3, tn), lambda i, j: (0, j)),
                  pl.BlockSpec((1, tn), lambda i, j: (0, j))],
        out_specs=pl.BlockSpec((1, tm, tn), lambda i, j: (j // per, i, j % per)),
        out_shape=jax.ShapeDtypeStruct((3, M, D), F32),
        compiler_params=_params("parallel", "parallel"),
    )(h, h, h, w_in, conv_w, conv_b.reshape(1, N))


def _taps_kernel(a_ref, w_ref, b_ref, dl_ref, keep_ref, o_ref, ss_ref, *, seq_len):
    i = pl.program_id(1)
    tm = a_ref.shape[0]
    acc = jnp.dot(a_ref[...].astype(BF16), w_ref[...].astype(BF16), preferred_element_type=F32) + b_ref[...]
    row = i * tm + lax.broadcasted_iota(jnp.int32, acc.shape, 0)
    t = row.astype(F32) / seq_len
    hf = acc * jnp.exp(-t * dl_ref[...])
    hf = jnp.where(row == 0, hf * keep_ref[...], hf)
    o_ref[...] = hf.astype(o_ref.dtype)
    part = jnp.sum(hf * hf, axis=0, keepdims=True)

    @pl.when(i == 0)
    def _():
        ss_ref[...] = part

    @pl.when(i > 0)
    def _():
        ss_ref[...] += part


def _hyena_filter_taps(L, w1, b1, w2, b2, w3, b3, freq, *, tm=512, tn=1024):
    D = D_MODEL
    N = 2 * HYENA_ORDER * D
    t = jnp.arange(L, dtype=F32) / L
    w = 2 * math.pi * jnp.arange(L, dtype=F32) / L
    bands = jnp.linspace(1e-4, HYENA_BANDS - 1, HYENA_BANDS, dtype=F32)
    ang = w[:, None] * bands[None, :]
    z = jnp.concatenate([t[:, None], jnp.cos(ang), -jnp.sin(ang)], axis=-1)
    a = jnp.sin(freq * (z @ w1 + b1))
    a = jnp.sin(freq * (a @ w2 + b2))
    max_decay = math.log(HYENA_TARGET) / HYENA_FAST_PCT
    min_decay = math.log(HYENA_TARGET) / HYENA_SLOW_PCT
    decay = jnp.tile(jnp.abs(jnp.linspace(min_decay, max_decay, D, dtype=F32)), 2 * HYENA_ORDER).reshape(1, N)
    keep0 = jnp.where((jnp.arange(N) // D) % 2 == 1, 0.0, 1.0).astype(F32).reshape(1, N)
    K = a.shape[1]
    taps, ss = pl.pallas_call(
        functools.partial(_taps_kernel, seq_len=L),
        grid=(N // tn, L // tm),
        in_specs=[pl.BlockSpec((tm, K), lambda j, i: (i, 0)),
                  pl.BlockSpec((K, tn), lambda j, i: (0, j)),
                  pl.BlockSpec((1, tn), lambda j, i: (0, j)),
                  pl.BlockSpec((1, tn), lambda j, i: (0, j)),
                  pl.BlockSpec((1, tn), lambda j, i: (0, j))],
        out_specs=[pl.BlockSpec((tm, tn), lambda j, i: (i, j)), pl.BlockSpec((1, tn), lambda j, i: (0, j))],
        out_shape=[jax.ShapeDtypeStruct((L, N), BF16), jax.ShapeDtypeStruct((1, N), F32)],
        compiler_params=_params("parallel", "arbitrary"),
    )(a, w3, b3.reshape(1, N), decay, keep0)
    ss = ss.reshape(HYENA_ORDER, 2, D).sum(axis=1, keepdims=True)
    return taps, lax.rsqrt(ss + NORM_EPS)


def _hyena_layer(h, B, L, w_in, conv_w, conv_b, f_w1, f_b1, f_w2, f_b2, f_w3, f_b3, f_freq, skip):
    D = D_MODEL
    n2_in = L // FFT_N1
    f1k, gk, m2, m2t = _fft_tables(L)
    taps, inv = _hyena_filter_taps(L, f_w1, f_b1, f_w2, f_b2, f_w3, f_b3, f_freq)
    af = fft_stage1(taps.reshape(1, n2_in, FFT_N1, taps.shape[1]), f1k, lead_off=0, nb=1)
    kf = fft_filter_spectrum(af[0], m2, inv, D)
    u4 = hyena_in(h, w_in.astype(BF16), conv_w, conv_b, L).reshape(3 * B, n2_in, FFT_N1, D)
    z4 = u4
    for order in range(HYENA_ORDER):
        a = fft_stage1(z4, f1k, lead_off=0, nb=B)
        e = fft_stage2(a, m2, m2t, kf, order)
        last = order == HYENA_ORDER - 1
        z4 = fft_stage3(e, gk, z4, 0, u4, (order + 1) * B, skip[order], out_dtype=BF16 if last else F32)
    return z4.reshape(B * L, D)


def _moe_layer(x2, norm_gain, mod, B, T, w_router, w_gate, w_up, w_down, layer, *, tj=256, win=64):
    E = N_EXPERTS
    cap = EC_CAPACITY_FACTOR * T // E
    hp, aff = norm_route(x2, norm_gain, mod[:, 3:4], mod[:, 4:5], w_router.T.astype(BF16), rows_per_batch=T)
    pos, idx, g = route_select(aff, cap)
    idx3 = idx.reshape(B, E, cap)
    y = moe_ffn(idx3, hp, g, w_gate, w_up, w_down, layer, T)
    nj = T // tj
    edges = jnp.arange(nj + 1, dtype=jnp.int32) * tj
    first = jnp.sum((idx3[..., None] < edges).astype(jnp.int32), axis=2)
    nominal = (first[..., :nj] // 16) * 16
    ntr = jnp.maximum(jnp.max((first[..., 1:] - nominal + win - 1) // win, axis=1), 1)
    starts = nominal.transpose(0, 2, 1).reshape(-1)
    return moe_combine(starts, ntr.reshape(-1), y, pos.transpose(0, 2, 1), x2, mod[:, 5:6], tj=tj, win=win)


def kernel(x, c, ctx, c_ctx, ada_w, ada_b, norm_g, final_g, mla_w_in, mla_g_q, mla_g_kv, mla_w_uq, mla_w_ukv, mla_w_o, hy_w_in, hy_conv_w, hy_conv_b, hy_f_w1, hy_f_b1, hy_f_w2, hy_f_b2, hy_f_w3, hy_f_b3, hy_f_freq, hy_skip, hy_w_out, moe_w_router, moe_w_gate, moe_w_up, moe_w_down):
    B, L, D = x.shape
    Lc = ctx.shape[1]
    x2 = x.reshape(B * L, D)
    ctx2 = ctx.reshape(B * Lc, D)

    mod3 = _ada(jnp.concatenate([c, c_ctx[None]], axis=0), ada_w, ada_b, 0)
    mod, mod_c = mod3[:B], jnp.broadcast_to(mod3[B:], (B, 6, D))
    h = norm_mod(x2, norm_g[0, 0], mod[:, 0:1], mod[:, 1:2], rows_per_batch=L)
    h_c = norm_mod(ctx2, norm_g[0, 0], mod_c[:, 0:1], mod_c[:, 1:2], rows_per_batch=Lc)
    o = _mla_layer(h, h_c, B, L, Lc, mla_w_in[0], mla_g_q[0], mla_g_kv[0], mla_w_uq[0], mla_w_ukv[0])
    x2 = matmul(o.reshape(B * L, -1), mla_w_o[0].astype(BF16), tn=1024,
                res=x2, gate=mod[:, 2:3], rows_per_batch=L)
    x2 = _moe_layer(x2, norm_g[0, 1], mod, B, L, moe_w_router[0], moe_w_gate, moe_w_up, moe_w_down, 0)

    mod = _ada(c, ada_w, ada_b, 1)
    h = norm_mod(x2, norm_g[1, 0], mod[:, 0:1], mod[:, 1:2], rows_per_batch=L)
    z = _hyena_layer(h, B, L, hy_w_in[0], hy_conv_w[0], hy_conv_b[0], hy_f_w1[0], hy_f_b1[0], hy_f_w2[0],
                     hy_f_b2[0], hy_f_w3[0], hy_f_b3[0], hy_f_freq[0], hy_skip[0])
    x2 = matmul(z, hy_w_out[0].astype(BF16), tn=1024, res=x2, gate=mod[:, 2:3], rows_per_batch=L)
    x2 = _moe_layer(x2, norm_g[1, 1], mod, B, L, moe_w_router[1], moe_w_gate, moe_w_up, moe_w_down, 1)

    return norm_mod(x2, final_g, out_dtype=F32).reshape(B, L, D)
```

```python
import functools
import math

import jax
import jax.numpy as jnp
import numpy as np
from jax import lax
from jax.experimental import pallas as pl
from jax.experimental.pallas import tpu as pltpu

F32 = jnp.float32
BF16 = jnp.bfloat16

D_MODEL = 2048
GRID_W = 64
NORM_EPS = 1e-6
MLA_HEADS = 16
MLA_Q_RANK = 512
MLA_KV_RANK = 256
MLA_NOPE_DIM = 128
MLA_ROPE_DIM = 64
MLA_V_DIM = 128
ROPE_THETA = 10000.0
HYENA_ORDER = 2
HYENA_BANDS = 16
HYENA_TARGET = 1e-2
HYENA_FAST_PCT = 0.3
HYENA_SLOW_PCT = 1.5
N_EXPERTS = 16
EC_CAPACITY_FACTOR = 2
FFT_N1 = 128
FFT_N2 = 64
FFT_K2 = FFT_N2 // 2 + 1
FFT_K2P = 40
FFT_GROUP = 16
FFT_SUB = 8

V7X_VMEM_LIMIT_BYTES = 56 * 1024 * 1024
LANES = 128
MXU_DIM = 256
QK_PAD = MXU_DIM


def _params(*sem):
    return pltpu.CompilerParams(dimension_semantics=sem, vmem_limit_bytes=V7X_VMEM_LIMIT_BYTES)


def _mm_kernel(a_ref, b_ref, *rest, b_is_stacked, has_res):
    o_ref = rest[-1]
    b = b_ref[0] if b_is_stacked else b_ref[...]
    acc = jnp.dot(a_ref[...].astype(BF16), b.astype(BF16), preferred_element_type=F32)
    if has_res:
        res_ref, gate_ref = rest[0], rest[1]
        acc = res_ref[...] + gate_ref[0] * acc
    o_ref[...] = acc.astype(o_ref.dtype)


def matmul(a, b, *, b_index=None, out_dtype=F32, tm=512, tn=512, res=None, gate=None, rows_per_batch=None):
    M, K = a.shape
    N = b.shape[-1]
    tm, tn = min(tm, M), min(tn, N)
    assert M % tm == 0 and N % tn == 0, (M, N, tm, tn)
    stacked = b.ndim == 3
    if stacked:
        b_spec = pl.BlockSpec((1, K, tn), lambda i, j: (b_index, 0, j))
    else:
        b_spec = pl.BlockSpec((K, tn), lambda i, j: (0, j))
    in_specs = [pl.BlockSpec((tm, K), lambda i, j: (i, 0)), b_spec]
    args = [a, b]
    if res is not None:
        assert rows_per_batch % tm == 0
        tpb = rows_per_batch // tm
        in_specs += [pl.BlockSpec((tm, tn), lambda i, j: (i, j)),
                     pl.BlockSpec((1, 1, tn), lambda i, j: (i // tpb, 0, j))]
        args += [res, gate]
    return pl.pallas_call(
        functools.partial(_mm_kernel, b_is_stacked=stacked, has_res=res is not None),
        grid=(M // tm, N // tn),
        in_specs=in_specs,
        out_specs=pl.BlockSpec((tm, tn), lambda i, j: (i, j)),
        out_shape=jax.ShapeDtypeStruct((M, N), out_dtype),
        compiler_params=_params("parallel", "parallel"),
    )(*args)


def _norm_kernel(x_ref, g_ref, *rest, modulated):
    o_ref = rest[-1]
    x = x_ref[...]
    y = x * lax.rsqrt(jnp.mean(x * x, axis=-1, keepdims=True) + NORM_EPS) * g_ref[...]
    if modulated:
        shift_ref, scale_ref = rest[0], rest[1]
        y = y * (1 + scale_ref[0]) + shift_ref[0]
    o_ref[...] = y.astype(o_ref.dtype)


def norm_mod(x2, g, shift=None, scale=None, *, rows_per_batch=None, out_dtype=BF16, tm=256):
    M, D = x2.shape
    tm = min(tm, M)
    in_specs = [pl.BlockSpec((tm, D), lambda i: (i, 0)), pl.BlockSpec((1, D), lambda i: (0, 0))]
    args = [x2, g.reshape(1, D)]
    if shift is not None:
        tpb = rows_per_batch // tm
        in_specs += [pl.BlockSpec((1, 1, D), lambda i: (i // tpb, 0, 0))] * 2
        args += [shift, scale]
    return pl.pallas_call(
        functools.partial(_norm_kernel, modulated=shift is not None),
        grid=(M // tm,),
        in_specs=in_specs,
        out_specs=pl.BlockSpec((tm, D), lambda i: (i, 0)),
        out_shape=jax.ShapeDtypeStruct((M, D), out_dtype),
        compiler_params=_params("parallel"),
    )(*args)


def _attn_kernel(q_ref, kt_ref, v_ref, o_ref, *, chunks, c):
    q = q_ref[0, 0]
    m = l = acc = None
    for lo, hi in chunks:
        s = jnp.dot(q, kt_ref[0, 0, :, lo:hi], preferred_element_type=F32)
        m_chunk = jnp.max(s, axis=-1, keepdims=True)
        m_new = m_chunk if m is None else jnp.maximum(m, m_chunk)
        p = jnp.exp2(s * c - m_new * c)
        pv = jnp.dot(p.astype(BF16), v_ref[0, 0, lo:hi, :], preferred_element_type=F32)
        psum = jnp.sum(p, axis=-1, keepdims=True)
        if m is None:
            l, acc = psum, pv
        else:
            alpha = jnp.exp2((m - m_new) * c)
            l = alpha * l + psum
            acc = alpha * acc + pv
        m = m_new
    o_ref[0] = (acc / l).astype(o_ref.dtype)


def attention(q, kt, v, *, scale, tq=1024, tk=1024):
    B, H, L, _ = q.shape
    K = kt.shape[-1]
    V = v.shape[-1]
    chunks = tuple((lo, min(lo + tk, K)) for lo in range(0, K, tk))
    return pl.pallas_call(
        functools.partial(_attn_kernel, chunks=chunks, c=scale * math.log2(math.e)),
        grid=(B, H, L // tq),
        in_specs=[pl.BlockSpec((1, 1, tq, QK_PAD), lambda b, h, i: (b, h, i, 0)),
                  pl.BlockSpec((1, 1, QK_PAD, K), lambda b, h, i: (b, h, 0, 0)),
                  pl.BlockSpec((1, 1, K, V), lambda b, h, i: (b, h, 0, 0))],
        out_specs=pl.BlockSpec((1, tq, V), lambda b, h, i: (b, i, h)),
        out_shape=jax.ShapeDtypeStruct((B, L, H * V), BF16),
        compiler_params=_params("parallel", "parallel", "arbitrary"),
    )(q, kt, v)


def _norm_route_kernel(x_ref, g_ref, shift_ref, scale_ref, wr_ref, h_ref, aff_ref):
    x = x_ref[...]
    y = x * lax.rsqrt(jnp.mean(x * x, axis=-1, keepdims=True) + NORM_EPS) * g_ref[...]
    y = y * (1 + scale_ref[0]) + shift_ref[0]
    h_ref[...] = y
    logits = lax.dot_general(wr_ref[...], y.astype(BF16), (((1,), (1,)), ((), ())),
                             preferred_element_type=F32)
    p = jnp.exp(logits - jnp.max(logits, axis=0, keepdims=True))
    aff_ref[0] = p / jnp.sum(p, axis=0, keepdims=True)


def norm_route(x2, g, shift, scale, w_router_t, *, rows_per_batch, tm=256):
    M, D = x2.shape
    E = w_router_t.shape[0]
    tpb = rows_per_batch // tm
    return pl.pallas_call(
        _norm_route_kernel,
        grid=(M // tm,),
        in_specs=[pl.BlockSpec((tm, D), lambda i: (i, 0)),
                  pl.BlockSpec((1, D), lambda i: (0, 0)),
                  pl.BlockSpec((1, 1, D), lambda i: (i // tpb, 0, 0)),
                  pl.BlockSpec((1, 1, D), lambda i: (i // tpb, 0, 0)),
                  pl.BlockSpec((E, D), lambda i: (0, 0))],
        out_specs=[pl.BlockSpec((tm, D), lambda i: (i, 0)),
                   pl.BlockSpec((1, E, tm), lambda i: (i // tpb, 0, i % tpb))],
        out_shape=[jax.ShapeDtypeStruct((M, D), F32),
                   jax.ShapeDtypeStruct((M // rows_per_batch, E, rows_per_batch), F32)],
        compiler_params=_params("parallel"),
    )(x2, g.reshape(1, D), shift, scale, w_router_t)


def _lane_prefix_sum(x):
    E, T = x.shape
    r = lax.broadcasted_iota(jnp.int32, (LANES, LANES), 0)
    c = lax.broadcasted_iota(jnp.int32, (LANES, LANES), 1)
    tri = jnp.where(r <= c, 1.0, 0.0).astype(BF16)
    outs, carry = [], jnp.zeros((E, 1), F32)
    for j in range(T // LANES):
        seg = jnp.dot(x[:, j * LANES:(j + 1) * LANES].astype(BF16), tri, preferred_element_type=F32) + carry
        outs.append(seg)
        carry = seg[:, LANES - 1:LANES]
    return jnp.concatenate(outs, axis=1)


def _route_select_kernel(aff_ref, pos_ref, idx_ref, gate_ref, cum_s, *, cap, tch):
    aff = aff_ref[0]
    E, T = aff.shape

    def bisect(state):
        lo, hi, _ = state
        mid = 0.5 * (lo + hi)
        cnt = jnp.sum(jnp.where(aff >= mid, 1.0, 0.0), axis=1, keepdims=True)
        ge = cnt >= cap
        open_rows = jnp.where((mid > lo) & (mid < hi), 1.0, 0.0)
        return jnp.where(ge, mid, lo), jnp.where(ge, hi, mid), jnp.max(open_rows)

    lo0 = jnp.min(aff, axis=1, keepdims=True)
    hi0 = 2.0 * jnp.max(aff, axis=1, keepdims=True) + 1e-30
    lo, hi, _ = lax.while_loop(lambda st: st[2] > 0.0, bisect, (lo0, hi0, jnp.float32(1.0)))
    gt = aff >= hi
    eq = (aff >= lo) & (aff < hi)
    need = cap - jnp.sum(jnp.where(gt, 1.0, 0.0), axis=1, keepdims=True)
    eq_rank = _lane_prefix_sum(jnp.where(eq, 1.0, 0.0))
    sel = gt | (eq & (eq_rank <= need))
    cum = _lane_prefix_sum(jnp.where(sel, 1.0, 0.0))
    cumsel = jnp.where(sel, cum, 0.0)
    pos_ref[0] = (cumsel - 1.0).astype(jnp.int32)
    cum_s[0] = cum
    cum_s[1] = cumsel
    slot = lax.broadcasted_iota(jnp.int32, (cap, 1), 0).astype(F32)

    def per_expert(e, carry):
        idx_acc = jnp.zeros((cap, 1), F32)
        g_acc = jnp.zeros((cap, 1), F32)
        for lo in range(0, T, tch):
            cum_e = cum_s[0, pl.ds(e, 1), lo:lo + tch]
            cs_e = cum_s[1, pl.ds(e, 1), lo:lo + tch]
            aff_e = aff_ref[0, pl.ds(e, 1), lo:lo + tch]
            idx_acc = idx_acc + jnp.sum(jnp.where(cum_e <= slot, 1.0, 0.0), axis=1, keepdims=True)
            g_acc = g_acc + jnp.sum(jnp.where(cs_e == slot + 1.0, aff_e, 0.0), axis=1, keepdims=True)
        idx_ref[0, pl.ds(e, 1)] = idx_acc.astype(jnp.int32).reshape(1, cap, 1)
        gate_ref[0, pl.ds(e, 1)] = g_acc.reshape(1, cap, 1)
        return carry

    lax.fori_loop(0, E, per_expert, 0)


def route_select(aff, cap, *, tch=1024):
    B, E, T = aff.shape
    return pl.pallas_call(
        functools.partial(_route_select_kernel, cap=cap, tch=tch),
        grid=(B,),
        in_specs=[pl.BlockSpec((1, E, T), lambda b: (b, 0, 0))],
        out_specs=[pl.BlockSpec((1, E, T), lambda b: (b, 0, 0)),
                   pl.BlockSpec((1, E, cap, 1), lambda b: (b, 0, 0, 0)),
                   pl.BlockSpec((1, E, cap, 1), lambda b: (b, 0, 0, 0))],
        out_shape=[jax.ShapeDtypeStruct((B, E, T), jnp.int32),
                   jax.ShapeDtypeStruct((B, E, cap, 1), jnp.int32),
                   jax.ShapeDtypeStruct((B, E, cap, 1), F32)],
        scratch_shapes=[pltpu.VMEM((2, E, T), F32)],
        compiler_params=_params("parallel"),
    )(aff)


def _moe_ffn_kernel(idx_ref, idxn_ref, h_hbm, wg_ref, wu_ref, wd_ref, g_ref, o_ref, xland, xbf, h1, sem,
                    *, nf, n_exp, cap, tokens_per_batch):
    e = pl.program_id(0)
    f = pl.program_id(1)
    rows = xland.shape[0]

    def issue_gather(ids_ref):
        for b in range(rows // cap):
            def body(c, carry, b=b):
                i = b * cap + c
                src_row = ids_ref[0, 0, i] + b * tokens_per_batch
                pltpu.make_async_copy(h_hbm.at[pl.ds(src_row, 1)], xland.at[pl.ds(i, 1)], sem.at[0]).start()
                return carry
            lax.fori_loop(0, cap, body, 0, unroll=8)

    @pl.when((e == 0) & (f == 0))
    def _():
        issue_gather(idx_ref)

    @pl.when(f == 0)
    def _():
        pltpu.make_async_copy(h_hbm.at[pl.ds(0, rows)], xland, sem.at[0]).wait()
        xbf[...] = xland[...].astype(BF16)

        @pl.when(e + 1 < n_exp)
        def _():
            issue_gather(idxn_ref)

    @pl.when(f < nf)
    def _():
        x = xbf[...]
        a = jnp.dot(x, wg_ref[0, 0].astype(BF16), preferred_element_type=F32)
        u = jnp.dot(x, wu_ref[0, 0].astype(BF16), preferred_element_type=F32)
        h1[jnp.minimum(f, nf - 1)] = (jax.nn.silu(a) * u).astype(BF16)

    @pl.when(f >= nf)
    def _():
        hid = jnp.concatenate([h1[k] for k in range(nf)], axis=1)
        y = jnp.dot(hid, wd_ref[0, 0].astype(BF16), preferred_element_type=F32)
        o_ref[...] = (y.reshape(o_ref.shape) * g_ref[...]).astype(o_ref.dtype)


def moe_ffn(idx3, hp, g, w_gate, w_up, w_down, layer, T, *, tf=512, tc=512):
    B, E, C = idx3.shape
    D = hp.shape[1]
    F = w_gate.shape[-1]
    nf = F // tf
    tc = min(tc, D)
    ids = idx3.transpose(1, 0, 2).reshape(E, 1, B * C)
    smem_blk = lambda imap: pl.BlockSpec((1, 1, B * C), imap, memory_space=pltpu.SMEM)
    up_blk = pl.BlockSpec((1, 1, D, tf), lambda e, s: (layer, e, 0, jnp.minimum(s, nf - 1)))
    return pl.pallas_call(
        functools.partial(_moe_ffn_kernel, nf=nf, n_exp=E, cap=C, tokens_per_batch=T),
        grid=(E, nf + D // tc),
        in_specs=[smem_blk(lambda e, s: (e, 0, 0)),
                  smem_blk(lambda e, s: (jnp.minimum(e + 1, E - 1), 0, 0)),
                  pl.BlockSpec(memory_space=pl.ANY),
                  up_blk,
                  up_blk,
                  pl.BlockSpec((1, 1, F, tc), lambda e, s: (layer, e, 0, jnp.maximum(s - nf, 0))),
                  pl.BlockSpec((B, 1, C, 1), lambda e, s: (0, e, 0, 0))],
        out_specs=pl.BlockSpec((B, 1, C, tc), lambda e, s: (0, e, 0, jnp.maximum(s - nf, 0))),
        out_shape=jax.ShapeDtypeStruct((B, E, C, D), BF16),
        scratch_shapes=[pltpu.VMEM((B * C, D), F32),
                        pltpu.VMEM((B * C, D), BF16),
                        pltpu.VMEM((nf, B * C, tf), BF16),
                        pltpu.SemaphoreType.DMA((1,))],
        compiler_params=_params("arbitrary", "arbitrary"),
    )(ids, ids, hp, w_gate, w_up, w_down, g)


def _moe_combine_kernel(starts_ref, ntr_ref, y_hbm, post_ref, x_ref, gate_ref, ng_ref, *rest,
                        n_exp, cap, win, steps_per_batch, modulated):
    ybuf, yext, acc, sem, sem_ext = rest[-5:]
    out_refs = rest[2:-5] if modulated else rest[:-5]
    s = pl.program_id(0)
    slot = s % 2

    def window(step, e, k):
        nominal = starts_ref[step * n_exp + e] + k * win
        return nominal, pl.multiple_of(jnp.minimum(nominal, cap - win), 16)

    def window_copy(step, e, k, dst, dsem):
        _, actual = window(step, e, k)
        return pltpu.make_async_copy(y_hbm.at[step // steps_per_batch, e, pl.ds(actual, win)],
                                     dst.at[pl.ds(e * win, win)], dsem)

    def onehot(step, k):
        post = post_ref[0]
        per = max(LANES // win, 1)
        lane = lax.broadcasted_iota(jnp.int32, (1, per * win), 1)
        cols = []
        for e0 in range(0, n_exp, per):
            slots = tok = None
            for q in range(per):
                nominal, actual = window(step, e0 + q, k)
                s_q = actual + lane - q * win
                s_q = jnp.where(s_q >= nominal, s_q, -2)
                t_q = post[:, e0 + q:e0 + q + 1]
                mine = lane >= q * win
                slots = s_q if q == 0 else jnp.where(mine, s_q, slots)
                tok = t_q if q == 0 else jnp.where(mine, t_q, tok)
            cols.append(jnp.where(tok == slots, 1.0, 0.0).astype(BF16))
        return jnp.concatenate(cols, axis=1)

    @pl.when(s == 0)
    def _():
        for e in range(n_exp):
            window_copy(0, e, 0, ybuf.at[0], sem.at[0]).start()

    for e in range(n_exp):
        window_copy(s, e, 0, ybuf.at[slot], sem.at[slot]).wait()

    @pl.when(s + 1 < pl.num_programs(0))
    def _():
        for e in range(n_exp):
            window_copy(s + 1, e, 0, ybuf.at[1 - slot], sem.at[1 - slot]).start()

    acc[...] = jnp.dot(onehot(s, 0), ybuf[slot], preferred_element_type=F32)

    def extra(k, carry):
        for e in range(n_exp):
            window_copy(s, e, k, yext, sem_ext.at[0]).start()
        for e in range(n_exp):
            window_copy(s, e, k, yext, sem_ext.at[0]).wait()
        acc[...] += jnp.dot(onehot(s, k), yext[...], preferred_element_type=F32)
        return carry

    lax.fori_loop(1, ntr_ref[s], extra, 0)
    x = x_ref[...] + gate_ref[0] * acc[...]
    y = x * lax.rsqrt(jnp.mean(x * x, axis=-1, keepdims=True) + NORM_EPS) * ng_ref[...]
    if modulated:
        out_refs[0][...] = x
        y = y * (1 + rest[1][0]) + rest[0][0]
    out_refs[-1][...] = y.astype(out_refs[-1].dtype)


def moe_combine(starts, ntr, y, post, x2, gate, norm_g, shift=None, scale=None, *, tj=256, win=64):
    B, E, C, D = y.shape
    M = x2.shape[0]
    T = M // B
    spb = T // tj
    modulated = shift is not None
    row_blk = pl.BlockSpec((tj, D), lambda s, *_: (s, 0))
    per_batch = pl.BlockSpec((1, 1, D), lambda s, *_: (s // spb, 0, 0))
    in_specs = [pl.BlockSpec(memory_space=pl.ANY),
                pl.BlockSpec((1, tj, E), lambda s, *_: (s // spb, s % spb, 0)),
                row_blk, per_batch, pl.BlockSpec((1, D), lambda s, *_: (0, 0))]
    args = [starts, ntr, y, post, x2, gate, norm_g.reshape(1, D)]
    if modulated:
        in_specs += [per_batch, per_batch]
        args += [shift, scale]
        out_specs = [row_blk, row_blk]
        out_shape = [jax.ShapeDtypeStruct((M, D), F32), jax.ShapeDtypeStruct((M, D), BF16)]
    else:
        out_specs = row_blk
        out_shape = jax.ShapeDtypeStruct((M, D), F32)
    grid_spec = pltpu.PrefetchScalarGridSpec(
        num_scalar_prefetch=2,
        grid=(M // tj,),
        in_specs=in_specs,
        out_specs=out_specs,
        scratch_shapes=[pltpu.VMEM((2, E * win, D), BF16),
                        pltpu.VMEM((E * win, D), BF16),
                        pltpu.VMEM((tj, D), F32),
                        pltpu.SemaphoreType.DMA((2,)),
                        pltpu.SemaphoreType.DMA((1,))],
    )
    return pl.pallas_call(
        functools.partial(_moe_combine_kernel, n_exp=E, cap=C, win=win, steps_per_batch=spb, modulated=modulated),
        grid_spec=grid_spec,
        out_shape=out_shape,
        compiler_params=_params("arbitrary"),
    )(*args)


def _small_rmsnorm(x, g):
    return x * lax.rsqrt(jnp.mean(x * x, axis=-1, keepdims=True) + NORM_EPS) * g


def _rope_tables(rows):
    row = jnp.broadcast_to(jnp.arange(rows)[:, None], (rows, GRID_W)).reshape(-1).astype(F32)
    col = jnp.broadcast_to(jnp.arange(GRID_W)[None, :], (rows, GRID_W)).reshape(-1).astype(F32)
    n_freq = MLA_ROPE_DIM // 4
    inv = ROPE_THETA ** (-jnp.arange(n_freq, dtype=F32) / n_freq)
    ang = jnp.concatenate([row[:, None] * inv, col[:, None] * inv], axis=-1)
    return jnp.cos(ang), jnp.sin(ang)


def _ada(cvec3, ada_w, ada_b, layer):
    R = cvec3.shape[0]
    a = jnp.concatenate([jax.nn.silu(cvec3), jnp.zeros((8 - R, cvec3.shape[1]), F32)], axis=0)
    m = matmul(a, ada_w, b_index=layer, tm=8, tn=512)[:R] + ada_b[layer]
    return m.reshape(R, 6, D_MODEL)


def _q_proj_kernel(cq_ref, w_ref, cos_ref, sin_ref, o_ref):
    cq, cos, sin = cq_ref[...], cos_ref[...], sin_ref[...]
    for hh in range(o_ref.shape[1]):
        w = w_ref[:, hh * 2 * QK_PAD:(hh + 1) * 2 * QK_PAD]
        acc = jnp.dot(cq, w, preferred_element_type=F32)
        o_ref[0, hh] = (acc[:, :QK_PAD] * cos + acc[:, QK_PAD:] * sin).astype(o_ref.dtype)


def q_proj(cqn, w_q2, cosq, sinq, B, L, *, tm=1024, hg=4):
    M, R = cqn.shape
    tpb = L // tm
    return pl.pallas_call(
        _q_proj_kernel,
        grid=(M // tm, MLA_HEADS // hg),
        in_specs=[pl.BlockSpec((tm, R), lambda i, h: (i, 0)),
                  pl.BlockSpec((R, hg * 2 * QK_PAD), lambda i, h: (0, h)),
                  pl.BlockSpec((tm, QK_PAD), lambda i, h: (i % tpb, 0)),
                  pl.BlockSpec((tm, QK_PAD), lambda i, h: (i % tpb, 0))],
        out_specs=pl.BlockSpec((1, hg, tm, QK_PAD), lambda i, h: (i // tpb, h, i % tpb, 0)),
        out_shape=jax.ShapeDtypeStruct((B, MLA_HEADS, L, QK_PAD), BF16),
        compiler_params=_params("parallel", "parallel"),
    )(cqn, w_q2, cosq, sinq)


def _q_weights(w_uq):
    R = w_uq.shape[0]
    half = MLA_ROPE_DIM // 2
    w = w_uq.reshape(R, MLA_HEADS, MLA_NOPE_DIM + MLA_ROPE_DIM)
    nope, rope = w[..., :MLA_NOPE_DIM], w[..., MLA_NOPE_DIM:]
    rot = jnp.concatenate([-rope[..., half:], rope[..., :half]], axis=-1)
    pad = jnp.zeros((R, MLA_HEADS, QK_PAD - MLA_NOPE_DIM - MLA_ROPE_DIM), w.dtype)
    parts = [nope, rope, pad, jnp.zeros_like(nope), rot, pad]
    return jnp.concatenate(parts, axis=-1).reshape(R, MLA_HEADS * 2 * QK_PAD).astype(BF16)


def _krope_t_kernel(w_ref, h_ref, *rest, roped):
    o_ref = rest[-1]
    x = lax.dot_general(w_ref[...], h_ref[...], (((1,), (1,)), ((), ())), preferred_element_type=F32)
    if roped:
        half = x.shape[0] // 2
        x1, x2, c, s = x[:half], x[half:], rest[0][...], rest[1][...]
        x = jnp.concatenate([x1 * c - x2 * s, x2 * c + x1 * s], axis=0)
    o_ref[0] = x.astype(o_ref.dtype)


def krope_t(h2, w_kr_t, B, Lx, tt, cos_t=None, sin_t=None):
    D = h2.shape[1]
    R = w_kr_t.shape[0]
    per = Lx // tt
    roped = cos_t is not None
    in_specs = [pl.BlockSpec((R, D), lambda b, i: (0, 0)), pl.BlockSpec((tt, D), lambda b, i: (b * per + i, 0))]
    args = [w_kr_t, h2]
    if roped:
        in_specs += [pl.BlockSpec((R // 2, tt), lambda b, i: (0, i))] * 2
        args += [cos_t, sin_t]
    return pl.pallas_call(
        functools.partial(_krope_t_kernel, roped=roped),
        grid=(B, per),
        in_specs=in_specs,
        out_specs=pl.BlockSpec((1, R, tt), lambda b, i: (b, 0, i)),
        out_shape=jax.ShapeDtypeStruct((B, R, Lx), BF16),
        compiler_params=_params("parallel", "parallel"),
    )(*args)


def _kv_proj_kernel(c_ref, wn_ref, wv_ref, kr_ref, kt_ref, v_ref):
    c = c_ref[0]
    kn_t = lax.dot_general(wn_ref[0], c, (((1,), (1,)), ((), ())), preferred_element_type=F32)
    rope_end = MLA_NOPE_DIM + MLA_ROPE_DIM
    kt_ref[0, 0, :MLA_NOPE_DIM] = kn_t.astype(kt_ref.dtype)
    kt_ref[0, 0, MLA_NOPE_DIM:rope_end] = kr_ref[0]
    kt_ref[0, 0, rope_end:] = jnp.zeros((QK_PAD - rope_end, c.shape[0]), kt_ref.dtype)
    v_ref[0, 0] = jnp.dot(c, wv_ref[0], preferred_element_type=F32).astype(v_ref.dtype)


def kv_proj(ckvn, wn_t, wv, kr_t, *, tk=2176):
    B, K, R = ckvn.shape
    H, V = wv.shape[0], wv.shape[2]
    return pl.pallas_call(
        _kv_proj_kernel,
        grid=(B, H, K // tk),
        in_specs=[pl.BlockSpec((1, tk, R), lambda b, h, i: (b, i, 0)),
                  pl.BlockSpec((1, MLA_NOPE_DIM, R), lambda b, h, i: (h, 0, 0)),
                  pl.BlockSpec((1, R, V), lambda b, h, i: (h, 0, 0)),
                  pl.BlockSpec((1, MLA_ROPE_DIM, tk), lambda b, h, i: (b, 0, i))],
        out_specs=[pl.BlockSpec((1, 1, QK_PAD, tk), lambda b, h, i: (b, h, 0, i)),
                   pl.BlockSpec((1, 1, tk, V), lambda b, h, i: (b, h, i, 0))],
        out_shape=[jax.ShapeDtypeStruct((B, H, QK_PAD, K), BF16), jax.ShapeDtypeStruct((B, H, K, V), BF16)],
        compiler_params=_params("parallel", "parallel", "parallel"),
    )(ckvn, wn_t, wv, kr_t)


def _mla_layer(h, h_c, B, L, Lc, w_in, g_q, g_kv, w_uq, w_ukv):
    H = MLA_HEADS
    lat = MLA_Q_RANK + MLA_KV_RANK
    w_in_b = w_in.astype(BF16)
    proj = matmul(h, w_in_b[:, :lat], tn=lat)
    ckv_c = matmul(h_c, w_in_b[:, MLA_Q_RANK:lat], tn=MLA_KV_RANK)
    ckv = jnp.concatenate([proj[:, MLA_Q_RANK:].reshape(B, L, -1), ckv_c.reshape(B, Lc, -1)], axis=1)
    cos, sin = _rope_tables(L // GRID_W)
    ones, zeros = jnp.ones((L, MLA_NOPE_DIM), F32), jnp.zeros((L, MLA_NOPE_DIM), F32)
    pad = jnp.zeros((L, QK_PAD - MLA_NOPE_DIM - MLA_ROPE_DIM), F32)
    cosq = jnp.concatenate([ones, cos, cos, pad], axis=1)
    sinq = jnp.concatenate([zeros, sin, sin, pad], axis=1)
    qp = q_proj(_small_rmsnorm(proj[:, :MLA_Q_RANK], g_q).astype(BF16), _q_weights(w_uq), cosq, sinq, B, L)
    w_kr_t = w_in_b[:, lat:].T
    kr_t = jnp.concatenate([krope_t(h, w_kr_t, B, L, 1024, cos.T, sin.T), krope_t(h_c, w_kr_t, B, Lc, Lc)], axis=2)
    w_kv = w_ukv.astype(BF16).reshape(MLA_KV_RANK, H, MLA_NOPE_DIM + MLA_V_DIM)
    wn_t = w_kv[..., :MLA_NOPE_DIM].transpose(1, 2, 0)
    wv = w_kv[..., MLA_NOPE_DIM:].transpose(1, 0, 2)
    kt, v = kv_proj(_small_rmsnorm(ckv, g_kv).astype(BF16), wn_t, wv, kr_t)
    scale = (MLA_NOPE_DIM + MLA_ROPE_DIM) ** -0.5
    return attention(qp, kt, v, scale=scale)


def _fft_tables(L):
    n = 2 * L
    n2_in = L // FFT_N1
    k2 = np.arange(FFT_K2)
    th = 2 * np.pi * ((k2[:, None] * np.arange(n2_in)[None, :]) % FFT_N2) / FFT_N2
    f1 = np.zeros((2 * FFT_K2P, n2_in))
    f1[:FFT_K2] = np.cos(th)
    f1[FFT_K2P:FFT_K2P + FFT_K2] = -np.sin(th)
    wgt = np.where((k2 == 0) | (k2 == FFT_N2 // 2), 1.0, 2.0)
    g = np.zeros((n2_in, 2 * FFT_K2P))
    g[:, :FFT_K2] = (wgt[:, None] * np.cos(th)).T / n
    g[:, FFT_K2P:FFT_K2P + FFT_K2] = -(wgt[:, None] * np.sin(th)).T / n
    k1 = np.arange(FFT_N1)[:, None]
    n1 = np.arange(FFT_N1)[None, :]
    m2 = np.zeros((FFT_K2, 2 * FFT_N1, 2 * FFT_N1))
    for kk in range(FFT_K2):
        phi = 2 * np.pi * ((n1 * k1 * FFT_N2 + n1 * kk) % n) / n
        tr, ti = np.cos(phi), -np.sin(phi)
        m2[kk] = np.block([[tr, -ti], [ti, tr]])
    as_bf16 = lambda a: jnp.asarray(a, dtype=F32).astype(BF16)
    return (as_bf16(np.kron(f1, np.eye(FFT_SUB))), as_bf16(np.kron(g, np.eye(FFT_GROUP))), as_bf16(m2),
            as_bf16(np.transpose(m2, (0, 2, 1))))


def _fft_s1_kernel(f_ref, z_ref, o_ref):
    _, n2_in, g, ct = z_ref.shape
    z = z_ref[0].astype(F32)
    parts = []
    for s0 in range(0, g, FFT_SUB):
        zz = z[:, s0:s0 + FFT_SUB, :].reshape(n2_in * FFT_SUB, ct).astype(BF16)
        a = jnp.dot(f_ref[...], zz, preferred_element_type=F32)
        parts.append(a.reshape(2, FFT_K2P, FFT_SUB, ct))
    o_ref[0] = jnp.concatenate(parts, axis=2).astype(o_ref.dtype)


def fft_stage1(z4, f1k, *, lead_off, nb, ct=2048):
    _, n2_in, _, C = z4.shape
    g = FFT_GROUP
    return pl.pallas_call(
        _fft_s1_kernel,
        grid=(nb, FFT_N1 // g, C // ct),
        in_specs=[pl.BlockSpec(f1k.shape, lambda b, i, j: (0, 0)),
                  pl.BlockSpec((1, n2_in, g, ct), lambda b, i, j: (lead_off + b, 0, i, j))],
        out_specs=pl.BlockSpec((1, 2, FFT_K2P, g, ct), lambda b, i, j: (b, 0, 0, i, j)),
        out_shape=jax.ShapeDtypeStruct((nb, 2, FFT_K2P, FFT_N1, C), BF16),
        compiler_params=_params("parallel", "parallel", "parallel"),
    )(f1k, z4)


def _fft_s2_kernel(a_ref, m_ref, mt_ref, kf_ref, o_ref):
    k2 = pl.program_id(0)

    @pl.when(k2 < FFT_K2)
    def _():
        ct = a_ref.shape[-1]
        x = jnp.dot(m_ref[0], a_ref[...].reshape(2 * FFT_N1, ct), preferred_element_type=F32)
        xr, xi = x[:FFT_N1], x[FFT_N1:]
        kr, ki = kf_ref[0, 0, 0].astype(F32), kf_ref[0, 1, 0].astype(F32)
        p = jnp.concatenate([xr * kr - xi * ki, xr * ki + xi * kr], axis=0).astype(BF16)
        e = jnp.dot(mt_ref[0], p, preferred_element_type=F32)
        o_ref[...] = e.astype(o_ref.dtype).reshape(o_ref.shape)

    @pl.when(k2 >= FFT_K2)
    def _():
        o_ref[...] = jnp.zeros(o_ref.shape, o_ref.dtype)


def fft_stage2(a5, m2, m2t, kf, order):
    B, _, _, _, D = a5.shape
    last = FFT_K2 - 1
    blk = (1, 2, 1, FFT_N1, D)
    return pl.pallas_call(
        _fft_s2_kernel,
        grid=(FFT_K2P, B),
        in_specs=[pl.BlockSpec(blk, lambda k, b: (b, 0, jnp.minimum(k, last), 0, 0)),
                  pl.BlockSpec((1, 2 * FFT_N1, 2 * FFT_N1), lambda k, b: (jnp.minimum(k, last), 0, 0)),
                  pl.BlockSpec((1, 2 * FFT_N1, 2 * FFT_N1), lambda k, b: (jnp.minimum(k, last), 0, 0)),
                  pl.BlockSpec(blk, lambda k, b: (order, 0, jnp.minimum(k, last), 0, 0))],
        out_specs=pl.BlockSpec(blk, lambda k, b: (b, 0, k, 0, 0)),
        out_shape=jax.ShapeDtypeStruct(a5.shape, BF16),
        compiler_params=_params("parallel", "arbitrary"),
    )(a5, m2, m2t, kf)


def _fft_f2_kernel(f_ref, g_ref, m_ref, inv_ref, o_ref):
    k2 = pl.program_id(0)

    @pl.when(k2 < FFT_K2)
    def _():
        ct = f_ref.shape[-1]
        xf = jnp.dot(m_ref[0], f_ref[...].reshape(2 * FFT_N1, ct), preferred_element_type=F32)
        xg = jnp.dot(m_ref[0], g_ref[...].reshape(2 * FFT_N1, ct), preferred_element_type=F32)
        inv = inv_ref[0]
        o_ref[0, 0, 0] = ((xf[:FFT_N1] + xg[:FFT_N1]) * inv).astype(o_ref.dtype)
        o_ref[0, 1, 0] = ((xf[FFT_N1:] - xg[FFT_N1:]) * inv).astype(o_ref.dtype)

    @pl.when(k2 >= FFT_K2)
    def _():
        o_ref[...] = jnp.zeros(o_ref.shape, o_ref.dtype)


def fft_filter_spectrum(a4, m2, inv, D):
    last = FFT_K2 - 1
    blk = (2, 1, FFT_N1, D)
    return pl.pallas_call(
        _fft_f2_kernel,
        grid=(FFT_K2P, HYENA_ORDER),
        in_specs=[pl.BlockSpec(blk, lambda k, o: (0, jnp.minimum(k, last), 0, 2 * o)),
                  pl.BlockSpec(blk, lambda k, o: (0, jnp.minimum(k, last), 0, 2 * o + 1)),
                  pl.BlockSpec((1, 2 * FFT_N1, 2 * FFT_N1), lambda k, o: (jnp.minimum(k, last), 0, 0)),
                  pl.BlockSpec((1, 1, D), lambda k, o: (o, 0, 0))],
        out_specs=pl.BlockSpec((1, 2, 1, FFT_N1, D), lambda k, o: (o, 0, k, 0, 0)),
        out_shape=jax.ShapeDtypeStruct((HYENA_ORDER, 2, FFT_K2P, FFT_N1, D), BF16),
        compiler_params=_params("arbitrary", "parallel"),
    )(a4, a4, m2, inv)


def _fft_s3_kernel(g_ref, e_ref, z_ref, x_ref, skip_ref, o_ref):
    _, n2_in, g, ct = z_ref.shape
    e = e_ref[0].reshape(2 * FFT_K2P * g, ct)
    y = jnp.dot(g_ref[...], e, preferred_element_type=F32).reshape(n2_in, g, ct)
    o_ref[0] = (x_ref[0] * (y + skip_ref[...] * z_ref[0])).astype(o_ref.dtype)


def fft_stage3(e5, gk, z4, z_off, x4, x_off, skip, *, out_dtype):
    B, _, _, _, D = e5.shape
    n2_in = z4.shape[1]
    g = FFT_GROUP
    return pl.pallas_call(
        _fft_s3_kernel,
        grid=(B, FFT_N1 // g),
        in_specs=[pl.BlockSpec(gk.shape, lambda b, i: (0, 0)),
                  pl.BlockSpec((1, 2, FFT_K2P, g, D), lambda b, i: (b, 0, 0, i, 0)),
                  pl.BlockSpec((1, n2_in, g, D), lambda b, i: (z_off + b, 0, i, 0)),
                  pl.BlockSpec((1, n2_in, g, D), lambda b, i: (x_off + b, 0, i, 0)),
                  pl.BlockSpec((1, D), lambda b, i: (0, 0))],
        out_specs=pl.BlockSpec((1, n2_in, g, D), lambda b, i: (b, 0, i, 0)),
        out_shape=jax.ShapeDtypeStruct((B, n2_in, FFT_N1, D), out_dtype),
        compiler_params=_params("parallel", "parallel"),
    )(gk, e5, z4, x4, skip.reshape(1, D))


def _hyin_kernel(ap_ref, a_ref, an_ref, b_ref, cw_ref, cb_ref, o_ref, *, tiles_per_seq):
    i = pl.program_id(0)
    t = i % tiles_per_seq
    halo = ap_ref.shape[0]
    tm, tn = o_ref.shape[1:]
    cw = MXU_DIM
    row = lax.broadcasted_iota(jnp.int32, (tm, cw), 0)
    at_start = jnp.logical_and(row == 0, t == 0)
    at_end = jnp.logical_and(row == tm - 1, t == tiles_per_seq - 1)
    a_ext = jnp.concatenate([ap_ref[...], a_ref[...], an_ref[...]], axis=0)
    for c0 in range(0, tn, cw):
        acc = jnp.dot(a_ext, b_ref[:, c0:c0 + cw], preferred_element_type=F32)
        up = jnp.where(at_start, 0.0, acc[halo - 1:halo - 1 + tm])
        dn = jnp.where(at_end, 0.0, acc[halo + 1:halo + 1 + tm])
        w = cw_ref[:, c0:c0 + cw]
        o_ref[0, :, c0:c0 + cw] = up * w[0:1] + acc[halo:halo + tm] * w[1:2] + dn * w[2:3] + cb_ref[:, c0:c0 + cw]


def hyena_in(h, w_in, conv_w, conv_b, L, *, tm=512, tn=1024, halo=16):
    M, K = h.shape
    N = w_in.shape[1]
    D = N // 3
    per = D // tn
    return pl.pallas_call(
        functools.partial(_hyin_kernel, tiles_per_seq=L // tm),
        grid=(M // tm, N // tn),
        in_specs=[pl.BlockSpec((halo, K), lambda i, j: (jnp.maximum(i * (tm // halo) - 1, 0), 0)),
                  pl.BlockSpec((tm, K), lambda i, j: (i, 0)),
                  pl.BlockSpec((halo, K), lambda i, j: (jnp.minimum((i + 1) * (tm // halo), M // halo - 1), 0)),
                  pl.BlockSpec((K, tn), lambda i, j: (0, j)),
                  pl.BlockSpec((3, tn), lambda i, j: (0, j)),
                  pl.BlockSpec((1, tn), lambda i, j: (0, j))],
        out_specs=pl.BlockSpec((1, tm, tn), lambda i, j: (j // per, i, j % per)),
        out_shape=jax.ShapeDtypeStruct((3, M, D), F32),
        compiler_params=_params("parallel", "parallel"),
    )(h, h, h, w_in, conv_w, conv_b.reshape(1, N))


def _taps_kernel(a_ref, w_ref, b_ref, dl_ref, keep_ref, o_ref, ss_ref, *, seq_len):
    i = pl.program_id(1)
    tm = a_ref.shape[0]
    acc = jnp.dot(a_ref[...].astype(BF16), w_ref[...].astype(BF16), preferred_element_type=F32) + b_ref[...]
    row = i * tm + lax.broadcasted_iota(jnp.int32, acc.shape, 0)
    t = row.astype(F32) / seq_len
    hf = acc * jnp.exp(-t * dl_ref[...])
    hf = jnp.where(row == 0, hf * keep_ref[...], hf)
    o_ref[...] = hf.astype(o_ref.dtype)
    part = jnp.sum(hf * hf, axis=0, keepdims=True)

    @pl.when(i == 0)
    def _():
        ss_ref[...] = part

    @pl.when(i > 0)
    def _():
        ss_ref[...] += part


def _hyena_filter_taps(L, w1, b1, w2, b2, w3, b3, freq, *, tm=512, tn=1024):
    D = D_MODEL
    N = 2 * HYENA_ORDER * D
    t = jnp.arange(L, dtype=F32) / L
    w = 2 * math.pi * jnp.arange(L, dtype=F32) / L
    bands = jnp.linspace(1e-4, HYENA_BANDS - 1, HYENA_BANDS, dtype=F32)
    ang = w[:, None] * bands[None, :]
    z = jnp.concatenate([t[:, None], jnp.cos(ang), -jnp.sin(ang)], axis=-1)
    a = jnp.sin(freq * (z @ w1 + b1))
    a = jnp.sin(freq * (a @ w2 + b2))
    max_decay = math.log(HYENA_TARGET) / HYENA_FAST_PCT
    min_decay = math.log(HYENA_TARGET) / HYENA_SLOW_PCT
    decay = jnp.tile(jnp.abs(jnp.linspace(min_decay, max_decay, D, dtype=F32)), 2 * HYENA_ORDER).reshape(1, N)
    keep0 = jnp.where((jnp.arange(N) // D) % 2 == 1, 0.0, 1.0).astype(F32).reshape(1, N)
    K = a.shape[1]
    taps, ss = pl.pallas_call(
        functools.partial(_taps_kernel, seq_len=L),
        grid=(N // tn, L // tm),
        in_specs=[pl.BlockSpec((tm, K), lambda j, i: (i, 0)),
                  pl.BlockSpec((K, tn), lambda j, i: (0, j)),
                  pl.BlockSpec((1, tn), lambda j, i: (0, j)),
                  pl.BlockSpec((1, tn), lambda j, i: (0, j)),
                  pl.BlockSpec((1, tn), lambda j, i: (0, j))],
        out_specs=[pl.BlockSpec((tm, tn), lambda j, i: (i, j)), pl.BlockSpec((1, tn), lambda j, i: (0, j))],
        out_shape=[jax.ShapeDtypeStruct((L, N), BF16), jax.ShapeDtypeStruct((1, N), F32)],
        compiler_params=_params("parallel", "arbitrary"),
    )(a, w3, b3.reshape(1, N), decay, keep0)
    ss = ss.reshape(HYENA_ORDER, 2, D).sum(axis=1, keepdims=True)
    return taps, lax.rsqrt(ss + NORM_EPS)


def _hyena_layer(h, B, L, w_in, conv_w, conv_b, f_w1, f_b1, f_w2, f_b2, f_w3, f_b3, f_freq, skip):
    D = D_MODEL
    n2_in = L // FFT_N1
    f1k, gk, m2, m2t = _fft_tables(L)
    taps, inv = _hyena_filter_taps(L, f_w1, f_b1, f_w2, f_b2, f_w3, f_b3, f_freq)
    af = fft_stage1(taps.reshape(1, n2_in, FFT_N1, taps.shape[1]), f1k, lead_off=0, nb=1)
    kf = fft_filter_spectrum(af[0], m2, inv, D)
    u4 = hyena_in(h, w_in.astype(BF16), conv_w, conv_b, L).reshape(3 * B, n2_in, FFT_N1, D)
    z4 = u4
    for order in range(HYENA_ORDER):
        a = fft_stage1(z4, f1k, lead_off=0, nb=B)
        e = fft_stage2(a, m2, m2t, kf, order)
        last = order == HYENA_ORDER - 1
        z4 = fft_stage3(e, gk, z4, 0, u4, (order + 1) * B, skip[order], out_dtype=BF16 if last else F32)
    return z4.reshape(B * L, D)


def _moe_layer(x2, norm_gain, mod, B, T, w_router, w_gate, w_up, w_down, layer, next_g, next_shift=None,
               next_scale=None, *, tj=256, win=64):
    E = N_EXPERTS
    cap = EC_CAPACITY_FACTOR * T // E
    hp, aff = norm_route(x2, norm_gain, mod[:, 3:4], mod[:, 4:5], w_router.T.astype(BF16), rows_per_batch=T)
    pos, idx, g = route_select(aff, cap)
    idx3 = idx.reshape(B, E, cap)
    y = moe_ffn(idx3, hp, g, w_gate, w_up, w_down, layer, T)
    nj = T // tj
    edges = jnp.arange(nj + 1, dtype=jnp.int32) * tj
    first = jnp.sum((idx3[..., None] < edges).astype(jnp.int32), axis=2)
    nominal = (first[..., :nj] // 16) * 16
    ntr = jnp.maximum(jnp.max((first[..., 1:] - nominal + win - 1) // win, axis=1), 1)
    starts = nominal.transpose(0, 2, 1).reshape(-1)
    return moe_combine(starts, ntr.reshape(-1), y, pos.transpose(0, 2, 1), x2, mod[:, 5:6], next_g, next_shift,
                       next_scale, tj=tj, win=win)


def kernel(x, c, ctx, c_ctx, ada_w, ada_b, norm_g, final_g, mla_w_in, mla_g_q, mla_g_kv, mla_w_uq, mla_w_ukv, mla_w_o, hy_w_in, hy_conv_w, hy_conv_b, hy_f_w1, hy_f_b1, hy_f_w2, hy_f_b2, hy_f_w3, hy_f_b3, hy_f_freq, hy_skip, hy_w_out, moe_w_router, moe_w_gate, moe_w_up, moe_w_down):
    B, L, D = x.shape
    Lc = ctx.shape[1]
    x2 = x.reshape(B * L, D)
    ctx2 = ctx.reshape(B * Lc, D)

    mod3 = _ada(jnp.concatenate([c, c_ctx[None]], axis=0), ada_w, ada_b, 0)
    mod, mod_c = mod3[:B], jnp.broadcast_to(mod3[B:], (B, 6, D))
    h = norm_mod(x2, norm_g[0, 0], mod[:, 0:1], mod[:, 1:2], rows_per_batch=L)
    h_c = norm_mod(ctx2, norm_g[0, 0], mod_c[:, 0:1], mod_c[:, 1:2], rows_per_batch=Lc)
    o = _mla_layer(h, h_c, B, L, Lc, mla_w_in[0], mla_g_q[0], mla_g_kv[0], mla_w_uq[0], mla_w_ukv[0])
    x2 = matmul(o.reshape(B * L, -1), mla_w_o[0].astype(BF16), tn=1024,
                res=x2, gate=mod[:, 2:3], rows_per_batch=L)
    mod1 = _ada(c, ada_w, ada_b, 1)
    x2, h = _moe_layer(x2, norm_g[0, 1], mod, B, L, moe_w_router[0], moe_w_gate, moe_w_up, moe_w_down, 0,
                       norm_g[1, 0], mod1[:, 0:1], mod1[:, 1:2])

    mod = mod1
    z = _hyena_layer(h, B, L, hy_w_in[0], hy_conv_w[0], hy_conv_b[0], hy_f_w1[0], hy_f_b1[0], hy_f_w2[0],
                     hy_f_b2[0], hy_f_w3[0], hy_f_b3[0], hy_f_freq[0], hy_skip[0])
    x2 = matmul(z, hy_w_out[0].astype(BF16), tn=1024, res=x2, gate=mod[:, 2:3], rows_per_batch=L)
    out = _moe_layer(x2, norm_g[1, 1], mod, B, L, moe_w_router[1], moe_w_gate, moe_w_up, moe_w_down, 1, final_g)
    return out.reshape(B, L, D)
```

```python
import functools
import math

import jax
import jax.numpy as jnp
import numpy as np
from jax import lax
from jax.experimental import pallas as pl
from jax.experimental.pallas import tpu as pltpu

F32 = jnp.float32
BF16 = jnp.bfloat16

D_MODEL = 2048
GRID_W = 64
NORM_EPS = 1e-6
MLA_HEADS = 16
MLA_Q_RANK = 512
MLA_KV_RANK = 256
MLA_NOPE_DIM = 128
MLA_ROPE_DIM = 64
MLA_V_DIM = 128
ROPE_THETA = 10000.0
HYENA_ORDER = 2
HYENA_BANDS = 16
HYENA_TARGET = 1e-2
HYENA_FAST_PCT = 0.3
HYENA_SLOW_PCT = 1.5
N_EXPERTS = 16
EC_CAPACITY_FACTOR = 2
FFT_N1 = 128
FFT_N2 = 64
FFT_K2 = FFT_N2 // 2 + 1
FFT_K2P = 40
FFT_GROUP = 16
FFT_SUB = 8
FFT_KSTEP = 2

V7X_VMEM_LIMIT_BYTES = 56 * 1024 * 1024
LANES = 128
MXU_DIM = 256
QK_PAD = MXU_DIM


def _params(*sem):
    return pltpu.CompilerParams(dimension_semantics=sem, vmem_limit_bytes=V7X_VMEM_LIMIT_BYTES)


def _mm_kernel(a_ref, b_ref, *rest, b_is_stacked, has_res):
    o_ref = rest[-1]
    b = b_ref[0] if b_is_stacked else b_ref[...]
    acc = jnp.dot(a_ref[...].astype(BF16), b.astype(BF16), preferred_element_type=F32)
    if has_res:
        res_ref, gate_ref = rest[0], rest[1]
        acc = res_ref[...] + gate_ref[0] * acc
    o_ref[...] = acc.astype(o_ref.dtype)


def matmul(a, b, *, b_index=None, out_dtype=F32, tm=512, tn=512, res=None, gate=None, rows_per_batch=None):
    M, K = a.shape
    N = b.shape[-1]
    tm, tn = min(tm, M), min(tn, N)
    assert M % tm == 0 and N % tn == 0, (M, N, tm, tn)
    stacked = b.ndim == 3
    if stacked:
        b_spec = pl.BlockSpec((1, K, tn), lambda i, j: (b_index, 0, j))
    else:
        b_spec = pl.BlockSpec((K, tn), lambda i, j: (0, j))
    in_specs = [pl.BlockSpec((tm, K), lambda i, j: (i, 0)), b_spec]
    args = [a, b]
    if res is not None:
        assert rows_per_batch % tm == 0
        tpb = rows_per_batch // tm
        in_specs += [pl.BlockSpec((tm, tn), lambda i, j: (i, j)),
                     pl.BlockSpec((1, 1, tn), lambda i, j: (i // tpb, 0, j))]
        args += [res, gate]
    return pl.pallas_call(
        functools.partial(_mm_kernel, b_is_stacked=stacked, has_res=res is not None),
        grid=(M // tm, N // tn),
        in_specs=in_specs,
        out_specs=pl.BlockSpec((tm, tn), lambda i, j: (i, j)),
        out_shape=jax.ShapeDtypeStruct((M, N), out_dtype),
        compiler_params=_params("parallel", "parallel"),
    )(*args)


def _norm_kernel(x_ref, g_ref, *rest, modulated):
    o_ref = rest[-1]
    x = x_ref[...]
    y = x * lax.rsqrt(jnp.mean(x * x, axis=-1, keepdims=True) + NORM_EPS) * g_ref[...]
    if modulated:
        shift_ref, scale_ref = rest[0], rest[1]
        y = y * (1 + scale_ref[0]) + shift_ref[0]
    o_ref[...] = y.astype(o_ref.dtype)


def norm_mod(x2, g, shift=None, scale=None, *, rows_per_batch=None, out_dtype=BF16, tm=256):
    M, D = x2.shape
    tm = min(tm, M)
    in_specs = [pl.BlockSpec((tm, D), lambda i: (i, 0)), pl.BlockSpec((1, D), lambda i: (0, 0))]
    args = [x2, g.reshape(1, D)]
    if shift is not None:
        tpb = rows_per_batch // tm
        in_specs += [pl.BlockSpec((1, 1, D), lambda i: (i // tpb, 0, 0))] * 2
        args += [shift, scale]
    return pl.pallas_call(
        functools.partial(_norm_kernel, modulated=shift is not None),
        grid=(M // tm,),
        in_specs=in_specs,
        out_specs=pl.BlockSpec((tm, D), lambda i: (i, 0)),
        out_shape=jax.ShapeDtypeStruct((M, D), out_dtype),
        compiler_params=_params("parallel"),
    )(*args)


def _attn_kernel(q_ref, kt_ref, v_ref, o_ref, *, chunks, c):
    q = q_ref[0, 0]
    m = l = acc = None
    for lo, hi in chunks:
        s = jnp.dot(q, kt_ref[0, 0, :, lo:hi], preferred_element_type=F32)
        m_chunk = jnp.max(s, axis=-1, keepdims=True)
        m_new = m_chunk if m is None else jnp.maximum(m, m_chunk)
        p = jnp.exp2(s * c - m_new * c)
        pv = jnp.dot(p.astype(BF16), v_ref[0, 0, lo:hi, :], preferred_element_type=F32)
        psum = jnp.sum(p, axis=-1, keepdims=True)
        if m is None:
            l, acc = psum, pv
        else:
            alpha = jnp.exp2((m - m_new) * c)
            l = alpha * l + psum
            acc = alpha * acc + pv
        m = m_new
    o_ref[0] = (acc / l).astype(o_ref.dtype)


def attention(q, kt, v, *, scale, tq=1024, tk=1024):
    B, H, L, _ = q.shape
    K = kt.shape[-1]
    V = v.shape[-1]
    chunks = tuple((lo, min(lo + tk, K)) for lo in range(0, K, tk))
    return pl.pallas_call(
        functools.partial(_attn_kernel, chunks=chunks, c=scale * math.log2(math.e)),
        grid=(B, H, L // tq),
        in_specs=[pl.BlockSpec((1, 1, tq, QK_PAD), lambda b, h, i: (b, h, i, 0)),
                  pl.BlockSpec((1, 1, QK_PAD, K), lambda b, h, i: (b, h, 0, 0)),
                  pl.BlockSpec((1, 1, K, V), lambda b, h, i: (b, h, 0, 0))],
        out_specs=pl.BlockSpec((1, tq, V), lambda b, h, i: (b, i, h)),
        out_shape=jax.ShapeDtypeStruct((B, L, H * V), BF16),
        compiler_params=_params("parallel", "parallel", "arbitrary"),
    )(q, kt, v)


def _norm_route_kernel(x_ref, g_ref, shift_ref, scale_ref, wr_ref, h_ref, aff_ref):
    x = x_ref[...]
    y = x * lax.rsqrt(jnp.mean(x * x, axis=-1, keepdims=True) + NORM_EPS) * g_ref[...]
    y = y * (1 + scale_ref[0]) + shift_ref[0]
    h_ref[...] = y
    logits = lax.dot_general(wr_ref[...], y.astype(BF16), (((1,), (1,)), ((), ())),
                             preferred_element_type=F32)
    p = jnp.exp(logits - jnp.max(logits, axis=0, keepdims=True))
    aff_ref[0] = p / jnp.sum(p, axis=0, keepdims=True)


def norm_route(x2, g, shift, scale, w_router_t, *, rows_per_batch, tm=256):
    M, D = x2.shape
    E = w_router_t.shape[0]
    tpb = rows_per_batch // tm
    return pl.pallas_call(
        _norm_route_kernel,
        grid=(M // tm,),
        in_specs=[pl.BlockSpec((tm, D), lambda i: (i, 0)),
                  pl.BlockSpec((1, D), lambda i: (0, 0)),
                  pl.BlockSpec((1, 1, D), lambda i: (i // tpb, 0, 0)),
                  pl.BlockSpec((1, 1, D), lambda i: (i // tpb, 0, 0)),
                  pl.BlockSpec((E, D), lambda i: (0, 0))],
        out_specs=[pl.BlockSpec((tm, D), lambda i: (i, 0)),
                   pl.BlockSpec((1, E, tm), lambda i: (i // tpb, 0, i % tpb))],
        out_shape=[jax.ShapeDtypeStruct((M, D), F32),
                   jax.ShapeDtypeStruct((M // rows_per_batch, E, rows_per_batch), F32)],
        compiler_params=_params("parallel"),
    )(x2, g.reshape(1, D), shift, scale, w_router_t)


def _lane_prefix_sum(x):
    E, T = x.shape
    r = lax.broadcasted_iota(jnp.int32, (LANES, LANES), 0)
    c = lax.broadcasted_iota(jnp.int32, (LANES, LANES), 1)
    tri = jnp.where(r <= c, 1.0, 0.0).astype(BF16)
    outs, carry = [], jnp.zeros((E, 1), F32)
    for j in range(T // LANES):
        seg = jnp.dot(x[:, j * LANES:(j + 1) * LANES].astype(BF16), tri, preferred_element_type=F32) + carry
        outs.append(seg)
        carry = seg[:, LANES - 1:LANES]
    return jnp.concatenate(outs, axis=1)


def _route_select_kernel(aff_ref, pos_ref, idx_ref, gate_ref, cum_s, *, cap, tch):
    aff = aff_ref[0]
    E, T = aff.shape

    def bisect(state):
        lo, hi, _ = state
        mid = 0.5 * (lo + hi)
        cnt = jnp.sum(jnp.where(aff >= mid, 1.0, 0.0), axis=1, keepdims=True)
        ge = cnt >= cap
        open_rows = jnp.where((mid > lo) & (mid < hi), 1.0, 0.0)
        return jnp.where(ge, mid, lo), jnp.where(ge, hi, mid), jnp.max(open_rows)

    lo0 = jnp.min(aff, axis=1, keepdims=True)
    hi0 = 2.0 * jnp.max(aff, axis=1, keepdims=True) + 1e-30
    lo, hi, _ = lax.while_loop(lambda st: st[2] > 0.0, bisect, (lo0, hi0, jnp.float32(1.0)))
    gt = aff >= hi
    eq = (aff >= lo) & (aff < hi)
    need = cap - jnp.sum(jnp.where(gt, 1.0, 0.0), axis=1, keepdims=True)
    eq_rank = _lane_prefix_sum(jnp.where(eq, 1.0, 0.0))
    sel = gt | (eq & (eq_rank <= need))
    cum = _lane_prefix_sum(jnp.where(sel, 1.0, 0.0))
    cumsel = jnp.where(sel, cum, 0.0)
    pos_ref[0] = (cumsel - 1.0).astype(jnp.int32)
    cum_s[0] = cum
    cum_s[1] = cumsel
    slot = lax.broadcasted_iota(jnp.int32, (cap, 1), 0).astype(F32)

    def per_expert(e, carry):
        idx_acc = jnp.zeros((cap, 1), F32)
        g_acc = jnp.zeros((cap, 1), F32)
        for lo in range(0, T, tch):
            cum_e = cum_s[0, pl.ds(e, 1), lo:lo + tch]
            cs_e = cum_s[1, pl.ds(e, 1), lo:lo + tch]
            aff_e = aff_ref[0, pl.ds(e, 1), lo:lo + tch]
            idx_acc = idx_acc + jnp.sum(jnp.where(cum_e <= slot, 1.0, 0.0), axis=1, keepdims=True)
            g_acc = g_acc + jnp.sum(jnp.where(cs_e == slot + 1.0, aff_e, 0.0), axis=1, keepdims=True)
        idx_ref[0, pl.ds(e, 1)] = idx_acc.astype(jnp.int32).reshape(1, cap, 1)
        gate_ref[0, pl.ds(e, 1)] = g_acc.reshape(1, cap, 1)
        return carry

    lax.fori_loop(0, E, per_expert, 0)


def route_select(aff, cap, *, tch=1024):
    B, E, T = aff.shape
    return pl.pallas_call(
        functools.partial(_route_select_kernel, cap=cap, tch=tch),
        grid=(B,),
        in_specs=[pl.BlockSpec((1, E, T), lambda b: (b, 0, 0))],
        out_specs=[pl.BlockSpec((1, E, T), lambda b: (b, 0, 0)),
                   pl.BlockSpec((1, E, cap, 1), lambda b: (b, 0, 0, 0)),
                   pl.BlockSpec((1, E, cap, 1), lambda b: (b, 0, 0, 0))],
        out_shape=[jax.ShapeDtypeStruct((B, E, T), jnp.int32),
                   jax.ShapeDtypeStruct((B, E, cap, 1), jnp.int32),
                   jax.ShapeDtypeStruct((B, E, cap, 1), F32)],
        scratch_shapes=[pltpu.VMEM((2, E, T), F32)],
        compiler_params=_params("parallel"),
    )(aff)


def _moe_ffn_kernel(idx_ref, idxn_ref, h_hbm, wg_ref, wu_ref, wd_ref, g_ref, o_ref, xland, xbf, h1, sem,
                    *, nf, n_exp, cap, tokens_per_batch):
    e = pl.program_id(0)
    f = pl.program_id(1)
    rows = xland.shape[0]

    def issue_gather(ids_ref):
        for b in range(rows // cap):
            def body(c, carry, b=b):
                i = b * cap + c
                src_row = ids_ref[0, 0, i] + b * tokens_per_batch
                pltpu.make_async_copy(h_hbm.at[pl.ds(src_row, 1)], xland.at[pl.ds(i, 1)], sem.at[0]).start()
                return carry
            lax.fori_loop(0, cap, body, 0, unroll=8)

    @pl.when((e == 0) & (f == 0))
    def _():
        issue_gather(idx_ref)

    @pl.when(f == 0)
    def _():
        pltpu.make_async_copy(h_hbm.at[pl.ds(0, rows)], xland, sem.at[0]).wait()
        xbf[...] = xland[...].astype(BF16)

        @pl.when(e + 1 < n_exp)
        def _():
            issue_gather(idxn_ref)

    @pl.when(f < nf)
    def _():
        x = xbf[...]
        a = jnp.dot(x, wg_ref[0, 0].astype(BF16), preferred_element_type=F32)
        u = jnp.dot(x, wu_ref[0, 0].astype(BF16), preferred_element_type=F32)
        h1[jnp.minimum(f, nf - 1)] = (jax.nn.silu(a) * u).astype(BF16)

    @pl.when(f >= nf)
    def _():
        hid = jnp.concatenate([h1[k] for k in range(nf)], axis=1)
        y = jnp.dot(hid, wd_ref[0, 0].astype(BF16), preferred_element_type=F32)
        o_ref[...] = (y.reshape(o_ref.shape) * g_ref[...]).astype(o_ref.dtype)


def moe_ffn(idx3, hp, g, w_gate, w_up, w_down, layer, T, *, tf=256, tc=512):
    B, E, C = idx3.shape
    D = hp.shape[1]
    F = w_gate.shape[-1]
    nf = F // tf
    tc = min(tc, D)
    ids = idx3.transpose(1, 0, 2).reshape(E, 1, B * C)
    smem_blk = lambda imap: pl.BlockSpec((1, 1, B * C), imap, memory_space=pltpu.SMEM)
    up_blk = pl.BlockSpec((1, 1, D, tf), lambda e, s: (layer, e, 0, jnp.minimum(s, nf - 1)))
    return pl.pallas_call(
        functools.partial(_moe_ffn_kernel, nf=nf, n_exp=E, cap=C, tokens_per_batch=T),
        grid=(E, nf + D // tc),
        in_specs=[smem_blk(lambda e, s: (e, 0, 0)),
                  smem_blk(lambda e, s: (jnp.minimum(e + 1, E - 1), 0, 0)),
                  pl.BlockSpec(memory_space=pl.ANY),
                  up_blk,
                  up_blk,
                  pl.BlockSpec((1, 1, F, tc), lambda e, s: (layer, e, 0, jnp.maximum(s - nf, 0))),
                  pl.BlockSpec((B, 1, C, 1), lambda e, s: (0, e, 0, 0))],
        out_specs=pl.BlockSpec((B, 1, C, tc), lambda e, s: (0, e, 0, jnp.maximum(s - nf, 0))),
        out_shape=jax.ShapeDtypeStruct((B, E, C, D), BF16),
        scratch_shapes=[pltpu.VMEM((B * C, D), F32),
                        pltpu.VMEM((B * C, D), BF16),
                        pltpu.VMEM((nf, B * C, tf), BF16),
                        pltpu.SemaphoreType.DMA((1,))],
        compiler_params=_params("arbitrary", "arbitrary"),
    )(ids, ids, hp, w_gate, w_up, w_down, g)


def _moe_combine_kernel(starts_ref, ntr_ref, y_hbm, post_ref, x_ref, gate_ref, ng_ref, *rest,
                        n_exp, cap, win, steps_per_batch, modulated):
    ybuf, yext, acc, sem, sem_ext = rest[-5:]
    out_refs = rest[2:-5] if modulated else rest[:-5]
    s = pl.program_id(0)
    slot = s % 2

    def window(step, e, k):
        nominal = starts_ref[step * n_exp + e] + k * win
        return nominal, pl.multiple_of(jnp.minimum(nominal, cap - win), 16)

    def window_copy(step, e, k, dst, dsem):
        _, actual = window(step, e, k)
        return pltpu.make_async_copy(y_hbm.at[step // steps_per_batch, e, pl.ds(actual, win)],
                                     dst.at[pl.ds(e * win, win)], dsem)

    def onehot(step, k):
        post = post_ref[0]
        per = max(LANES // win, 1)
        lane = lax.broadcasted_iota(jnp.int32, (1, per * win), 1)
        cols = []
        for e0 in range(0, n_exp, per):
            slots = tok = None
            for q in range(per):
                nominal, actual = window(step, e0 + q, k)
                s_q = actual + lane - q * win
                s_q = jnp.where(s_q >= nominal, s_q, -2)
                t_q = post[:, e0 + q:e0 + q + 1]
                mine = lane >= q * win
                slots = s_q if q == 0 else jnp.where(mine, s_q, slots)
                tok = t_q if q == 0 else jnp.where(mine, t_q, tok)
            cols.append(jnp.where(tok == slots, 1.0, 0.0).astype(BF16))
        return jnp.concatenate(cols, axis=1)

    @pl.when(s == 0)
    def _():
        for e in range(n_exp):
            window_copy(0, e, 0, ybuf.at[0], sem.at[0]).start()

    for e in range(n_exp):
        window_copy(s, e, 0, ybuf.at[slot], sem.at[slot]).wait()

    @pl.when(s + 1 < pl.num_programs(0))
    def _():
        for e in range(n_exp):
            window_copy(s + 1, e, 0, ybuf.at[1 - slot], sem.at[1 - slot]).start()

    acc[...] = jnp.dot(onehot(s, 0), ybuf[slot], preferred_element_type=F32)

    def extra(k, carry):
        for e in range(n_exp):
            window_copy(s, e, k, yext, sem_ext.at[0]).start()
        for e in range(n_exp):
            window_copy(s, e, k, yext, sem_ext.at[0]).wait()
        acc[...] += jnp.dot(onehot(s, k), yext[...], preferred_element_type=F32)
        return carry

    lax.fori_loop(1, ntr_ref[s], extra, 0)
    x = x_ref[...] + gate_ref[0] * acc[...]
    y = x * lax.rsqrt(jnp.mean(x * x, axis=-1, keepdims=True) + NORM_EPS) * ng_ref[...]
    if modulated:
        out_refs[0][...] = x
        y = y * (1 + rest[1][0]) + rest[0][0]
    out_refs[-1][...] = y.astype(out_refs[-1].dtype)


def moe_combine(starts, ntr, y, post, x2, gate, norm_g, shift=None, scale=None, *, tj=256, win=64):
    B, E, C, D = y.shape
    M = x2.shape[0]
    T = M // B
    spb = T // tj
    modulated = shift is not None
    row_blk = pl.BlockSpec((tj, D), lambda s, *_: (s, 0))
    per_batch = pl.BlockSpec((1, 1, D), lambda s, *_: (s // spb, 0, 0))
    in_specs = [pl.BlockSpec(memory_space=pl.ANY),
                pl.BlockSpec((1, tj, E), lambda s, *_: (s // spb, s % spb, 0)),
                row_blk, per_batch, pl.BlockSpec((1, D), lambda s, *_: (0, 0))]
    args = [starts, ntr, y, post, x2, gate, norm_g.reshape(1, D)]
    if modulated:
        in_specs += [per_batch, per_batch]
        args += [shift, scale]
        out_specs = [row_blk, row_blk]
        out_shape = [jax.ShapeDtypeStruct((M, D), F32), jax.ShapeDtypeStruct((M, D), BF16)]
    else:
        out_specs = row_blk
        out_shape = jax.ShapeDtypeStruct((M, D), F32)
    grid_spec = pltpu.PrefetchScalarGridSpec(
        num_scalar_prefetch=2,
        grid=(M // tj,),
        in_specs=in_specs,
        out_specs=out_specs,
        scratch_shapes=[pltpu.VMEM((2, E * win, D), BF16),
                        pltpu.VMEM((E * win, D), BF16),
                        pltpu.VMEM((tj, D), F32),
                        pltpu.SemaphoreType.DMA((2,)),
                        pltpu.SemaphoreType.DMA((1,))],
    )
    return pl.pallas_call(
        functools.partial(_moe_combine_kernel, n_exp=E, cap=C, win=win, steps_per_batch=spb, modulated=modulated),
        grid_spec=grid_spec,
        out_shape=out_shape,
        compiler_params=_params("arbitrary"),
    )(*args)


def _small_rmsnorm(x, g):
    return x * lax.rsqrt(jnp.mean(x * x, axis=-1, keepdims=True) + NORM_EPS) * g


def _rope_tables(rows):
    row = jnp.broadcast_to(jnp.arange(rows)[:, None], (rows, GRID_W)).reshape(-1).astype(F32)
    col = jnp.broadcast_to(jnp.arange(GRID_W)[None, :], (rows, GRID_W)).reshape(-1).astype(F32)
    n_freq = MLA_ROPE_DIM // 4
    inv = ROPE_THETA ** (-jnp.arange(n_freq, dtype=F32) / n_freq)
    ang = jnp.concatenate([row[:, None] * inv, col[:, None] * inv], axis=-1)
    return jnp.cos(ang), jnp.sin(ang)


def _ada(cvec3, ada_w, ada_b, layer):
    R = cvec3.shape[0]
    a = jnp.concatenate([jax.nn.silu(cvec3), jnp.zeros((8 - R, cvec3.shape[1]), F32)], axis=0)
    m = matmul(a, ada_w, b_index=layer, tm=8, tn=512)[:R] + ada_b[layer]
    return m.reshape(R, 6, D_MODEL)


def _q_proj_kernel(cq_ref, w_ref, cos_ref, sin_ref, o_ref):
    cq, cos, sin = cq_ref[...], cos_ref[...], sin_ref[...]
    for hh in range(o_ref.shape[1]):
        w = w_ref[:, hh * 2 * QK_PAD:(hh + 1) * 2 * QK_PAD]
        acc = jnp.dot(cq, w, preferred_element_type=F32)
        o_ref[0, hh] = (acc[:, :QK_PAD] * cos + acc[:, QK_PAD:] * sin).astype(o_ref.dtype)


def q_proj(cqn, w_q2, cosq, sinq, B, L, *, tm=1024, hg=4):
    M, R = cqn.shape
    tpb = L // tm
    return pl.pallas_call(
        _q_proj_kernel,
        grid=(M // tm, MLA_HEADS // hg),
        in_specs=[pl.BlockSpec((tm, R), lambda i, h: (i, 0)),
                  pl.BlockSpec((R, hg * 2 * QK_PAD), lambda i, h: (0, h)),
                  pl.BlockSpec((tm, QK_PAD), lambda i, h: (i % tpb, 0)),
                  pl.BlockSpec((tm, QK_PAD), lambda i, h: (i % tpb, 0))],
        out_specs=pl.BlockSpec((1, hg, tm, QK_PAD), lambda i, h: (i // tpb, h, i % tpb, 0)),
        out_shape=jax.ShapeDtypeStruct((B, MLA_HEADS, L, QK_PAD), BF16),
        compiler_params=_params("parallel", "parallel"),
    )(cqn, w_q2, cosq, sinq)


def _q_weights(w_uq):
    R = w_uq.shape[0]
    half = MLA_ROPE_DIM // 2
    w = w_uq.reshape(R, MLA_HEADS, MLA_NOPE_DIM + MLA_ROPE_DIM)
    nope, rope = w[..., :MLA_NOPE_DIM], w[..., MLA_NOPE_DIM:]
    rot = jnp.concatenate([-rope[..., half:], rope[..., :half]], axis=-1)
    pad = jnp.zeros((R, MLA_HEADS, QK_PAD - MLA_NOPE_DIM - MLA_ROPE_DIM), w.dtype)
    parts = [nope, rope, pad, jnp.zeros_like(nope), rot, pad]
    return jnp.concatenate(parts, axis=-1).reshape(R, MLA_HEADS * 2 * QK_PAD).astype(BF16)


def _krope_t_kernel(w_ref, h_ref, *rest, roped):
    o_ref = rest[-1]
    x = lax.dot_general(w_ref[...], h_ref[...], (((1,), (1,)), ((), ())), preferred_element_type=F32)
    if roped:
        half = x.shape[0] // 2
        x1, x2, c, s = x[:half], x[half:], rest[0][...], rest[1][...]
        x = jnp.concatenate([x1 * c - x2 * s, x2 * c + x1 * s], axis=0)
    o_ref[0] = x.astype(o_ref.dtype)


def krope_t(h2, w_kr_t, B, Lx, tt, cos_t=None, sin_t=None):
    D = h2.shape[1]
    R = w_kr_t.shape[0]
    per = Lx // tt
    roped = cos_t is not None
    in_specs = [pl.BlockSpec((R, D), lambda b, i: (0, 0)), pl.BlockSpec((tt, D), lambda b, i: (b * per + i, 0))]
    args = [w_kr_t, h2]
    if roped:
        in_specs += [pl.BlockSpec((R // 2, tt), lambda b, i: (0, i))] * 2
        args += [cos_t, sin_t]
    return pl.pallas_call(
        functools.partial(_krope_t_kernel, roped=roped),
        grid=(B, per),
        in_specs=in_specs,
        out_specs=pl.BlockSpec((1, R, tt), lambda b, i: (b, 0, i)),
        out_shape=jax.ShapeDtypeStruct((B, R, Lx), BF16),
        compiler_params=_params("parallel", "parallel"),
    )(*args)


def _kv_proj_kernel(c_ref, wn_ref, wv_ref, kr_ref, kt_ref, v_ref):
    c = c_ref[0]
    kn_t = lax.dot_general(wn_ref[0], c, (((1,), (1,)), ((), ())), preferred_element_type=F32)
    rope_end = MLA_NOPE_DIM + MLA_ROPE_DIM
    kt_ref[0, 0, :MLA_NOPE_DIM] = kn_t.astype(kt_ref.dtype)
    kt_ref[0, 0, MLA_NOPE_DIM:rope_end] = kr_ref[0]
    kt_ref[0, 0, rope_end:] = jnp.zeros((QK_PAD - rope_end, c.shape[0]), kt_ref.dtype)
    v_ref[0, 0] = jnp.dot(c, wv_ref[0], preferred_element_type=F32).astype(v_ref.dtype)


def kv_proj(ckvn, wn_t, wv, kr_t, *, tk=2176):
    B, K, R = ckvn.shape
    H, V = wv.shape[0], wv.shape[2]
    return pl.pallas_call(
        _kv_proj_kernel,
        grid=(B, H, K // tk),
        in_specs=[pl.BlockSpec((1, tk, R), lambda b, h, i: (b, i, 0)),
                  pl.BlockSpec((1, MLA_NOPE_DIM, R), lambda b, h, i: (h, 0, 0)),
                  pl.BlockSpec((1, R, V), lambda b, h, i: (h, 0, 0)),
                  pl.BlockSpec((1, MLA_ROPE_DIM, tk), lambda b, h, i: (b, 0, i))],
        out_specs=[pl.BlockSpec((1, 1, QK_PAD, tk), lambda b, h, i: (b, h, 0, i)),
                   pl.BlockSpec((1, 1, tk, V), lambda b, h, i: (b, h, i, 0))],
        out_shape=[jax.ShapeDtypeStruct((B, H, QK_PAD, K), BF16), jax.ShapeDtypeStruct((B, H, K, V), BF16)],
        compiler_params=_params("parallel", "parallel", "parallel"),
    )(ckvn, wn_t, wv, kr_t)


def _mla_layer(h, h_c, B, L, Lc, w_in, g_q, g_kv, w_uq, w_ukv):
    H = MLA_HEADS
    lat = MLA_Q_RANK + MLA_KV_RANK
    w_in_b = w_in.astype(BF16)
    proj = matmul(h, w_in_b[:, :lat], tn=lat)
    ckv_c = matmul(h_c, w_in_b[:, MLA_Q_RANK:lat], tn=MLA_KV_RANK)
    ckv = jnp.concatenate([proj[:, MLA_Q_RANK:].reshape(B, L, -1), ckv_c.reshape(B, Lc, -1)], axis=1)
    cos, sin = _rope_tables(L // GRID_W)
    ones, zeros = jnp.ones((L, MLA_NOPE_DIM), F32), jnp.zeros((L, MLA_NOPE_DIM), F32)
    pad = jnp.zeros((L, QK_PAD - MLA_NOPE_DIM - MLA_ROPE_DIM), F32)
    cosq = jnp.concatenate([ones, cos, cos, pad], axis=1)
    sinq = jnp.concatenate([zeros, sin, sin, pad], axis=1)
    qp = q_proj(_small_rmsnorm(proj[:, :MLA_Q_RANK], g_q).astype(BF16), _q_weights(w_uq), cosq, sinq, B, L)
    w_kr_t = w_in_b[:, lat:].T
    kr_t = jnp.concatenate([krope_t(h, w_kr_t, B, L, 1024, cos.T, sin.T), krope_t(h_c, w_kr_t, B, Lc, Lc)], axis=2)
    w_kv = w_ukv.astype(BF16).reshape(MLA_KV_RANK, H, MLA_NOPE_DIM + MLA_V_DIM)
    wn_t = w_kv[..., :MLA_NOPE_DIM].transpose(1, 2, 0)
    wv = w_kv[..., MLA_NOPE_DIM:].transpose(1, 0, 2)
    kt, v = kv_proj(_small_rmsnorm(ckv, g_kv).astype(BF16), wn_t, wv, kr_t)
    scale = (MLA_NOPE_DIM + MLA_ROPE_DIM) ** -0.5
    return attention(qp, kt, v, scale=scale)


def _fft_tables(L):
    n = 2 * L
    n2_in = L // FFT_N1
    k2 = np.arange(FFT_K2)
    th = 2 * np.pi * ((k2[:, None] * np.arange(n2_in)[None, :]) % FFT_N2) / FFT_N2
    f1 = np.zeros((2 * FFT_K2P, n2_in))
    f1[:FFT_K2] = np.cos(th)
    f1[FFT_K2P:FFT_K2P + FFT_K2] = -np.sin(th)
    wgt = np.where((k2 == 0) | (k2 == FFT_N2 // 2), 1.0, 2.0)
    g = np.zeros((n2_in, 2 * FFT_K2P))
    g[:, :FFT_K2] = (wgt[:, None] * np.cos(th)).T / n
    g[:, FFT_K2P:FFT_K2P + FFT_K2] = -(wgt[:, None] * np.sin(th)).T / n
    k1 = np.arange(FFT_N1)[:, None]
    n1 = np.arange(FFT_N1)[None, :]
    m2 = np.zeros((-(-FFT_K2 // FFT_KSTEP) * FFT_KSTEP, 2 * FFT_N1, 2 * FFT_N1))
    for kk in range(FFT_K2):
        phi = 2 * np.pi * ((n1 * k1 * FFT_N2 + n1 * kk) % n) / n
        tr, ti = np.cos(phi), -np.sin(phi)
        m2[kk] = np.block([[tr, -ti], [ti, tr]])
    as_bf16 = lambda a: jnp.asarray(a, dtype=F32).astype(BF16)
    return (as_bf16(np.kron(f1, np.eye(FFT_SUB))), as_bf16(np.kron(g, np.eye(FFT_GROUP))), as_bf16(m2),
            as_bf16(np.transpose(m2, (0, 2, 1))))


def _fft_s1_kernel(f_ref, z_ref, o_ref):
    _, n2_in, g, ct = z_ref.shape
    z = z_ref[0].astype(F32)
    parts = []
    for s0 in range(0, g, FFT_SUB):
        zz = z[:, s0:s0 + FFT_SUB, :].reshape(n2_in * FFT_SUB, ct).astype(BF16)
        a = jnp.dot(f_ref[...], zz, preferred_element_type=F32)
        parts.append(a.reshape(2, FFT_K2P, FFT_SUB, ct))
    o_ref[0] = jnp.concatenate(parts, axis=2).astype(o_ref.dtype)


def fft_stage1(z4, f1k, *, lead_off, nb, ct=2048):
    _, n2_in, _, C = z4.shape
    g = FFT_GROUP
    return pl.pallas_call(
        _fft_s1_kernel,
        grid=(nb, FFT_N1 // g, C // ct),
        in_specs=[pl.BlockSpec(f1k.shape, lambda b, i, j: (0, 0)),
                  pl.BlockSpec((1, n2_in, g, ct), lambda b, i, j: (lead_off + b, 0, i, j))],
        out_specs=pl.BlockSpec((1, 2, FFT_K2P, g, ct), lambda b, i, j: (b, 0, 0, i, j)),
        out_shape=jax.ShapeDtypeStruct((nb, 2, FFT_K2P, FFT_N1, C), BF16),
        compiler_params=_params("parallel", "parallel", "parallel"),
    )(f1k, z4)


def _fft_s2_kernel(a_ref, m_ref, mt_ref, kf_ref, o_ref):
    step = pl.program_id(0)
    ct = a_ref.shape[-1]
    for q in range(FFT_KSTEP):
        k2 = step * FFT_KSTEP + q

        @pl.when(k2 < FFT_K2)
        def _(q=q):
            x = jnp.dot(m_ref[q], a_ref[0, :, q].reshape(2 * FFT_N1, ct), preferred_element_type=F32)
            xr, xi = x[:FFT_N1], x[FFT_N1:]
            kr, ki = kf_ref[0, 0, q].astype(F32), kf_ref[0, 1, q].astype(F32)
            p = jnp.concatenate([xr * kr - xi * ki, xr * ki + xi * kr], axis=0).astype(BF16)
            e = jnp.dot(mt_ref[q], p, preferred_element_type=F32)
            o_ref[0, :, q] = e.astype(o_ref.dtype).reshape(2, FFT_N1, ct)

        @pl.when(k2 >= FFT_K2)
        def _(q=q):
            o_ref[0, :, q] = jnp.zeros((2, FFT_N1, ct), o_ref.dtype)


def fft_stage2(a5, m2, m2t, kf, order):
    B, _, _, _, D = a5.shape
    last = (FFT_K2 - 1) // FFT_KSTEP
    blk = (1, 2, FFT_KSTEP, FFT_N1, D)
    tbl = pl.BlockSpec((FFT_KSTEP, 2 * FFT_N1, 2 * FFT_N1), lambda k, b: (jnp.minimum(k, last), 0, 0))
    return pl.pallas_call(
        _fft_s2_kernel,
        grid=(FFT_K2P // FFT_KSTEP, B),
        in_specs=[pl.BlockSpec(blk, lambda k, b: (b, 0, jnp.minimum(k, last), 0, 0)),
                  tbl,
                  tbl,
                  pl.BlockSpec(blk, lambda k, b: (order, 0, jnp.minimum(k, last), 0, 0))],
        out_specs=pl.BlockSpec(blk, lambda k, b: (b, 0, k, 0, 0)),
        out_shape=jax.ShapeDtypeStruct(a5.shape, BF16),
        compiler_params=_params("parallel", "arbitrary"),
    )(a5, m2, m2t, kf)


def _fft_f2_kernel(f_ref, g_ref, m_ref, inv_ref, o_ref):
    step = pl.program_id(0)
    ct = f_ref.shape[-1]
    inv = inv_ref[0]
    for q in range(FFT_KSTEP):
        k2 = step * FFT_KSTEP + q

        @pl.when(k2 < FFT_K2)
        def _(q=q):
            xf = jnp.dot(m_ref[q], f_ref[:, q].reshape(2 * FFT_N1, ct), preferred_element_type=F32)
            xg = jnp.dot(m_ref[q], g_ref[:, q].reshape(2 * FFT_N1, ct), preferred_element_type=F32)
            o_ref[0, 0, q] = ((xf[:FFT_N1] + xg[:FFT_N1]) * inv).astype(o_ref.dtype)
            o_ref[0, 1, q] = ((xf[FFT_N1:] - xg[FFT_N1:]) * inv).astype(o_ref.dtype)

        @pl.when(k2 >= FFT_K2)
        def _(q=q):
            o_ref[0, :, q] = jnp.zeros((2, FFT_N1, ct), o_ref.dtype)


def fft_filter_spectrum(a4, m2, inv, D):
    last = (FFT_K2 - 1) // FFT_KSTEP
    blk = (2, FFT_KSTEP, FFT_N1, D)
    return pl.pallas_call(
        _fft_f2_kernel,
        grid=(FFT_K2P // FFT_KSTEP, HYENA_ORDER),
        in_specs=[pl.BlockSpec(blk, lambda k, o: (0, jnp.minimum(k, last), 0, 2 * o)),
                  pl.BlockSpec(blk, lambda k, o: (0, jnp.minimum(k, last), 0, 2 * o + 1)),
                  pl.BlockSpec((FFT_KSTEP, 2 * FFT_N1, 2 * FFT_N1), lambda k, o: (jnp.minimum(k, last), 0, 0)),
                  pl.BlockSpec((1, 1, D), lambda k, o: (o, 0, 0))],
        out_specs=pl.BlockSpec((1, 2, FFT_KSTEP, FFT_N1, D), lambda k, o: (o, 0, k, 0, 0)),
        out_shape=jax.ShapeDtypeStruct((HYENA_ORDER, 2, FFT_K2P, FFT_N1, D), BF16),
        compiler_params=_params("arbitrary", "parallel"),
    )(a4, a4, m2, inv)


def _fft_s3_kernel(g_ref, e_ref, z_ref, x_ref, skip_ref, o_ref):
    _, n2_in, g, ct = z_ref.shape
    e = e_ref[0].reshape(2 * FFT_K2P * g, ct)
    y = jnp.dot(g_ref[...], e, preferred_element_type=F32).reshape(n2_in, g, ct)
    o_ref[0] = (x_ref[0] * (y + skip_ref[...] * z_ref[0])).astype(o_ref.dtype)


def fft_stage3(e5, gk, z4, z_off, x4, x_off, skip, *, out_dtype):
    B, _, _, _, D = e5.shape
    n2_in = z4.shape[1]
    g = FFT_GROUP
    return pl.pallas_call(
        _fft_s3_kernel,
        grid=(B, FFT_N1 // g),
        in_specs=[pl.BlockSpec(gk.shape, lambda b, i: (0, 0)),
                  pl.BlockSpec((1, 2, FFT_K2P, g, D), lambda b, i: (b, 0, 0, i, 0)),
                  pl.BlockSpec((1, n2_in, g, D), lambda b, i: (z_off + b, 0, i, 0)),
                  pl.BlockSpec((1, n2_in, g, D), lambda b, i: (x_off + b, 0, i, 0)),
                  pl.BlockSpec((1, D), lambda b, i: (0, 0))],
        out_specs=pl.BlockSpec((1, n2_in, g, D), lambda b, i: (b, 0, i, 0)),
        out_shape=jax.ShapeDtypeStruct((B, n2_in, FFT_N1, D), out_dtype),
        compiler_params=_params("parallel", "parallel"),
    )(gk, e5, z4, x4, skip.reshape(1, D))


def _hyin_kernel(ap_ref, a_ref, an_ref, b_ref, cw_ref, cb_ref, o_ref, *, tiles_per_seq):
    i = pl.program_id(0)
    t = i % tiles_per_seq
    halo = ap_ref.shape[0]
    tm, tn = o_ref.shape[1:]
    cw = MXU_DIM
    row = lax.broadcasted_iota(jnp.int32, (tm, cw), 0)
    at_start = jnp.logical_and(row == 0, t == 0)
    at_end = jnp.logical_and(row == tm - 1, t == tiles_per_seq - 1)
    a_ext = jnp.concatenate([ap_ref[...], a_ref[...], an_ref[...]], axis=0)
    for c0 in range(0, tn, cw):
        acc = jnp.dot(a_ext, b_ref[:, c0:c0 + cw], preferred_element_type=F32)
        up = jnp.where(at_start, 0.0, acc[halo - 1:halo - 1 + tm])
        dn = jnp.where(at_end, 0.0, acc[halo + 1:halo + 1 + tm])
        w = cw_ref[:, c0:c0 + cw]
        o_ref[0, :, c0:c0 + cw] = up * w[0:1] + acc[halo:halo + tm] * w[1:2] + dn * w[2:3] + cb_ref[:, c0:c0 + cw]


def hyena_in(h, w_in, conv_w, conv_b, L, *, tm=1024, tn=1024, halo=16):
    M, K = h.shape
    N = w_in.shape[1]
    D = N // 3
    per = D // tn
    return pl.pallas_call(
        functools.partial(_hyin_kernel, tiles_per_seq=L // tm),
        grid=(M // tm, N // tn),
        in_specs=[pl.BlockSpec((halo, K), lambda i, j: (jnp.maximum(i * (tm // halo) - 1, 0), 0)),
                  pl.BlockSpec((tm, K), lambda i, j: (i, 0)),
                  pl.BlockSpec((halo, K), lambda i, j: (jnp.minimum((i + 1) * (tm // halo), M // halo - 1), 0)),
                  pl.BlockSpec((K, tn), lambda i, j: (0, j)),
                  pl.BlockSpec((3, tn), lambda i, j: (0, j)),
                  pl.BlockSpec((1, tn), lambda i, j: (0, j))],
        out_specs=pl.BlockSpec((1, tm, tn), lambda i, j: (j // per, i, j % per)),
        out_shape=jax.ShapeDtypeStruct((3, M, D), F32),
        compiler_params=_params("parallel", "parallel"),
    )(h, h, h, w_in, conv_w, conv_b.reshape(1, N))


def _taps_kernel(a_ref, w_ref, b_ref, dl_ref, keep_ref, o_ref, ss_ref, *, seq_len):
    i = pl.program_id(1)
    tm = a_ref.shape[0]
    acc = jnp.dot(a_ref[...].astype(BF16), w_ref[...].astype(BF16), preferred_element_type=F32) + b_ref[...]
    row = i * tm + lax.broadcasted_iota(jnp.int32, acc.shape, 0)
    t = row.astype(F32) / seq_len
    hf = acc * jnp.exp(-t * dl_ref[...])
    hf = jnp.where(row == 0, hf * keep_ref[...], hf)
    o_ref[...] = hf.astype(o_ref.dtype)
    part = jnp.sum(hf * hf, axis=0, keepdims=True)

    @pl.when(i == 0)
    def _():
        ss_ref[...] = part

    @pl.when(i > 0)
    def _():
        ss_ref[...] += part


def _hyena_filter_taps(L, w1, b1, w2, b2, w3, b3, freq, *, tm=512, tn=1024):
    D = D_MODEL
    N = 2 * HYENA_ORDER * D
    t = jnp.arange(L, dtype=F32) / L
    w = 2 * math.pi * jnp.arange(L, dtype=F32) / L
    bands = jnp.linspace(1e-4, HYENA_BANDS - 1, HYENA_BANDS, dtype=F32)
    ang = w[:, None] * bands[None, :]
    z = jnp.concatenate([t[:, None], jnp.cos(ang), -jnp.sin(ang)], axis=-1)
    a = jnp.sin(freq * (z @ w1 + b1))
    a = jnp.sin(freq * (a @ w2 + b2))
    max_decay = math.log(HYENA_TARGET) / HYENA_FAST_PCT
    min_decay = math.log(HYENA_TARGET) / HYENA_SLOW_PCT
    decay = jnp.tile(jnp.abs(jnp.linspace(min_decay, max_decay, D, dtype=F32)), 2 * HYENA_ORDER).reshape(1, N)
    keep0 = jnp.where((jnp.arange(N) // D) % 2 == 1, 0.0, 1.0).astype(F32).reshape(1, N)
    K = a.shape[1]
    taps, ss = pl.pallas_call(
        functools.partial(_taps_kernel, seq_len=L),
        grid=(N // tn, L // tm),
        in_specs=[pl.BlockSpec((tm, K), lambda j, i: (i, 0)),
                  pl.BlockSpec((K, tn), lambda j, i: (0, j)),
                  pl.BlockSpec((1, tn), lambda j, i: (0, j)),
                  pl.BlockSpec((1, tn), lambda j, i: (0, j)),
                  pl.BlockSpec((1, tn), lambda j, i: (0, j))],
        out_specs=[pl.BlockSpec((tm, tn), lambda j, i: (i, j)), pl.BlockSpec((1, tn), lambda j, i: (0, j))],
        out_shape=[jax.ShapeDtypeStruct((L, N), BF16), jax.ShapeDtypeStruct((1, N), F32)],
        compiler_params=_params("parallel", "arbitrary"),
    )(a, w3, b3.reshape(1, N), decay, keep0)
    ss = ss.reshape(HYENA_ORDER, 2, D).sum(axis=1, keepdims=True)
    return taps, lax.rsqrt(ss + NORM_EPS)


def _hyena_layer(h, B, L, w_in, conv_w, conv_b, f_w1, f_b1, f_w2, f_b2, f_w3, f_b3, f_freq, skip):
    D = D_MODEL
    n2_in = L // FFT_N1
    f1k, gk, m2, m2t = _fft_tables(L)
    taps, inv = _hyena_filter_taps(L, f_w1, f_b1, f_w2, f_b2, f_w3, f_b3, f_freq)
    af = fft_stage1(taps.reshape(1, n2_in, FFT_N1, taps.shape[1]), f1k, lead_off=0, nb=1)
    kf = fft_filter_spectrum(af[0], m2, inv, D)
    u4 = hyena_in(h, w_in.astype(BF16), conv_w, conv_b, L).reshape(3 * B, n2_in, FFT_N1, D)
    z4 = u4
    for order in range(HYENA_ORDER):
        a = fft_stage1(z4, f1k, lead_off=0, nb=B)
        e = fft_stage2(a, m2, m2t, kf, order)
        last = order == HYENA_ORDER - 1
        z4 = fft_stage3(e, gk, z4, 0, u4, (order + 1) * B, skip[order], out_dtype=BF16 if last else F32)
    return z4.reshape(B * L, D)


def _moe_layer(x2, norm_gain, mod, B, T, w_router, w_gate, w_up, w_down, layer, next_g, next_shift=None,
               next_scale=None, *, tj=256, win=64):
    E = N_EXPERTS
    cap = EC_CAPACITY_FACTOR * T // E
    hp, aff = norm_route(x2, norm_gain, mod[:, 3:4], mod[:, 4:5], w_router.T.astype(BF16), rows_per_batch=T)
    pos, idx, g = route_select(aff, cap)
    idx3 = idx.reshape(B, E, cap)
    y = moe_ffn(idx3, hp, g, w_gate, w_up, w_down, layer, T)
    nj = T // tj
    edges = jnp.arange(nj + 1, dtype=jnp.int32) * tj
    first = jnp.sum((idx3[..., None] < edges).astype(jnp.int32), axis=2)
    nominal = (first[..., :nj] // 16) * 16
    ntr = jnp.maximum(jnp.max((first[..., 1:] - nominal + win - 1) // win, axis=1), 1)
    starts = nominal.transpose(0, 2, 1).reshape(-1)
    return moe_combine(starts, ntr.reshape(-1), y, pos.transpose(0, 2, 1), x2, mod[:, 5:6], next_g, next_shift,
                       next_scale, tj=tj, win=win)


def kernel(x, c, ctx, c_ctx, ada_w, ada_b, norm_g, final_g, mla_w_in, mla_g_q, mla_g_kv, mla_w_uq, mla_w_ukv, mla_w_o, hy_w_in, hy_conv_w, hy_conv_b, hy_f_w1, hy_f_b1, hy_f_w2, hy_f_b2, hy_f_w3, hy_f_b3, hy_f_freq, hy_skip, hy_w_out, moe_w_router, moe_w_gate, moe_w_up, moe_w_down):
    B, L, D = x.shape
    Lc = ctx.shape[1]
    x2 = x.reshape(B * L, D)
    ctx2 = ctx.reshape(B * Lc, D)

    mod3 = _ada(jnp.concatenate([c, c_ctx[None]], axis=0), ada_w, ada_b, 0)
    mod, mod_c = mod3[:B], jnp.broadcast_to(mod3[B:], (B, 6, D))
    h = norm_mod(x2, norm_g[0, 0], mod[:, 0:1], mod[:, 1:2], rows_per_batch=L)
    h_c = norm_mod(ctx2, norm_g[0, 0], mod_c[:, 0:1], mod_c[:, 1:2], rows_per_batch=Lc)
    o = _mla_layer(h, h_c, B, L, Lc, mla_w_in[0], mla_g_q[0], mla_g_kv[0], mla_w_uq[0], mla_w_ukv[0])
    x2 = matmul(o.reshape(B * L, -1), mla_w_o[0].astype(BF16), tn=1024,
                res=x2, gate=mod[:, 2:3], rows_per_batch=L)
    mod1 = _ada(c, ada_w, ada_b, 1)
    x2, h = _moe_layer(x2, norm_g[0, 1], mod, B, L, moe_w_router[0], moe_w_gate, moe_w_up, moe_w_down, 0,
                       norm_g[1, 0], mod1[:, 0:1], mod1[:, 1:2])

    mod = mod1
    z = _hyena_layer(h, B, L, hy_w_in[0], hy_conv_w[0], hy_conv_b[0], hy_f_w1[0], hy_f_b1[0], hy_f_w2[0],
                     hy_f_b2[0], hy_f_w3[0], hy_f_b3[0], hy_f_freq[0], hy_skip[0])
    x2 = matmul(z, hy_w_out[0].astype(BF16), tn=1024, res=x2, gate=mod[:, 2:3], rows_per_batch=L)
    out = _moe_layer(x2, norm_g[1, 1], mod, B, L, moe_w_router[1], moe_w_gate, moe_w_up, moe_w_down, 1, final_g)
    return out.reshape(B, L, D)
```

```python
import functools
import math

import jax
import jax.numpy as jnp
import numpy as np
from jax import lax
from jax.experimental import pallas as pl
from jax.experimental.pallas import tpu as pltpu

F32 = jnp.float32
BF16 = jnp.bfloat16

D_MODEL = 2048
GRID_W = 64
NORM_EPS = 1e-6
MLA_HEADS = 16
MLA_Q_RANK = 512
MLA_KV_RANK = 256
MLA_NOPE_DIM = 128
MLA_ROPE_DIM = 64
MLA_V_DIM = 128
ROPE_THETA = 10000.0
HYENA_ORDER = 2
HYENA_BANDS = 16
HYENA_TARGET = 1e-2
HYENA_FAST_PCT = 0.3
HYENA_SLOW_PCT = 1.5
N_EXPERTS = 16
EC_CAPACITY_FACTOR = 2
FFT_N1 = 128
FFT_N2 = 64
FFT_K2 = FFT_N2 // 2 + 1
FFT_K2P = 40
FFT_GROUP = 16
FFT_SUB = 8
FFT_KSTEP = 4

V7X_VMEM_LIMIT_BYTES = 56 * 1024 * 1024
LANES = 128
MXU_DIM = 256
QK_PAD = MXU_DIM


def _params(*sem):
    return pltpu.CompilerParams(dimension_semantics=sem, vmem_limit_bytes=V7X_VMEM_LIMIT_BYTES)


def _mm_kernel(a_ref, b_ref, *rest, b_is_stacked, has_res):
    o_ref = rest[-1]
    b = b_ref[0] if b_is_stacked else b_ref[...]
    acc = jnp.dot(a_ref[...].astype(BF16), b.astype(BF16), preferred_element_type=F32)
    if has_res:
        res_ref, gate_ref = rest[0], rest[1]
        acc = res_ref[...] + gate_ref[0] * acc
    o_ref[...] = acc.astype(o_ref.dtype)


def matmul(a, b, *, b_index=None, out_dtype=F32, tm=512, tn=512, res=None, gate=None, rows_per_batch=None):
    M, K = a.shape
    N = b.shape[-1]
    tm, tn = min(tm, M), min(tn, N)
    assert M % tm == 0 and N % tn == 0, (M, N, tm, tn)
    stacked = b.ndim == 3
    if stacked:
        b_spec = pl.BlockSpec((1, K, tn), lambda i, j: (b_index, 0, j))
    else:
        b_spec = pl.BlockSpec((K, tn), lambda i, j: (0, j))
    in_specs = [pl.BlockSpec((tm, K), lambda i, j: (i, 0)), b_spec]
    args = [a, b]
    if res is not None:
        assert rows_per_batch % tm == 0
        tpb = rows_per_batch // tm
        in_specs += [pl.BlockSpec((tm, tn), lambda i, j: (i, j)),
                     pl.BlockSpec((1, 1, tn), lambda i, j: (i // tpb, 0, j))]
        args += [res, gate]
    return pl.pallas_call(
        functools.partial(_mm_kernel, b_is_stacked=stacked, has_res=res is not None),
        grid=(M // tm, N // tn),
        in_specs=in_specs,
        out_specs=pl.BlockSpec((tm, tn), lambda i, j: (i, j)),
        out_shape=jax.ShapeDtypeStruct((M, N), out_dtype),
        compiler_params=_params("parallel", "parallel"),
    )(*args)


def _norm_kernel(x_ref, g_ref, *rest, modulated):
    o_ref = rest[-1]
    x = x_ref[...]
    y = x * lax.rsqrt(jnp.mean(x * x, axis=-1, keepdims=True) + NORM_EPS) * g_ref[...]
    if modulated:
        shift_ref, scale_ref = rest[0], rest[1]
        y = y * (1 + scale_ref[0]) + shift_ref[0]
    o_ref[...] = y.astype(o_ref.dtype)


def norm_mod(x2, g, shift=None, scale=None, *, rows_per_batch=None, out_dtype=BF16, tm=256):
    M, D = x2.shape
    tm = min(tm, M)
    in_specs = [pl.BlockSpec((tm, D), lambda i: (i, 0)), pl.BlockSpec((1, D), lambda i: (0, 0))]
    args = [x2, g.reshape(1, D)]
    if shift is not None:
        tpb = rows_per_batch // tm
        in_specs += [pl.BlockSpec((1, 1, D), lambda i: (i // tpb, 0, 0))] * 2
        args += [shift, scale]
    return pl.pallas_call(
        functools.partial(_norm_kernel, modulated=shift is not None),
        grid=(M // tm,),
        in_specs=in_specs,
        out_specs=pl.BlockSpec((tm, D), lambda i: (i, 0)),
        out_shape=jax.ShapeDtypeStruct((M, D), out_dtype),
        compiler_params=_params("parallel"),
    )(*args)


def _attn_kernel(q_ref, kt_ref, v_ref, o_ref, *, chunks, c):
    q = q_ref[0, 0]
    m = l = acc = None
    for lo, hi in chunks:
        s = jnp.dot(q, kt_ref[0, 0, :, lo:hi], preferred_element_type=F32)
        m_chunk = jnp.max(s, axis=-1, keepdims=True)
        m_new = m_chunk if m is None else jnp.maximum(m, m_chunk)
        p = jnp.exp2(s * c - m_new * c)
        pv = jnp.dot(p.astype(BF16), v_ref[0, 0, lo:hi, :], preferred_element_type=F32)
        psum = jnp.sum(p, axis=-1, keepdims=True)
        if m is None:
            l, acc = psum, pv
        else:
            alpha = jnp.exp2((m - m_new) * c)
            l = alpha * l + psum
            acc = alpha * acc + pv
        m = m_new
    o_ref[0] = (acc / l).astype(o_ref.dtype)


def attention(q, kt, v, *, scale, tq=2048, tk=1024):
    B, H, L, _ = q.shape
    K = kt.shape[-1]
    V = v.shape[-1]
    chunks = tuple((lo, min(lo + tk, K)) for lo in range(0, K, tk))
    return pl.pallas_call(
        functools.partial(_attn_kernel, chunks=chunks, c=scale * math.log2(math.e)),
        grid=(B, H, L // tq),
        in_specs=[pl.BlockSpec((1, 1, tq, QK_PAD), lambda b, h, i: (b, h, i, 0)),
                  pl.BlockSpec((1, 1, QK_PAD, K), lambda b, h, i: (b, h, 0, 0)),
                  pl.BlockSpec((1, 1, K, V), lambda b, h, i: (b, h, 0, 0))],
        out_specs=pl.BlockSpec((1, tq, V), lambda b, h, i: (b, i, h)),
        out_shape=jax.ShapeDtypeStruct((B, L, H * V), BF16),
        compiler_params=_params("parallel", "parallel", "arbitrary"),
    )(q, kt, v)


def _norm_route_kernel(x_ref, g_ref, shift_ref, scale_ref, wr_ref, h_ref, aff_ref):
    x = x_ref[...]
    y = x * lax.rsqrt(jnp.mean(x * x, axis=-1, keepdims=True) + NORM_EPS) * g_ref[...]
    y = y * (1 + scale_ref[0]) + shift_ref[0]
    h_ref[...] = y
    logits = lax.dot_general(wr_ref[...], y.astype(BF16), (((1,), (1,)), ((), ())),
                             preferred_element_type=F32)
    p = jnp.exp(logits - jnp.max(logits, axis=0, keepdims=True))
    aff_ref[0] = p / jnp.sum(p, axis=0, keepdims=True)


def norm_route(x2, g, shift, scale, w_router_t, *, rows_per_batch, tm=256):
    M, D = x2.shape
    E = w_router_t.shape[0]
    tpb = rows_per_batch // tm
    return pl.pallas_call(
        _norm_route_kernel,
        grid=(M // tm,),
        in_specs=[pl.BlockSpec((tm, D), lambda i: (i, 0)),
                  pl.BlockSpec((1, D), lambda i: (0, 0)),
                  pl.BlockSpec((1, 1, D), lambda i: (i // tpb, 0, 0)),
                  pl.BlockSpec((1, 1, D), lambda i: (i // tpb, 0, 0)),
                  pl.BlockSpec((E, D), lambda i: (0, 0))],
        out_specs=[pl.BlockSpec((tm, D), lambda i: (i, 0)),
                   pl.BlockSpec((1, E, tm), lambda i: (i // tpb, 0, i % tpb))],
        out_shape=[jax.ShapeDtypeStruct((M, D), F32),
                   jax.ShapeDtypeStruct((M // rows_per_batch, E, rows_per_batch), F32)],
        compiler_params=_params("parallel"),
    )(x2, g.reshape(1, D), shift, scale, w_router_t)


def _lane_prefix_sum(x):
    E, T = x.shape
    r = lax.broadcasted_iota(jnp.int32, (LANES, LANES), 0)
    c = lax.broadcasted_iota(jnp.int32, (LANES, LANES), 1)
    tri = jnp.where(r <= c, 1.0, 0.0).astype(BF16)
    outs, carry = [], jnp.zeros((E, 1), F32)
    for j in range(T // LANES):
        seg = jnp.dot(x[:, j * LANES:(j + 1) * LANES].astype(BF16), tri, preferred_element_type=F32) + carry
        outs.append(seg)
        carry = seg[:, LANES - 1:LANES]
    return jnp.concatenate(outs, axis=1)


def _route_select_kernel(aff_ref, pos_ref, idx_ref, gate_ref, cum_s, *, cap, tch):
    aff = aff_ref[0]
    E, T = aff.shape

    def bisect(state):
        lo, hi, _ = state
        mid = 0.5 * (lo + hi)
        cnt = jnp.sum(jnp.where(aff >= mid, 1.0, 0.0), axis=1, keepdims=True)
        ge = cnt >= cap
        open_rows = jnp.where((mid > lo) & (mid < hi), 1.0, 0.0)
        return jnp.where(ge, mid, lo), jnp.where(ge, hi, mid), jnp.max(open_rows)

    lo0 = jnp.min(aff, axis=1, keepdims=True)
    hi0 = 2.0 * jnp.max(aff, axis=1, keepdims=True) + 1e-30
    lo, hi, _ = lax.while_loop(lambda st: st[2] > 0.0, bisect, (lo0, hi0, jnp.float32(1.0)))
    gt = aff >= hi
    eq = (aff >= lo) & (aff < hi)
    need = cap - jnp.sum(jnp.where(gt, 1.0, 0.0), axis=1, keepdims=True)
    eq_rank = _lane_prefix_sum(jnp.where(eq, 1.0, 0.0))
    sel = gt | (eq & (eq_rank <= need))
    cum = _lane_prefix_sum(jnp.where(sel, 1.0, 0.0))
    cumsel = jnp.where(sel, cum, 0.0)
    pos_ref[0] = (cumsel - 1.0).astype(jnp.int32)
    cum_s[0] = cum
    cum_s[1] = cumsel
    slot = lax.broadcasted_iota(jnp.int32, (cap, 1), 0).astype(F32)

    def per_expert(e, carry):
        idx_acc = jnp.zeros((cap, 1), F32)
        g_acc = jnp.zeros((cap, 1), F32)
        for lo in range(0, T, tch):
            cum_e = cum_s[0, pl.ds(e, 1), lo:lo + tch]
            cs_e = cum_s[1, pl.ds(e, 1), lo:lo + tch]
            aff_e = aff_ref[0, pl.ds(e, 1), lo:lo + tch]
            idx_acc = idx_acc + jnp.sum(jnp.where(cum_e <= slot, 1.0, 0.0), axis=1, keepdims=True)
            g_acc = g_acc + jnp.sum(jnp.where(cs_e == slot + 1.0, aff_e, 0.0), axis=1, keepdims=True)
        idx_ref[0, pl.ds(e, 1)] = idx_acc.astype(jnp.int32).reshape(1, cap, 1)
        gate_ref[0, pl.ds(e, 1)] = g_acc.reshape(1, cap, 1)
        return carry

    lax.fori_loop(0, E, per_expert, 0)


def route_select(aff, cap, *, tch=1024):
    B, E, T = aff.shape
    return pl.pallas_call(
        functools.partial(_route_select_kernel, cap=cap, tch=tch),
        grid=(B,),
        in_specs=[pl.BlockSpec((1, E, T), lambda b: (b, 0, 0))],
        out_specs=[pl.BlockSpec((1, E, T), lambda b: (b, 0, 0)),
                   pl.BlockSpec((1, E, cap, 1), lambda b: (b, 0, 0, 0)),
                   pl.BlockSpec((1, E, cap, 1), lambda b: (b, 0, 0, 0))],
        out_shape=[jax.ShapeDtypeStruct((B, E, T), jnp.int32),
                   jax.ShapeDtypeStruct((B, E, cap, 1), jnp.int32),
                   jax.ShapeDtypeStruct((B, E, cap, 1), F32)],
        scratch_shapes=[pltpu.VMEM((2, E, T), F32)],
        compiler_params=_params("parallel"),
    )(aff)


def _moe_ffn_kernel(idx_ref, idxn_ref, h_hbm, wg_ref, wu_ref, wd_ref, g_ref, o_ref, xland, xbf, h1, sem,
                    *, nf, n_exp, cap, tokens_per_batch):
    e = pl.program_id(0)
    f = pl.program_id(1)
    rows = xland.shape[0]

    def issue_gather(ids_ref):
        for b in range(rows // cap):
            def body(c, carry, b=b):
                i = b * cap + c
                src_row = ids_ref[0, 0, i] + b * tokens_per_batch
                pltpu.make_async_copy(h_hbm.at[pl.ds(src_row, 1)], xland.at[pl.ds(i, 1)], sem.at[0]).start()
                return carry
            lax.fori_loop(0, cap, body, 0, unroll=8)

    @pl.when((e == 0) & (f == 0))
    def _():
        issue_gather(idx_ref)

    @pl.when(f == 0)
    def _():
        pltpu.make_async_copy(h_hbm.at[pl.ds(0, rows)], xland, sem.at[0]).wait()
        xbf[...] = xland[...].astype(BF16)

        @pl.when(e + 1 < n_exp)
        def _():
            issue_gather(idxn_ref)

    @pl.when(f < nf)
    def _():
        x = xbf[...]
        a = jnp.dot(x, wg_ref[0, 0].astype(BF16), preferred_element_type=F32)
        u = jnp.dot(x, wu_ref[0, 0].astype(BF16), preferred_element_type=F32)
        h1[jnp.minimum(f, nf - 1)] = (jax.nn.silu(a) * u).astype(BF16)

    @pl.when(f >= nf)
    def _():
        hid = jnp.concatenate([h1[k] for k in range(nf)], axis=1)
        y = jnp.dot(hid, wd_ref[0, 0].astype(BF16), preferred_element_type=F32)
        o_ref[...] = (y.reshape(o_ref.shape) * g_ref[...]).astype(o_ref.dtype)


def moe_ffn(idx3, hp, g, w_gate, w_up, w_down, layer, T, *, tf=256, tc=512):
    B, E, C = idx3.shape
    D = hp.shape[1]
    F = w_gate.shape[-1]
    nf = F // tf
    tc = min(tc, D)
    ids = idx3.transpose(1, 0, 2).reshape(E, 1, B * C)
    smem_blk = lambda imap: pl.BlockSpec((1, 1, B * C), imap, memory_space=pltpu.SMEM)
    up_blk = pl.BlockSpec((1, 1, D, tf), lambda e, s: (layer, e, 0, jnp.minimum(s, nf - 1)))
    return pl.pallas_call(
        functools.partial(_moe_ffn_kernel, nf=nf, n_exp=E, cap=C, tokens_per_batch=T),
        grid=(E, nf + D // tc),
        in_specs=[smem_blk(lambda e, s: (e, 0, 0)),
                  smem_blk(lambda e, s: (jnp.minimum(e + 1, E - 1), 0, 0)),
                  pl.BlockSpec(memory_space=pl.ANY),
                  up_blk,
                  up_blk,
                  pl.BlockSpec((1, 1, F, tc), lambda e, s: (layer, e, 0, jnp.maximum(s - nf, 0))),
                  pl.BlockSpec((B, 1, C, 1), lambda e, s: (0, e, 0, 0))],
        out_specs=pl.BlockSpec((B, 1, C, tc), lambda e, s: (0, e, 0, jnp.maximum(s - nf, 0))),
        out_shape=jax.ShapeDtypeStruct((B, E, C, D), BF16),
        scratch_shapes=[pltpu.VMEM((B * C, D), F32),
                        pltpu.VMEM((B * C, D), BF16),
                        pltpu.VMEM((nf, B * C, tf), BF16),
                        pltpu.SemaphoreType.DMA((1,))],
        compiler_params=_params("arbitrary", "arbitrary"),
    )(ids, ids, hp, w_gate, w_up, w_down, g)


def _moe_combine_kernel(starts_ref, ntr_ref, y_hbm, post_ref, x_ref, gate_ref, ng_ref, *rest,
                        n_exp, cap, win, steps_per_batch, modulated):
    ybuf, yext, acc, sem, sem_ext = rest[-5:]
    out_refs = rest[2:-5] if modulated else rest[:-5]
    s = pl.program_id(0)
    slot = s % 2

    def window(step, e, k):
        nominal = starts_ref[step * n_exp + e] + k * win
        return nominal, pl.multiple_of(jnp.minimum(nominal, cap - win), 16)

    def window_copy(step, e, k, dst, dsem):
        _, actual = window(step, e, k)
        return pltpu.make_async_copy(y_hbm.at[step // steps_per_batch, e, pl.ds(actual, win)],
                                     dst.at[pl.ds(e * win, win)], dsem)

    def onehot(step, k):
        post = post_ref[0]
        per = max(LANES // win, 1)
        lane = lax.broadcasted_iota(jnp.int32, (1, per * win), 1)
        cols = []
        for e0 in range(0, n_exp, per):
            slots = tok = None
            for q in range(per):
                nominal, actual = window(step, e0 + q, k)
                s_q = actual + lane - q * win
                s_q = jnp.where(s_q >= nominal, s_q, -2)
                t_q = post[:, e0 + q:e0 + q + 1]
                mine = lane >= q * win
                slots = s_q if q == 0 else jnp.where(mine, s_q, slots)
                tok = t_q if q == 0 else jnp.where(mine, t_q, tok)
            cols.append(jnp.where(tok == slots, 1.0, 0.0).astype(BF16))
        return jnp.concatenate(cols, axis=1)

    @pl.when(s == 0)
    def _():
        for e in range(n_exp):
            window_copy(0, e, 0, ybuf.at[0], sem.at[0]).start()

    for e in range(n_exp):
        window_copy(s, e, 0, ybuf.at[slot], sem.at[slot]).wait()

    @pl.when(s + 1 < pl.num_programs(0))
    def _():
        for e in range(n_exp):
            window_copy(s + 1, e, 0, ybuf.at[1 - slot], sem.at[1 - slot]).start()

    acc[...] = jnp.dot(onehot(s, 0), ybuf[slot], preferred_element_type=F32)

    def extra(k, carry):
        for e in range(n_exp):
            window_copy(s, e, k, yext, sem_ext.at[0]).start()
        for e in range(n_exp):
            window_copy(s, e, k, yext, sem_ext.at[0]).wait()
        acc[...] += jnp.dot(onehot(s, k), yext[...], preferred_element_type=F32)
        return carry

    lax.fori_loop(1, ntr_ref[s], extra, 0)
    x = x_ref[...] + gate_ref[0] * acc[...]
    y = x * lax.rsqrt(jnp.mean(x * x, axis=-1, keepdims=True) + NORM_EPS) * ng_ref[...]
    if modulated:
        out_refs[0][...] = x
        y = y * (1 + rest[1][0]) + rest[0][0]
    out_refs[-1][...] = y.astype(out_refs[-1].dtype)


def moe_combine(starts, ntr, y, post, x2, gate, norm_g, shift=None, scale=None, *, tj=256, win=64):
    B, E, C, D = y.shape
    M = x2.shape[0]
    T = M // B
    spb = T // tj
    modulated = shift is not None
    row_blk = pl.BlockSpec((tj, D), lambda s, *_: (s, 0))
    per_batch = pl.BlockSpec((1, 1, D), lambda s, *_: (s // spb, 0, 0))
    in_specs = [pl.BlockSpec(memory_space=pl.ANY),
                pl.BlockSpec((1, tj, E), lambda s, *_: (s // spb, s % spb, 0)),
                row_blk, per_batch, pl.BlockSpec((1, D), lambda s, *_: (0, 0))]
    args = [starts, ntr, y, post, x2, gate, norm_g.reshape(1, D)]
    if modulated:
        in_specs += [per_batch, per_batch]
        args += [shift, scale]
        out_specs = [row_blk, row_blk]
        out_shape = [jax.ShapeDtypeStruct((M, D), F32), jax.ShapeDtypeStruct((M, D), BF16)]
    else:
        out_specs = row_blk
        out_shape = jax.ShapeDtypeStruct((M, D), F32)
    grid_spec = pltpu.PrefetchScalarGridSpec(
        num_scalar_prefetch=2,
        grid=(M // tj,),
        in_specs=in_specs,
        out_specs=out_specs,
        scratch_shapes=[pltpu.VMEM((2, E * win, D), BF16),
                        pltpu.VMEM((E * win, D), BF16),
                        pltpu.VMEM((tj, D), F32),
                        pltpu.SemaphoreType.DMA((2,)),
                        pltpu.SemaphoreType.DMA((1,))],
    )
    return pl.pallas_call(
        functools.partial(_moe_combine_kernel, n_exp=E, cap=C, win=win, steps_per_batch=spb, modulated=modulated),
        grid_spec=grid_spec,
        out_shape=out_shape,
        compiler_params=_params("arbitrary"),
    )(*args)


def _small_rmsnorm(x, g):
    return x * lax.rsqrt(jnp.mean(x * x, axis=-1, keepdims=True) + NORM_EPS) * g


def _rope_tables(rows):
    row = jnp.broadcast_to(jnp.arange(rows)[:, None], (rows, GRID_W)).reshape(-1).astype(F32)
    col = jnp.broadcast_to(jnp.arange(GRID_W)[None, :], (rows, GRID_W)).reshape(-1).astype(F32)
    n_freq = MLA_ROPE_DIM // 4
    inv = ROPE_THETA ** (-jnp.arange(n_freq, dtype=F32) / n_freq)
    ang = jnp.concatenate([row[:, None] * inv, col[:, None] * inv], axis=-1)
    return jnp.cos(ang), jnp.sin(ang)


def _ada(cvec3, ada_w, ada_b, layer):
    R = cvec3.shape[0]
    a = jnp.concatenate([jax.nn.silu(cvec3), jnp.zeros((8 - R, cvec3.shape[1]), F32)], axis=0)
    m = matmul(a, ada_w, b_index=layer, tm=8, tn=512)[:R] + ada_b[layer]
    return m.reshape(R, 6, D_MODEL)


def _q_proj_kernel(cq_ref, w_ref, cos_ref, sin_ref, o_ref):
    cq, cos, sin = cq_ref[...], cos_ref[...], sin_ref[...]
    for hh in range(o_ref.shape[1]):
        w = w_ref[:, hh * 2 * QK_PAD:(hh + 1) * 2 * QK_PAD]
        acc = jnp.dot(cq, w, preferred_element_type=F32)
        o_ref[0, hh] = (acc[:, :QK_PAD] * cos + acc[:, QK_PAD:] * sin).astype(o_ref.dtype)


def q_proj(cqn, w_q2, cosq, sinq, B, L, *, tm=1024, hg=4):
    M, R = cqn.shape
    tpb = L // tm
    return pl.pallas_call(
        _q_proj_kernel,
        grid=(M // tm, MLA_HEADS // hg),
        in_specs=[pl.BlockSpec((tm, R), lambda i, h: (i, 0)),
                  pl.BlockSpec((R, hg * 2 * QK_PAD), lambda i, h: (0, h)),
                  pl.BlockSpec((tm, QK_PAD), lambda i, h: (i % tpb, 0)),
                  pl.BlockSpec((tm, QK_PAD), lambda i, h: (i % tpb, 0))],
        out_specs=pl.BlockSpec((1, hg, tm, QK_PAD), lambda i, h: (i // tpb, h, i % tpb, 0)),
        out_shape=jax.ShapeDtypeStruct((B, MLA_HEADS, L, QK_PAD), BF16),
        compiler_params=_params("parallel", "parallel"),
    )(cqn, w_q2, cosq, sinq)


def _q_weights(w_uq):
    R = w_uq.shape[0]
    half = MLA_ROPE_DIM // 2
    w = w_uq.reshape(R, MLA_HEADS, MLA_NOPE_DIM + MLA_ROPE_DIM)
    nope, rope = w[..., :MLA_NOPE_DIM], w[..., MLA_NOPE_DIM:]
    rot = jnp.concatenate([-rope[..., half:], rope[..., :half]], axis=-1)
    pad = jnp.zeros((R, MLA_HEADS, QK_PAD - MLA_NOPE_DIM - MLA_ROPE_DIM), w.dtype)
    parts = [nope, rope, pad, jnp.zeros_like(nope), rot, pad]
    return jnp.concatenate(parts, axis=-1).reshape(R, MLA_HEADS * 2 * QK_PAD).astype(BF16)


def _krope_t_kernel(w_ref, h_ref, *rest, roped):
    o_ref = rest[-1]
    x = lax.dot_general(w_ref[...], h_ref[...], (((1,), (1,)), ((), ())), preferred_element_type=F32)
    if roped:
        half = x.shape[0] // 2
        x1, x2, c, s = x[:half], x[half:], rest[0][...], rest[1][...]
        x = jnp.concatenate([x1 * c - x2 * s, x2 * c + x1 * s], axis=0)
    o_ref[0] = x.astype(o_ref.dtype)


def krope_t(h2, w_kr_t, B, Lx, tt, cos_t=None, sin_t=None):
    D = h2.shape[1]
    R = w_kr_t.shape[0]
    per = Lx // tt
    roped = cos_t is not None
    in_specs = [pl.BlockSpec((R, D), lambda b, i: (0, 0)), pl.BlockSpec((tt, D), lambda b, i: (b * per + i, 0))]
    args = [w_kr_t, h2]
    if roped:
        in_specs += [pl.BlockSpec((R // 2, tt), lambda b, i: (0, i))] * 2
        args += [cos_t, sin_t]
    return pl.pallas_call(
        functools.partial(_krope_t_kernel, roped=roped),
        grid=(B, per),
        in_specs=in_specs,
        out_specs=pl.BlockSpec((1, R, tt), lambda b, i: (b, 0, i)),
        out_shape=jax.ShapeDtypeStruct((B, R, Lx), BF16),
        compiler_params=_params("parallel", "parallel"),
    )(*args)


def _kv_proj_kernel(c_ref, wn_ref, wv_ref, kr_ref, kt_ref, v_ref):
    c = c_ref[0]
    kn_t = lax.dot_general(wn_ref[0], c, (((1,), (1,)), ((), ())), preferred_element_type=F32)
    rope_end = MLA_NOPE_DIM + MLA_ROPE_DIM
    kt_ref[0, 0, :MLA_NOPE_DIM] = kn_t.astype(kt_ref.dtype)
    kt_ref[0, 0, MLA_NOPE_DIM:rope_end] = kr_ref[0]
    kt_ref[0, 0, rope_end:] = jnp.zeros((QK_PAD - rope_end, c.shape[0]), kt_ref.dtype)
    v_ref[0, 0] = jnp.dot(c, wv_ref[0], preferred_element_type=F32).astype(v_ref.dtype)


def kv_proj(ckvn, wn_t, wv, kr_t, *, tk=2176):
    B, K, R = ckvn.shape
    H, V = wv.shape[0], wv.shape[2]
    return pl.pallas_call(
        _kv_proj_kernel,
        grid=(B, H, K // tk),
        in_specs=[pl.BlockSpec((1, tk, R), lambda b, h, i: (b, i, 0)),
                  pl.BlockSpec((1, MLA_NOPE_DIM, R), lambda b, h, i: (h, 0, 0)),
                  pl.BlockSpec((1, R, V), lambda b, h, i: (h, 0, 0)),
                  pl.BlockSpec((1, MLA_ROPE_DIM, tk), lambda b, h, i: (b, 0, i))],
        out_specs=[pl.BlockSpec((1, 1, QK_PAD, tk), lambda b, h, i: (b, h, 0, i)),
                   pl.BlockSpec((1, 1, tk, V), lambda b, h, i: (b, h, i, 0))],
        out_shape=[jax.ShapeDtypeStruct((B, H, QK_PAD, K), BF16), jax.ShapeDtypeStruct((B, H, K, V), BF16)],
        compiler_params=_params("parallel", "parallel", "parallel"),
    )(ckvn, wn_t, wv, kr_t)


def _mla_layer(h, h_c, B, L, Lc, w_in, g_q, g_kv, w_uq, w_ukv):
    H = MLA_HEADS
    lat = MLA_Q_RANK + MLA_KV_RANK
    w_in_b = w_in.astype(BF16)
    proj = matmul(h, w_in_b[:, :lat], tn=lat)
    ckv_c = matmul(h_c, w_in_b[:, MLA_Q_RANK:lat], tn=MLA_KV_RANK)
    ckv = jnp.concatenate([proj[:, MLA_Q_RANK:].reshape(B, L, -1), ckv_c.reshape(B, Lc, -1)], axis=1)
    cos, sin = _rope_tables(L // GRID_W)
    ones, zeros = jnp.ones((L, MLA_NOPE_DIM), F32), jnp.zeros((L, MLA_NOPE_DIM), F32)
    pad = jnp.zeros((L, QK_PAD - MLA_NOPE_DIM - MLA_ROPE_DIM), F32)
    cosq = jnp.concatenate([ones, cos, cos, pad], axis=1)
    sinq = jnp.concatenate([zeros, sin, sin, pad], axis=1)
    qp = q_proj(_small_rmsnorm(proj[:, :MLA_Q_RANK], g_q).astype(BF16), _q_weights(w_uq), cosq, sinq, B, L)
    w_kr_t = w_in_b[:, lat:].T
    kr_t = jnp.concatenate([krope_t(h, w_kr_t, B, L, 1024, cos.T, sin.T), krope_t(h_c, w_kr_t, B, Lc, Lc)], axis=2)
    w_kv = w_ukv.astype(BF16).reshape(MLA_KV_RANK, H, MLA_NOPE_DIM + MLA_V_DIM)
    wn_t = w_kv[..., :MLA_NOPE_DIM].transpose(1, 2, 0)
    wv = w_kv[..., MLA_NOPE_DIM:].transpose(1, 0, 2)
    kt, v = kv_proj(_small_rmsnorm(ckv, g_kv).astype(BF16), wn_t, wv, kr_t)
    scale = (MLA_NOPE_DIM + MLA_ROPE_DIM) ** -0.5
    return attention(qp, kt, v, scale=scale)


def _fft_tables(L):
    n = 2 * L
    n2_in = L // FFT_N1
    k2 = np.arange(FFT_K2)
    th = 2 * np.pi * ((k2[:, None] * np.arange(n2_in)[None, :]) % FFT_N2) / FFT_N2
    f1 = np.zeros((2 * FFT_K2P, n2_in))
    f1[:FFT_K2] = np.cos(th)
    f1[FFT_K2P:FFT_K2P + FFT_K2] = -np.sin(th)
    wgt = np.where((k2 == 0) | (k2 == FFT_N2 // 2), 1.0, 2.0)
    g = np.zeros((n2_in, 2 * FFT_K2P))
    g[:, :FFT_K2] = (wgt[:, None] * np.cos(th)).T / n
    g[:, FFT_K2P:FFT_K2P + FFT_K2] = -(wgt[:, None] * np.sin(th)).T / n
    k1 = np.arange(FFT_N1)[:, None]
    n1 = np.arange(FFT_N1)[None, :]
    m2 = np.zeros((-(-FFT_K2 // FFT_KSTEP) * FFT_KSTEP, 2 * FFT_N1, 2 * FFT_N1))
    for kk in range(FFT_K2):
        phi = 2 * np.pi * ((n1 * k1 * FFT_N2 + n1 * kk) % n) / n
        tr, ti = np.cos(phi), -np.sin(phi)
        m2[kk] = np.block([[tr, -ti], [ti, tr]])
    as_bf16 = lambda a: jnp.asarray(a, dtype=F32).astype(BF16)
    return (as_bf16(np.kron(f1, np.eye(FFT_SUB))), as_bf16(np.kron(g, np.eye(FFT_GROUP))), as_bf16(m2),
            as_bf16(np.transpose(m2, (0, 2, 1))))


def _fft_s1_kernel(f_ref, z_ref, o_ref):
    _, n2_in, g, ct = z_ref.shape
    z = z_ref[0].astype(F32)
    parts = []
    for s0 in range(0, g, FFT_SUB):
        zz = z[:, s0:s0 + FFT_SUB, :].reshape(n2_in * FFT_SUB, ct).astype(BF16)
        a = jnp.dot(f_ref[...], zz, preferred_element_type=F32)
        parts.append(a.reshape(2, FFT_K2P, FFT_SUB, ct))
    o_ref[0] = jnp.concatenate(parts, axis=2).astype(o_ref.dtype)


def fft_stage1(z4, f1k, *, lead_off, nb, ct=2048):
    _, n2_in, _, C = z4.shape
    g = FFT_GROUP
    return pl.pallas_call(
        _fft_s1_kernel,
        grid=(nb, FFT_N1 // g, C // ct),
        in_specs=[pl.BlockSpec(f1k.shape, lambda b, i, j: (0, 0)),
                  pl.BlockSpec((1, n2_in, g, ct), lambda b, i, j: (lead_off + b, 0, i, j))],
        out_specs=pl.BlockSpec((1, 2, FFT_K2P, g, ct), lambda b, i, j: (b, 0, 0, i, j)),
        out_shape=jax.ShapeDtypeStruct((nb, 2, FFT_K2P, FFT_N1, C), BF16),
        compiler_params=_params("parallel", "parallel", "parallel"),
    )(f1k, z4)


def _fft_s2_kernel(a_ref, m_ref, mt_ref, kf_ref, o_ref):
    step = pl.program_id(0)
    ct = a_ref.shape[-1]
    for q in range(FFT_KSTEP):
        k2 = step * FFT_KSTEP + q

        @pl.when(k2 < FFT_K2)
        def _(q=q):
            x = jnp.dot(m_ref[q], a_ref[0, :, q].reshape(2 * FFT_N1, ct), preferred_element_type=F32)
            xr, xi = x[:FFT_N1], x[FFT_N1:]
            kr, ki = kf_ref[0, 0, q].astype(F32), kf_ref[0, 1, q].astype(F32)
            p = jnp.concatenate([xr * kr - xi * ki, xr * ki + xi * kr], axis=0).astype(BF16)
            e = jnp.dot(mt_ref[q], p, preferred_element_type=F32)
            o_ref[0, :, q] = e.astype(o_ref.dtype).reshape(2, FFT_N1, ct)

        @pl.when(k2 >= FFT_K2)
        def _(q=q):
            o_ref[0, :, q] = jnp.zeros((2, FFT_N1, ct), o_ref.dtype)


def fft_stage2(a5, m2, m2t, kf, order):
    B, _, _, _, D = a5.shape
    last = (FFT_K2 - 1) // FFT_KSTEP
    blk = (1, 2, FFT_KSTEP, FFT_N1, D)
    tbl = pl.BlockSpec((FFT_KSTEP, 2 * FFT_N1, 2 * FFT_N1), lambda k, b: (jnp.minimum(k, last), 0, 0))
    return pl.pallas_call(
        _fft_s2_kernel,
        grid=(FFT_K2P // FFT_KSTEP, B),
        in_specs=[pl.BlockSpec(blk, lambda k, b: (b, 0, jnp.minimum(k, last), 0, 0)),
                  tbl,
                  tbl,
                  pl.BlockSpec(blk, lambda k, b: (order, 0, jnp.minimum(k, last), 0, 0))],
        out_specs=pl.BlockSpec(blk, lambda k, b: (b, 0, k, 0, 0)),
        out_shape=jax.ShapeDtypeStruct(a5.shape, BF16),
        compiler_params=_params("parallel", "arbitrary"),
    )(a5, m2, m2t, kf)


def _fft_f2_kernel(f_ref, g_ref, m_ref, inv_ref, o_ref):
    step = pl.program_id(0)
    ct = f_ref.shape[-1]
    inv = inv_ref[0]
    for q in range(FFT_KSTEP):
        k2 = step * FFT_KSTEP + q

        @pl.when(k2 < FFT_K2)
        def _(q=q):
            xf = jnp.dot(m_ref[q], f_ref[:, q].reshape(2 * FFT_N1, ct), preferred_element_type=F32)
            xg = jnp.dot(m_ref[q], g_ref[:, q].reshape(2 * FFT_N1, ct), preferred_element_type=F32)
            o_ref[0, 0, q] = ((xf[:FFT_N1] + xg[:FFT_N1]) * inv).astype(o_ref.dtype)
            o_ref[0, 1, q] = ((xf[FFT_N1:] - xg[FFT_N1:]) * inv).astype(o_ref.dtype)

        @pl.when(k2 >= FFT_K2)
        def _(q=q):
            o_ref[0, :, q] = jnp.zeros((2, FFT_N1, ct), o_ref.dtype)


def fft_filter_spectrum(a4, m2, inv, D):
    last = (FFT_K2 - 1) // FFT_KSTEP
    blk = (2, FFT_KSTEP, FFT_N1, D)
    return pl.pallas_call(
        _fft_f2_kernel,
        grid=(FFT_K2P // FFT_KSTEP, HYENA_ORDER),
        in_specs=[pl.BlockSpec(blk, lambda k, o: (0, jnp.minimum(k, last), 0, 2 * o)),
                  pl.BlockSpec(blk, lambda k, o: (0, jnp.minimum(k, last), 0, 2 * o + 1)),
                  pl.BlockSpec((FFT_KSTEP, 2 * FFT_N1, 2 * FFT_N1), lambda k, o: (jnp.minimum(k, last), 0, 0)),
                  pl.BlockSpec((1, 1, D), lambda k, o: (o, 0, 0))],
        out_specs=pl.BlockSpec((1, 2, FFT_KSTEP, FFT_N1, D), lambda k, o: (o, 0, k, 0, 0)),
        out_shape=jax.ShapeDtypeStruct((HYENA_ORDER, 2, FFT_K2P, FFT_N1, D), BF16),
        compiler_params=_params("arbitrary", "parallel"),
    )(a4, a4, m2, inv)


def _fft_s3_kernel(g_ref, e_ref, z_ref, x_ref, skip_ref, o_ref):
    _, n2_in, g, ct = z_ref.shape
    e = e_ref[0].reshape(2 * FFT_K2P * g, ct)
    y = jnp.dot(g_ref[...], e, preferred_element_type=F32).reshape(n2_in, g, ct)
    o_ref[0] = (x_ref[0] * (y + skip_ref[...] * z_ref[0])).astype(o_ref.dtype)


def fft_stage3(e5, gk, z4, z_off, x4, x_off, skip, *, out_dtype):
    B, _, _, _, D = e5.shape
    n2_in = z4.shape[1]
    g = FFT_GROUP
    return pl.pallas_call(
        _fft_s3_kernel,
        grid=(B, FFT_N1 // g),
        in_specs=[pl.BlockSpec(gk.shape, lambda b, i: (0, 0)),
                  pl.BlockSpec((1, 2, FFT_K2P, g, D), lambda b, i: (b, 0, 0, i, 0)),
                  pl.BlockSpec((1, n2_in, g, D), lambda b, i: (z_off + b, 0, i, 0)),
                  pl.BlockSpec((1, n2_in, g, D), lambda b, i: (x_off + b, 0, i, 0)),
                  pl.BlockSpec((1, D), lambda b, i: (0, 0))],
        out_specs=pl.BlockSpec((1, n2_in, g, D), lambda b, i: (b, 0, i, 0)),
        out_shape=jax.ShapeDtypeStruct((B, n2_in, FFT_N1, D), out_dtype),
        compiler_params=_params("parallel", "parallel"),
    )(gk, e5, z4, x4, skip.reshape(1, D))


def _hyin_kernel(ap_ref, a_ref, an_ref, b_ref, cw_ref, cb_ref, o_ref, *, tiles_per_seq):
    i = pl.program_id(0)
    t = i % tiles_per_seq
    halo = ap_ref.shape[0]
    tm, tn = o_ref.shape[1:]
    cw = MXU_DIM
    row = lax.broadcasted_iota(jnp.int32, (tm, cw), 0)
    at_start = jnp.logical_and(row == 0, t == 0)
    at_end = jnp.logical_and(row == tm - 1, t == tiles_per_seq - 1)
    a_ext = jnp.concatenate([ap_ref[...], a_ref[...], an_ref[...]], axis=0)
    for c0 in range(0, tn, cw):
        acc = jnp.dot(a_ext, b_ref[:, c0:c0 + cw], preferred_element_type=F32)
        up = jnp.where(at_start, 0.0, acc[halo - 1:halo - 1 + tm])
        dn = jnp.where(at_end, 0.0, acc[halo + 1:halo + 1 + tm])
        w = cw_ref[:, c0:c0 + cw]
        o_ref[0, :, c0:c0 + cw] = up * w[0:1] + acc[halo:halo + tm] * w[1:2] + dn * w[2:3] + cb_ref[:, c0:c0 + cw]


def hyena_in(h, w_in, conv_w, conv_b, L, *, tm=1024, tn=1024, halo=16):
    M, K = h.shape
    N = w_in.shape[1]
    D = N // 3
    per = D // tn
    return pl.pallas_call(
        functools.partial(_hyin_kernel, tiles_per_seq=L // tm),
        grid=(M // tm, N // tn),
        in_specs=[pl.BlockSpec((halo, K), lambda i, j: (jnp.maximum(i * (tm // halo) - 1, 0), 0)),
                  pl.BlockSpec((tm, K), lambda i, j: (i, 0)),
                  pl.BlockSpec((halo, K), lambda i, j: (jnp.minimum((i + 1) * (tm // halo), M // halo - 1), 0)),
                  pl.BlockSpec((K, tn), lambda i, j: (0, j)),
                  pl.BlockSpec((3, tn), lambda i, j: (0, j)),
                  pl.BlockSpec((1, tn), lambda i, j: (0, j))],
        out_specs=pl.BlockSpec((1, tm, tn), lambda i, j: (j // per, i, j % per)),
        out_shape=jax.ShapeDtypeStruct((3, M, D), F32),
        compiler_params=_params("parallel", "parallel"),
    )(h, h, h, w_in, conv_w, conv_b.reshape(1, N))


def _taps_kernel(a_ref, w_ref, b_ref, dl_ref, keep_ref, o_ref, ss_ref, *, seq_len):
    i = pl.program_id(1)
    tm = a_ref.shape[0]
    acc = jnp.dot(a_ref[...].astype(BF16), w_ref[...].astype(BF16), preferred_element_type=F32) + b_ref[...]
    row = i * tm + lax.broadcasted_iota(jnp.int32, acc.shape, 0)
    t = row.astype(F32) / seq_len
    hf = acc * jnp.exp(-t * dl_ref[...])
    hf = jnp.where(row == 0, hf * keep_ref[...], hf)
    o_ref[...] = hf.astype(o_ref.dtype)
    part = jnp.sum(hf * hf, axis=0, keepdims=True)

    @pl.when(i == 0)
    def _():
        ss_ref[...] = part

    @pl.when(i > 0)
    def _():
        ss_ref[...] += part


def _hyena_filter_taps(L, w1, b1, w2, b2, w3, b3, freq, *, tm=512, tn=1024):
    D = D_MODEL
    N = 2 * HYENA_ORDER * D
    t = jnp.arange(L, dtype=F32) / L
    w = 2 * math.pi * jnp.arange(L, dtype=F32) / L
    bands = jnp.linspace(1e-4, HYENA_BANDS - 1, HYENA_BANDS, dtype=F32)
    ang = w[:, None] * bands[None, :]
    z = jnp.concatenate([t[:, None], jnp.cos(ang), -jnp.sin(ang)], axis=-1)
    a = jnp.sin(freq * (z @ w1 + b1))
    a = jnp.sin(freq * (a @ w2 + b2))
    max_decay = math.log(HYENA_TARGET) / HYENA_FAST_PCT
    min_decay = math.log(HYENA_TARGET) / HYENA_SLOW_PCT
    decay = jnp.tile(jnp.abs(jnp.linspace(min_decay, max_decay, D, dtype=F32)), 2 * HYENA_ORDER).reshape(1, N)
    keep0 = jnp.where((jnp.arange(N) // D) % 2 == 1, 0.0, 1.0).astype(F32).reshape(1, N)
    K = a.shape[1]
    taps, ss = pl.pallas_call(
        functools.partial(_taps_kernel, seq_len=L),
        grid=(N // tn, L // tm),
        in_specs=[pl.BlockSpec((tm, K), lambda j, i: (i, 0)),
                  pl.BlockSpec((K, tn), lambda j, i: (0, j)),
                  pl.BlockSpec((1, tn), lambda j, i: (0, j)),
                  pl.BlockSpec((1, tn), lambda j, i: (0, j)),
                  pl.BlockSpec((1, tn), lambda j, i: (0, j))],
        out_specs=[pl.BlockSpec((tm, tn), lambda j, i: (i, j)), pl.BlockSpec((1, tn), lambda j, i: (0, j))],
        out_shape=[jax.ShapeDtypeStruct((L, N), BF16), jax.ShapeDtypeStruct((1, N), F32)],
        compiler_params=_params("parallel", "arbitrary"),
    )(a, w3, b3.reshape(1, N), decay, keep0)
    ss = ss.reshape(HYENA_ORDER, 2, D).sum(axis=1, keepdims=True)
    return taps, lax.rsqrt(ss + NORM_EPS)


def _hyena_layer(h, B, L, w_in, conv_w, conv_b, f_w1, f_b1, f_w2, f_b2, f_w3, f_b3, f_freq, skip):
    D = D_MODEL
    n2_in = L // FFT_N1
    f1k, gk, m2, m2t = _fft_tables(L)
    taps, inv = _hyena_filter_taps(L, f_w1, f_b1, f_w2, f_b2, f_w3, f_b3, f_freq)
    af = fft_stage1(taps.reshape(1, n2_in, FFT_N1, taps.shape[1]), f1k, lead_off=0, nb=1)
    kf = fft_filter_spectrum(af[0], m2, inv, D)
    u4 = hyena_in(h, w_in.astype(BF16), conv_w, conv_b, L).reshape(3 * B, n2_in, FFT_N1, D)
    z4 = u4
    for order in range(HYENA_ORDER):
        a = fft_stage1(z4, f1k, lead_off=0, nb=B)
        e = fft_stage2(a, m2, m2t, kf, order)
        last = order == HYENA_ORDER - 1
        z4 = fft_stage3(e, gk, z4, 0, u4, (order + 1) * B, skip[order], out_dtype=BF16 if last else F32)
    return z4.reshape(B * L, D)


def _moe_layer(x2, norm_gain, mod, B, T, w_router, w_gate, w_up, w_down, layer, next_g, next_shift=None,
               next_scale=None, *, tj=256, win=64):
    E = N_EXPERTS
    cap = EC_CAPACITY_FACTOR * T // E
    hp, aff = norm_route(x2, norm_gain, mod[:, 3:4], mod[:, 4:5], w_router.T.astype(BF16), rows_per_batch=T)
    pos, idx, g = route_select(aff, cap)
    idx3 = idx.reshape(B, E, cap)
    y = moe_ffn(idx3, hp, g, w_gate, w_up, w_down, layer, T)
    nj = T // tj
    edges = jnp.arange(nj + 1, dtype=jnp.int32) * tj
    first = jnp.sum((idx3[..., None] < edges).astype(jnp.int32), axis=2)
    nominal = (first[..., :nj] // 16) * 16
    ntr = jnp.maximum(jnp.max((first[..., 1:] - nominal + win - 1) // win, axis=1), 1)
    starts = nominal.transpose(0, 2, 1).reshape(-1)
    return moe_combine(starts, ntr.reshape(-1), y, pos.transpose(0, 2, 1), x2, mod[:, 5:6], next_g, next_shift,
                       next_scale, tj=tj, win=win)


def kernel(x, c, ctx, c_ctx, ada_w, ada_b, norm_g, final_g, mla_w_in, mla_g_q, mla_g_kv, mla_w_uq, mla_w_ukv, mla_w_o, hy_w_in, hy_conv_w, hy_conv_b, hy_f_w1, hy_f_b1, hy_f_w2, hy_f_b2, hy_f_w3, hy_f_b3, hy_f_freq, hy_skip, hy_w_out, moe_w_router, moe_w_gate, moe_w_up, moe_w_down):
    B, L, D = x.shape
    Lc = ctx.shape[1]
    x2 = x.reshape(B * L, D)
    ctx2 = ctx.reshape(B * Lc, D)

    mod3 = _ada(jnp.concatenate([c, c_ctx[None]], axis=0), ada_w, ada_b, 0)
    mod, mod_c = mod3[:B], jnp.broadcast_to(mod3[B:], (B, 6, D))
    h = norm_mod(x2, norm_g[0, 0], mod[:, 0:1], mod[:, 1:2], rows_per_batch=L)
    h_c = norm_mod(ctx2, norm_g[0, 0], mod_c[:, 0:1], mod_c[:, 1:2], rows_per_batch=Lc)
    o = _mla_layer(h, h_c, B, L, Lc, mla_w_in[0], mla_g_q[0], mla_g_kv[0], mla_w_uq[0], mla_w_ukv[0])
    x2 = matmul(o.reshape(B * L, -1), mla_w_o[0].astype(BF16), tn=D,
                res=x2, gate=mod[:, 2:3], rows_per_batch=L)
    mod1 = _ada(c, ada_w, ada_b, 1)
    x2, h = _moe_layer(x2, norm_g[0, 1], mod, B, L, moe_w_router[0], moe_w_gate, moe_w_up, moe_w_down, 0,
                       norm_g[1, 0], mod1[:, 0:1], mod1[:, 1:2])

    mod = mod1
    z = _hyena_layer(h, B, L, hy_w_in[0], hy_conv_w[0], hy_conv_b[0], hy_f_w1[0], hy_f_b1[0], hy_f_w2[0],
                     hy_f_b2[0], hy_f_w3[0], hy_f_b3[0], hy_f_freq[0], hy_skip[0])
    x2 = matmul(z, hy_w_out[0].astype(BF16), tn=D, res=x2, gate=mod[:, 2:3], rows_per_batch=L)
    out = _moe_layer(x2, norm_g[1, 1], mod, B, L, moe_w_router[1], moe_w_gate, moe_w_up, moe_w_down, 1, final_g)
    return out.reshape(B, L, D)
```

```python
import functools
import math

import jax
import jax.numpy as jnp
import numpy as np
from jax import lax
from jax.experimental import pallas as pl
from jax.experimental.pallas import tpu as pltpu

F32 = jnp.float32
BF16 = jnp.bfloat16

D_MODEL = 2048
GRID_W = 64
NORM_EPS = 1e-6
MLA_HEADS = 16
MLA_Q_RANK = 512
MLA_KV_RANK = 256
MLA_NOPE_DIM = 128
MLA_ROPE_DIM = 64
MLA_V_DIM = 128
ROPE_THETA = 10000.0
HYENA_ORDER = 2
HYENA_BANDS = 16
HYENA_TARGET = 1e-2
HYENA_FAST_PCT = 0.3
HYENA_SLOW_PCT = 1.5
N_EXPERTS = 16
EC_CAPACITY_FACTOR = 2
FFT_N1 = 128
FFT_N2 = 64
FFT_K2 = FFT_N2 // 2 + 1
FFT_K2P = 40
FFT_GROUP = 16
FFT_SUB = 8
FFT_KSTEP = 4

V7X_VMEM_LIMIT_BYTES = 56 * 1024 * 1024
LANES = 128
MXU_DIM = 256
QK_PAD = MXU_DIM


def _params(*sem):
    return pltpu.CompilerParams(dimension_semantics=sem, vmem_limit_bytes=V7X_VMEM_LIMIT_BYTES)


def _mm_kernel(a_ref, b_ref, *rest, b_is_stacked, has_res):
    o_ref = rest[-1]
    b = b_ref[0] if b_is_stacked else b_ref[...]
    acc = jnp.dot(a_ref[...].astype(BF16), b.astype(BF16), preferred_element_type=F32)
    if has_res:
        res_ref, gate_ref = rest[0], rest[1]
        acc = res_ref[...] + gate_ref[0] * acc
    o_ref[...] = acc.astype(o_ref.dtype)


def matmul(a, b, *, b_index=None, out_dtype=F32, tm=512, tn=512, res=None, gate=None, rows_per_batch=None):
    M, K = a.shape
    N = b.shape[-1]
    tm, tn = min(tm, M), min(tn, N)
    assert M % tm == 0 and N % tn == 0, (M, N, tm, tn)
    stacked = b.ndim == 3
    if stacked:
        b_spec = pl.BlockSpec((1, K, tn), lambda i, j: (b_index, 0, j))
    else:
        b_spec = pl.BlockSpec((K, tn), lambda i, j: (0, j))
    in_specs = [pl.BlockSpec((tm, K), lambda i, j: (i, 0)), b_spec]
    args = [a, b]
    if res is not None:
        assert rows_per_batch % tm == 0
        tpb = rows_per_batch // tm
        in_specs += [pl.BlockSpec((tm, tn), lambda i, j: (i, j)),
                     pl.BlockSpec((1, 1, tn), lambda i, j: (i // tpb, 0, j))]
        args += [res, gate]
    return pl.pallas_call(
        functools.partial(_mm_kernel, b_is_stacked=stacked, has_res=res is not None),
        grid=(M // tm, N // tn),
        in_specs=in_specs,
        out_specs=pl.BlockSpec((tm, tn), lambda i, j: (i, j)),
        out_shape=jax.ShapeDtypeStruct((M, N), out_dtype),
        compiler_params=_params("parallel", "parallel"),
    )(*args)


def _norm_kernel(x_ref, g_ref, *rest, modulated):
    o_ref = rest[-1]
    x = x_ref[...]
    y = x * lax.rsqrt(jnp.mean(x * x, axis=-1, keepdims=True) + NORM_EPS) * g_ref[...]
    if modulated:
        shift_ref, scale_ref = rest[0], rest[1]
        y = y * (1 + scale_ref[0]) + shift_ref[0]
    o_ref[...] = y.astype(o_ref.dtype)


def norm_mod(x2, g, shift=None, scale=None, *, rows_per_batch=None, out_dtype=BF16, tm=256):
    M, D = x2.shape
    tm = min(tm, M)
    in_specs = [pl.BlockSpec((tm, D), lambda i: (i, 0)), pl.BlockSpec((1, D), lambda i: (0, 0))]
    args = [x2, g.reshape(1, D)]
    if shift is not None:
        tpb = rows_per_batch // tm
        in_specs += [pl.BlockSpec((1, 1, D), lambda i: (i // tpb, 0, 0))] * 2
        args += [shift, scale]
    return pl.pallas_call(
        functools.partial(_norm_kernel, modulated=shift is not None),
        grid=(M // tm,),
        in_specs=in_specs,
        out_specs=pl.BlockSpec((tm, D), lambda i: (i, 0)),
        out_shape=jax.ShapeDtypeStruct((M, D), out_dtype),
        compiler_params=_params("parallel"),
    )(*args)


def _attn_kernel(q_ref, kt_ref, v_ref, o_ref, *, chunks, c):
    q = q_ref[0, 0]
    m = l = acc = None
    for lo, hi in chunks:
        s = jnp.dot(q, kt_ref[0, 0, :, lo:hi], preferred_element_type=F32)
        m_chunk = jnp.max(s, axis=-1, keepdims=True)
        m_new = m_chunk if m is None else jnp.maximum(m, m_chunk)
        p = jnp.exp2(s * c - m_new * c)
        pv = jnp.dot(p.astype(BF16), v_ref[0, 0, lo:hi, :], preferred_element_type=F32)
        psum = jnp.sum(p, axis=-1, keepdims=True)
        if m is None:
            l, acc = psum, pv
        else:
            alpha = jnp.exp2((m - m_new) * c)
            l = alpha * l + psum
            acc = alpha * acc + pv
        m = m_new
    o_ref[0] = (acc / l).astype(o_ref.dtype)


def attention(q, kt, v, *, scale, tq=2048, tk=1024):
    B, H, L, _ = q.shape
    K = kt.shape[-1]
    V = v.shape[-1]
    chunks = tuple((lo, min(lo + tk, K)) for lo in range(0, K, tk))
    return pl.pallas_call(
        functools.partial(_attn_kernel, chunks=chunks, c=scale * math.log2(math.e)),
        grid=(B, H, L // tq),
        in_specs=[pl.BlockSpec((1, 1, tq, QK_PAD), lambda b, h, i: (b, h, i, 0)),
                  pl.BlockSpec((1, 1, QK_PAD, K), lambda b, h, i: (b, h, 0, 0)),
                  pl.BlockSpec((1, 1, K, V), lambda b, h, i: (b, h, 0, 0))],
        out_specs=pl.BlockSpec((1, tq, V), lambda b, h, i: (b, i, h)),
        out_shape=jax.ShapeDtypeStruct((B, L, H * V), BF16),
        compiler_params=_params("parallel", "parallel", "arbitrary"),
    )(q, kt, v)


def _norm_route_kernel(x_ref, g_ref, shift_ref, scale_ref, wr_ref, h_ref, aff_ref):
    x = x_ref[...]
    y = x * lax.rsqrt(jnp.mean(x * x, axis=-1, keepdims=True) + NORM_EPS) * g_ref[...]
    y = y * (1 + scale_ref[0]) + shift_ref[0]
    h_ref[...] = y
    logits = lax.dot_general(wr_ref[...], y.astype(BF16), (((1,), (1,)), ((), ())),
                             preferred_element_type=F32)
    p = jnp.exp(logits - jnp.max(logits, axis=0, keepdims=True))
    aff_ref[0] = p / jnp.sum(p, axis=0, keepdims=True)


def norm_route(x2, g, shift, scale, w_router_t, *, rows_per_batch, tm=256):
    M, D = x2.shape
    E = w_router_t.shape[0]
    tpb = rows_per_batch // tm
    return pl.pallas_call(
        _norm_route_kernel,
        grid=(M // tm,),
        in_specs=[pl.BlockSpec((tm, D), lambda i: (i, 0)),
                  pl.BlockSpec((1, D), lambda i: (0, 0)),
                  pl.BlockSpec((1, 1, D), lambda i: (i // tpb, 0, 0)),
                  pl.BlockSpec((1, 1, D), lambda i: (i // tpb, 0, 0)),
                  pl.BlockSpec((E, D), lambda i: (0, 0))],
        out_specs=[pl.BlockSpec((tm, D), lambda i: (i, 0)),
                   pl.BlockSpec((1, E, tm), lambda i: (i // tpb, 0, i % tpb))],
        out_shape=[jax.ShapeDtypeStruct((M, D), F32),
                   jax.ShapeDtypeStruct((M // rows_per_batch, E, rows_per_batch), F32)],
        compiler_params=_params("parallel"),
    )(x2, g.reshape(1, D), shift, scale, w_router_t)


def _lane_prefix_sum(x):
    E, T = x.shape
    r = lax.broadcasted_iota(jnp.int32, (LANES, LANES), 0)
    c = lax.broadcasted_iota(jnp.int32, (LANES, LANES), 1)
    tri = jnp.where(r <= c, 1.0, 0.0).astype(BF16)
    outs, carry = [], jnp.zeros((E, 1), F32)
    for j in range(T // LANES):
        seg = jnp.dot(x[:, j * LANES:(j + 1) * LANES].astype(BF16), tri, preferred_element_type=F32) + carry
        outs.append(seg)
        carry = seg[:, LANES - 1:LANES]
    return jnp.concatenate(outs, axis=1)


def _route_select_kernel(aff_ref, pos_ref, idx_ref, gate_ref, cum_s, *, cap, tch):
    aff = aff_ref[0]
    E, T = aff.shape

    def bisect(state):
        lo, hi, _ = state
        mid = 0.5 * (lo + hi)
        cnt = jnp.sum(jnp.where(aff >= mid, 1.0, 0.0), axis=1, keepdims=True)
        ge = cnt >= cap
        open_rows = jnp.where((mid > lo) & (mid < hi), 1.0, 0.0)
        return jnp.where(ge, mid, lo), jnp.where(ge, hi, mid), jnp.max(open_rows)

    lo0 = jnp.min(aff, axis=1, keepdims=True)
    hi0 = 2.0 * jnp.max(aff, axis=1, keepdims=True) + 1e-30
    lo, hi, _ = lax.while_loop(lambda st: st[2] > 0.0, bisect, (lo0, hi0, jnp.float32(1.0)))
    gt = aff >= hi
    eq = (aff >= lo) & (aff < hi)
    need = cap - jnp.sum(jnp.where(gt, 1.0, 0.0), axis=1, keepdims=True)
    eq_rank = _lane_prefix_sum(jnp.where(eq, 1.0, 0.0))
    sel = gt | (eq & (eq_rank <= need))
    cum = _lane_prefix_sum(jnp.where(sel, 1.0, 0.0))
    cumsel = jnp.where(sel, cum, 0.0)
    pos_ref[0] = (cumsel - 1.0).astype(jnp.int32)
    cum_s[0] = cum
    cum_s[1] = cumsel
    slot = lax.broadcasted_iota(jnp.int32, (cap, 1), 0).astype(F32)

    def per_expert(e, carry):
        idx_acc = jnp.zeros((cap, 1), F32)
        g_acc = jnp.zeros((cap, 1), F32)
        for lo in range(0, T, tch):
            cum_e = cum_s[0, pl.ds(e, 1), lo:lo + tch]
            cs_e = cum_s[1, pl.ds(e, 1), lo:lo + tch]
            aff_e = aff_ref[0, pl.ds(e, 1), lo:lo + tch]
            idx_acc = idx_acc + jnp.sum(jnp.where(cum_e <= slot, 1.0, 0.0), axis=1, keepdims=True)
            g_acc = g_acc + jnp.sum(jnp.where(cs_e == slot + 1.0, aff_e, 0.0), axis=1, keepdims=True)
        idx_ref[0, pl.ds(e, 1)] = idx_acc.astype(jnp.int32).reshape(1, cap, 1)
        gate_ref[0, pl.ds(e, 1)] = g_acc.reshape(1, cap, 1)
        return carry

    lax.fori_loop(0, E, per_expert, 0)


def route_select(aff, cap, *, tch=1024):
    B, E, T = aff.shape
    return pl.pallas_call(
        functools.partial(_route_select_kernel, cap=cap, tch=tch),
        grid=(B,),
        in_specs=[pl.BlockSpec((1, E, T), lambda b: (b, 0, 0))],
        out_specs=[pl.BlockSpec((1, E, T), lambda b: (b, 0, 0)),
                   pl.BlockSpec((1, E, cap, 1), lambda b: (b, 0, 0, 0)),
                   pl.BlockSpec((1, E, cap, 1), lambda b: (b, 0, 0, 0))],
        out_shape=[jax.ShapeDtypeStruct((B, E, T), jnp.int32),
                   jax.ShapeDtypeStruct((B, E, cap, 1), jnp.int32),
                   jax.ShapeDtypeStruct((B, E, cap, 1), F32)],
        scratch_shapes=[pltpu.VMEM((2, E, T), F32)],
        compiler_params=_params("parallel"),
    )(aff)


def _moe_ffn_kernel(idx_ref, idxn_ref, h_hbm, wg_ref, wu_ref, wd_ref, g_ref, o_ref, xland, xbf, h1, sem,
                    *, nf, n_steps, n_exp, cap, tokens_per_batch):
    e = pl.program_id(0)
    f = pl.program_id(1)
    rows = xland.shape[0]

    def issue_gather(ids_ref):
        for b in range(rows // cap):
            def body(c, carry, b=b):
                i = b * cap + c
                src_row = ids_ref[0, 0, i] + b * tokens_per_batch
                pltpu.make_async_copy(h_hbm.at[pl.ds(src_row, 1)], xland.at[pl.ds(i, 1)], sem.at[0]).start()
                return carry
            lax.fori_loop(0, cap, body, 0, unroll=8)

    @pl.when((e == 0) & (f == 0))
    def _():
        issue_gather(idx_ref)

    def wait_gather():
        pltpu.make_async_copy(h_hbm.at[pl.ds(0, rows)], xland, sem.at[0]).wait()

    @pl.when(f == 0)
    def _():
        wait_gather()
        xbf[...] = xland[...].astype(BF16)

    share = rows // n_steps

    def issue_share():
        base = f * share
        off = (base // cap) * tokens_per_batch
        for j in range(share):
            pltpu.make_async_copy(h_hbm.at[pl.ds(idxn_ref[0, 0, base + j] + off, 1)],
                                  xland.at[pl.ds(base + j, 1)], sem.at[0]).start()

    @pl.when(f < nf)
    def _():
        x = xbf[...]
        a = jnp.dot(x, wg_ref[0, 0].astype(BF16), preferred_element_type=F32)
        issue_share()
        u = jnp.dot(x, wu_ref[0, 0].astype(BF16), preferred_element_type=F32)
        h1[jnp.minimum(f, nf - 1)] = (jax.nn.silu(a) * u).astype(BF16)

    @pl.when(f >= nf)
    def _():
        hid = jnp.concatenate([h1[k] for k in range(nf)], axis=1)
        issue_share()
        y = jnp.dot(hid, wd_ref[0, 0].astype(BF16), preferred_element_type=F32)
        o_ref[...] = (y.reshape(o_ref.shape) * g_ref[...]).astype(o_ref.dtype)

    @pl.when((e == n_exp - 1) & (f == n_steps - 1))
    def _():
        wait_gather()


def moe_ffn(idx3, hp, g, w_gate, w_up, w_down, layer, T, *, tf=256, tc=512):
    B, E, C = idx3.shape
    D = hp.shape[1]
    F = w_gate.shape[-1]
    nf = F // tf
    tc = min(tc, D)
    n_steps = nf + D // tc
    assert (B * C) % n_steps == 0 and C % ((B * C) // n_steps) == 0, (B, C, n_steps)
    ids = idx3.transpose(1, 0, 2).reshape(E, 1, B * C)
    smem_blk = lambda imap: pl.BlockSpec((1, 1, B * C), imap, memory_space=pltpu.SMEM)
    up_blk = pl.BlockSpec((1, 1, D, tf), lambda e, s: (layer, e, 0, jnp.minimum(s, nf - 1)))
    return pl.pallas_call(
        functools.partial(_moe_ffn_kernel, nf=nf, n_steps=n_steps, n_exp=E, cap=C, tokens_per_batch=T),
        grid=(E, n_steps),
        in_specs=[smem_blk(lambda e, s: (e, 0, 0)),
                  smem_blk(lambda e, s: (jnp.minimum(e + 1, E - 1), 0, 0)),
                  pl.BlockSpec(memory_space=pl.ANY),
                  up_blk,
                  up_blk,
                  pl.BlockSpec((1, 1, F, tc), lambda e, s: (layer, e, 0, jnp.maximum(s - nf, 0))),
                  pl.BlockSpec((B, 1, C, 1), lambda e, s: (0, e, 0, 0))],
        out_specs=pl.BlockSpec((B, 1, C, tc), lambda e, s: (0, e, 0, jnp.maximum(s - nf, 0))),
        out_shape=jax.ShapeDtypeStruct((B, E, C, D), BF16),
        scratch_shapes=[pltpu.VMEM((B * C, D), F32),
                        pltpu.VMEM((B * C, D), BF16),
                        pltpu.VMEM((nf, B * C, tf), BF16),
                        pltpu.SemaphoreType.DMA((1,))],
        compiler_params=_params("arbitrary", "arbitrary"),
    )(ids, ids, hp, w_gate, w_up, w_down, g)


def _moe_combine_kernel(starts_ref, ntr_ref, y_hbm, post_ref, x_ref, gate_ref, ng_ref, *rest,
                        n_exp, cap, win, steps_per_batch, modulated):
    ybuf, yext, acc, sem, sem_ext = rest[-5:]
    out_refs = rest[2:-5] if modulated else rest[:-5]
    s = pl.program_id(0)
    slot = s % 2

    def window(step, e, k):
        nominal = starts_ref[step * n_exp + e] + k * win
        return nominal, pl.multiple_of(jnp.minimum(nominal, cap - win), 16)

    def window_copy(step, e, k, dst, dsem):
        _, actual = window(step, e, k)
        return pltpu.make_async_copy(y_hbm.at[step // steps_per_batch, e, pl.ds(actual, win)],
                                     dst.at[pl.ds(e * win, win)], dsem)

    def onehot(step, k):
        post = post_ref[0]
        per = max(LANES // win, 1)
        lane = lax.broadcasted_iota(jnp.int32, (1, per * win), 1)
        cols = []
        for e0 in range(0, n_exp, per):
            slots = tok = None
            for q in range(per):
                nominal, actual = window(step, e0 + q, k)
                s_q = actual + lane - q * win
                s_q = jnp.where(s_q >= nominal, s_q, -2)
                t_q = post[:, e0 + q:e0 + q + 1]
                mine = lane >= q * win
                slots = s_q if q == 0 else jnp.where(mine, s_q, slots)
                tok = t_q if q == 0 else jnp.where(mine, t_q, tok)
            cols.append(jnp.where(tok == slots, 1.0, 0.0).astype(BF16))
        return jnp.concatenate(cols, axis=1)

    @pl.when(s == 0)
    def _():
        for e in range(n_exp):
            window_copy(0, e, 0, ybuf.at[0], sem.at[0]).start()

    for e in range(n_exp):
        window_copy(s, e, 0, ybuf.at[slot], sem.at[slot]).wait()

    @pl.when(s + 1 < pl.num_programs(0))
    def _():
        for e in range(n_exp):
            window_copy(s + 1, e, 0, ybuf.at[1 - slot], sem.at[1 - slot]).start()

    acc[...] = jnp.dot(onehot(s, 0), ybuf[slot], preferred_element_type=F32)

    def extra(k, carry):
        for e in range(n_exp):
            window_copy(s, e, k, yext, sem_ext.at[0]).start()
        for e in range(n_exp):
            window_copy(s, e, k, yext, sem_ext.at[0]).wait()
        acc[...] += jnp.dot(onehot(s, k), yext[...], preferred_element_type=F32)
        return carry

    lax.fori_loop(1, ntr_ref[s], extra, 0)
    x = x_ref[...] + gate_ref[0] * acc[...]
    y = x * lax.rsqrt(jnp.mean(x * x, axis=-1, keepdims=True) + NORM_EPS) * ng_ref[...]
    if modulated:
        out_refs[0][...] = x
        y = y * (1 + rest[1][0]) + rest[0][0]
    out_refs[-1][...] = y.astype(out_refs[-1].dtype)


def moe_combine(starts, ntr, y, post, x2, gate, norm_g, shift=None, scale=None, *, tj=256, win=64):
    B, E, C, D = y.shape
    M = x2.shape[0]
    T = M // B
    spb = T // tj
    modulated = shift is not None
    row_blk = pl.BlockSpec((tj, D), lambda s, *_: (s, 0))
    per_batch = pl.BlockSpec((1, 1, D), lambda s, *_: (s // spb, 0, 0))
    in_specs = [pl.BlockSpec(memory_space=pl.ANY),
                pl.BlockSpec((1, tj, E), lambda s, *_: (s // spb, s % spb, 0)),
                row_blk, per_batch, pl.BlockSpec((1, D), lambda s, *_: (0, 0))]
    args = [starts, ntr, y, post, x2, gate, norm_g.reshape(1, D)]
    if modulated:
        in_specs += [per_batch, per_batch]
        args += [shift, scale]
        out_specs = [row_blk, row_blk]
        out_shape = [jax.ShapeDtypeStruct((M, D), F32), jax.ShapeDtypeStruct((M, D), BF16)]
    else:
        out_specs = row_blk
        out_shape = jax.ShapeDtypeStruct((M, D), F32)
    grid_spec = pltpu.PrefetchScalarGridSpec(
        num_scalar_prefetch=2,
        grid=(M // tj,),
        in_specs=in_specs,
        out_specs=out_specs,
        scratch_shapes=[pltpu.VMEM((2, E * win, D), BF16),
                        pltpu.VMEM((E * win, D), BF16),
                        pltpu.VMEM((tj, D), F32),
                        pltpu.SemaphoreType.DMA((2,)),
                        pltpu.SemaphoreType.DMA((1,))],
    )
    return pl.pallas_call(
        functools.partial(_moe_combine_kernel, n_exp=E, cap=C, win=win, steps_per_batch=spb, modulated=modulated),
        grid_spec=grid_spec,
        out_shape=out_shape,
        compiler_params=_params("arbitrary"),
    )(*args)


def _small_rmsnorm(x, g):
    return x * lax.rsqrt(jnp.mean(x * x, axis=-1, keepdims=True) + NORM_EPS) * g


def _rope_tables(rows):
    row = jnp.broadcast_to(jnp.arange(rows)[:, None], (rows, GRID_W)).reshape(-1).astype(F32)
    col = jnp.broadcast_to(jnp.arange(GRID_W)[None, :], (rows, GRID_W)).reshape(-1).astype(F32)
    n_freq = MLA_ROPE_DIM // 4
    inv = ROPE_THETA ** (-jnp.arange(n_freq, dtype=F32) / n_freq)
    ang = jnp.concatenate([row[:, None] * inv, col[:, None] * inv], axis=-1)
    return jnp.cos(ang), jnp.sin(ang)


def _ada(cvec3, ada_w, ada_b, layer):
    R = cvec3.shape[0]
    a = jnp.concatenate([jax.nn.silu(cvec3), jnp.zeros((8 - R, cvec3.shape[1]), F32)], axis=0)
    m = matmul(a, ada_w, b_index=layer, tm=8, tn=512)[:R] + ada_b[layer]
    return m.reshape(R, 6, D_MODEL)


def _q_proj_kernel(cq_ref, w_ref, cos_ref, sin_ref, o_ref):
    cq, cos, sin = cq_ref[...], cos_ref[...], sin_ref[...]
    for hh in range(o_ref.shape[1]):
        w = w_ref[:, hh * 2 * QK_PAD:(hh + 1) * 2 * QK_PAD]
        acc = jnp.dot(cq, w, preferred_element_type=F32)
        o_ref[0, hh] = (acc[:, :QK_PAD] * cos + acc[:, QK_PAD:] * sin).astype(o_ref.dtype)


def q_proj(cqn, w_q2, cosq, sinq, B, L, *, tm=1024, hg=4):
    M, R = cqn.shape
    tpb = L // tm
    return pl.pallas_call(
        _q_proj_kernel,
        grid=(M // tm, MLA_HEADS // hg),
        in_specs=[pl.BlockSpec((tm, R), lambda i, h: (i, 0)),
                  pl.BlockSpec((R, hg * 2 * QK_PAD), lambda i, h: (0, h)),
                  pl.BlockSpec((tm, QK_PAD), lambda i, h: (i % tpb, 0)),
                  pl.BlockSpec((tm, QK_PAD), lambda i, h: (i % tpb, 0))],
        out_specs=pl.BlockSpec((1, hg, tm, QK_PAD), lambda i, h: (i // tpb, h, i % tpb, 0)),
        out_shape=jax.ShapeDtypeStruct((B, MLA_HEADS, L, QK_PAD), BF16),
        compiler_params=_params("parallel", "parallel"),
    )(cqn, w_q2, cosq, sinq)


def _q_weights(w_uq):
    R = w_uq.shape[0]
    half = MLA_ROPE_DIM // 2
    w = w_uq.reshape(R, MLA_HEADS, MLA_NOPE_DIM + MLA_ROPE_DIM)
    nope, rope = w[..., :MLA_NOPE_DIM], w[..., MLA_NOPE_DIM:]
    rot = jnp.concatenate([-rope[..., half:], rope[..., :half]], axis=-1)
    pad = jnp.zeros((R, MLA_HEADS, QK_PAD - MLA_NOPE_DIM - MLA_ROPE_DIM), w.dtype)
    parts = [nope, rope, pad, jnp.zeros_like(nope), rot, pad]
    return jnp.concatenate(parts, axis=-1).reshape(R, MLA_HEADS * 2 * QK_PAD).astype(BF16)


def _krope_t_kernel(w_ref, h_ref, *rest, roped):
    o_ref = rest[-1]
    x = lax.dot_general(w_ref[...], h_ref[...], (((1,), (1,)), ((), ())), preferred_element_type=F32)
    if roped:
        half = x.shape[0] // 2
        x1, x2, c, s = x[:half], x[half:], rest[0][...], rest[1][...]
        x = jnp.concatenate([x1 * c - x2 * s, x2 * c + x1 * s], axis=0)
    o_ref[0] = x.astype(o_ref.dtype)


def krope_t(h2, w_kr_t, B, Lx, tt, cos_t=None, sin_t=None):
    D = h2.shape[1]
    R = w_kr_t.shape[0]
    per = Lx // tt
    roped = cos_t is not None
    in_specs = [pl.BlockSpec((R, D), lambda b, i: (0, 0)), pl.BlockSpec((tt, D), lambda b, i: (b * per + i, 0))]
    args = [w_kr_t, h2]
    if roped:
        in_specs += [pl.BlockSpec((R // 2, tt), lambda b, i: (0, i))] * 2
        args += [cos_t, sin_t]
    return pl.pallas_call(
        functools.partial(_krope_t_kernel, roped=roped),
        grid=(B, per),
        in_specs=in_specs,
        out_specs=pl.BlockSpec((1, R, tt), lambda b, i: (b, 0, i)),
        out_shape=jax.ShapeDtypeStruct((B, R, Lx), BF16),
        compiler_params=_params("parallel", "parallel"),
    )(*args)


def _kv_proj_kernel(c_ref, wn_ref, wv_ref, kr_ref, kt_ref, v_ref):
    c = c_ref[0]
    kn_t = lax.dot_general(wn_ref[0], c, (((1,), (1,)), ((), ())), preferred_element_type=F32)
    rope_end = MLA_NOPE_DIM + MLA_ROPE_DIM
    kt_ref[0, 0, :MLA_NOPE_DIM] = kn_t.astype(kt_ref.dtype)
    kt_ref[0, 0, MLA_NOPE_DIM:rope_end] = kr_ref[0]
    kt_ref[0, 0, rope_end:] = jnp.zeros((QK_PAD - rope_end, c.shape[0]), kt_ref.dtype)
    v_ref[0, 0] = jnp.dot(c, wv_ref[0], preferred_element_type=F32).astype(v_ref.dtype)


def kv_proj(ckvn, wn_t, wv, kr_t, *, tk=2176):
    B, K, R = ckvn.shape
    H, V = wv.shape[0], wv.shape[2]
    return pl.pallas_call(
        _kv_proj_kernel,
        grid=(B, H, K // tk),
        in_specs=[pl.BlockSpec((1, tk, R), lambda b, h, i: (b, i, 0)),
                  pl.BlockSpec((1, MLA_NOPE_DIM, R), lambda b, h, i: (h, 0, 0)),
                  pl.BlockSpec((1, R, V), lambda b, h, i: (h, 0, 0)),
                  pl.BlockSpec((1, MLA_ROPE_DIM, tk), lambda b, h, i: (b, 0, i))],
        out_specs=[pl.BlockSpec((1, 1, QK_PAD, tk), lambda b, h, i: (b, h, 0, i)),
                   pl.BlockSpec((1, 1, tk, V), lambda b, h, i: (b, h, i, 0))],
        out_shape=[jax.ShapeDtypeStruct((B, H, QK_PAD, K), BF16), jax.ShapeDtypeStruct((B, H, K, V), BF16)],
        compiler_params=_params("parallel", "parallel", "parallel"),
    )(ckvn, wn_t, wv, kr_t)


def _mla_layer(h, h_c, B, L, Lc, w_in, g_q, g_kv, w_uq, w_ukv):
    H = MLA_HEADS
    lat = MLA_Q_RANK + MLA_KV_RANK
    w_in_b = w_in.astype(BF16)
    proj = matmul(h, w_in_b[:, :lat], tn=lat)
    ckv_c = matmul(h_c, w_in_b[:, MLA_Q_RANK:lat], tn=MLA_KV_RANK)
    ckv = jnp.concatenate([proj[:, MLA_Q_RANK:].reshape(B, L, -1), ckv_c.reshape(B, Lc, -1)], axis=1)
    cos, sin = _rope_tables(L // GRID_W)
    ones, zeros = jnp.ones((L, MLA_NOPE_DIM), F32), jnp.zeros((L, MLA_NOPE_DIM), F32)
    pad = jnp.zeros((L, QK_PAD - MLA_NOPE_DIM - MLA_ROPE_DIM), F32)
    cosq = jnp.concatenate([ones, cos, cos, pad], axis=1)
    sinq = jnp.concatenate([zeros, sin, sin, pad], axis=1)
    qp = q_proj(_small_rmsnorm(proj[:, :MLA_Q_RANK], g_q).astype(BF16), _q_weights(w_uq), cosq, sinq, B, L)
    w_kr_t = w_in_b[:, lat:].T
    kr_t = jnp.concatenate([krope_t(h, w_kr_t, B, L, 1024, cos.T, sin.T), krope_t(h_c, w_kr_t, B, Lc, Lc)], axis=2)
    w_kv = w_ukv.astype(BF16).reshape(MLA_KV_RANK, H, MLA_NOPE_DIM + MLA_V_DIM)
    wn_t = w_kv[..., :MLA_NOPE_DIM].transpose(1, 2, 0)
    wv = w_kv[..., MLA_NOPE_DIM:].transpose(1, 0, 2)
    kt, v = kv_proj(_small_rmsnorm(ckv, g_kv).astype(BF16), wn_t, wv, kr_t)
    scale = (MLA_NOPE_DIM + MLA_ROPE_DIM) ** -0.5
    return attention(qp, kt, v, scale=scale)


def _fft_tables(L):
    n = 2 * L
    n2_in = L // FFT_N1
    k2 = np.arange(FFT_K2)
    th = 2 * np.pi * ((k2[:, None] * np.arange(n2_in)[None, :]) % FFT_N2) / FFT_N2
    f1 = np.zeros((2 * FFT_K2P, n2_in))
    f1[:FFT_K2] = np.cos(th)
    f1[FFT_K2P:FFT_K2P + FFT_K2] = -np.sin(th)
    wgt = np.where((k2 == 0) | (k2 == FFT_N2 // 2), 1.0, 2.0)
    g = np.zeros((n2_in, 2 * FFT_K2P))
    g[:, :FFT_K2] = (wgt[:, None] * np.cos(th)).T / n
    g[:, FFT_K2P:FFT_K2P + FFT_K2] = -(wgt[:, None] * np.sin(th)).T / n
    k1 = np.arange(FFT_N1)[:, None]
    n1 = np.arange(FFT_N1)[None, :]
    m2 = np.zeros((-(-FFT_K2 // FFT_KSTEP) * FFT_KSTEP, 2 * FFT_N1, 2 * FFT_N1))
    for kk in range(FFT_K2):
        phi = 2 * np.pi * ((n1 * k1 * FFT_N2 + n1 * kk) % n) / n
        tr, ti = np.cos(phi), -np.sin(phi)
        m2[kk] = np.block([[tr, -ti], [ti, tr]])
    as_bf16 = lambda a: jnp.asarray(a, dtype=F32).astype(BF16)
    return (as_bf16(np.kron(f1, np.eye(FFT_SUB))), as_bf16(np.kron(g, np.eye(FFT_GROUP))), as_bf16(m2),
            as_bf16(np.transpose(m2, (0, 2, 1))))


def _fft_s1_kernel(f_ref, z_ref, o_ref):
    _, n2_in, g, ct = z_ref.shape
    z = z_ref[0].astype(F32)
    parts = []
    for s0 in range(0, g, FFT_SUB):
        zz = z[:, s0:s0 + FFT_SUB, :].reshape(n2_in * FFT_SUB, ct).astype(BF16)
        a = jnp.dot(f_ref[...], zz, preferred_element_type=F32)
        parts.append(a.reshape(2, FFT_K2P, FFT_SUB, ct))
    o_ref[0] = jnp.concatenate(parts, axis=2).astype(o_ref.dtype)


def fft_stage1(z4, f1k, *, lead_off, nb, ct=2048):
    _, n2_in, _, C = z4.shape
    g = FFT_GROUP
    return pl.pallas_call(
        _fft_s1_kernel,
        grid=(nb, FFT_N1 // g, C // ct),
        in_specs=[pl.BlockSpec(f1k.shape, lambda b, i, j: (0, 0)),
                  pl.BlockSpec((1, n2_in, g, ct), lambda b, i, j: (lead_off + b, 0, i, j))],
        out_specs=pl.BlockSpec((1, 2, FFT_K2P, g, ct), lambda b, i, j: (b, 0, 0, i, j)),
        out_shape=jax.ShapeDtypeStruct((nb, 2, FFT_K2P, FFT_N1, C), BF16),
        compiler_params=_params("parallel", "parallel", "parallel"),
    )(f1k, z4)


def _fft_s2_kernel(a_ref, m_ref, mt_ref, kf_ref, o_ref):
    step = pl.program_id(0)
    ct = a_ref.shape[-1]
    for q in range(FFT_KSTEP):
        k2 = step * FFT_KSTEP + q

        @pl.when(k2 < FFT_K2)
        def _(q=q):
            x = jnp.dot(m_ref[q], a_ref[0, :, q].reshape(2 * FFT_N1, ct), preferred_element_type=F32)
            xr, xi = x[:FFT_N1], x[FFT_N1:]
            kr, ki = kf_ref[0, 0, q].astype(F32), kf_ref[0, 1, q].astype(F32)
            p = jnp.concatenate([xr * kr - xi * ki, xr * ki + xi * kr], axis=0).astype(BF16)
            e = jnp.dot(mt_ref[q], p, preferred_element_type=F32)
            o_ref[0, :, q] = e.astype(o_ref.dtype).reshape(2, FFT_N1, ct)

        @pl.when(k2 >= FFT_K2)
        def _(q=q):
            o_ref[0, :, q] = jnp.zeros((2, FFT_N1, ct), o_ref.dtype)


def fft_stage2(a5, m2, m2t, kf, order):
    B, _, _, _, D = a5.shape
    last = (FFT_K2 - 1) // FFT_KSTEP
    blk = (1, 2, FFT_KSTEP, FFT_N1, D)
    tbl = pl.BlockSpec((FFT_KSTEP, 2 * FFT_N1, 2 * FFT_N1), lambda k, b: (jnp.minimum(k, last), 0, 0))
    return pl.pallas_call(
        _fft_s2_kernel,
        grid=(FFT_K2P // FFT_KSTEP, B),
        in_specs=[pl.BlockSpec(blk, lambda k, b: (b, 0, jnp.minimum(k, last), 0, 0)),
                  tbl,
                  tbl,
                  pl.BlockSpec(blk, lambda k, b: (order, 0, jnp.minimum(k, last), 0, 0))],
        out_specs=pl.BlockSpec(blk, lambda k, b: (b, 0, k, 0, 0)),
        out_shape=jax.ShapeDtypeStruct(a5.shape, BF16),
        compiler_params=_params("parallel", "arbitrary"),
    )(a5, m2, m2t, kf)


def _fft_f2_kernel(f_ref, g_ref, m_ref, inv_ref, o_ref):
    step = pl.program_id(0)
    ct = f_ref.shape[-1]
    inv = inv_ref[0]
    for q in range(FFT_KSTEP):
        k2 = step * FFT_KSTEP + q

        @pl.when(k2 < FFT_K2)
        def _(q=q):
            xf = jnp.dot(m_ref[q], f_ref[:, q].reshape(2 * FFT_N1, ct), preferred_element_type=F32)
            xg = jnp.dot(m_ref[q], g_ref[:, q].reshape(2 * FFT_N1, ct), preferred_element_type=F32)
            o_ref[0, 0, q] = ((xf[:FFT_N1] + xg[:FFT_N1]) * inv).astype(o_ref.dtype)
            o_ref[0, 1, q] = ((xf[FFT_N1:] - xg[FFT_N1:]) * inv).astype(o_ref.dtype)

        @pl.when(k2 >= FFT_K2)
        def _(q=q):
            o_ref[0, :, q] = jnp.zeros((2, FFT_N1, ct), o_ref.dtype)


def fft_filter_spectrum(a4, m2, inv, D):
    last = (FFT_K2 - 1) // FFT_KSTEP
    blk = (2, FFT_KSTEP, FFT_N1, D)
    return pl.pallas_call(
        _fft_f2_kernel,
        grid=(FFT_K2P // FFT_KSTEP, HYENA_ORDER),
        in_specs=[pl.BlockSpec(blk, lambda k, o: (0, jnp.minimum(k, last), 0, 2 * o)),
                  pl.BlockSpec(blk, lambda k, o: (0, jnp.minimum(k, last), 0, 2 * o + 1)),
                  pl.BlockSpec((FFT_KSTEP, 2 * FFT_N1, 2 * FFT_N1), lambda k, o: (jnp.minimum(k, last), 0, 0)),
                  pl.BlockSpec((1, 1, D), lambda k, o: (o, 0, 0))],
        out_specs=pl.BlockSpec((1, 2, FFT_KSTEP, FFT_N1, D), lambda k, o: (o, 0, k, 0, 0)),
        out_shape=jax.ShapeDtypeStruct((HYENA_ORDER, 2, FFT_K2P, FFT_N1, D), BF16),
        compiler_params=_params("arbitrary", "parallel"),
    )(a4, a4, m2, inv)


def _fft_s3_kernel(g_ref, e_ref, z_ref, x_ref, skip_ref, o_ref):
    _, n2_in, g, ct = z_ref.shape
    e = e_ref[0].reshape(2 * FFT_K2P * g, ct)
    y = jnp.dot(g_ref[...], e, preferred_element_type=F32).reshape(n2_in, g, ct)
    o_ref[0] = (x_ref[0] * (y + skip_ref[...] * z_ref[0])).astype(o_ref.dtype)


def fft_stage3(e5, gk, z4, z_off, x4, x_off, skip, *, out_dtype):
    B, _, _, _, D = e5.shape
    n2_in = z4.shape[1]
    g = FFT_GROUP
    return pl.pallas_call(
        _fft_s3_kernel,
        grid=(B, FFT_N1 // g),
        in_specs=[pl.BlockSpec(gk.shape, lambda b, i: (0, 0)),
                  pl.BlockSpec((1, 2, FFT_K2P, g, D), lambda b, i: (b, 0, 0, i, 0)),
                  pl.BlockSpec((1, n2_in, g, D), lambda b, i: (z_off + b, 0, i, 0)),
                  pl.BlockSpec((1, n2_in, g, D), lambda b, i: (x_off + b, 0, i, 0)),
                  pl.BlockSpec((1, D), lambda b, i: (0, 0))],
        out_specs=pl.BlockSpec((1, n2_in, g, D), lambda b, i: (b, 0, i, 0)),
        out_shape=jax.ShapeDtypeStruct((B, n2_in, FFT_N1, D), out_dtype),
        compiler_params=_params("parallel", "parallel"),
    )(gk, e5, z4, x4, skip.reshape(1, D))


def _hyin_kernel(ap_ref, a_ref, an_ref, b_ref, cw_ref, cb_ref, o_ref, *, tiles_per_seq):
    i = pl.program_id(0)
    t = i % tiles_per_seq
    halo = ap_ref.shape[0]
    tm, tn = o_ref.shape[1:]
    cw = MXU_DIM
    row = lax.broadcasted_iota(jnp.int32, (tm, cw), 0)
    at_start = jnp.logical_and(row == 0, t == 0)
    at_end = jnp.logical_and(row == tm - 1, t == tiles_per_seq - 1)
    a_ext = jnp.concatenate([ap_ref[...], a_ref[...], an_ref[...]], axis=0)
    for c0 in range(0, tn, cw):
        acc = jnp.dot(a_ext, b_ref[:, c0:c0 + cw], preferred_element_type=F32)
        up = jnp.where(at_start, 0.0, acc[halo - 1:halo - 1 + tm])
        dn = jnp.where(at_end, 0.0, acc[halo + 1:halo + 1 + tm])
        w = cw_ref[:, c0:c0 + cw]
        o_ref[0, :, c0:c0 + cw] = up * w[0:1] + acc[halo:halo + tm] * w[1:2] + dn * w[2:3] + cb_ref[:, c0:c0 + cw]


def hyena_in(h, w_in, conv_w, conv_b, L, *, tm=1024, tn=1024, halo=16):
    M, K = h.shape
    N = w_in.shape[1]
    D = N // 3
    per = D // tn
    return pl.pallas_call(
        functools.partial(_hyin_kernel, tiles_per_seq=L // tm),
        grid=(M // tm, N // tn),
        in_specs=[pl.BlockSpec((halo, K), lambda i, j: (jnp.maximum(i * (tm // halo) - 1, 0), 0)),
                  pl.BlockSpec((tm, K), lambda i, j: (i, 0)),
                  pl.BlockSpec((halo, K), lambda i, j: (jnp.minimum((i + 1) * (tm // halo), M // halo - 1), 0)),
                  pl.BlockSpec((K, tn), lambda i, j: (0, j)),
                  pl.BlockSpec((3, tn), lambda i, j: (0, j)),
                  pl.BlockSpec((1, tn), lambda i, j: (0, j))],
        out_specs=pl.BlockSpec((1, tm, tn), lambda i, j: (j // per, i, j % per)),
        out_shape=jax.ShapeDtypeStruct((3, M, D), F32),
        compiler_params=_params("parallel", "parallel"),
    )(h, h, h, w_in, conv_w, conv_b.reshape(1, N))


def _taps_kernel(a_ref, w_ref, b_ref, dl_ref, keep_ref, o_ref, ss_ref, *, seq_len):
    i = pl.program_id(1)
    tm = a_ref.shape[0]
    acc = jnp.dot(a_ref[...].astype(BF16), w_ref[...].astype(BF16), preferred_element_type=F32) + b_ref[...]
    row = i * tm + lax.broadcasted_iota(jnp.int32, acc.shape, 0)
    t = row.astype(F32) / seq_len
    hf = acc * jnp.exp(-t * dl_ref[...])
    hf = jnp.where(row == 0, hf * keep_ref[...], hf)
    o_ref[...] = hf.astype(o_ref.dtype)
    part = jnp.sum(hf * hf, axis=0, keepdims=True)

    @pl.when(i == 0)
    def _():
        ss_ref[...] = part

    @pl.when(i > 0)
    def _():
        ss_ref[...] += part


def _hyena_filter_taps(L, w1, b1, w2, b2, w3, b3, freq, *, tm=512, tn=1024):
    D = D_MODEL
    N = 2 * HYENA_ORDER * D
    t = jnp.arange(L, dtype=F32) / L
    w = 2 * math.pi * jnp.arange(L, dtype=F32) / L
    bands = jnp.linspace(1e-4, HYENA_BANDS - 1, HYENA_BANDS, dtype=F32)
    ang = w[:, None] * bands[None, :]
    z = jnp.concatenate([t[:, None], jnp.cos(ang), -jnp.sin(ang)], axis=-1)
    a = jnp.sin(freq * (z @ w1 + b1))
    a = jnp.sin(freq * (a @ w2 + b2))
    max_decay = math.log(HYENA_TARGET) / HYENA_FAST_PCT
    min_decay = math.log(HYENA_TARGET) / HYENA_SLOW_PCT
    decay = jnp.tile(jnp.abs(jnp.linspace(min_decay, max_decay, D, dtype=F32)), 2 * HYENA_ORDER).reshape(1, N)
    keep0 = jnp.where((jnp.arange(N) // D) % 2 == 1, 0.0, 1.0).astype(F32).reshape(1, N)
    K = a.shape[1]
    taps, ss = pl.pallas_call(
        functools.partial(_taps_kernel, seq_len=L),
        grid=(N // tn, L // tm),
        in_specs=[pl.BlockSpec((tm, K), lambda j, i: (i, 0)),
                  pl.BlockSpec((K, tn), lambda j, i: (0, j)),
                  pl.BlockSpec((1, tn), lambda j, i: (0, j)),
                  pl.BlockSpec((1, tn), lambda j, i: (0, j)),
                  pl.BlockSpec((1, tn), lambda j, i: (0, j))],
        out_specs=[pl.BlockSpec((tm, tn), lambda j, i: (i, j)), pl.BlockSpec((1, tn), lambda j, i: (0, j))],
        out_shape=[jax.ShapeDtypeStruct((L, N), BF16), jax.ShapeDtypeStruct((1, N), F32)],
        compiler_params=_params("parallel", "arbitrary"),
    )(a, w3, b3.reshape(1, N), decay, keep0)
    ss = ss.reshape(HYENA_ORDER, 2, D).sum(axis=1, keepdims=True)
    return taps, lax.rsqrt(ss + NORM_EPS)


def _hyena_layer(h, B, L, w_in, conv_w, conv_b, f_w1, f_b1, f_w2, f_b2, f_w3, f_b3, f_freq, skip):
    D = D_MODEL
    n2_in = L // FFT_N1
    f1k, gk, m2, m2t = _fft_tables(L)
    taps, inv = _hyena_filter_taps(L, f_w1, f_b1, f_w2, f_b2, f_w3, f_b3, f_freq)
    af = fft_stage1(taps.reshape(1, n2_in, FFT_N1, taps.shape[1]), f1k, lead_off=0, nb=1)
    kf = fft_filter_spectrum(af[0], m2, inv, D)
    u4 = hyena_in(h, w_in.astype(BF16), conv_w, conv_b, L).reshape(3 * B, n2_in, FFT_N1, D)
    z4 = u4
    for order in range(HYENA_ORDER):
        a = fft_stage1(z4, f1k, lead_off=0, nb=B)
        e = fft_stage2(a, m2, m2t, kf, order)
        last = order == HYENA_ORDER - 1
        z4 = fft_stage3(e, gk, z4, 0, u4, (order + 1) * B, skip[order], out_dtype=BF16 if last else F32)
    return z4.reshape(B * L, D)


def _moe_layer(x2, norm_gain, mod, B, T, w_router, w_gate, w_up, w_down, layer, next_g, next_shift=None,
               next_scale=None, *, tj=256, win=64):
    E = N_EXPERTS
    cap = EC_CAPACITY_FACTOR * T // E
    hp, aff = norm_route(x2, norm_gain, mod[:, 3:4], mod[:, 4:5], w_router.T.astype(BF16), rows_per_batch=T)
    pos, idx, g = route_select(aff, cap)
    idx3 = idx.reshape(B, E, cap)
    y = moe_ffn(idx3, hp, g, w_gate, w_up, w_down, layer, T)
    nj = T // tj
    edges = jnp.arange(nj + 1, dtype=jnp.int32) * tj
    first = jnp.sum((idx3[..., None] < edges).astype(jnp.int32), axis=2)
    nominal = (first[..., :nj] // 16) * 16
    ntr = jnp.maximum(jnp.max((first[..., 1:] - nominal + win - 1) // win, axis=1), 1)
    starts = nominal.transpose(0, 2, 1).reshape(-1)
    return moe_combine(starts, ntr.reshape(-1), y, pos.transpose(0, 2, 1), x2, mod[:, 5:6], next_g, next_shift,
                       next_scale, tj=tj, win=win)


def kernel(x, c, ctx, c_ctx, ada_w, ada_b, norm_g, final_g, mla_w_in, mla_g_q, mla_g_kv, mla_w_uq, mla_w_ukv, mla_w_o, hy_w_in, hy_conv_w, hy_conv_b, hy_f_w1, hy_f_b1, hy_f_w2, hy_f_b2, hy_f_w3, hy_f_b3, hy_f_freq, hy_skip, hy_w_out, moe_w_router, moe_w_gate, moe_w_up, moe_w_down):
    B, L, D = x.shape
    Lc = ctx.shape[1]
    x2 = x.reshape(B * L, D)
    ctx2 = ctx.reshape(B * Lc, D)

    mod3 = _ada(jnp.concatenate([c, c_ctx[None]], axis=0), ada_w, ada_b, 0)
    mod, mod_c = mod3[:B], jnp.broadcast_to(mod3[B:], (B, 6, D))
    h = norm_mod(x2, norm_g[0, 0], mod[:, 0:1], mod[:, 1:2], rows_per_batch=L)
    h_c = norm_mod(ctx2, norm_g[0, 0], mod_c[:, 0:1], mod_c[:, 1:2], rows_per_batch=Lc)
    o = _mla_layer(h, h_c, B, L, Lc, mla_w_in[0], mla_g_q[0], mla_g_kv[0], mla_w_uq[0], mla_w_ukv[0])
    x2 = matmul(o.reshape(B * L, -1), mla_w_o[0].astype(BF16), tn=D,
                res=x2, gate=mod[:, 2:3], rows_per_batch=L)
    mod1 = _ada(c, ada_w, ada_b, 1)
    x2, h = _moe_layer(x2, norm_g[0, 1], mod, B, L, moe_w_router[0], moe_w_gate, moe_w_up, moe_w_down, 0,
                       norm_g[1, 0], mod1[:, 0:1], mod1[:, 1:2])

    mod = mod1
    z = _hyena_layer(h, B, L, hy_w_in[0], hy_conv_w[0], hy_conv_b[0], hy_f_w1[0], hy_f_b1[0], hy_f_w2[0],
                     hy_f_b2[0], hy_f_w3[0], hy_f_b3[0], hy_f_freq[0], hy_skip[0])
    x2 = matmul(z, hy_w_out[0].astype(BF16), tn=D, res=x2, gate=mod[:, 2:3], rows_per_batch=L)
    out = _moe_layer(x2, norm_g[1, 1], mod, B, L, moe_w_router[1], moe_w_gate, moe_w_up, moe_w_down, 1, final_g)
    return out.reshape(B, L, D)
```

```python
import functools
import math

import jax
import jax.numpy as jnp
import numpy as np
from jax import lax
from jax.experimental import pallas as pl
from jax.experimental.pallas import tpu as pltpu

F32 = jnp.float32
BF16 = jnp.bfloat16

D_MODEL = 2048
GRID_W = 64
NORM_EPS = 1e-6
MLA_HEADS = 16
MLA_Q_RANK = 512
MLA_KV_RANK = 256
MLA_NOPE_DIM = 128
MLA_ROPE_DIM = 64
MLA_V_DIM = 128
ROPE_THETA = 10000.0
HYENA_ORDER = 2
HYENA_BANDS = 16
HYENA_TARGET = 1e-2
HYENA_FAST_PCT = 0.3
HYENA_SLOW_PCT = 1.5
N_EXPERTS = 16
EC_CAPACITY_FACTOR = 2
FFT_N1 = 128
FFT_N2 = 64
FFT_K2 = FFT_N2 // 2 + 1
FFT_K2P = 40
FFT_GROUP = 16
FFT_SUB = 8
FFT_KSTEP = 4

V7X_VMEM_LIMIT_BYTES = 56 * 1024 * 1024
LANES = 128
MXU_DIM = 256
QK_PAD = MXU_DIM


def _params(*sem):
    return pltpu.CompilerParams(dimension_semantics=sem, vmem_limit_bytes=V7X_VMEM_LIMIT_BYTES)


def _mm_kernel(a_ref, b_ref, *rest, b_is_stacked, has_res):
    o_ref = rest[-1]
    b = b_ref[0] if b_is_stacked else b_ref[...]
    acc = jnp.dot(a_ref[...].astype(BF16), b.astype(BF16), preferred_element_type=F32)
    if has_res:
        res_ref, gate_ref = rest[0], rest[1]
        acc = res_ref[...] + gate_ref[0] * acc
    o_ref[...] = acc.astype(o_ref.dtype)


def matmul(a, b, *, b_index=None, out_dtype=F32, tm=512, tn=512, res=None, gate=None, rows_per_batch=None):
    M, K = a.shape
    N = b.shape[-1]
    tm, tn = min(tm, M), min(tn, N)
    assert M % tm == 0 and N % tn == 0, (M, N, tm, tn)
    stacked = b.ndim == 3
    if stacked:
        b_spec = pl.BlockSpec((1, K, tn), lambda i, j: (b_index, 0, j))
    else:
        b_spec = pl.BlockSpec((K, tn), lambda i, j: (0, j))
    in_specs = [pl.BlockSpec((tm, K), lambda i, j: (i, 0)), b_spec]
    args = [a, b]
    if res is not None:
        assert rows_per_batch % tm == 0
        tpb = rows_per_batch // tm
        in_specs += [pl.BlockSpec((tm, tn), lambda i, j: (i, j)),
                     pl.BlockSpec((1, 1, tn), lambda i, j: (i // tpb, 0, j))]
        args += [res, gate]
    return pl.pallas_call(
        functools.partial(_mm_kernel, b_is_stacked=stacked, has_res=res is not None),
        grid=(M // tm, N // tn),
        in_specs=in_specs,
        out_specs=pl.BlockSpec((tm, tn), lambda i, j: (i, j)),
        out_shape=jax.ShapeDtypeStruct((M, N), out_dtype),
        compiler_params=_params("parallel", "parallel"),
    )(*args)


def _norm_kernel(x_ref, g_ref, *rest, modulated):
    o_ref = rest[-1]
    x = x_ref[...]
    y = x * lax.rsqrt(jnp.mean(x * x, axis=-1, keepdims=True) + NORM_EPS) * g_ref[...]
    if modulated:
        shift_ref, scale_ref = rest[0], rest[1]
        y = y * (1 + scale_ref[0]) + shift_ref[0]
    o_ref[...] = y.astype(o_ref.dtype)


def norm_mod(x2, g, shift=None, scale=None, *, rows_per_batch=None, out_dtype=BF16, tm=256):
    M, D = x2.shape
    tm = min(tm, M)
    in_specs = [pl.BlockSpec((tm, D), lambda i: (i, 0)), pl.BlockSpec((1, D), lambda i: (0, 0))]
    args = [x2, g.reshape(1, D)]
    if shift is not None:
        tpb = rows_per_batch // tm
        in_specs += [pl.BlockSpec((1, 1, D), lambda i: (i // tpb, 0, 0))] * 2
        args += [shift, scale]
    return pl.pallas_call(
        functools.partial(_norm_kernel, modulated=shift is not None),
        grid=(M // tm,),
        in_specs=in_specs,
        out_specs=pl.BlockSpec((tm, D), lambda i: (i, 0)),
        out_shape=jax.ShapeDtypeStruct((M, D), out_dtype),
        compiler_params=_params("parallel"),
    )(*args)


def _attn_kernel(q_ref, kt_ref, v_ref, o_ref, *, chunks, c):
    q = q_ref[0, 0]
    m = l = acc = None
    for lo, hi in chunks:
        s = jnp.dot(q, kt_ref[0, 0, :, lo:hi], preferred_element_type=F32)
        m_chunk = jnp.max(s, axis=-1, keepdims=True)
        m_new = m_chunk if m is None else jnp.maximum(m, m_chunk)
        p = jnp.exp2(s * c - m_new * c)
        pv = jnp.dot(p.astype(BF16), v_ref[0, 0, lo:hi, :], preferred_element_type=F32)
        psum = jnp.sum(p, axis=-1, keepdims=True)
        if m is None:
            l, acc = psum, pv
        else:
            alpha = jnp.exp2((m - m_new) * c)
            l = alpha * l + psum
            acc = alpha * acc + pv
        m = m_new
    o_ref[0] = (acc / l).astype(o_ref.dtype)


def attention(q, kt, v, *, scale, tq=2048, tk=1024):
    B, H, L, _ = q.shape
    K = kt.shape[-1]
    V = v.shape[-1]
    chunks = tuple((lo, min(lo + tk, K)) for lo in range(0, K, tk))
    return pl.pallas_call(
        functools.partial(_attn_kernel, chunks=chunks, c=scale * math.log2(math.e)),
        grid=(B, H, L // tq),
        in_specs=[pl.BlockSpec((1, 1, tq, QK_PAD), lambda b, h, i: (b, h, i, 0)),
                  pl.BlockSpec((1, 1, QK_PAD, K), lambda b, h, i: (b, h, 0, 0)),
                  pl.BlockSpec((1, 1, K, V), lambda b, h, i: (b, h, 0, 0))],
        out_specs=pl.BlockSpec((1, tq, V), lambda b, h, i: (b, i, h)),
        out_shape=jax.ShapeDtypeStruct((B, L, H * V), BF16),
        compiler_params=_params("parallel", "parallel", "arbitrary"),
    )(q, kt, v)


def _norm_route_kernel(x_ref, g_ref, shift_ref, scale_ref, wr_ref, h_ref, aff_ref):
    x = x_ref[...]
    y = x * lax.rsqrt(jnp.mean(x * x, axis=-1, keepdims=True) + NORM_EPS) * g_ref[...]
    y = y * (1 + scale_ref[0]) + shift_ref[0]
    h_ref[...] = y
    logits = lax.dot_general(wr_ref[...], y.astype(BF16), (((1,), (1,)), ((), ())),
                             preferred_element_type=F32)
    p = jnp.exp(logits - jnp.max(logits, axis=0, keepdims=True))
    aff_ref[0] = p / jnp.sum(p, axis=0, keepdims=True)


def norm_route(x2, g, shift, scale, w_router_t, *, rows_per_batch, tm=256):
    M, D = x2.shape
    E = w_router_t.shape[0]
    tpb = rows_per_batch // tm
    return pl.pallas_call(
        _norm_route_kernel,
        grid=(M // tm,),
        in_specs=[pl.BlockSpec((tm, D), lambda i: (i, 0)),
                  pl.BlockSpec((1, D), lambda i: (0, 0)),
                  pl.BlockSpec((1, 1, D), lambda i: (i // tpb, 0, 0)),
                  pl.BlockSpec((1, 1, D), lambda i: (i // tpb, 0, 0)),
                  pl.BlockSpec((E, D), lambda i: (0, 0))],
        out_specs=[pl.BlockSpec((tm, D), lambda i: (i, 0)),
                   pl.BlockSpec((1, E, tm), lambda i: (i // tpb, 0, i % tpb))],
        out_shape=[jax.ShapeDtypeStruct((M, D), F32),
                   jax.ShapeDtypeStruct((M // rows_per_batch, E, rows_per_batch), F32)],
        compiler_params=_params("parallel"),
    )(x2, g.reshape(1, D), shift, scale, w_router_t)


def _lane_prefix_sum(x):
    E, T = x.shape
    r = lax.broadcasted_iota(jnp.int32, (LANES, LANES), 0)
    c = lax.broadcasted_iota(jnp.int32, (LANES, LANES), 1)
    tri = jnp.where(r <= c, 1.0, 0.0).astype(BF16)
    outs, carry = [], jnp.zeros((E, 1), F32)
    for j in range(T // LANES):
        seg = jnp.dot(x[:, j * LANES:(j + 1) * LANES].astype(BF16), tri, preferred_element_type=F32) + carry
        outs.append(seg)
        carry = seg[:, LANES - 1:LANES]
    return jnp.concatenate(outs, axis=1)


def _route_select_kernel(aff_ref, pos_ref, idx_ref, gate_ref, cum_s, vals_s, *, cap, tch):
    aff = aff_ref[0]
    E, T = aff.shape

    def bisect(state):
        lo, hi, _ = state
        mid = 0.5 * (lo + hi)
        cnt = jnp.sum(jnp.where(aff >= mid, 1.0, 0.0), axis=1, keepdims=True)
        ge = cnt >= cap
        open_rows = jnp.where((mid > lo) & (mid < hi), 1.0, 0.0)
        return jnp.where(ge, mid, lo), jnp.where(ge, hi, mid), jnp.max(open_rows)

    lo0 = jnp.min(aff, axis=1, keepdims=True)
    hi0 = 2.0 * jnp.max(aff, axis=1, keepdims=True) + 1e-30
    lo, hi, _ = lax.while_loop(lambda st: st[2] > 0.0, bisect, (lo0, hi0, jnp.float32(1.0)))
    gt = aff >= hi
    eq = (aff >= lo) & (aff < hi)
    need = cap - jnp.sum(jnp.where(gt, 1.0, 0.0), axis=1, keepdims=True)
    eq_rank = _lane_prefix_sum(jnp.where(eq, 1.0, 0.0))
    sel = gt | (eq & (eq_rank <= need))
    cum = _lane_prefix_sum(jnp.where(sel, 1.0, 0.0))
    cumsel = jnp.where(sel, cum, 0.0)
    pos_ref[0] = (cumsel - 1.0).astype(jnp.int32)
    cum_s[0] = cum
    cum_s[1] = cumsel
    slot = lax.broadcasted_iota(jnp.int32, (cap, 1), 0).astype(F32)
    tok = lax.broadcasted_iota(jnp.int32, (1, T), 1)
    tok_hi = jnp.right_shift(tok, 6).astype(F32)
    tok_lo = jnp.bitwise_and(tok, 63).astype(F32)
    vals_s[...] = jnp.zeros(vals_s.shape, F32)

    def per_expert(e, carry):
        a = aff_ref[0, pl.ds(e, 1), :]
        a1 = a.astype(BF16).astype(F32)
        a2 = (a - a1).astype(BF16).astype(F32)
        a3 = ((a - a1) - a2).astype(BF16).astype(F32)
        vals_s[0:8, :] = jnp.concatenate([tok_hi, tok_lo, a1, a2, a3, jnp.zeros((3, T), F32)], axis=0)
        acc = jnp.zeros((cap, LANES), F32)
        for lo in range(0, T, tch):
            cs_e = cum_s[1, pl.ds(e, 1), lo:lo + tch]
            hit = jnp.where(cs_e == slot + 1.0, 1.0, 0.0).astype(BF16)
            acc = acc + lax.dot_general(hit, vals_s[:, lo:lo + tch].astype(BF16), (((1,), (1,)), ((), ())),
                                        preferred_element_type=F32)
        idx_ref[0, pl.ds(e, 1)] = (acc[:, 0:1] * 64.0 + acc[:, 1:2]).astype(jnp.int32).reshape(1, cap, 1)
        gate_ref[0, pl.ds(e, 1)] = ((acc[:, 2:3] + acc[:, 3:4]) + acc[:, 4:5]).reshape(1, cap, 1)
        return carry

    lax.fori_loop(0, E, per_expert, 0)


def route_select(aff, cap, *, tch=1024):
    B, E, T = aff.shape
    return pl.pallas_call(
        functools.partial(_route_select_kernel, cap=cap, tch=tch),
        grid=(B,),
        in_specs=[pl.BlockSpec((1, E, T), lambda b: (b, 0, 0))],
        out_specs=[pl.BlockSpec((1, E, T), lambda b: (b, 0, 0)),
                   pl.BlockSpec((1, E, cap, 1), lambda b: (b, 0, 0, 0)),
                   pl.BlockSpec((1, E, cap, 1), lambda b: (b, 0, 0, 0))],
        out_shape=[jax.ShapeDtypeStruct((B, E, T), jnp.int32),
                   jax.ShapeDtypeStruct((B, E, cap, 1), jnp.int32),
                   jax.ShapeDtypeStruct((B, E, cap, 1), F32)],
        scratch_shapes=[pltpu.VMEM((2, E, T), F32), pltpu.VMEM((LANES, T), F32)],
        compiler_params=_params("parallel"),
    )(aff)


def _moe_ffn_kernel(idx_ref, idxn_ref, h_hbm, wg_ref, wu_ref, wd_ref, g_ref, o_ref, xland, xbf, h1, sem,
                    *, nf, n_steps, n_exp, cap, tokens_per_batch):
    e = pl.program_id(0)
    f = pl.program_id(1)
    rows = xland.shape[0]

    def issue_gather(ids_ref):
        for b in range(rows // cap):
            def body(c, carry, b=b):
                i = b * cap + c
                src_row = ids_ref[0, 0, i] + b * tokens_per_batch
                pltpu.make_async_copy(h_hbm.at[pl.ds(src_row, 1)], xland.at[pl.ds(i, 1)], sem.at[0]).start()
                return carry
            lax.fori_loop(0, cap, body, 0, unroll=8)

    @pl.when((e == 0) & (f == 0))
    def _():
        issue_gather(idx_ref)

    def wait_gather():
        pltpu.make_async_copy(h_hbm.at[pl.ds(0, rows)], xland, sem.at[0]).wait()

    @pl.when(f == 0)
    def _():
        wait_gather()
        xbf[...] = xland[...].astype(BF16)

    share = rows // n_steps

    def issue_share():
        base = f * share
        off = (base // cap) * tokens_per_batch
        for j in range(share):
            pltpu.make_async_copy(h_hbm.at[pl.ds(idxn_ref[0, 0, base + j] + off, 1)],
                                  xland.at[pl.ds(base + j, 1)], sem.at[0]).start()

    @pl.when(f < nf)
    def _():
        x = xbf[...]
        a = jnp.dot(x, wg_ref[0, 0].astype(BF16), preferred_element_type=F32)
        issue_share()
        u = jnp.dot(x, wu_ref[0, 0].astype(BF16), preferred_element_type=F32)
        h1[jnp.minimum(f, nf - 1)] = (jax.nn.silu(a) * u).astype(BF16)

    @pl.when(f >= nf)
    def _():
        hid = jnp.concatenate([h1[k] for k in range(nf)], axis=1)
        issue_share()
        y = jnp.dot(hid, wd_ref[0, 0].astype(BF16), preferred_element_type=F32)
        o_ref[...] = (y.reshape(o_ref.shape) * g_ref[...]).astype(o_ref.dtype)

    @pl.when((e == n_exp - 1) & (f == n_steps - 1))
    def _():
        wait_gather()


def moe_ffn(idx3, hp, g, w_gate, w_up, w_down, layer, T, *, tf=256, tc=512):
    B, E, C = idx3.shape
    D = hp.shape[1]
    F = w_gate.shape[-1]
    nf = F // tf
    tc = min(tc, D)
    n_steps = nf + D // tc
    assert (B * C) % n_steps == 0 and C % ((B * C) // n_steps) == 0, (B, C, n_steps)
    ids = idx3.transpose(1, 0, 2).reshape(E, 1, B * C)
    smem_blk = lambda imap: pl.BlockSpec((1, 1, B * C), imap, memory_space=pltpu.SMEM)
    up_blk = pl.BlockSpec((1, 1, D, tf), lambda e, s: (layer, e, 0, jnp.minimum(s, nf - 1)))
    return pl.pallas_call(
        functools.partial(_moe_ffn_kernel, nf=nf, n_steps=n_steps, n_exp=E, cap=C, tokens_per_batch=T),
        grid=(E, n_steps),
        in_specs=[smem_blk(lambda e, s: (e, 0, 0)),
                  smem_blk(lambda e, s: (jnp.minimum(e + 1, E - 1), 0, 0)),
                  pl.BlockSpec(memory_space=pl.ANY),
                  up_blk,
                  up_blk,
                  pl.BlockSpec((1, 1, F, tc), lambda e, s: (layer, e, 0, jnp.maximum(s - nf, 0))),
                  pl.BlockSpec((B, 1, C, 1), lambda e, s: (0, e, 0, 0))],
        out_specs=pl.BlockSpec((B, 1, C, tc), lambda e, s: (0, e, 0, jnp.maximum(s - nf, 0))),
        out_shape=jax.ShapeDtypeStruct((B, E, C, D), BF16),
        scratch_shapes=[pltpu.VMEM((B * C, D), F32),
                        pltpu.VMEM((B * C, D), BF16),
                        pltpu.VMEM((nf, B * C, tf), BF16),
                        pltpu.SemaphoreType.DMA((1,))],
        compiler_params=_params("arbitrary", "arbitrary"),
    )(ids, ids, hp, w_gate, w_up, w_down, g)


def _moe_combine_kernel(starts_ref, ntr_ref, y_hbm, post_ref, x_ref, gate_ref, ng_ref, *rest,
                        n_exp, cap, win, steps_per_batch, modulated):
    ybuf, yext, acc, sem, sem_ext = rest[-5:]
    out_refs = rest[2:-5] if modulated else rest[:-5]
    s = pl.program_id(0)
    slot = s % 2

    def window(step, e, k):
        nominal = starts_ref[step * n_exp + e] + k * win
        return nominal, pl.multiple_of(jnp.minimum(nominal, cap - win), 16)

    def window_copy(step, e, k, dst, dsem):
        _, actual = window(step, e, k)
        return pltpu.make_async_copy(y_hbm.at[step // steps_per_batch, e, pl.ds(actual, win)],
                                     dst.at[pl.ds(e * win, win)], dsem)

    def onehot(step, k):
        post = post_ref[0]
        per = max(LANES // win, 1)
        lane = lax.broadcasted_iota(jnp.int32, (1, per * win), 1)
        cols = []
        for e0 in range(0, n_exp, per):
            slots = tok = None
            for q in range(per):
                nominal, actual = window(step, e0 + q, k)
                s_q = actual + lane - q * win
                s_q = jnp.where(s_q >= nominal, s_q, -2)
                t_q = post[:, e0 + q:e0 + q + 1]
                mine = lane >= q * win
                slots = s_q if q == 0 else jnp.where(mine, s_q, slots)
                tok = t_q if q == 0 else jnp.where(mine, t_q, tok)
            cols.append(jnp.where(tok == slots, 1.0, 0.0).astype(BF16))
        return jnp.concatenate(cols, axis=1)

    @pl.when(s == 0)
    def _():
        for e in range(n_exp):
            window_copy(0, e, 0, ybuf.at[0], sem.at[0]).start()

    for e in range(n_exp):
        window_copy(s, e, 0, ybuf.at[slot], sem.at[slot]).wait()

    @pl.when(s + 1 < pl.num_programs(0))
    def _():
        for e in range(n_exp):
            window_copy(s + 1, e, 0, ybuf.at[1 - slot], sem.at[1 - slot]).start()

    acc[...] = jnp.dot(onehot(s, 0), ybuf[slot], preferred_element_type=F32)

    def extra(k, carry):
        for e in range(n_exp):
            window_copy(s, e, k, yext, sem_ext.at[0]).start()
        for e in range(n_exp):
            window_copy(s, e, k, yext, sem_ext.at[0]).wait()
        acc[...] += jnp.dot(onehot(s, k), yext[...], preferred_element_type=F32)
        return carry

    lax.fori_loop(1, ntr_ref[s], extra, 0)
    x = x_ref[...] + gate_ref[0] * acc[...]
    y = x * lax.rsqrt(jnp.mean(x * x, axis=-1, keepdims=True) + NORM_EPS) * ng_ref[...]
    if modulated:
        out_refs[0][...] = x
        y = y * (1 + rest[1][0]) + rest[0][0]
    out_refs[-1][...] = y.astype(out_refs[-1].dtype)


def moe_combine(starts, ntr, y, post, x2, gate, norm_g, shift=None, scale=None, *, tj=256, win=64):
    B, E, C, D = y.shape
    M = x2.shape[0]
    T = M // B
    spb = T // tj
    modulated = shift is not None
    row_blk = pl.BlockSpec((tj, D), lambda s, *_: (s, 0))
    per_batch = pl.BlockSpec((1, 1, D), lambda s, *_: (s // spb, 0, 0))
    in_specs = [pl.BlockSpec(memory_space=pl.ANY),
                pl.BlockSpec((1, tj, E), lambda s, *_: (s // spb, s % spb, 0)),
                row_blk, per_batch, pl.BlockSpec((1, D), lambda s, *_: (0, 0))]
    args = [starts, ntr, y, post, x2, gate, norm_g.reshape(1, D)]
    if modulated:
        in_specs += [per_batch, per_batch]
        args += [shift, scale]
        out_specs = [row_blk, row_blk]
        out_shape = [jax.ShapeDtypeStruct((M, D), F32), jax.ShapeDtypeStruct((M, D), BF16)]
    else:
        out_specs = row_blk
        out_shape = jax.ShapeDtypeStruct((M, D), F32)
    grid_spec = pltpu.PrefetchScalarGridSpec(
        num_scalar_prefetch=2,
        grid=(M // tj,),
        in_specs=in_specs,
        out_specs=out_specs,
        scratch_shapes=[pltpu.VMEM((2, E * win, D), BF16),
                        pltpu.VMEM((E * win, D), BF16),
                        pltpu.VMEM((tj, D), F32),
                        pltpu.SemaphoreType.DMA((2,)),
                        pltpu.SemaphoreType.DMA((1,))],
    )
    return pl.pallas_call(
        functools.partial(_moe_combine_kernel, n_exp=E, cap=C, win=win, steps_per_batch=spb, modulated=modulated),
        grid_spec=grid_spec,
        out_shape=out_shape,
        compiler_params=_params("arbitrary"),
    )(*args)


def _small_rmsnorm(x, g):
    return x * lax.rsqrt(jnp.mean(x * x, axis=-1, keepdims=True) + NORM_EPS) * g


def _rope_tables(rows):
    row = jnp.broadcast_to(jnp.arange(rows)[:, None], (rows, GRID_W)).reshape(-1).astype(F32)
    col = jnp.broadcast_to(jnp.arange(GRID_W)[None, :], (rows, GRID_W)).reshape(-1).astype(F32)
    n_freq = MLA_ROPE_DIM // 4
    inv = ROPE_THETA ** (-jnp.arange(n_freq, dtype=F32) / n_freq)
    ang = jnp.concatenate([row[:, None] * inv, col[:, None] * inv], axis=-1)
    return jnp.cos(ang), jnp.sin(ang)


def _ada(cvec3, ada_w, ada_b, layer):
    R = cvec3.shape[0]
    a = jnp.concatenate([jax.nn.silu(cvec3), jnp.zeros((8 - R, cvec3.shape[1]), F32)], axis=0)
    m = matmul(a, ada_w, b_index=layer, tm=8, tn=512)[:R] + ada_b[layer]
    return m.reshape(R, 6, D_MODEL)


def _q_proj_kernel(cq_ref, w_ref, cos_ref, sin_ref, o_ref):
    cq, cos, sin = cq_ref[...], cos_ref[...], sin_ref[...]
    for hh in range(o_ref.shape[1]):
        w = w_ref[:, hh * 2 * QK_PAD:(hh + 1) * 2 * QK_PAD]
        acc = jnp.dot(cq, w, preferred_element_type=F32)
        o_ref[0, hh] = (acc[:, :QK_PAD] * cos + acc[:, QK_PAD:] * sin).astype(o_ref.dtype)


def q_proj(cqn, w_q2, cosq, sinq, B, L, *, tm=1024, hg=4):
    M, R = cqn.shape
    tpb = L // tm
    return pl.pallas_call(
        _q_proj_kernel,
        grid=(M // tm, MLA_HEADS // hg),
        in_specs=[pl.BlockSpec((tm, R), lambda i, h: (i, 0)),
                  pl.BlockSpec((R, hg * 2 * QK_PAD), lambda i, h: (0, h)),
                  pl.BlockSpec((tm, QK_PAD), lambda i, h: (i % tpb, 0)),
                  pl.BlockSpec((tm, QK_PAD), lambda i, h: (i % tpb, 0))],
        out_specs=pl.BlockSpec((1, hg, tm, QK_PAD), lambda i, h: (i // tpb, h, i % tpb, 0)),
        out_shape=jax.ShapeDtypeStruct((B, MLA_HEADS, L, QK_PAD), BF16),
        compiler_params=_params("parallel", "parallel"),
    )(cqn, w_q2, cosq, sinq)


def _q_weights(w_uq):
    R = w_uq.shape[0]
    half = MLA_ROPE_DIM // 2
    w = w_uq.reshape(R, MLA_HEADS, MLA_NOPE_DIM + MLA_ROPE_DIM)
    nope, rope = w[..., :MLA_NOPE_DIM], w[..., MLA_NOPE_DIM:]
    rot = jnp.concatenate([-rope[..., half:], rope[..., :half]], axis=-1)
    pad = jnp.zeros((R, MLA_HEADS, QK_PAD - MLA_NOPE_DIM - MLA_ROPE_DIM), w.dtype)
    parts = [nope, rope, pad, jnp.zeros_like(nope), rot, pad]
    return jnp.concatenate(parts, axis=-1).reshape(R, MLA_HEADS * 2 * QK_PAD).astype(BF16)


def _krope_t_kernel(w_ref, h_ref, *rest, roped):
    o_ref = rest[-1]
    x = lax.dot_general(w_ref[...], h_ref[...], (((1,), (1,)), ((), ())), preferred_element_type=F32)
    if roped:
        half = x.shape[0] // 2
        x1, x2, c, s = x[:half], x[half:], rest[0][...], rest[1][...]
        x = jnp.concatenate([x1 * c - x2 * s, x2 * c + x1 * s], axis=0)
    o_ref[0] = x.astype(o_ref.dtype)


def krope_t(h2, w_kr_t, B, Lx, tt, cos_t=None, sin_t=None):
    D = h2.shape[1]
    R = w_kr_t.shape[0]
    per = Lx // tt
    roped = cos_t is not None
    in_specs = [pl.BlockSpec((R, D), lambda b, i: (0, 0)), pl.BlockSpec((tt, D), lambda b, i: (b * per + i, 0))]
    args = [w_kr_t, h2]
    if roped:
        in_specs += [pl.BlockSpec((R // 2, tt), lambda b, i: (0, i))] * 2
        args += [cos_t, sin_t]
    return pl.pallas_call(
        functools.partial(_krope_t_kernel, roped=roped),
        grid=(B, per),
        in_specs=in_specs,
        out_specs=pl.BlockSpec((1, R, tt), lambda b, i: (b, 0, i)),
        out_shape=jax.ShapeDtypeStruct((B, R, Lx), BF16),
        compiler_params=_params("parallel", "parallel"),
    )(*args)


def _kv_proj_kernel(c_ref, wn_ref, wv_ref, kr_ref, kt_ref, v_ref):
    c = c_ref[0]
    kn_t = lax.dot_general(wn_ref[0], c, (((1,), (1,)), ((), ())), preferred_element_type=F32)
    rope_end = MLA_NOPE_DIM + MLA_ROPE_DIM
    kt_ref[0, 0, :MLA_NOPE_DIM] = kn_t.astype(kt_ref.dtype)
    kt_ref[0, 0, MLA_NOPE_DIM:rope_end] = kr_ref[0]
    kt_ref[0, 0, rope_end:] = jnp.zeros((QK_PAD - rope_end, c.shape[0]), kt_ref.dtype)
    v_ref[0, 0] = jnp.dot(c, wv_ref[0], preferred_element_type=F32).astype(v_ref.dtype)


def kv_proj(ckvn, wn_t, wv, kr_t, *, tk=2176):
    B, K, R = ckvn.shape
    H, V = wv.shape[0], wv.shape[2]
    return pl.pallas_call(
        _kv_proj_kernel,
        grid=(B, H, K // tk),
        in_specs=[pl.BlockSpec((1, tk, R), lambda b, h, i: (b, i, 0)),
                  pl.BlockSpec((1, MLA_NOPE_DIM, R), lambda b, h, i: (h, 0, 0)),
                  pl.BlockSpec((1, R, V), lambda b, h, i: (h, 0, 0)),
                  pl.BlockSpec((1, MLA_ROPE_DIM, tk), lambda b, h, i: (b, 0, i))],
        out_specs=[pl.BlockSpec((1, 1, QK_PAD, tk), lambda b, h, i: (b, h, 0, i)),
                   pl.BlockSpec((1, 1, tk, V), lambda b, h, i: (b, h, i, 0))],
        out_shape=[jax.ShapeDtypeStruct((B, H, QK_PAD, K), BF16), jax.ShapeDtypeStruct((B, H, K, V), BF16)],
        compiler_params=_params("parallel", "parallel", "parallel"),
    )(ckvn, wn_t, wv, kr_t)


def _mla_layer(h, h_c, B, L, Lc, w_in, g_q, g_kv, w_uq, w_ukv):
    H = MLA_HEADS
    lat = MLA_Q_RANK + MLA_KV_RANK
    w_in_b = w_in.astype(BF16)
    proj = matmul(h, w_in_b[:, :lat], tn=lat)
    ckv_c = matmul(h_c, w_in_b[:, MLA_Q_RANK:lat], tn=MLA_KV_RANK)
    ckv = jnp.concatenate([proj[:, MLA_Q_RANK:].reshape(B, L, -1), ckv_c.reshape(B, Lc, -1)], axis=1)
    cos, sin = _rope_tables(L // GRID_W)
    ones, zeros = jnp.ones((L, MLA_NOPE_DIM), F32), jnp.zeros((L, MLA_NOPE_DIM), F32)
    pad = jnp.zeros((L, QK_PAD - MLA_NOPE_DIM - MLA_ROPE_DIM), F32)
    cosq = jnp.concatenate([ones, cos, cos, pad], axis=1)
    sinq = jnp.concatenate([zeros, sin, sin, pad], axis=1)
    qp = q_proj(_small_rmsnorm(proj[:, :MLA_Q_RANK], g_q).astype(BF16), _q_weights(w_uq), cosq, sinq, B, L)
    w_kr_t = w_in_b[:, lat:].T
    kr_t = jnp.concatenate([krope_t(h, w_kr_t, B, L, 1024, cos.T, sin.T), krope_t(h_c, w_kr_t, B, Lc, Lc)], axis=2)
    w_kv = w_ukv.astype(BF16).reshape(MLA_KV_RANK, H, MLA_NOPE_DIM + MLA_V_DIM)
    wn_t = w_kv[..., :MLA_NOPE_DIM].transpose(1, 2, 0)
    wv = w_kv[..., MLA_NOPE_DIM:].transpose(1, 0, 2)
    kt, v = kv_proj(_small_rmsnorm(ckv, g_kv).astype(BF16), wn_t, wv, kr_t)
    scale = (MLA_NOPE_DIM + MLA_ROPE_DIM) ** -0.5
    return attention(qp, kt, v, scale=scale)


def _fft_tables(L):
    n = 2 * L
    n2_in = L // FFT_N1
    k2 = np.arange(FFT_K2)
    th = 2 * np.pi * ((k2[:, None] * np.arange(n2_in)[None, :]) % FFT_N2) / FFT_N2
    f1 = np.zeros((2 * FFT_K2P, n2_in))
    f1[:FFT_K2] = np.cos(th)
    f1[FFT_K2P:FFT_K2P + FFT_K2] = -np.sin(th)
    wgt = np.where((k2 == 0) | (k2 == FFT_N2 // 2), 1.0, 2.0)
    g = np.zeros((n2_in, 2 * FFT_K2P))
    g[:, :FFT_K2] = (wgt[:, None] * np.cos(th)).T / n
    g[:, FFT_K2P:FFT_K2P + FFT_K2] = -(wgt[:, None] * np.sin(th)).T / n
    k1 = np.arange(FFT_N1)[:, None]
    n1 = np.arange(FFT_N1)[None, :]
    m2 = np.zeros((-(-FFT_K2 // FFT_KSTEP) * FFT_KSTEP, 2 * FFT_N1, 2 * FFT_N1))
    for kk in range(FFT_K2):
        phi = 2 * np.pi * ((n1 * k1 * FFT_N2 + n1 * kk) % n) / n
        tr, ti = np.cos(phi), -np.sin(phi)
        m2[kk] = np.block([[tr, -ti], [ti, tr]])
    as_bf16 = lambda a: jnp.asarray(a, dtype=F32).astype(BF16)
    return (as_bf16(np.kron(f1, np.eye(FFT_SUB))), as_bf16(np.kron(g, np.eye(FFT_GROUP))), as_bf16(m2),
            as_bf16(np.transpose(m2, (0, 2, 1))))


def _fft_s1_kernel(f_ref, z_ref, o_ref):
    _, n2_in, g, ct = z_ref.shape
    z = z_ref[0].astype(F32)
    parts = []
    for s0 in range(0, g, FFT_SUB):
        zz = z[:, s0:s0 + FFT_SUB, :].reshape(n2_in * FFT_SUB, ct).astype(BF16)
        a = jnp.dot(f_ref[...], zz, preferred_element_type=F32)
        parts.append(a.reshape(2, FFT_K2P, FFT_SUB, ct))
    o_ref[0] = jnp.concatenate(parts, axis=2).astype(o_ref.dtype)


def fft_stage1(z4, f1k, *, lead_off, nb, ct=2048):
    _, n2_in, _, C = z4.shape
    g = FFT_GROUP
    return pl.pallas_call(
        _fft_s1_kernel,
        grid=(nb, FFT_N1 // g, C // ct),
        in_specs=[pl.BlockSpec(f1k.shape, lambda b, i, j: (0, 0)),
                  pl.BlockSpec((1, n2_in, g, ct), lambda b, i, j: (lead_off + b, 0, i, j))],
        out_specs=pl.BlockSpec((1, 2, FFT_K2P, g, ct), lambda b, i, j: (b, 0, 0, i, j)),
        out_shape=jax.ShapeDtypeStruct((nb, 2, FFT_K2P, FFT_N1, C), BF16),
        compiler_params=_params("parallel", "parallel", "parallel"),
    )(f1k, z4)


def _fft_s2_kernel(a_ref, m_ref, mt_ref, kf_ref, o_ref):
    step = pl.program_id(0)
    ct = a_ref.shape[-1]
    for q in range(FFT_KSTEP):
        k2 = step * FFT_KSTEP + q

        @pl.when(k2 < FFT_K2)
        def _(q=q):
            x = jnp.dot(m_ref[q], a_ref[0, :, q].reshape(2 * FFT_N1, ct), preferred_element_type=F32)
            xr, xi = x[:FFT_N1], x[FFT_N1:]
            kr, ki = kf_ref[0, 0, q].astype(F32), kf_ref[0, 1, q].astype(F32)
            p = jnp.concatenate([xr * kr - xi * ki, xr * ki + xi * kr], axis=0).astype(BF16)
            e = jnp.dot(mt_ref[q], p, preferred_element_type=F32)
            o_ref[0, :, q] = e.astype(o_ref.dtype).reshape(2, FFT_N1, ct)

        @pl.when(k2 >= FFT_K2)
        def _(q=q):
            o_ref[0, :, q] = jnp.zeros((2, FFT_N1, ct), o_ref.dtype)


def fft_stage2(a5, m2, m2t, kf, order):
    B, _, _, _, D = a5.shape
    last = (FFT_K2 - 1) // FFT_KSTEP
    blk = (1, 2, FFT_KSTEP, FFT_N1, D)
    tbl = pl.BlockSpec((FFT_KSTEP, 2 * FFT_N1, 2 * FFT_N1), lambda k, b: (jnp.minimum(k, last), 0, 0))
    return pl.pallas_call(
        _fft_s2_kernel,
        grid=(FFT_K2P // FFT_KSTEP, B),
        in_specs=[pl.BlockSpec(blk, lambda k, b: (b, 0, jnp.minimum(k, last), 0, 0)),
                  tbl,
                  tbl,
                  pl.BlockSpec(blk, lambda k, b: (order, 0, jnp.minimum(k, last), 0, 0))],
        out_specs=pl.BlockSpec(blk, lambda k, b: (b, 0, k, 0, 0)),
        out_shape=jax.ShapeDtypeStruct(a5.shape, BF16),
        compiler_params=_params("parallel", "arbitrary"),
    )(a5, m2, m2t, kf)


def _fft_f2_kernel(f_ref, g_ref, m_ref, inv_ref, o_ref):
    step = pl.program_id(0)
    ct = f_ref.shape[-1]
    inv = inv_ref[0]
    for q in range(FFT_KSTEP):
        k2 = step * FFT_KSTEP + q

        @pl.when(k2 < FFT_K2)
        def _(q=q):
            xf = jnp.dot(m_ref[q], f_ref[:, q].reshape(2 * FFT_N1, ct), preferred_element_type=F32)
            xg = jnp.dot(m_ref[q], g_ref[:, q].reshape(2 * FFT_N1, ct), preferred_element_type=F32)
            o_ref[0, 0, q] = ((xf[:FFT_N1] + xg[:FFT_N1]) * inv).astype(o_ref.dtype)
            o_ref[0, 1, q] = ((xf[FFT_N1:] - xg[FFT_N1:]) * inv).astype(o_ref.dtype)

        @pl.when(k2 >= FFT_K2)
        def _(q=q):
            o_ref[0, :, q] = jnp.zeros((2, FFT_N1, ct), o_ref.dtype)


def fft_filter_spectrum(a4, m2, inv, D):
    last = (FFT_K2 - 1) // FFT_KSTEP
    blk = (2, FFT_KSTEP, FFT_N1, D)
    return pl.pallas_call(
        _fft_f2_kernel,
        grid=(FFT_K2P // FFT_KSTEP, HYENA_ORDER),
        in_specs=[pl.BlockSpec(blk, lambda k, o: (0, jnp.minimum(k, last), 0, 2 * o)),
                  pl.BlockSpec(blk, lambda k, o: (0, jnp.minimum(k, last), 0, 2 * o + 1)),
                  pl.BlockSpec((FFT_KSTEP, 2 * FFT_N1, 2 * FFT_N1), lambda k, o: (jnp.minimum(k, last), 0, 0)),
                  pl.BlockSpec((1, 1, D), lambda k, o: (o, 0, 0))],
        out_specs=pl.BlockSpec((1, 2, FFT_KSTEP, FFT_N1, D), lambda k, o: (o, 0, k, 0, 0)),
        out_shape=jax.ShapeDtypeStruct((HYENA_ORDER, 2, FFT_K2P, FFT_N1, D), BF16),
        compiler_params=_params("arbitrary", "parallel"),
    )(a4, a4, m2, inv)


def _fft_s3_kernel(g_ref, e_ref, z_ref, x_ref, skip_ref, o_ref):
    _, n2_in, g, ct = z_ref.shape
    e = e_ref[0].reshape(2 * FFT_K2P * g, ct)
    y = jnp.dot(g_ref[...], e, preferred_element_type=F32).reshape(n2_in, g, ct)
    o_ref[0] = (x_ref[0] * (y + skip_ref[...] * z_ref[0])).astype(o_ref.dtype)


def fft_stage3(e5, gk, z4, z_off, x4, x_off, skip, *, out_dtype):
    B, _, _, _, D = e5.shape
    n2_in = z4.shape[1]
    g = FFT_GROUP
    return pl.pallas_call(
        _fft_s3_kernel,
        grid=(B, FFT_N1 // g),
        in_specs=[pl.BlockSpec(gk.shape, lambda b, i: (0, 0)),
                  pl.BlockSpec((1, 2, FFT_K2P, g, D), lambda b, i: (b, 0, 0, i, 0)),
                  pl.BlockSpec((1, n2_in, g, D), lambda b, i: (z_off + b, 0, i, 0)),
                  pl.BlockSpec((1, n2_in, g, D), lambda b, i: (x_off + b, 0, i, 0)),
                  pl.BlockSpec((1, D), lambda b, i: (0, 0))],
        out_specs=pl.BlockSpec((1, n2_in, g, D), lambda b, i: (b, 0, i, 0)),
        out_shape=jax.ShapeDtypeStruct((B, n2_in, FFT_N1, D), out_dtype),
        compiler_params=_params("parallel", "parallel"),
    )(gk, e5, z4, x4, skip.reshape(1, D))


def _hyin_kernel(ap_ref, a_ref, an_ref, b_ref, cw_ref, cb_ref, o_ref, *, tiles_per_seq):
    i = pl.program_id(0)
    t = i % tiles_per_seq
    halo = ap_ref.shape[0]
    tm, tn = o_ref.shape[1:]
    cw = MXU_DIM
    row = lax.broadcasted_iota(jnp.int32, (tm, cw), 0)
    at_start = jnp.logical_and(row == 0, t == 0)
    at_end = jnp.logical_and(row == tm - 1, t == tiles_per_seq - 1)
    a_ext = jnp.concatenate([ap_ref[...], a_ref[...], an_ref[...]], axis=0)
    for c0 in range(0, tn, cw):
        acc = jnp.dot(a_ext, b_ref[:, c0:c0 + cw], preferred_element_type=F32)
        up = jnp.where(at_start, 0.0, acc[halo - 1:halo - 1 + tm])
        dn = jnp.where(at_end, 0.0, acc[halo + 1:halo + 1 + tm])
        w = cw_ref[:, c0:c0 + cw]
        o_ref[0, :, c0:c0 + cw] = up * w[0:1] + acc[halo:halo + tm] * w[1:2] + dn * w[2:3] + cb_ref[:, c0:c0 + cw]


def hyena_in(h, w_in, conv_w, conv_b, L, *, tm=1024, tn=1024, halo=16):
    M, K = h.shape
    N = w_in.shape[1]
    D = N // 3
    per = D // tn
    return pl.pallas_call(
        functools.partial(_hyin_kernel, tiles_per_seq=L // tm),
        grid=(M // tm, N // tn),
        in_specs=[pl.BlockSpec((halo, K), lambda i, j: (jnp.maximum(i * (tm // halo) - 1, 0), 0)),
                  pl.BlockSpec((tm, K), lambda i, j: (i, 0)),
                  pl.BlockSpec((halo, K), lambda i, j: (jnp.minimum((i + 1) * (tm // halo), M // halo - 1), 0)),
                  pl.BlockSpec((K, tn), lambda i, j: (0, j)),
                  pl.BlockSpec((3, tn), lambda i, j: (0, j)),
                  pl.BlockSpec((1, tn), lambda i, j: (0, j))],
        out_specs=pl.BlockSpec((1, tm, tn), lambda i, j: (j // per, i, j % per)),
        out_shape=jax.ShapeDtypeStruct((3, M, D), F32),
        compiler_params=_params("parallel", "parallel"),
    )(h, h, h, w_in, conv_w, conv_b.reshape(1, N))


def _taps_kernel(a_ref, w_ref, b_ref, dl_ref, keep_ref, o_ref, ss_ref, *, seq_len):
    i = pl.program_id(1)
    tm = a_ref.shape[0]
    acc = jnp.dot(a_ref[...].astype(BF16), w_ref[...].astype(BF16), preferred_element_type=F32) + b_ref[...]
    row = i * tm + lax.broadcasted_iota(jnp.int32, acc.shape, 0)
    t = row.astype(F32) / seq_len
    hf = acc * jnp.exp(-t * dl_ref[...])
    hf = jnp.where(row == 0, hf * keep_ref[...], hf)
    o_ref[...] = hf.astype(o_ref.dtype)
    part = jnp.sum(hf * hf, axis=0, keepdims=True)

    @pl.when(i == 0)
    def _():
        ss_ref[...] = part

    @pl.when(i > 0)
    def _():
        ss_ref[...] += part


def _hyena_filter_taps(L, w1, b1, w2, b2, w3, b3, freq, *, tm=512, tn=1024):
    D = D_MODEL
    N = 2 * HYENA_ORDER * D
    t = jnp.arange(L, dtype=F32) / L
    w = 2 * math.pi * jnp.arange(L, dtype=F32) / L
    bands = jnp.linspace(1e-4, HYENA_BANDS - 1, HYENA_BANDS, dtype=F32)
    ang = w[:, None] * bands[None, :]
    z = jnp.concatenate([t[:, None], jnp.cos(ang), -jnp.sin(ang)], axis=-1)
    a = jnp.sin(freq * (z @ w1 + b1))
    a = jnp.sin(freq * (a @ w2 + b2))
    max_decay = math.log(HYENA_TARGET) / HYENA_FAST_PCT
    min_decay = math.log(HYENA_TARGET) / HYENA_SLOW_PCT
    decay = jnp.tile(jnp.abs(jnp.linspace(min_decay, max_decay, D, dtype=F32)), 2 * HYENA_ORDER).reshape(1, N)
    keep0 = jnp.where((jnp.arange(N) // D) % 2 == 1, 0.0, 1.0).astype(F32).reshape(1, N)
    K = a.shape[1]
    taps, ss = pl.pallas_call(
        functools.partial(_taps_kernel, seq_len=L),
        grid=(N // tn, L // tm),
        in_specs=[pl.BlockSpec((tm, K), lambda j, i: (i, 0)),
                  pl.BlockSpec((K, tn), lambda j, i: (0, j)),
                  pl.BlockSpec((1, tn), lambda j, i: (0, j)),
                  pl.BlockSpec((1, tn), lambda j, i: (0, j)),
                  pl.BlockSpec((1, tn), lambda j, i: (0, j))],
        out_specs=[pl.BlockSpec((tm, tn), lambda j, i: (i, j)), pl.BlockSpec((1, tn), lambda j, i: (0, j))],
        out_shape=[jax.ShapeDtypeStruct((L, N), BF16), jax.ShapeDtypeStruct((1, N), F32)],
        compiler_params=_params("parallel", "arbitrary"),
    )(a, w3, b3.reshape(1, N), decay, keep0)
    ss = ss.reshape(HYENA_ORDER, 2, D).sum(axis=1, keepdims=True)
    return taps, lax.rsqrt(ss + NORM_EPS)


def _hyena_layer(h, B, L, w_in, conv_w, conv_b, f_w1, f_b1, f_w2, f_b2, f_w3, f_b3, f_freq, skip):
    D = D_MODEL
    n2_in = L // FFT_N1
    f1k, gk, m2, m2t = _fft_tables(L)
    taps, inv = _hyena_filter_taps(L, f_w1, f_b1, f_w2, f_b2, f_w3, f_b3, f_freq)
    af = fft_stage1(taps.reshape(1, n2_in, FFT_N1, taps.shape[1]), f1k, lead_off=0, nb=1)
    kf = fft_filter_spectrum(af[0], m2, inv, D)
    u4 = hyena_in(h, w_in.astype(BF16), conv_w, conv_b, L).reshape(3 * B, n2_in, FFT_N1, D)
    z4 = u4
    for order in range(HYENA_ORDER):
        a = fft_stage1(z4, f1k, lead_off=0, nb=B)
        e = fft_stage2(a, m2, m2t, kf, order)
        last = order == HYENA_ORDER - 1
        z4 = fft_stage3(e, gk, z4, 0, u4, (order + 1) * B, skip[order], out_dtype=BF16 if last else F32)
    return z4.reshape(B * L, D)


def _moe_layer(x2, norm_gain, mod, B, T, w_router, w_gate, w_up, w_down, layer, next_g, next_shift=None,
               next_scale=None, *, tj=256, win=64):
    E = N_EXPERTS
    cap = EC_CAPACITY_FACTOR * T // E
    hp, aff = norm_route(x2, norm_gain, mod[:, 3:4], mod[:, 4:5], w_router.T.astype(BF16), rows_per_batch=T)
    pos, idx, g = route_select(aff, cap)
    idx3 = idx.reshape(B, E, cap)
    y = moe_ffn(idx3, hp, g, w_gate, w_up, w_down, layer, T)
    nj = T // tj
    edges = jnp.arange(nj + 1, dtype=jnp.int32) * tj
    first = jnp.sum((idx3[..., None] < edges).astype(jnp.int32), axis=2)
    nominal = (first[..., :nj] // 16) * 16
    ntr = jnp.maximum(jnp.max((first[..., 1:] - nominal + win - 1) // win, axis=1), 1)
    starts = nominal.transpose(0, 2, 1).reshape(-1)
    return moe_combine(starts, ntr.reshape(-1), y, pos.transpose(0, 2, 1), x2, mod[:, 5:6], next_g, next_shift,
                       next_scale, tj=tj, win=win)


def kernel(x, c, ctx, c_ctx, ada_w, ada_b, norm_g, final_g, mla_w_in, mla_g_q, mla_g_kv, mla_w_uq, mla_w_ukv, mla_w_o, hy_w_in, hy_conv_w, hy_conv_b, hy_f_w1, hy_f_b1, hy_f_w2, hy_f_b2, hy_f_w3, hy_f_b3, hy_f_freq, hy_skip, hy_w_out, moe_w_router, moe_w_gate, moe_w_up, moe_w_down):
    B, L, D = x.shape
    Lc = ctx.shape[1]
    x2 = x.reshape(B * L, D)
    ctx2 = ctx.reshape(B * Lc, D)

    mod3 = _ada(jnp.concatenate([c, c_ctx[None]], axis=0), ada_w, ada_b, 0)
    mod, mod_c = mod3[:B], jnp.broadcast_to(mod3[B:], (B, 6, D))
    h = norm_mod(x2, norm_g[0, 0], mod[:, 0:1], mod[:, 1:2], rows_per_batch=L)
    h_c = norm_mod(ctx2, norm_g[0, 0], mod_c[:, 0:1], mod_c[:, 1:2], rows_per_batch=Lc)
    o = _mla_layer(h, h_c, B, L, Lc, mla_w_in[0], mla_g_q[0], mla_g_kv[0], mla_w_uq[0], mla_w_ukv[0])
    x2 = matmul(o.reshape(B * L, -1), mla_w_o[0].astype(BF16), tn=D,
                res=x2, gate=mod[:, 2:3], rows_per_batch=L)
    mod1 = _ada(c, ada_w, ada_b, 1)
    x2, h = _moe_layer(x2, norm_g[0, 1], mod, B, L, moe_w_router[0], moe_w_gate, moe_w_up, moe_w_down, 0,
                       norm_g[1, 0], mod1[:, 0:1], mod1[:, 1:2])

    mod = mod1
    z = _hyena_layer(h, B, L, hy_w_in[0], hy_conv_w[0], hy_conv_b[0], hy_f_w1[0], hy_f_b1[0], hy_f_w2[0],
                     hy_f_b2[0], hy_f_w3[0], hy_f_b3[0], hy_f_freq[0], hy_skip[0])
    x2 = matmul(z, hy_w_out[0].astype(BF16), tn=D, res=x2, gate=mod[:, 2:3], rows_per_batch=L)
    out = _moe_layer(x2, norm_g[1, 1], mod, B, L, moe_w_router[1], moe_w_gate, moe_w_up, moe_w_down, 1, final_g)
    return out.reshape(B, L, D)
```

```python
import functools
import math

import jax
import jax.numpy as jnp
import numpy as np
from jax import lax
from jax.experimental import pallas as pl
from jax.experimental.pallas import tpu as pltpu

F32 = jnp.float32
BF16 = jnp.bfloat16

D_MODEL = 2048
GRID_W = 64
NORM_EPS = 1e-6
MLA_HEADS = 16
MLA_Q_RANK = 512
MLA_KV_RANK = 256
MLA_NOPE_DIM = 128
MLA_ROPE_DIM = 64
MLA_V_DIM = 128
ROPE_THETA = 10000.0
HYENA_ORDER = 2
HYENA_BANDS = 16
HYENA_TARGET = 1e-2
HYENA_FAST_PCT = 0.3
HYENA_SLOW_PCT = 1.5
N_EXPERTS = 16
EC_CAPACITY_FACTOR = 2
FFT_N1 = 128
FFT_N2 = 64
FFT_K2 = FFT_N2 // 2 + 1
FFT_K2P = 40
FFT_GROUP = 16
FFT_SUB = 8
FFT_KSTEP = 4

V7X_VMEM_LIMIT_BYTES = 56 * 1024 * 1024
LANES = 128
MXU_DIM = 256
QK_PAD = MXU_DIM


def _params(*sem):
    return pltpu.CompilerParams(dimension_semantics=sem, vmem_limit_bytes=V7X_VMEM_LIMIT_BYTES)


def _mm_kernel(a_ref, b_ref, *rest, b_is_stacked, has_res):
    o_ref = rest[-1]
    b = b_ref[0] if b_is_stacked else b_ref[...]
    acc = jnp.dot(a_ref[...].astype(BF16), b.astype(BF16), preferred_element_type=F32)
    if has_res:
        res_ref, gate_ref = rest[0], rest[1]
        acc = res_ref[...] + gate_ref[0] * acc
    o_ref[...] = acc.astype(o_ref.dtype)


def matmul(a, b, *, b_index=None, out_dtype=F32, tm=512, tn=512, res=None, gate=None, rows_per_batch=None):
    M, K = a.shape
    N = b.shape[-1]
    tm, tn = min(tm, M), min(tn, N)
    assert M % tm == 0 and N % tn == 0, (M, N, tm, tn)
    stacked = b.ndim == 3
    if stacked:
        b_spec = pl.BlockSpec((1, K, tn), lambda i, j: (b_index, 0, j))
    else:
        b_spec = pl.BlockSpec((K, tn), lambda i, j: (0, j))
    in_specs = [pl.BlockSpec((tm, K), lambda i, j: (i, 0)), b_spec]
    args = [a, b]
    if res is not None:
        assert rows_per_batch % tm == 0
        tpb = rows_per_batch // tm
        in_specs += [pl.BlockSpec((tm, tn), lambda i, j: (i, j)),
                     pl.BlockSpec((1, 1, tn), lambda i, j: (i // tpb, 0, j))]
        args += [res, gate]
    return pl.pallas_call(
        functools.partial(_mm_kernel, b_is_stacked=stacked, has_res=res is not None),
        grid=(M // tm, N // tn),
        in_specs=in_specs,
        out_specs=pl.BlockSpec((tm, tn), lambda i, j: (i, j)),
        out_shape=jax.ShapeDtypeStruct((M, N), out_dtype),
        compiler_params=_params("parallel", "parallel"),
    )(*args)


def _norm_kernel(x_ref, g_ref, *rest, modulated):
    o_ref = rest[-1]
    x = x_ref[...]
    y = x * lax.rsqrt(jnp.mean(x * x, axis=-1, keepdims=True) + NORM_EPS) * g_ref[...]
    if modulated:
        shift_ref, scale_ref = rest[0], rest[1]
        y = y * (1 + scale_ref[0]) + shift_ref[0]
    o_ref[...] = y.astype(o_ref.dtype)


def norm_mod(x2, g, shift=None, scale=None, *, rows_per_batch=None, out_dtype=BF16, tm=256):
    M, D = x2.shape
    tm = min(tm, M)
    in_specs = [pl.BlockSpec((tm, D), lambda i: (i, 0)), pl.BlockSpec((1, D), lambda i: (0, 0))]
    args = [x2, g.reshape(1, D)]
    if shift is not None:
        tpb = rows_per_batch // tm
        in_specs += [pl.BlockSpec((1, 1, D), lambda i: (i // tpb, 0, 0))] * 2
        args += [shift, scale]
    return pl.pallas_call(
        functools.partial(_norm_kernel, modulated=shift is not None),
        grid=(M // tm,),
        in_specs=in_specs,
        out_specs=pl.BlockSpec((tm, D), lambda i: (i, 0)),
        out_shape=jax.ShapeDtypeStruct((M, D), out_dtype),
        compiler_params=_params("parallel"),
    )(*args)


def _attn_kernel(q_ref, kt_ref, v_ref, o_ref, *, chunks, c):
    q = q_ref[0, 0]
    m = l = acc = None
    for lo, hi in chunks:
        s = jnp.dot(q, kt_ref[0, 0, :, lo:hi], preferred_element_type=F32)
        m_chunk = jnp.max(s, axis=-1, keepdims=True)
        m_new = m_chunk if m is None else jnp.maximum(m, m_chunk)
        p = jnp.exp2(s * c - m_new * c)
        pv = jnp.dot(p.astype(BF16), v_ref[0, 0, lo:hi, :], preferred_element_type=F32)
        psum = jnp.sum(p, axis=-1, keepdims=True)
        if m is None:
            l, acc = psum, pv
        else:
            alpha = jnp.exp2((m - m_new) * c)
            l = alpha * l + psum
            acc = alpha * acc + pv
        m = m_new
    o_ref[0] = (acc / l).astype(o_ref.dtype)


def attention(q, kt, v, *, scale, tq=2048, tk=1024):
    B, H, L, _ = q.shape
    K = kt.shape[-1]
    V = v.shape[-1]
    chunks = tuple((lo, min(lo + tk, K)) for lo in range(0, K, tk))
    return pl.pallas_call(
        functools.partial(_attn_kernel, chunks=chunks, c=scale * math.log2(math.e)),
        grid=(B, H, L // tq),
        in_specs=[pl.BlockSpec((1, 1, tq, QK_PAD), lambda b, h, i: (b, h, i, 0)),
                  pl.BlockSpec((1, 1, QK_PAD, K), lambda b, h, i: (b, h, 0, 0)),
                  pl.BlockSpec((1, 1, K, V), lambda b, h, i: (b, h, 0, 0))],
        out_specs=pl.BlockSpec((1, tq, V), lambda b, h, i: (b, i, h)),
        out_shape=jax.ShapeDtypeStruct((B, L, H * V), BF16),
        compiler_params=_params("parallel", "parallel", "arbitrary"),
    )(q, kt, v)


def _proj_norm_route_kernel(a_ref, b_ref, res_ref, gate_ref, g_ref, shift_ref, scale_ref, wr_ref,
                            x_ref, h_ref, aff_ref):
    acc = jnp.dot(a_ref[...], b_ref[...], preferred_element_type=F32)
    x = res_ref[...] + gate_ref[0] * acc
    x_ref[...] = x
    y = x * lax.rsqrt(jnp.mean(x * x, axis=-1, keepdims=True) + NORM_EPS) * g_ref[...]
    y = y * (1 + scale_ref[0]) + shift_ref[0]
    h_ref[...] = y
    logits = lax.dot_general(wr_ref[...], y.astype(BF16), (((1,), (1,)), ((), ())),
                             preferred_element_type=F32)
    p = jnp.exp(logits - jnp.max(logits, axis=0, keepdims=True))
    aff_ref[0] = p / jnp.sum(p, axis=0, keepdims=True)


def proj_norm_route(a, b, res, gate, g, shift, scale, w_router_t, *, rows_per_batch, tm=256):
    M, K = a.shape
    D = b.shape[1]
    E = w_router_t.shape[0]
    tpb = rows_per_batch // tm
    row_blk = pl.BlockSpec((tm, D), lambda i: (i, 0))
    per_batch = pl.BlockSpec((1, 1, D), lambda i: (i // tpb, 0, 0))
    return pl.pallas_call(
        _proj_norm_route_kernel,
        grid=(M // tm,),
        in_specs=[pl.BlockSpec((tm, K), lambda i: (i, 0)),
                  pl.BlockSpec((K, D), lambda i: (0, 0)),
                  row_blk, per_batch,
                  pl.BlockSpec((1, D), lambda i: (0, 0)),
                  per_batch, per_batch,
                  pl.BlockSpec((E, D), lambda i: (0, 0))],
        out_specs=[row_blk, row_blk, pl.BlockSpec((1, E, tm), lambda i: (i // tpb, 0, i % tpb))],
        out_shape=[jax.ShapeDtypeStruct((M, D), F32), jax.ShapeDtypeStruct((M, D), F32),
                   jax.ShapeDtypeStruct((M // rows_per_batch, E, rows_per_batch), F32)],
        compiler_params=_params("parallel"),
    )(a, b, res, gate, g.reshape(1, D), shift, scale, w_router_t)


def _lane_prefix_sum(x):
    E, T = x.shape
    r = lax.broadcasted_iota(jnp.int32, (LANES, LANES), 0)
    c = lax.broadcasted_iota(jnp.int32, (LANES, LANES), 1)
    tri = jnp.where(r <= c, 1.0, 0.0).astype(BF16)
    outs, carry = [], jnp.zeros((E, 1), F32)
    for j in range(T // LANES):
        seg = jnp.dot(x[:, j * LANES:(j + 1) * LANES].astype(BF16), tri, preferred_element_type=F32) + carry
        outs.append(seg)
        carry = seg[:, LANES - 1:LANES]
    return jnp.concatenate(outs, axis=1)


def _route_select_kernel(aff_ref, pos_ref, idx_ref, gate_ref, cum_s, *, cap, tch):
    aff = aff_ref[0]
    E, T = aff.shape

    def bisect(state):
        lo, hi, _ = state
        mid = 0.5 * (lo + hi)
        cnt = jnp.sum(jnp.where(aff >= mid, 1.0, 0.0), axis=1, keepdims=True)
        ge = cnt >= cap
        open_rows = jnp.where((mid > lo) & (mid < hi), 1.0, 0.0)
        return jnp.where(ge, mid, lo), jnp.where(ge, hi, mid), jnp.max(open_rows)

    lo0 = jnp.min(aff, axis=1, keepdims=True)
    hi0 = 2.0 * jnp.max(aff, axis=1, keepdims=True) + 1e-30
    lo, hi, _ = lax.while_loop(lambda st: st[2] > 0.0, bisect, (lo0, hi0, jnp.float32(1.0)))
    gt = aff >= hi
    eq = (aff >= lo) & (aff < hi)
    need = cap - jnp.sum(jnp.where(gt, 1.0, 0.0), axis=1, keepdims=True)
    eq_rank = _lane_prefix_sum(jnp.where(eq, 1.0, 0.0))
    sel = gt | (eq & (eq_rank <= need))
    cum = _lane_prefix_sum(jnp.where(sel, 1.0, 0.0))
    cumsel = jnp.where(sel, cum, 0.0)
    pos_ref[0] = (cumsel - 1.0).astype(jnp.int32)
    cum_s[0] = cum
    cum_s[1] = cumsel
    slot = lax.broadcasted_iota(jnp.int32, (cap, 1), 0).astype(F32)

    def per_expert(e, carry):
        idx_acc = jnp.zeros((cap, 1), F32)
        g_acc = jnp.zeros((cap, 1), F32)
        for lo in range(0, T, tch):
            cum_e = cum_s[0, pl.ds(e, 1), lo:lo + tch]
            cs_e = cum_s[1, pl.ds(e, 1), lo:lo + tch]
            aff_e = aff_ref[0, pl.ds(e, 1), lo:lo + tch]
            idx_acc = idx_acc + jnp.sum(jnp.where(cum_e <= slot, 1.0, 0.0), axis=1, keepdims=True)
            g_acc = g_acc + jnp.sum(jnp.where(cs_e == slot + 1.0, aff_e, 0.0), axis=1, keepdims=True)
        idx_ref[0, pl.ds(e, 1)] = idx_acc.astype(jnp.int32).reshape(1, cap, 1)
        gate_ref[0, pl.ds(e, 1)] = g_acc.reshape(1, cap, 1)
        return carry

    lax.fori_loop(0, E, per_expert, 0)


def route_select(aff, cap, *, tch=1024):
    B, E, T = aff.shape
    return pl.pallas_call(
        functools.partial(_route_select_kernel, cap=cap, tch=tch),
        grid=(B,),
        in_specs=[pl.BlockSpec((1, E, T), lambda b: (b, 0, 0))],
        out_specs=[pl.BlockSpec((1, E, T), lambda b: (b, 0, 0)),
                   pl.BlockSpec((1, E, cap, 1), lambda b: (b, 0, 0, 0)),
                   pl.BlockSpec((1, E, cap, 1), lambda b: (b, 0, 0, 0))],
        out_shape=[jax.ShapeDtypeStruct((B, E, T), jnp.int32),
                   jax.ShapeDtypeStruct((B, E, cap, 1), jnp.int32),
                   jax.ShapeDtypeStruct((B, E, cap, 1), F32)],
        scratch_shapes=[pltpu.VMEM((2, E, T), F32)],
        compiler_params=_params("parallel"),
    )(aff)


def _moe_ffn_kernel(idx_ref, idxn_ref, h_hbm, wg_ref, wu_ref, wd_ref, g_ref, o_ref, xland, xbf, h1, sem,
                    *, nf, n_steps, n_exp, cap, tokens_per_batch):
    e = pl.program_id(0)
    f = pl.program_id(1)
    rows = xland.shape[0]

    def issue_gather(ids_ref):
        for b in range(rows // cap):
            def body(c, carry, b=b):
                i = b * cap + c
                src_row = ids_ref[0, 0, i] + b * tokens_per_batch
                pltpu.make_async_copy(h_hbm.at[pl.ds(src_row, 1)], xland.at[pl.ds(i, 1)], sem.at[0]).start()
                return carry
            lax.fori_loop(0, cap, body, 0, unroll=8)

    @pl.when((e == 0) & (f == 0))
    def _():
        issue_gather(idx_ref)

    def wait_gather():
        pltpu.make_async_copy(h_hbm.at[pl.ds(0, rows)], xland, sem.at[0]).wait()

    @pl.when(f == 0)
    def _():
        wait_gather()
        xbf[...] = xland[...].astype(BF16)

    share = rows // n_steps

    def issue_share():
        base = f * share
        off = (base // cap) * tokens_per_batch
        for j in range(share):
            pltpu.make_async_copy(h_hbm.at[pl.ds(idxn_ref[0, 0, base + j] + off, 1)],
                                  xland.at[pl.ds(base + j, 1)], sem.at[0]).start()

    @pl.when(f < nf)
    def _():
        x = xbf[...]
        a = jnp.dot(x, wg_ref[0, 0].astype(BF16), preferred_element_type=F32)
        issue_share()
        u = jnp.dot(x, wu_ref[0, 0].astype(BF16), preferred_element_type=F32)
        h1[jnp.minimum(f, nf - 1)] = (jax.nn.silu(a) * u).astype(BF16)

    @pl.when(f >= nf)
    def _():
        hid = jnp.concatenate([h1[k] for k in range(nf)], axis=1)
        issue_share()
        y = jnp.dot(hid, wd_ref[0, 0].astype(BF16), preferred_element_type=F32)
        o_ref[...] = (y.reshape(o_ref.shape) * g_ref[...]).astype(o_ref.dtype)

    @pl.when((e == n_exp - 1) & (f == n_steps - 1))
    def _():
        wait_gather()


def moe_ffn(idx3, hp, g, w_gate, w_up, w_down, layer, T, *, tf=256, tc=512):
    B, E, C = idx3.shape
    D = hp.shape[1]
    F = w_gate.shape[-1]
    nf = F // tf
    tc = min(tc, D)
    n_steps = nf + D // tc
    assert (B * C) % n_steps == 0 and C % ((B * C) // n_steps) == 0, (B, C, n_steps)
    ids = idx3.transpose(1, 0, 2).reshape(E, 1, B * C)
    smem_blk = lambda imap: pl.BlockSpec((1, 1, B * C), imap, memory_space=pltpu.SMEM)
    up_blk = pl.BlockSpec((1, 1, D, tf), lambda e, s: (layer, e, 0, jnp.minimum(s, nf - 1)))
    return pl.pallas_call(
        functools.partial(_moe_ffn_kernel, nf=nf, n_steps=n_steps, n_exp=E, cap=C, tokens_per_batch=T),
        grid=(E, n_steps),
        in_specs=[smem_blk(lambda e, s: (e, 0, 0)),
                  smem_blk(lambda e, s: (jnp.minimum(e + 1, E - 1), 0, 0)),
                  pl.BlockSpec(memory_space=pl.ANY),
                  up_blk,
                  up_blk,
                  pl.BlockSpec((1, 1, F, tc), lambda e, s: (layer, e, 0, jnp.maximum(s - nf, 0))),
                  pl.BlockSpec((B, 1, C, 1), lambda e, s: (0, e, 0, 0))],
        out_specs=pl.BlockSpec((B, 1, C, tc), lambda e, s: (0, e, 0, jnp.maximum(s - nf, 0))),
        out_shape=jax.ShapeDtypeStruct((B, E, C, D), BF16),
        scratch_shapes=[pltpu.VMEM((B * C, D), F32),
                        pltpu.VMEM((B * C, D), BF16),
                        pltpu.VMEM((nf, B * C, tf), BF16),
                        pltpu.SemaphoreType.DMA((1,))],
        compiler_params=_params("arbitrary", "arbitrary"),
    )(ids, ids, hp, w_gate, w_up, w_down, g)


def _moe_combine_kernel(starts_ref, ntr_ref, y_hbm, post_ref, x_ref, gate_ref, ng_ref, *rest,
                        n_exp, cap, win, steps_per_batch, modulated):
    ybuf, yext, acc, sem, sem_ext = rest[-5:]
    out_refs = rest[2:-5] if modulated else rest[:-5]
    s = pl.program_id(0)
    slot = s % 2

    def window(step, e, k):
        nominal = starts_ref[step * n_exp + e] + k * win
        return nominal, pl.multiple_of(jnp.minimum(nominal, cap - win), 16)

    def window_copy(step, e, k, dst, dsem):
        _, actual = window(step, e, k)
        return pltpu.make_async_copy(y_hbm.at[step // steps_per_batch, e, pl.ds(actual, win)],
                                     dst.at[pl.ds(e * win, win)], dsem)

    def onehot(step, k):
        post = post_ref[0]
        per = max(LANES // win, 1)
        lane = lax.broadcasted_iota(jnp.int32, (1, per * win), 1)
        cols = []
        for e0 in range(0, n_exp, per):
            slots = tok = None
            for q in range(per):
                nominal, actual = window(step, e0 + q, k)
                s_q = actual + lane - q * win
                s_q = jnp.where(s_q >= nominal, s_q, -2)
                t_q = post[:, e0 + q:e0 + q + 1]
                mine = lane >= q * win
                slots = s_q if q == 0 else jnp.where(mine, s_q, slots)
                tok = t_q if q == 0 else jnp.where(mine, t_q, tok)
            cols.append(jnp.where(tok == slots, 1.0, 0.0).astype(BF16))
        return jnp.concatenate(cols, axis=1)

    @pl.when(s == 0)
    def _():
        for e in range(n_exp):
            window_copy(0, e, 0, ybuf.at[0], sem.at[0]).start()

    for e in range(n_exp):
        window_copy(s, e, 0, ybuf.at[slot], sem.at[slot]).wait()

    @pl.when(s + 1 < pl.num_programs(0))
    def _():
        for e in range(n_exp):
            window_copy(s + 1, e, 0, ybuf.at[1 - slot], sem.at[1 - slot]).start()

    acc[...] = jnp.dot(onehot(s, 0), ybuf[slot], preferred_element_type=F32)

    def extra(k, carry):
        for e in range(n_exp):
            window_copy(s, e, k, yext, sem_ext.at[0]).start()
        for e in range(n_exp):
            window_copy(s, e, k, yext, sem_ext.at[0]).wait()
        acc[...] += jnp.dot(onehot(s, k), yext[...], preferred_element_type=F32)
        return carry

    lax.fori_loop(1, ntr_ref[s], extra, 0)
    x = x_ref[...] + gate_ref[0] * acc[...]
    y = x * lax.rsqrt(jnp.mean(x * x, axis=-1, keepdims=True) + NORM_EPS) * ng_ref[...]
    if modulated:
        out_refs[0][...] = x
        y = y * (1 + rest[1][0]) + rest[0][0]
    out_refs[-1][...] = y.astype(out_refs[-1].dtype)


def moe_combine(starts, ntr, y, post, x2, gate, norm_g, shift=None, scale=None, *, tj=256, win=64):
    B, E, C, D = y.shape
    M = x2.shape[0]
    T = M // B
    spb = T // tj
    modulated = shift is not None
    row_blk = pl.BlockSpec((tj, D), lambda s, *_: (s, 0))
    per_batch = pl.BlockSpec((1, 1, D), lambda s, *_: (s // spb, 0, 0))
    in_specs = [pl.BlockSpec(memory_space=pl.ANY),
                pl.BlockSpec((1, tj, E), lambda s, *_: (s // spb, s % spb, 0)),
                row_blk, per_batch, pl.BlockSpec((1, D), lambda s, *_: (0, 0))]
    args = [starts, ntr, y, post, x2, gate, norm_g.reshape(1, D)]
    if modulated:
        in_specs += [per_batch, per_batch]
        args += [shift, scale]
        out_specs = [row_blk, row_blk]
        out_shape = [jax.ShapeDtypeStruct((M, D), F32), jax.ShapeDtypeStruct((M, D), BF16)]
    else:
        out_specs = row_blk
        out_shape = jax.ShapeDtypeStruct((M, D), F32)
    grid_spec = pltpu.PrefetchScalarGridSpec(
        num_scalar_prefetch=2,
        grid=(M // tj,),
        in_specs=in_specs,
        out_specs=out_specs,
        scratch_shapes=[pltpu.VMEM((2, E * win, D), BF16),
                        pltpu.VMEM((E * win, D), BF16),
                        pltpu.VMEM((tj, D), F32),
                        pltpu.SemaphoreType.DMA((2,)),
                        pltpu.SemaphoreType.DMA((1,))],
    )
    return pl.pallas_call(
        functools.partial(_moe_combine_kernel, n_exp=E, cap=C, win=win, steps_per_batch=spb, modulated=modulated),
        grid_spec=grid_spec,
        out_shape=out_shape,
        compiler_params=_params("arbitrary"),
    )(*args)


def _small_rmsnorm(x, g):
    return x * lax.rsqrt(jnp.mean(x * x, axis=-1, keepdims=True) + NORM_EPS) * g


def _rope_tables(rows):
    row = jnp.broadcast_to(jnp.arange(rows)[:, None], (rows, GRID_W)).reshape(-1).astype(F32)
    col = jnp.broadcast_to(jnp.arange(GRID_W)[None, :], (rows, GRID_W)).reshape(-1).astype(F32)
    n_freq = MLA_ROPE_DIM // 4
    inv = ROPE_THETA ** (-jnp.arange(n_freq, dtype=F32) / n_freq)
    ang = jnp.concatenate([row[:, None] * inv, col[:, None] * inv], axis=-1)
    return jnp.cos(ang), jnp.sin(ang)


def _ada(cvec3, ada_w, ada_b, layer):
    R = cvec3.shape[0]
    a = jnp.concatenate([jax.nn.silu(cvec3), jnp.zeros((8 - R, cvec3.shape[1]), F32)], axis=0)
    m = matmul(a, ada_w, b_index=layer, tm=8, tn=512)[:R] + ada_b[layer]
    return m.reshape(R, 6, D_MODEL)


def _q_proj_kernel(cq_ref, w_ref, cos_ref, sin_ref, o_ref):
    cq, cos, sin = cq_ref[...], cos_ref[...], sin_ref[...]
    for hh in range(o_ref.shape[1]):
        w = w_ref[:, hh * 2 * QK_PAD:(hh + 1) * 2 * QK_PAD]
        acc = jnp.dot(cq, w, preferred_element_type=F32)
        o_ref[0, hh] = (acc[:, :QK_PAD] * cos + acc[:, QK_PAD:] * sin).astype(o_ref.dtype)


def q_proj(cqn, w_q2, cosq, sinq, B, L, *, tm=1024, hg=4):
    M, R = cqn.shape
    tpb = L // tm
    return pl.pallas_call(
        _q_proj_kernel,
        grid=(M // tm, MLA_HEADS // hg),
        in_specs=[pl.BlockSpec((tm, R), lambda i, h: (i, 0)),
                  pl.BlockSpec((R, hg * 2 * QK_PAD), lambda i, h: (0, h)),
                  pl.BlockSpec((tm, QK_PAD), lambda i, h: (i % tpb, 0)),
                  pl.BlockSpec((tm, QK_PAD), lambda i, h: (i % tpb, 0))],
        out_specs=pl.BlockSpec((1, hg, tm, QK_PAD), lambda i, h: (i // tpb, h, i % tpb, 0)),
        out_shape=jax.ShapeDtypeStruct((B, MLA_HEADS, L, QK_PAD), BF16),
        compiler_params=_params("parallel", "parallel"),
    )(cqn, w_q2, cosq, sinq)


def _q_weights(w_uq):
    R = w_uq.shape[0]
    half = MLA_ROPE_DIM // 2
    w = w_uq.reshape(R, MLA_HEADS, MLA_NOPE_DIM + MLA_ROPE_DIM)
    nope, rope = w[..., :MLA_NOPE_DIM], w[..., MLA_NOPE_DIM:]
    rot = jnp.concatenate([-rope[..., half:], rope[..., :half]], axis=-1)
    pad = jnp.zeros((R, MLA_HEADS, QK_PAD - MLA_NOPE_DIM - MLA_ROPE_DIM), w.dtype)
    parts = [nope, rope, pad, jnp.zeros_like(nope), rot, pad]
    return jnp.concatenate(parts, axis=-1).reshape(R, MLA_HEADS * 2 * QK_PAD).astype(BF16)


def _krope_t_kernel(w_ref, h_ref, *rest, roped):
    o_ref = rest[-1]
    x = lax.dot_general(w_ref[...], h_ref[...], (((1,), (1,)), ((), ())), preferred_element_type=F32)
    if roped:
        half = x.shape[0] // 2
        x1, x2, c, s = x[:half], x[half:], rest[0][...], rest[1][...]
        x = jnp.concatenate([x1 * c - x2 * s, x2 * c + x1 * s], axis=0)
    o_ref[0] = x.astype(o_ref.dtype)


def krope_t(h2, w_kr_t, B, Lx, tt, cos_t=None, sin_t=None):
    D = h2.shape[1]
    R = w_kr_t.shape[0]
    per = Lx // tt
    roped = cos_t is not None
    in_specs = [pl.BlockSpec((R, D), lambda b, i: (0, 0)), pl.BlockSpec((tt, D), lambda b, i: (b * per + i, 0))]
    args = [w_kr_t, h2]
    if roped:
        in_specs += [pl.BlockSpec((R // 2, tt), lambda b, i: (0, i))] * 2
        args += [cos_t, sin_t]
    return pl.pallas_call(
        functools.partial(_krope_t_kernel, roped=roped),
        grid=(B, per),
        in_specs=in_specs,
        out_specs=pl.BlockSpec((1, R, tt), lambda b, i: (b, 0, i)),
        out_shape=jax.ShapeDtypeStruct((B, R, Lx), BF16),
        compiler_params=_params("parallel", "parallel"),
    )(*args)


def _kv_proj_kernel(c_ref, wn_ref, wv_ref, kr_ref, kt_ref, v_ref):
    c = c_ref[0]
    kn_t = lax.dot_general(wn_ref[0], c, (((1,), (1,)), ((), ())), preferred_element_type=F32)
    rope_end = MLA_NOPE_DIM + MLA_ROPE_DIM
    kt_ref[0, 0, :MLA_NOPE_DIM] = kn_t.astype(kt_ref.dtype)
    kt_ref[0, 0, MLA_NOPE_DIM:rope_end] = kr_ref[0]
    kt_ref[0, 0, rope_end:] = jnp.zeros((QK_PAD - rope_end, c.shape[0]), kt_ref.dtype)
    v_ref[0, 0] = jnp.dot(c, wv_ref[0], preferred_element_type=F32).astype(v_ref.dtype)


def kv_proj(ckvn, wn_t, wv, kr_t, *, tk=2176):
    B, K, R = ckvn.shape
    H, V = wv.shape[0], wv.shape[2]
    return pl.pallas_call(
        _kv_proj_kernel,
        grid=(B, H, K // tk),
        in_specs=[pl.BlockSpec((1, tk, R), lambda b, h, i: (b, i, 0)),
                  pl.BlockSpec((1, MLA_NOPE_DIM, R), lambda b, h, i: (h, 0, 0)),
                  pl.BlockSpec((1, R, V), lambda b, h, i: (h, 0, 0)),
                  pl.BlockSpec((1, MLA_ROPE_DIM, tk), lambda b, h, i: (b, 0, i))],
        out_specs=[pl.BlockSpec((1, 1, QK_PAD, tk), lambda b, h, i: (b, h, 0, i)),
                   pl.BlockSpec((1, 1, tk, V), lambda b, h, i: (b, h, i, 0))],
        out_shape=[jax.ShapeDtypeStruct((B, H, QK_PAD, K), BF16), jax.ShapeDtypeStruct((B, H, K, V), BF16)],
        compiler_params=_params("parallel", "parallel", "parallel"),
    )(ckvn, wn_t, wv, kr_t)


def _mla_layer(h, h_c, B, L, Lc, w_in, g_q, g_kv, w_uq, w_ukv):
    H = MLA_HEADS
    lat = MLA_Q_RANK + MLA_KV_RANK
    w_in_b = w_in.astype(BF16)
    proj = matmul(h, w_in_b[:, :lat], tn=lat)
    ckv_c = matmul(h_c, w_in_b[:, MLA_Q_RANK:lat], tn=MLA_KV_RANK)
    ckv = jnp.concatenate([proj[:, MLA_Q_RANK:].reshape(B, L, -1), ckv_c.reshape(B, Lc, -1)], axis=1)
    cos, sin = _rope_tables(L // GRID_W)
    ones, zeros = jnp.ones((L, MLA_NOPE_DIM), F32), jnp.zeros((L, MLA_NOPE_DIM), F32)
    pad = jnp.zeros((L, QK_PAD - MLA_NOPE_DIM - MLA_ROPE_DIM), F32)
    cosq = jnp.concatenate([ones, cos, cos, pad], axis=1)
    sinq = jnp.concatenate([zeros, sin, sin, pad], axis=1)
    qp = q_proj(_small_rmsnorm(proj[:, :MLA_Q_RANK], g_q).astype(BF16), _q_weights(w_uq), cosq, sinq, B, L)
    w_kr_t = w_in_b[:, lat:].T
    kr_t = jnp.concatenate([krope_t(h, w_kr_t, B, L, 1024, cos.T, sin.T), krope_t(h_c, w_kr_t, B, Lc, Lc)], axis=2)
    w_kv = w_ukv.astype(BF16).reshape(MLA_KV_RANK, H, MLA_NOPE_DIM + MLA_V_DIM)
    wn_t = w_kv[..., :MLA_NOPE_DIM].transpose(1, 2, 0)
    wv = w_kv[..., MLA_NOPE_DIM:].transpose(1, 0, 2)
    kt, v = kv_proj(_small_rmsnorm(ckv, g_kv).astype(BF16), wn_t, wv, kr_t)
    scale = (MLA_NOPE_DIM + MLA_ROPE_DIM) ** -0.5
    return attention(qp, kt, v, scale=scale)


def _fft_tables(L):
    n = 2 * L
    n2_in = L // FFT_N1
    k2 = np.arange(FFT_K2)
    th = 2 * np.pi * ((k2[:, None] * np.arange(n2_in)[None, :]) % FFT_N2) / FFT_N2
    f1 = np.zeros((2 * FFT_K2P, n2_in))
    f1[:FFT_K2] = np.cos(th)
    f1[FFT_K2P:FFT_K2P + FFT_K2] = -np.sin(th)
    wgt = np.where((k2 == 0) | (k2 == FFT_N2 // 2), 1.0, 2.0)
    g = np.zeros((n2_in, 2 * FFT_K2P))
    g[:, :FFT_K2] = (wgt[:, None] * np.cos(th)).T / n
    g[:, FFT_K2P:FFT_K2P + FFT_K2] = -(wgt[:, None] * np.sin(th)).T / n
    k1 = np.arange(FFT_N1)[:, None]
    n1 = np.arange(FFT_N1)[None, :]
    m2 = np.zeros((-(-FFT_K2 // FFT_KSTEP) * FFT_KSTEP, 2 * FFT_N1, 2 * FFT_N1))
    for kk in range(FFT_K2):
        phi = 2 * np.pi * ((n1 * k1 * FFT_N2 + n1 * kk) % n) / n
        tr, ti = np.cos(phi), -np.sin(phi)
        m2[kk] = np.block([[tr, -ti], [ti, tr]])
    as_bf16 = lambda a: jnp.asarray(a, dtype=F32).astype(BF16)
    return (as_bf16(np.kron(f1, np.eye(FFT_SUB))), as_bf16(np.kron(g, np.eye(FFT_GROUP))), as_bf16(m2),
            as_bf16(np.transpose(m2, (0, 2, 1))))


def _fft_s1_kernel(f_ref, z_ref, o_ref):
    _, n2_in, g, ct = z_ref.shape
    z = z_ref[0].astype(F32)
    parts = []
    for s0 in range(0, g, FFT_SUB):
        zz = z[:, s0:s0 + FFT_SUB, :].reshape(n2_in * FFT_SUB, ct).astype(BF16)
        a = jnp.dot(f_ref[...], zz, preferred_element_type=F32)
        parts.append(a.reshape(2, FFT_K2P, FFT_SUB, ct))
    o_ref[0] = jnp.concatenate(parts, axis=2).astype(o_ref.dtype)


def fft_stage1(z4, f1k, *, lead_off, nb, ct=2048):
    _, n2_in, _, C = z4.shape
    g = FFT_GROUP
    return pl.pallas_call(
        _fft_s1_kernel,
        grid=(nb, FFT_N1 // g, C // ct),
        in_specs=[pl.BlockSpec(f1k.shape, lambda b, i, j: (0, 0)),
                  pl.BlockSpec((1, n2_in, g, ct), lambda b, i, j: (lead_off + b, 0, i, j))],
        out_specs=pl.BlockSpec((1, 2, FFT_K2P, g, ct), lambda b, i, j: (b, 0, 0, i, j)),
        out_shape=jax.ShapeDtypeStruct((nb, 2, FFT_K2P, FFT_N1, C), BF16),
        compiler_params=_params("parallel", "parallel", "parallel"),
    )(f1k, z4)


def _fft_s2_kernel(a_ref, m_ref, mt_ref, kf_ref, o_ref):
    step = pl.program_id(0)
    ct = a_ref.shape[-1]
    for q in range(FFT_KSTEP):
        k2 = step * FFT_KSTEP + q

        @pl.when(k2 < FFT_K2)
        def _(q=q):
            x = jnp.dot(m_ref[q], a_ref[0, :, q].reshape(2 * FFT_N1, ct), preferred_element_type=F32)
            xr, xi = x[:FFT_N1], x[FFT_N1:]
            kr, ki = kf_ref[0, 0, q].astype(F32), kf_ref[0, 1, q].astype(F32)
            p = jnp.concatenate([xr * kr - xi * ki, xr * ki + xi * kr], axis=0).astype(BF16)
            e = jnp.dot(mt_ref[q], p, preferred_element_type=F32)
            o_ref[0, :, q] = e.astype(o_ref.dtype).reshape(2, FFT_N1, ct)

        @pl.when(k2 >= FFT_K2)
        def _(q=q):
            o_ref[0, :, q] = jnp.zeros((2, FFT_N1, ct), o_ref.dtype)


def fft_stage2(a5, m2, m2t, kf, order):
    B, _, _, _, D = a5.shape
    last = (FFT_K2 - 1) // FFT_KSTEP
    blk = (1, 2, FFT_KSTEP, FFT_N1, D)
    tbl = pl.BlockSpec((FFT_KSTEP, 2 * FFT_N1, 2 * FFT_N1), lambda k, b: (jnp.minimum(k, last), 0, 0))
    return pl.pallas_call(
        _fft_s2_kernel,
        grid=(FFT_K2P // FFT_KSTEP, B),
        in_specs=[pl.BlockSpec(blk, lambda k, b: (b, 0, jnp.minimum(k, last), 0, 0)),
                  tbl,
                  tbl,
                  pl.BlockSpec(blk, lambda k, b: (order, 0, jnp.minimum(k, last), 0, 0))],
        out_specs=pl.BlockSpec(blk, lambda k, b: (b, 0, k, 0, 0)),
        out_shape=jax.ShapeDtypeStruct(a5.shape, BF16),
        compiler_params=_params("parallel", "arbitrary"),
    )(a5, m2, m2t, kf)


def _fft_f2_kernel(f_ref, g_ref, m_ref, inv_ref, o_ref):
    step = pl.program_id(0)
    ct = f_ref.shape[-1]
    inv = inv_ref[0]
    for q in range(FFT_KSTEP):
        k2 = step * FFT_KSTEP + q

        @pl.when(k2 < FFT_K2)
        def _(q=q):
            xf = jnp.dot(m_ref[q], f_ref[:, q].reshape(2 * FFT_N1, ct), preferred_element_type=F32)
            xg = jnp.dot(m_ref[q], g_ref[:, q].reshape(2 * FFT_N1, ct), preferred_element_type=F32)
            o_ref[0, 0, q] = ((xf[:FFT_N1] + xg[:FFT_N1]) * inv).astype(o_ref.dtype)
            o_ref[0, 1, q] = ((xf[FFT_N1:] - xg[FFT_N1:]) * inv).astype(o_ref.dtype)

        @pl.when(k2 >= FFT_K2)
        def _(q=q):
            o_ref[0, :, q] = jnp.zeros((2, FFT_N1, ct), o_ref.dtype)


def fft_filter_spectrum(a4, m2, inv, D):
    last = (FFT_K2 - 1) // FFT_KSTEP
    blk = (2, FFT_KSTEP, FFT_N1, D)
    return pl.pallas_call(
        _fft_f2_kernel,
        grid=(FFT_K2P // FFT_KSTEP, HYENA_ORDER),
        in_specs=[pl.BlockSpec(blk, lambda k, o: (0, jnp.minimum(k, last), 0, 2 * o)),
                  pl.BlockSpec(blk, lambda k, o: (0, jnp.minimum(k, last), 0, 2 * o + 1)),
                  pl.BlockSpec((FFT_KSTEP, 2 * FFT_N1, 2 * FFT_N1), lambda k, o: (jnp.minimum(k, last), 0, 0)),
                  pl.BlockSpec((1, 1, D), lambda k, o: (o, 0, 0))],
        out_specs=pl.BlockSpec((1, 2, FFT_KSTEP, FFT_N1, D), lambda k, o: (o, 0, k, 0, 0)),
        out_shape=jax.ShapeDtypeStruct((HYENA_ORDER, 2, FFT_K2P, FFT_N1, D), BF16),
        compiler_params=_params("arbitrary", "parallel"),
    )(a4, a4, m2, inv)


def _fft_s3_kernel(g_ref, e_ref, z_ref, x_ref, skip_ref, o_ref):
    _, n2_in, g, ct = z_ref.shape
    e = e_ref[0].reshape(2 * FFT_K2P * g, ct)
    y = jnp.dot(g_ref[...], e, preferred_element_type=F32).reshape(n2_in, g, ct)
    o_ref[0] = (x_ref[0] * (y + skip_ref[...] * z_ref[0])).astype(o_ref.dtype)


def fft_stage3(e5, gk, z4, z_off, x4, x_off, skip, *, out_dtype):
    B, _, _, _, D = e5.shape
    n2_in = z4.shape[1]
    g = FFT_GROUP
    return pl.pallas_call(
        _fft_s3_kernel,
        grid=(B, FFT_N1 // g),
        in_specs=[pl.BlockSpec(gk.shape, lambda b, i: (0, 0)),
                  pl.BlockSpec((1, 2, FFT_K2P, g, D), lambda b, i: (b, 0, 0, i, 0)),
                  pl.BlockSpec((1, n2_in, g, D), lambda b, i: (z_off + b, 0, i, 0)),
                  pl.BlockSpec((1, n2_in, g, D), lambda b, i: (x_off + b, 0, i, 0)),
                  pl.BlockSpec((1, D), lambda b, i: (0, 0))],
        out_specs=pl.BlockSpec((1, n2_in, g, D), lambda b, i: (b, 0, i, 0)),
        out_shape=jax.ShapeDtypeStruct((B, n2_in, FFT_N1, D), out_dtype),
        compiler_params=_params("parallel", "parallel"),
    )(gk, e5, z4, x4, skip.reshape(1, D))


def _hyin_kernel(ap_ref, a_ref, an_ref, b_ref, cw_ref, cb_ref, o_ref, *, tiles_per_seq):
    i = pl.program_id(0)
    t = i % tiles_per_seq
    halo = ap_ref.shape[0]
    tm, tn = o_ref.shape[1:]
    cw = MXU_DIM
    row = lax.broadcasted_iota(jnp.int32, (tm, cw), 0)
    at_start = jnp.logical_and(row == 0, t == 0)
    at_end = jnp.logical_and(row == tm - 1, t == tiles_per_seq - 1)
    a_ext = jnp.concatenate([ap_ref[...], a_ref[...], an_ref[...]], axis=0)
    for c0 in range(0, tn, cw):
        acc = jnp.dot(a_ext, b_ref[:, c0:c0 + cw], preferred_element_type=F32)
        up = jnp.where(at_start, 0.0, acc[halo - 1:halo - 1 + tm])
        dn = jnp.where(at_end, 0.0, acc[halo + 1:halo + 1 + tm])
        w = cw_ref[:, c0:c0 + cw]
        o_ref[0, :, c0:c0 + cw] = up * w[0:1] + acc[halo:halo + tm] * w[1:2] + dn * w[2:3] + cb_ref[:, c0:c0 + cw]


def hyena_in(h, w_in, conv_w, conv_b, L, *, tm=1024, tn=1024, halo=16):
    M, K = h.shape
    N = w_in.shape[1]
    D = N // 3
    per = D // tn
    return pl.pallas_call(
        functools.partial(_hyin_kernel, tiles_per_seq=L // tm),
        grid=(M // tm, N // tn),
        in_specs=[pl.BlockSpec((halo, K), lambda i, j: (jnp.maximum(i * (tm // halo) - 1, 0), 0)),
                  pl.BlockSpec((tm, K), lambda i, j: (i, 0)),
                  pl.BlockSpec((halo, K), lambda i, j: (jnp.minimum((i + 1) * (tm // halo), M // halo - 1), 0)),
                  pl.BlockSpec((K, tn), lambda i, j: (0, j)),
                  pl.BlockSpec((3, tn), lambda i, j: (0, j)),
                  pl.BlockSpec((1, tn), lambda i, j: (0, j))],
        out_specs=pl.BlockSpec((1, tm, tn), lambda i, j: (j // per, i, j % per)),
        out_shape=jax.ShapeDtypeStruct((3, M, D), F32),
        compiler_params=_params("parallel", "parallel"),
    )(h, h, h, w_in, conv_w, conv_b.reshape(1, N))


def _taps_kernel(a_ref, w_ref, b_ref, dl_ref, keep_ref, o_ref, ss_ref, *, seq_len):
    i = pl.program_id(1)
    tm = a_ref.shape[0]
    acc = jnp.dot(a_ref[...].astype(BF16), w_ref[...].astype(BF16), preferred_element_type=F32) + b_ref[...]
    row = i * tm + lax.broadcasted_iota(jnp.int32, acc.shape, 0)
    t = row.astype(F32) / seq_len
    hf = acc * jnp.exp(-t * dl_ref[...])
    hf = jnp.where(row == 0, hf * keep_ref[...], hf)
    o_ref[...] = hf.astype(o_ref.dtype)
    part = jnp.sum(hf * hf, axis=0, keepdims=True)

    @pl.when(i == 0)
    def _():
        ss_ref[...] = part

    @pl.when(i > 0)
    def _():
        ss_ref[...] += part


def _hyena_filter_taps(L, w1, b1, w2, b2, w3, b3, freq, *, tm=512, tn=1024):
    D = D_MODEL
    N = 2 * HYENA_ORDER * D
    t = jnp.arange(L, dtype=F32) / L
    w = 2 * math.pi * jnp.arange(L, dtype=F32) / L
    bands = jnp.linspace(1e-4, HYENA_BANDS - 1, HYENA_BANDS, dtype=F32)
    ang = w[:, None] * bands[None, :]
    z = jnp.concatenate([t[:, None], jnp.cos(ang), -jnp.sin(ang)], axis=-1)
    a = jnp.sin(freq * (z @ w1 + b1))
    a = jnp.sin(freq * (a @ w2 + b2))
    max_decay = math.log(HYENA_TARGET) / HYENA_FAST_PCT
    min_decay = math.log(HYENA_TARGET) / HYENA_SLOW_PCT
    decay = jnp.tile(jnp.abs(jnp.linspace(min_decay, max_decay, D, dtype=F32)), 2 * HYENA_ORDER).reshape(1, N)
    keep0 = jnp.where((jnp.arange(N) // D) % 2 == 1, 0.0, 1.0).astype(F32).reshape(1, N)
    K = a.shape[1]
    taps, ss = pl.pallas_call(
        functools.partial(_taps_kernel, seq_len=L),
        grid=(N // tn, L // tm),
        in_specs=[pl.BlockSpec((tm, K), lambda j, i: (i, 0)),
                  pl.BlockSpec((K, tn), lambda j, i: (0, j)),
                  pl.BlockSpec((1, tn), lambda j, i: (0, j)),
                  pl.BlockSpec((1, tn), lambda j, i: (0, j)),
                  pl.BlockSpec((1, tn), lambda j, i: (0, j))],
        out_specs=[pl.BlockSpec((tm, tn), lambda j, i: (i, j)), pl.BlockSpec((1, tn), lambda j, i: (0, j))],
        out_shape=[jax.ShapeDtypeStruct((L, N), BF16), jax.ShapeDtypeStruct((1, N), F32)],
        compiler_params=_params("parallel", "arbitrary"),
    )(a, w3, b3.reshape(1, N), decay, keep0)
    ss = ss.reshape(HYENA_ORDER, 2, D).sum(axis=1, keepdims=True)
    return taps, lax.rsqrt(ss + NORM_EPS)


def _hyena_layer(h, B, L, w_in, conv_w, conv_b, f_w1, f_b1, f_w2, f_b2, f_w3, f_b3, f_freq, skip):
    D = D_MODEL
    n2_in = L // FFT_N1
    f1k, gk, m2, m2t = _fft_tables(L)
    taps, inv = _hyena_filter_taps(L, f_w1, f_b1, f_w2, f_b2, f_w3, f_b3, f_freq)
    af = fft_stage1(taps.reshape(1, n2_in, FFT_N1, taps.shape[1]), f1k, lead_off=0, nb=1)
    kf = fft_filter_spectrum(af[0], m2, inv, D)
    u4 = hyena_in(h, w_in.astype(BF16), conv_w, conv_b, L).reshape(3 * B, n2_in, FFT_N1, D)
    z4 = u4
    for order in range(HYENA_ORDER):
        a = fft_stage1(z4, f1k, lead_off=0, nb=B)
        e = fft_stage2(a, m2, m2t, kf, order)
        last = order == HYENA_ORDER - 1
        z4 = fft_stage3(e, gk, z4, 0, u4, (order + 1) * B, skip[order], out_dtype=BF16 if last else F32)
    return z4.reshape(B * L, D)


def _moe_layer(a, w_proj, x_in, norm_gain, mod, B, T, w_router, w_gate, w_up, w_down, layer, next_g,
               next_shift=None, next_scale=None, *, tj=256, win=64):
    E = N_EXPERTS
    cap = EC_CAPACITY_FACTOR * T // E
    x2, hp, aff = proj_norm_route(a, w_proj.astype(BF16), x_in, mod[:, 2:3], norm_gain, mod[:, 3:4], mod[:, 4:5],
                                  w_router.T.astype(BF16), rows_per_batch=T)
    pos, idx, g = route_select(aff, cap)
    idx3 = idx.reshape(B, E, cap)
    y = moe_ffn(idx3, hp, g, w_gate, w_up, w_down, layer, T)
    nj = T // tj
    edges = jnp.arange(nj + 1, dtype=jnp.int32) * tj
    first = jnp.sum((idx3[..., None] < edges).astype(jnp.int32), axis=2)
    nominal = (first[..., :nj] // 16) * 16
    ntr = jnp.maximum(jnp.max((first[..., 1:] - nominal + win - 1) // win, axis=1), 1)
    starts = nominal.transpose(0, 2, 1).reshape(-1)
    return moe_combine(starts, ntr.reshape(-1), y, pos.transpose(0, 2, 1), x2, mod[:, 5:6], next_g, next_shift,
                       next_scale, tj=tj, win=win)


def kernel(x, c, ctx, c_ctx, ada_w, ada_b, norm_g, final_g, mla_w_in, mla_g_q, mla_g_kv, mla_w_uq, mla_w_ukv, mla_w_o, hy_w_in, hy_conv_w, hy_conv_b, hy_f_w1, hy_f_b1, hy_f_w2, hy_f_b2, hy_f_w3, hy_f_b3, hy_f_freq, hy_skip, hy_w_out, moe_w_router, moe_w_gate, moe_w_up, moe_w_down):
    B, L, D = x.shape
    Lc = ctx.shape[1]
    x2 = x.reshape(B * L, D)
    ctx2 = ctx.reshape(B * Lc, D)

    mod3 = _ada(jnp.concatenate([c, c_ctx[None]], axis=0), ada_w, ada_b, 0)
    mod, mod_c = mod3[:B], jnp.broadcast_to(mod3[B:], (B, 6, D))
    h = norm_mod(x2, norm_g[0, 0], mod[:, 0:1], mod[:, 1:2], rows_per_batch=L)
    h_c = norm_mod(ctx2, norm_g[0, 0], mod_c[:, 0:1], mod_c[:, 1:2], rows_per_batch=Lc)
    o = _mla_layer(h, h_c, B, L, Lc, mla_w_in[0], mla_g_q[0], mla_g_kv[0], mla_w_uq[0], mla_w_ukv[0])
    mod1 = _ada(c, ada_w, ada_b, 1)
    x2, h = _moe_layer(o.reshape(B * L, -1), mla_w_o[0], x2, norm_g[0, 1], mod, B, L, moe_w_router[0], moe_w_gate,
                       moe_w_up, moe_w_down, 0, norm_g[1, 0], mod1[:, 0:1], mod1[:, 1:2])

    mod = mod1
    z = _hyena_layer(h, B, L, hy_w_in[0], hy_conv_w[0], hy_conv_b[0], hy_f_w1[0], hy_f_b1[0], hy_f_w2[0],
                     hy_f_b2[0], hy_f_w3[0], hy_f_b3[0], hy_f_freq[0], hy_skip[0])
    out = _moe_layer(z, hy_w_out[0], x2, norm_g[1, 1], mod, B, L, moe_w_router[1], moe_w_gate, moe_w_up, moe_w_down, 1,
                     final_g)
    return out.reshape(B, L, D)
```

```python
import functools
import math

import jax
import jax.numpy as jnp
import numpy as np
from jax import lax
from jax.experimental import pallas as pl
from jax.experimental.pallas import tpu as pltpu

F32 = jnp.float32
BF16 = jnp.bfloat16

D_MODEL = 2048
GRID_W = 64
NORM_EPS = 1e-6
MLA_HEADS = 16
MLA_Q_RANK = 512
MLA_KV_RANK = 256
MLA_NOPE_DIM = 128
MLA_ROPE_DIM = 64
MLA_V_DIM = 128
ROPE_THETA = 10000.0
HYENA_ORDER = 2
HYENA_BANDS = 16
HYENA_TARGET = 1e-2
HYENA_FAST_PCT = 0.3
HYENA_SLOW_PCT = 1.5
N_EXPERTS = 16
EC_CAPACITY_FACTOR = 2
FFT_N1 = 128
FFT_N2 = 64
FFT_K2 = FFT_N2 // 2 + 1
FFT_K2P = 40
FFT_GROUP = 16
FFT_SUB = 8
FFT_KSTEP = 4

V7X_VMEM_LIMIT_BYTES = 56 * 1024 * 1024
LANES = 128
MXU_DIM = 256
QK_PAD = MXU_DIM


def _params(*sem):
    return pltpu.CompilerParams(dimension_semantics=sem, vmem_limit_bytes=V7X_VMEM_LIMIT_BYTES)


def _mm_kernel(a_ref, b_ref, o_ref, *, b_is_stacked):
    b = b_ref[0] if b_is_stacked else b_ref[...]
    acc = jnp.dot(a_ref[...].astype(BF16), b.astype(BF16), preferred_element_type=F32)
    o_ref[...] = acc.astype(o_ref.dtype)


def matmul(a, b, *, b_index=None, out_dtype=F32, tm=512, tn=512):
    M, K = a.shape
    N = b.shape[-1]
    tm, tn = min(tm, M), min(tn, N)
    assert M % tm == 0 and N % tn == 0, (M, N, tm, tn)
    stacked = b.ndim == 3
    if stacked:
        b_spec = pl.BlockSpec((1, K, tn), lambda i, j: (b_index, 0, j))
    else:
        b_spec = pl.BlockSpec((K, tn), lambda i, j: (0, j))
    return pl.pallas_call(
        functools.partial(_mm_kernel, b_is_stacked=stacked),
        grid=(M // tm, N // tn),
        in_specs=[pl.BlockSpec((tm, K), lambda i, j: (i, 0)), b_spec],
        out_specs=pl.BlockSpec((tm, tn), lambda i, j: (i, j)),
        out_shape=jax.ShapeDtypeStruct((M, N), out_dtype),
        compiler_params=_params("parallel", "parallel"),
    )(a, b)


def _norm_kernel(x_ref, g_ref, *rest, modulated):
    o_ref = rest[-1]
    x = x_ref[...]
    y = x * lax.rsqrt(jnp.mean(x * x, axis=-1, keepdims=True) + NORM_EPS) * g_ref[...]
    if modulated:
        shift_ref, scale_ref = rest[0], rest[1]
        y = y * (1 + scale_ref[0]) + shift_ref[0]
    o_ref[...] = y.astype(o_ref.dtype)


def norm_mod(x2, g, shift=None, scale=None, *, rows_per_batch=None, out_dtype=BF16, tm=256):
    M, D = x2.shape
    tm = min(tm, M)
    in_specs = [pl.BlockSpec((tm, D), lambda i: (i, 0)), pl.BlockSpec((1, D), lambda i: (0, 0))]
    args = [x2, g.reshape(1, D)]
    if shift is not None:
        tpb = rows_per_batch // tm
        in_specs += [pl.BlockSpec((1, 1, D), lambda i: (i // tpb, 0, 0))] * 2
        args += [shift, scale]
    return pl.pallas_call(
        functools.partial(_norm_kernel, modulated=shift is not None),
        grid=(M // tm,),
        in_specs=in_specs,
        out_specs=pl.BlockSpec((tm, D), lambda i: (i, 0)),
        out_shape=jax.ShapeDtypeStruct((M, D), out_dtype),
        compiler_params=_params("parallel"),
    )(*args)


def _attn_kernel(q_ref, kt_ref, v_ref, o_ref, *, chunks, c):
    q = q_ref[0, 0]
    m = l = acc = None
    for lo, hi in chunks:
        s = jnp.dot(q, kt_ref[0, 0, :, lo:hi], preferred_element_type=F32)
        m_chunk = jnp.max(s, axis=-1, keepdims=True)
        m_new = m_chunk if m is None else jnp.maximum(m, m_chunk)
        p = jnp.exp2(s * c - m_new * c)
        pv = jnp.dot(p.astype(BF16), v_ref[0, 0, lo:hi, :], preferred_element_type=F32)
        psum = jnp.sum(p, axis=-1, keepdims=True)
        if m is None:
            l, acc = psum, pv
        else:
            alpha = jnp.exp2((m - m_new) * c)
            l = alpha * l + psum
            acc = alpha * acc + pv
        m = m_new
    o_ref[0] = (acc / l).astype(o_ref.dtype)


def attention(q, kt, v, *, scale, tq=2048, tk=1024):
    B, H, L, _ = q.shape
    K = kt.shape[-1]
    V = v.shape[-1]
    chunks = tuple((lo, min(lo + tk, K)) for lo in range(0, K, tk))
    return pl.pallas_call(
        functools.partial(_attn_kernel, chunks=chunks, c=scale * math.log2(math.e)),
        grid=(B, H, L // tq),
        in_specs=[pl.BlockSpec((1, 1, tq, QK_PAD), lambda b, h, i: (b, h, i, 0)),
                  pl.BlockSpec((1, 1, QK_PAD, K), lambda b, h, i: (b, h, 0, 0)),
                  pl.BlockSpec((1, 1, K, V), lambda b, h, i: (b, h, 0, 0))],
        out_specs=pl.BlockSpec((1, tq, V), lambda b, h, i: (b, i, h)),
        out_shape=jax.ShapeDtypeStruct((B, L, H * V), BF16),
        compiler_params=_params("parallel", "parallel", "arbitrary"),
    )(q, kt, v)


def _proj_norm_route_kernel(a_ref, b_ref, res_ref, gate_ref, g_ref, shift_ref, scale_ref, wr_ref,
                            x_ref, h_ref, aff_ref):
    acc = jnp.dot(a_ref[...], b_ref[...], preferred_element_type=F32)
    x = res_ref[...] + gate_ref[0] * acc
    x_ref[...] = x
    y = x * lax.rsqrt(jnp.mean(x * x, axis=-1, keepdims=True) + NORM_EPS) * g_ref[...]
    y = y * (1 + scale_ref[0]) + shift_ref[0]
    h_ref[...] = y
    logits = lax.dot_general(wr_ref[...], y.astype(BF16), (((1,), (1,)), ((), ())),
                             preferred_element_type=F32)
    p = jnp.exp(logits - jnp.max(logits, axis=0, keepdims=True))
    aff_ref[0] = p / jnp.sum(p, axis=0, keepdims=True)


def proj_norm_route(a, b, res, gate, g, shift, scale, w_router_t, *, rows_per_batch, tm=256):
    M, K = a.shape
    D = b.shape[1]
    E = w_router_t.shape[0]
    tpb = rows_per_batch // tm
    row_blk = pl.BlockSpec((tm, D), lambda i: (i, 0))
    per_batch = pl.BlockSpec((1, 1, D), lambda i: (i // tpb, 0, 0))
    return pl.pallas_call(
        _proj_norm_route_kernel,
        grid=(M // tm,),
        in_specs=[pl.BlockSpec((tm, K), lambda i: (i, 0)),
                  pl.BlockSpec((K, D), lambda i: (0, 0)),
                  row_blk, per_batch,
                  pl.BlockSpec((1, D), lambda i: (0, 0)),
                  per_batch, per_batch,
                  pl.BlockSpec((E, D), lambda i: (0, 0))],
        out_specs=[row_blk, row_blk, pl.BlockSpec((1, E, tm), lambda i: (i // tpb, 0, i % tpb))],
        out_shape=[jax.ShapeDtypeStruct((M, D), F32), jax.ShapeDtypeStruct((M, D), F32),
                   jax.ShapeDtypeStruct((M // rows_per_batch, E, rows_per_batch), F32)],
        compiler_params=_params("parallel"),
    )(a, b, res, gate, g.reshape(1, D), shift, scale, w_router_t)


def _lane_prefix_sum(x):
    E, T = x.shape
    r = lax.broadcasted_iota(jnp.int32, (LANES, LANES), 0)
    c = lax.broadcasted_iota(jnp.int32, (LANES, LANES), 1)
    tri = jnp.where(r <= c, 1.0, 0.0).astype(BF16)
    outs, carry = [], jnp.zeros((E, 1), F32)
    for j in range(T // LANES):
        seg = jnp.dot(x[:, j * LANES:(j + 1) * LANES].astype(BF16), tri, preferred_element_type=F32) + carry
        outs.append(seg)
        carry = seg[:, LANES - 1:LANES]
    return jnp.concatenate(outs, axis=1)


def _route_select_kernel(aff_ref, pos_ref, idx_ref, gate_ref, cum_s, *, cap, tch):
    aff = aff_ref[0]
    E, T = aff.shape

    def bisect(state):
        lo, hi, _ = state
        mid = 0.5 * (lo + hi)
        cnt = jnp.sum(jnp.where(aff >= mid, 1.0, 0.0), axis=1, keepdims=True)
        ge = cnt >= cap
        open_rows = jnp.where((mid > lo) & (mid < hi), 1.0, 0.0)
        return jnp.where(ge, mid, lo), jnp.where(ge, hi, mid), jnp.max(open_rows)

    lo0 = jnp.min(aff, axis=1, keepdims=True)
    hi0 = 2.0 * jnp.max(aff, axis=1, keepdims=True) + 1e-30
    lo, hi, _ = lax.while_loop(lambda st: st[2] > 0.0, bisect, (lo0, hi0, jnp.float32(1.0)))
    gt = aff >= hi
    eq = (aff >= lo) & (aff < hi)
    need = cap - jnp.sum(jnp.where(gt, 1.0, 0.0), axis=1, keepdims=True)
    eq_rank = _lane_prefix_sum(jnp.where(eq, 1.0, 0.0))
    sel = gt | (eq & (eq_rank <= need))
    cum = _lane_prefix_sum(jnp.where(sel, 1.0, 0.0))
    cumsel = jnp.where(sel, cum, 0.0)
    pos_ref[0] = (cumsel - 1.0).astype(jnp.int32)
    cum_s[0] = cum
    cum_s[1] = cumsel
    slot = lax.broadcasted_iota(jnp.int32, (cap, 1), 0).astype(F32)

    def per_expert(e, carry):
        idx_acc = jnp.zeros((cap, 1), F32)
        g_acc = jnp.zeros((cap, 1), F32)
        for lo in range(0, T, tch):
            cum_e = cum_s[0, pl.ds(e, 1), lo:lo + tch]
            cs_e = cum_s[1, pl.ds(e, 1), lo:lo + tch]
            aff_e = aff_ref[0, pl.ds(e, 1), lo:lo + tch]
            idx_acc = idx_acc + jnp.sum(jnp.where(cum_e <= slot, 1.0, 0.0), axis=1, keepdims=True)
            g_acc = g_acc + jnp.sum(jnp.where(cs_e == slot + 1.0, aff_e, 0.0), axis=1, keepdims=True)
        idx_ref[0, pl.ds(e, 1)] = idx_acc.astype(jnp.int32).reshape(1, cap, 1)
        gate_ref[0, pl.ds(e, 1)] = g_acc.reshape(1, cap, 1)
        return carry

    lax.fori_loop(0, E, per_expert, 0)


def route_select(aff, cap, *, tch=1024):
    B, E, T = aff.shape
    return pl.pallas_call(
        functools.partial(_route_select_kernel, cap=cap, tch=tch),
        grid=(B,),
        in_specs=[pl.BlockSpec((1, E, T), lambda b: (b, 0, 0))],
        out_specs=[pl.BlockSpec((1, E, T), lambda b: (b, 0, 0)),
                   pl.BlockSpec((1, E, cap, 1), lambda b: (b, 0, 0, 0)),
                   pl.BlockSpec((1, E, cap, 1), lambda b: (b, 0, 0, 0))],
        out_shape=[jax.ShapeDtypeStruct((B, E, T), jnp.int32),
                   jax.ShapeDtypeStruct((B, E, cap, 1), jnp.int32),
                   jax.ShapeDtypeStruct((B, E, cap, 1), F32)],
        scratch_shapes=[pltpu.VMEM((2, E, T), F32)],
        compiler_params=_params("parallel"),
    )(aff)


def _moe_ffn_kernel(idx_ref, idxn_ref, h_hbm, wg_ref, wu_ref, wd_ref, g_ref, o_ref, xland, xbf, h1, sem,
                    *, nf, n_steps, n_exp, cap, tokens_per_batch):
    e = pl.program_id(0)
    f = pl.program_id(1)
    rows = xland.shape[0]

    def issue_gather(ids_ref):
        for b in range(rows // cap):
            def body(c, carry, b=b):
                i = b * cap + c
                src_row = ids_ref[0, 0, i] + b * tokens_per_batch
                pltpu.make_async_copy(h_hbm.at[pl.ds(src_row, 1)], xland.at[pl.ds(i, 1)], sem.at[0]).start()
                return carry
            lax.fori_loop(0, cap, body, 0, unroll=8)

    @pl.when((e == 0) & (f == 0))
    def _():
        issue_gather(idx_ref)

    def wait_gather():
        pltpu.make_async_copy(h_hbm.at[pl.ds(0, rows)], xland, sem.at[0]).wait()

    @pl.when(f == 0)
    def _():
        wait_gather()
        xbf[...] = xland[...].astype(BF16)

    share = rows // n_steps

    def issue_share():
        base = f * share
        off = (base // cap) * tokens_per_batch
        for j in range(share):
            pltpu.make_async_copy(h_hbm.at[pl.ds(idxn_ref[0, 0, base + j] + off, 1)],
                                  xland.at[pl.ds(base + j, 1)], sem.at[0]).start()

    @pl.when(f < nf)
    def _():
        x = xbf[...]
        a = jnp.dot(x, wg_ref[0, 0].astype(BF16), preferred_element_type=F32)
        issue_share()
        u = jnp.dot(x, wu_ref[0, 0].astype(BF16), preferred_element_type=F32)
        h1[jnp.minimum(f, nf - 1)] = (jax.nn.silu(a) * u).astype(BF16)

    @pl.when(f >= nf)
    def _():
        hid = jnp.concatenate([h1[k] for k in range(nf)], axis=1)
        issue_share()
        y = jnp.dot(hid, wd_ref[0, 0].astype(BF16), preferred_element_type=F32)
        o_ref[...] = (y.reshape(o_ref.shape) * g_ref[...]).astype(o_ref.dtype)

    @pl.when((e == n_exp - 1) & (f == n_steps - 1))
    def _():
        wait_gather()


def moe_ffn(idx3, hp, g, w_gate, w_up, w_down, layer, T, *, tf=256, tc=512):
    B, E, C = idx3.shape
    D = hp.shape[1]
    F = w_gate.shape[-1]
    nf = F // tf
    tc = min(tc, D)
    n_steps = nf + D // tc
    assert (B * C) % n_steps == 0 and C % ((B * C) // n_steps) == 0, (B, C, n_steps)
    ids = idx3.transpose(1, 0, 2).reshape(E, 1, B * C)
    smem_blk = lambda imap: pl.BlockSpec((1, 1, B * C), imap, memory_space=pltpu.SMEM)
    up_blk = pl.BlockSpec((1, 1, D, tf), lambda e, s: (layer, e, 0, jnp.minimum(s, nf - 1)))
    return pl.pallas_call(
        functools.partial(_moe_ffn_kernel, nf=nf, n_steps=n_steps, n_exp=E, cap=C, tokens_per_batch=T),
        grid=(E, n_steps),
        in_specs=[smem_blk(lambda e, s: (e, 0, 0)),
                  smem_blk(lambda e, s: (jnp.minimum(e + 1, E - 1), 0, 0)),
                  pl.BlockSpec(memory_space=pl.ANY),
                  up_blk,
                  up_blk,
                  pl.BlockSpec((1, 1, F, tc), lambda e, s: (layer, e, 0, jnp.maximum(s - nf, 0))),
                  pl.BlockSpec((B, 1, C, 1), lambda e, s: (0, e, 0, 0))],
        out_specs=pl.BlockSpec((B, 1, C, tc), lambda e, s: (0, e, 0, jnp.maximum(s - nf, 0))),
        out_shape=jax.ShapeDtypeStruct((B, E, C, D), BF16),
        scratch_shapes=[pltpu.VMEM((B * C, D), F32),
                        pltpu.VMEM((B * C, D), BF16),
                        pltpu.VMEM((nf, B * C, tf), BF16),
                        pltpu.SemaphoreType.DMA((1,))],
        compiler_params=_params("arbitrary", "arbitrary"),
    )(ids, ids, hp, w_gate, w_up, w_down, g)


def _moe_combine_kernel(starts_ref, ntr_ref, y_hbm, post_ref, x_ref, gate_ref, ng_ref, *rest,
                        n_exp, cap, win, steps_per_batch, modulated):
    ybuf, yext, acc, sem, sem_ext = rest[-5:]
    out_refs = rest[2:-5] if modulated else rest[:-5]
    s = pl.program_id(0)
    slot = s % 2

    def window(step, e, k):
        nominal = starts_ref[step * n_exp + e] + k * win
        return nominal, pl.multiple_of(jnp.minimum(nominal, cap - win), 16)

    def window_copy(step, e, k, dst, dsem):
        _, actual = window(step, e, k)
        return pltpu.make_async_copy(y_hbm.at[step // steps_per_batch, e, pl.ds(actual, win)],
                                     dst.at[pl.ds(e * win, win)], dsem)

    def onehot(step, k):
        post = post_ref[0]
        per = max(LANES // win, 1)
        lane = lax.broadcasted_iota(jnp.int32, (1, per * win), 1)
        cols = []
        for e0 in range(0, n_exp, per):
            slots = tok = None
            for q in range(per):
                nominal, actual = window(step, e0 + q, k)
                s_q = actual + lane - q * win
                s_q = jnp.where(s_q >= nominal, s_q, -2)
                t_q = post[:, e0 + q:e0 + q + 1]
                mine = lane >= q * win
                slots = s_q if q == 0 else jnp.where(mine, s_q, slots)
                tok = t_q if q == 0 else jnp.where(mine, t_q, tok)
            cols.append(jnp.where(tok == slots, 1.0, 0.0).astype(BF16))
        return jnp.concatenate(cols, axis=1)

    @pl.when(s == 0)
    def _():
        for e in range(n_exp):
            window_copy(0, e, 0, ybuf.at[0], sem.at[0]).start()

    for e in range(n_exp):
        window_copy(s, e, 0, ybuf.at[slot], sem.at[slot]).wait()

    @pl.when(s + 1 < pl.num_programs(0))
    def _():
        for e in range(n_exp):
            window_copy(s + 1, e, 0, ybuf.at[1 - slot], sem.at[1 - slot]).start()

    acc[...] = jnp.dot(onehot(s, 0), ybuf[slot], preferred_element_type=F32)

    def extra(k, carry):
        for e in range(n_exp):
            window_copy(s, e, k, yext, sem_ext.at[0]).start()
        for e in range(n_exp):
            window_copy(s, e, k, yext, sem_ext.at[0]).wait()
        acc[...] += jnp.dot(onehot(s, k), yext[...], preferred_element_type=F32)
        return carry

    lax.fori_loop(1, ntr_ref[s], extra, 0)
    x = x_ref[...] + gate_ref[0] * acc[...]
    y = x * lax.rsqrt(jnp.mean(x * x, axis=-1, keepdims=True) + NORM_EPS) * ng_ref[...]
    if modulated:
        out_refs[0][...] = x
        y = y * (1 + rest[1][0]) + rest[0][0]
    out_refs[-1][...] = y.astype(out_refs[-1].dtype)


def moe_combine(starts, ntr, y, post, x2, gate, norm_g, shift=None, scale=None, *, tj=256, win=64):
    B, E, C, D = y.shape
    M = x2.shape[0]
    T = M // B
    spb = T // tj
    modulated = shift is not None
    row_blk = pl.BlockSpec((tj, D), lambda s, *_: (s, 0))
    per_batch = pl.BlockSpec((1, 1, D), lambda s, *_: (s // spb, 0, 0))
    in_specs = [pl.BlockSpec(memory_space=pl.ANY),
                pl.BlockSpec((1, tj, E), lambda s, *_: (s // spb, s % spb, 0)),
                row_blk, per_batch, pl.BlockSpec((1, D), lambda s, *_: (0, 0))]
    args = [starts, ntr, y, post, x2, gate, norm_g.reshape(1, D)]
    if modulated:
        in_specs += [per_batch, per_batch]
        args += [shift, scale]
        out_specs = [row_blk, row_blk]
        out_shape = [jax.ShapeDtypeStruct((M, D), F32), jax.ShapeDtypeStruct((M, D), BF16)]
    else:
        out_specs = row_blk
        out_shape = jax.ShapeDtypeStruct((M, D), F32)
    grid_spec = pltpu.PrefetchScalarGridSpec(
        num_scalar_prefetch=2,
        grid=(M // tj,),
        in_specs=in_specs,
        out_specs=out_specs,
        scratch_shapes=[pltpu.VMEM((2, E * win, D), BF16),
                        pltpu.VMEM((E * win, D), BF16),
                        pltpu.VMEM((tj, D), F32),
                        pltpu.SemaphoreType.DMA((2,)),
                        pltpu.SemaphoreType.DMA((1,))],
    )
    return pl.pallas_call(
        functools.partial(_moe_combine_kernel, n_exp=E, cap=C, win=win, steps_per_batch=spb, modulated=modulated),
        grid_spec=grid_spec,
        out_shape=out_shape,
        compiler_params=_params("arbitrary"),
    )(*args)


def _small_rmsnorm(x, g):
    return x * lax.rsqrt(jnp.mean(x * x, axis=-1, keepdims=True) + NORM_EPS) * g


def _rope_tables(rows):
    row = jnp.broadcast_to(jnp.arange(rows)[:, None], (rows, GRID_W)).reshape(-1).astype(F32)
    col = jnp.broadcast_to(jnp.arange(GRID_W)[None, :], (rows, GRID_W)).reshape(-1).astype(F32)
    n_freq = MLA_ROPE_DIM // 4
    inv = ROPE_THETA ** (-jnp.arange(n_freq, dtype=F32) / n_freq)
    ang = jnp.concatenate([row[:, None] * inv, col[:, None] * inv], axis=-1)
    return jnp.cos(ang), jnp.sin(ang)


def _ada(cvec3, ada_w, ada_b, layer):
    R = cvec3.shape[0]
    a = jnp.concatenate([jax.nn.silu(cvec3), jnp.zeros((8 - R, cvec3.shape[1]), F32)], axis=0)
    m = matmul(a, ada_w, b_index=layer, tm=8, tn=512)[:R] + ada_b[layer]
    return m.reshape(R, 6, D_MODEL)


def _q_proj_kernel(cq_ref, w_ref, cos_ref, sin_ref, o_ref):
    cq, cos, sin = cq_ref[...], cos_ref[...], sin_ref[...]
    for hh in range(o_ref.shape[1]):
        w = w_ref[:, hh * 2 * QK_PAD:(hh + 1) * 2 * QK_PAD]
        acc = jnp.dot(cq, w, preferred_element_type=F32)
        o_ref[0, hh] = (acc[:, :QK_PAD] * cos + acc[:, QK_PAD:] * sin).astype(o_ref.dtype)


def q_proj(cqn, w_q2, cosq, sinq, B, L, *, tm=1024, hg=4):
    M, R = cqn.shape
    tpb = L // tm
    return pl.pallas_call(
        _q_proj_kernel,
        grid=(M // tm, MLA_HEADS // hg),
        in_specs=[pl.BlockSpec((tm, R), lambda i, h: (i, 0)),
                  pl.BlockSpec((R, hg * 2 * QK_PAD), lambda i, h: (0, h)),
                  pl.BlockSpec((tm, QK_PAD), lambda i, h: (i % tpb, 0)),
                  pl.BlockSpec((tm, QK_PAD), lambda i, h: (i % tpb, 0))],
        out_specs=pl.BlockSpec((1, hg, tm, QK_PAD), lambda i, h: (i // tpb, h, i % tpb, 0)),
        out_shape=jax.ShapeDtypeStruct((B, MLA_HEADS, L, QK_PAD), BF16),
        compiler_params=_params("parallel", "parallel"),
    )(cqn, w_q2, cosq, sinq)


def _q_weights(w_uq):
    R = w_uq.shape[0]
    half = MLA_ROPE_DIM // 2
    w = w_uq.reshape(R, MLA_HEADS, MLA_NOPE_DIM + MLA_ROPE_DIM)
    nope, rope = w[..., :MLA_NOPE_DIM], w[..., MLA_NOPE_DIM:]
    rot = jnp.concatenate([-rope[..., half:], rope[..., :half]], axis=-1)
    pad = jnp.zeros((R, MLA_HEADS, QK_PAD - MLA_NOPE_DIM - MLA_ROPE_DIM), w.dtype)
    parts = [nope, rope, pad, jnp.zeros_like(nope), rot, pad]
    return jnp.concatenate(parts, axis=-1).reshape(R, MLA_HEADS * 2 * QK_PAD).astype(BF16)


def _krope_t_kernel(w_ref, h_ref, *rest, roped):
    o_ref = rest[-1]
    x = lax.dot_general(w_ref[...], h_ref[...], (((1,), (1,)), ((), ())), preferred_element_type=F32)
    if roped:
        half = x.shape[0] // 2
        x1, x2, c, s = x[:half], x[half:], rest[0][...], rest[1][...]
        x = jnp.concatenate([x1 * c - x2 * s, x2 * c + x1 * s], axis=0)
    o_ref[0] = x.astype(o_ref.dtype)


def krope_t(h2, w_kr_t, B, Lx, tt, cos_t=None, sin_t=None):
    D = h2.shape[1]
    R = w_kr_t.shape[0]
    per = Lx // tt
    roped = cos_t is not None
    in_specs = [pl.BlockSpec((R, D), lambda b, i: (0, 0)), pl.BlockSpec((tt, D), lambda b, i: (b * per + i, 0))]
    args = [w_kr_t, h2]
    if roped:
        in_specs += [pl.BlockSpec((R // 2, tt), lambda b, i: (0, i))] * 2
        args += [cos_t, sin_t]
    return pl.pallas_call(
        functools.partial(_krope_t_kernel, roped=roped),
        grid=(B, per),
        in_specs=in_specs,
        out_specs=pl.BlockSpec((1, R, tt), lambda b, i: (b, 0, i)),
        out_shape=jax.ShapeDtypeStruct((B, R, Lx), BF16),
        compiler_params=_params("parallel", "parallel"),
    )(*args)


def _kv_proj_kernel(c_ref, wn_ref, wv_ref, kr_ref, kt_ref, v_ref):
    c = c_ref[0]
    kn_t = lax.dot_general(wn_ref[0], c, (((1,), (1,)), ((), ())), preferred_element_type=F32)
    rope_end = MLA_NOPE_DIM + MLA_ROPE_DIM
    kt_ref[0, 0, :MLA_NOPE_DIM] = kn_t.astype(kt_ref.dtype)
    kt_ref[0, 0, MLA_NOPE_DIM:rope_end] = kr_ref[0]
    kt_ref[0, 0, rope_end:] = jnp.zeros((QK_PAD - rope_end, c.shape[0]), kt_ref.dtype)
    v_ref[0, 0] = jnp.dot(c, wv_ref[0], preferred_element_type=F32).astype(v_ref.dtype)


def kv_proj(ckvn, wn_t, wv, kr_t, *, tk=2176):
    B, K, R = ckvn.shape
    H, V = wv.shape[0], wv.shape[2]
    return pl.pallas_call(
        _kv_proj_kernel,
        grid=(B, H, K // tk),
        in_specs=[pl.BlockSpec((1, tk, R), lambda b, h, i: (b, i, 0)),
                  pl.BlockSpec((1, MLA_NOPE_DIM, R), lambda b, h, i: (h, 0, 0)),
                  pl.BlockSpec((1, R, V), lambda b, h, i: (h, 0, 0)),
                  pl.BlockSpec((1, MLA_ROPE_DIM, tk), lambda b, h, i: (b, 0, i))],
        out_specs=[pl.BlockSpec((1, 1, QK_PAD, tk), lambda b, h, i: (b, h, 0, i)),
                   pl.BlockSpec((1, 1, tk, V), lambda b, h, i: (b, h, i, 0))],
        out_shape=[jax.ShapeDtypeStruct((B, H, QK_PAD, K), BF16), jax.ShapeDtypeStruct((B, H, K, V), BF16)],
        compiler_params=_params("parallel", "parallel", "parallel"),
    )(ckvn, wn_t, wv, kr_t)


def _mla_layer(h, h_c, B, L, Lc, w_in, g_q, g_kv, w_uq, w_ukv):
    H = MLA_HEADS
    lat = MLA_Q_RANK + MLA_KV_RANK
    w_in_b = w_in.astype(BF16)
    proj = matmul(h, w_in_b[:, :lat], tn=lat)
    ckv_c = matmul(h_c, w_in_b[:, MLA_Q_RANK:lat], tn=MLA_KV_RANK)
    ckv = jnp.concatenate([proj[:, MLA_Q_RANK:].reshape(B, L, -1), ckv_c.reshape(B, Lc, -1)], axis=1)
    cos, sin = _rope_tables(L // GRID_W)
    ones, zeros = jnp.ones((L, MLA_NOPE_DIM), F32), jnp.zeros((L, MLA_NOPE_DIM), F32)
    pad = jnp.zeros((L, QK_PAD - MLA_NOPE_DIM - MLA_ROPE_DIM), F32)
    cosq = jnp.concatenate([ones, cos, cos, pad], axis=1)
    sinq = jnp.concatenate([zeros, sin, sin, pad], axis=1)
    qp = q_proj(_small_rmsnorm(proj[:, :MLA_Q_RANK], g_q).astype(BF16), _q_weights(w_uq), cosq, sinq, B, L)
    w_kr_t = w_in_b[:, lat:].T
    kr_t = jnp.concatenate([krope_t(h, w_kr_t, B, L, 1024, cos.T, sin.T), krope_t(h_c, w_kr_t, B, Lc, Lc)], axis=2)
    w_kv = w_ukv.astype(BF16).reshape(MLA_KV_RANK, H, MLA_NOPE_DIM + MLA_V_DIM)
    wn_t = w_kv[..., :MLA_NOPE_DIM].transpose(1, 2, 0)
    wv = w_kv[..., MLA_NOPE_DIM:].transpose(1, 0, 2)
    kt, v = kv_proj(_small_rmsnorm(ckv, g_kv).astype(BF16), wn_t, wv, kr_t)
    scale = (MLA_NOPE_DIM + MLA_ROPE_DIM) ** -0.5
    return attention(qp, kt, v, scale=scale)


def _fft_tables(L):
    n = 2 * L
    n2_in = L // FFT_N1
    k2 = np.arange(FFT_K2)
    th = 2 * np.pi * ((k2[:, None] * np.arange(n2_in)[None, :]) % FFT_N2) / FFT_N2
    f1 = np.zeros((2 * FFT_K2P, n2_in))
    f1[:FFT_K2] = np.cos(th)
    f1[FFT_K2P:FFT_K2P + FFT_K2] = -np.sin(th)
    wgt = np.where((k2 == 0) | (k2 == FFT_N2 // 2), 1.0, 2.0)
    g = np.zeros((n2_in, 2 * FFT_K2P))
    g[:, :FFT_K2] = (wgt[:, None] * np.cos(th)).T / n
    g[:, FFT_K2P:FFT_K2P + FFT_K2] = -(wgt[:, None] * np.sin(th)).T / n
    k1 = np.arange(FFT_N1)[:, None]
    n1 = np.arange(FFT_N1)[None, :]
    m2 = np.zeros((-(-FFT_K2 // FFT_KSTEP) * FFT_KSTEP, 2 * FFT_N1, 2 * FFT_N1))
    for kk in range(FFT_K2):
        phi = 2 * np.pi * ((n1 * k1 * FFT_N2 + n1 * kk) % n) / n
        tr, ti = np.cos(phi), -np.sin(phi)
        m2[kk] = np.block([[tr, -ti], [ti, tr]])
    as_bf16 = lambda a: jnp.asarray(a, dtype=F32).astype(BF16)
    return (as_bf16(np.kron(f1, np.eye(FFT_SUB))), as_bf16(np.kron(g, np.eye(FFT_GROUP))), as_bf16(m2),
            as_bf16(np.transpose(m2, (0, 2, 1))))


def _fft_s1_kernel(f_ref, z_ref, o_ref):
    _, n2_in, g, ct = z_ref.shape
    z = z_ref[0].astype(F32)
    parts = []
    for s0 in range(0, g, FFT_SUB):
        zz = z[:, s0:s0 + FFT_SUB, :].reshape(n2_in * FFT_SUB, ct).astype(BF16)
        a = jnp.dot(f_ref[...], zz, preferred_element_type=F32)
        parts.append(a.reshape(2, FFT_K2P, FFT_SUB, ct))
    o_ref[0] = jnp.concatenate(parts, axis=2).astype(o_ref.dtype)


def fft_stage1(z4, f1k, *, lead_off, nb, ct=2048):
    _, n2_in, _, C = z4.shape
    g = FFT_GROUP
    return pl.pallas_call(
        _fft_s1_kernel,
        grid=(nb, FFT_N1 // g, C // ct),
        in_specs=[pl.BlockSpec(f1k.shape, lambda b, i, j: (0, 0)),
                  pl.BlockSpec((1, n2_in, g, ct), lambda b, i, j: (lead_off + b, 0, i, j))],
        out_specs=pl.BlockSpec((1, 2, FFT_K2P, g, ct), lambda b, i, j: (b, 0, 0, i, j)),
        out_shape=jax.ShapeDtypeStruct((nb, 2, FFT_K2P, FFT_N1, C), BF16),
        compiler_params=_params("parallel", "parallel", "parallel"),
    )(f1k, z4)


def _fft_s2_kernel(a_ref, m_ref, mt_ref, kf_ref, o_ref):
    step = pl.program_id(0)
    ct = a_ref.shape[-1]
    for q in range(FFT_KSTEP):
        k2 = step * FFT_KSTEP + q

        @pl.when(k2 < FFT_K2)
        def _(q=q):
            x = jnp.dot(m_ref[q], a_ref[0, :, q].reshape(2 * FFT_N1, ct), preferred_element_type=F32)
            xr, xi = x[:FFT_N1], x[FFT_N1:]
            kr, ki = kf_ref[0, 0, q].astype(F32), kf_ref[0, 1, q].astype(F32)
            p = jnp.concatenate([xr * kr - xi * ki, xr * ki + xi * kr], axis=0).astype(BF16)
            e = jnp.dot(mt_ref[q], p, preferred_element_type=F32)
            o_ref[0, :, q] = e.astype(o_ref.dtype).reshape(2, FFT_N1, ct)

        @pl.when(k2 >= FFT_K2)
        def _(q=q):
            o_ref[0, :, q] = jnp.zeros((2, FFT_N1, ct), o_ref.dtype)


def fft_stage2(a5, m2, m2t, kf, order):
    B, _, _, _, D = a5.shape
    last = (FFT_K2 - 1) // FFT_KSTEP
    blk = (1, 2, FFT_KSTEP, FFT_N1, D)
    tbl = pl.BlockSpec((FFT_KSTEP, 2 * FFT_N1, 2 * FFT_N1), lambda k, b: (jnp.minimum(k, last), 0, 0))
    return pl.pallas_call(
        _fft_s2_kernel,
        grid=(FFT_K2P // FFT_KSTEP, B),
        in_specs=[pl.BlockSpec(blk, lambda k, b: (b, 0, jnp.minimum(k, last), 0, 0)),
                  tbl,
                  tbl,
                  pl.BlockSpec(blk, lambda k, b: (order, 0, jnp.minimum(k, last), 0, 0))],
        out_specs=pl.BlockSpec(blk, lambda k, b: (b, 0, k, 0, 0)),
        out_shape=jax.ShapeDtypeStruct(a5.shape, BF16),
        compiler_params=_params("parallel", "arbitrary"),
    )(a5, m2, m2t, kf)


def _fft_f2_kernel(f_ref, g_ref, m_ref, inv_ref, o_ref):
    step = pl.program_id(0)
    ct = f_ref.shape[-1]
    inv = inv_ref[0]
    for q in range(FFT_KSTEP):
        k2 = step * FFT_KSTEP + q

        @pl.when(k2 < FFT_K2)
        def _(q=q):
            xf = jnp.dot(m_ref[q], f_ref[:, q].reshape(2 * FFT_N1, ct), preferred_element_type=F32)
            xg = jnp.dot(m_ref[q], g_ref[:, q].reshape(2 * FFT_N1, ct), preferred_element_type=F32)
            o_ref[0, 0, q] = ((xf[:FFT_N1] + xg[:FFT_N1]) * inv).astype(o_ref.dtype)
            o_ref[0, 1, q] = ((xf[FFT_N1:] - xg[FFT_N1:]) * inv).astype(o_ref.dtype)

        @pl.when(k2 >= FFT_K2)
        def _(q=q):
            o_ref[0, :, q] = jnp.zeros((2, FFT_N1, ct), o_ref.dtype)


def fft_filter_spectrum(a4, m2, inv, D):
    last = (FFT_K2 - 1) // FFT_KSTEP
    blk = (2, FFT_KSTEP, FFT_N1, D)
    return pl.pallas_call(
        _fft_f2_kernel,
        grid=(FFT_K2P // FFT_KSTEP, HYENA_ORDER),
        in_specs=[pl.BlockSpec(blk, lambda k, o: (0, jnp.minimum(k, last), 0, 2 * o)),
                  pl.BlockSpec(blk, lambda k, o: (0, jnp.minimum(k, last), 0, 2 * o + 1)),
                  pl.BlockSpec((FFT_KSTEP, 2 * FFT_N1, 2 * FFT_N1), lambda k, o: (jnp.minimum(k, last), 0, 0)),
                  pl.BlockSpec((1, 1, D), lambda k, o: (o, 0, 0))],
        out_specs=pl.BlockSpec((1, 2, FFT_KSTEP, FFT_N1, D), lambda k, o: (o, 0, k, 0, 0)),
        out_shape=jax.ShapeDtypeStruct((HYENA_ORDER, 2, FFT_K2P, FFT_N1, D), BF16),
        compiler_params=_params("arbitrary", "parallel"),
    )(a4, a4, m2, inv)


def _fft_s3_kernel(g_ref, e_ref, z_ref, x_ref, skip_ref, o_ref):
    _, n2_in, g, ct = z_ref.shape
    e = e_ref[0].reshape(2 * FFT_K2P * g, ct)
    y = jnp.dot(g_ref[...], e, preferred_element_type=F32).reshape(n2_in, g, ct)
    o_ref[0] = (x_ref[0] * (y + skip_ref[...] * z_ref[0])).astype(o_ref.dtype)


def fft_stage3(e5, gk, z4, z_off, x4, x_off, skip, *, out_dtype):
    B, _, _, _, D = e5.shape
    n2_in = z4.shape[1]
    g = FFT_GROUP
    return pl.pallas_call(
        _fft_s3_kernel,
        grid=(B, FFT_N1 // g),
        in_specs=[pl.BlockSpec(gk.shape, lambda b, i: (0, 0)),
                  pl.BlockSpec((1, 2, FFT_K2P, g, D), lambda b, i: (b, 0, 0, i, 0)),
                  pl.BlockSpec((1, n2_in, g, D), lambda b, i: (z_off + b, 0, i, 0)),
                  pl.BlockSpec((1, n2_in, g, D), lambda b, i: (x_off + b, 0, i, 0)),
                  pl.BlockSpec((1, D), lambda b, i: (0, 0))],
        out_specs=pl.BlockSpec((1, n2_in, g, D), lambda b, i: (b, 0, i, 0)),
        out_shape=jax.ShapeDtypeStruct((B, n2_in, FFT_N1, D), out_dtype),
        compiler_params=_params("parallel", "parallel"),
    )(gk, e5, z4, x4, skip.reshape(1, D))


def _hyin_kernel(ap_ref, a_ref, an_ref, b_ref, cw_ref, cb_ref, o_ref, *, tiles_per_seq):
    i = pl.program_id(0)
    t = i % tiles_per_seq
    halo = ap_ref.shape[0]
    tm, tn = o_ref.shape[1:]
    cw = MXU_DIM
    row = lax.broadcasted_iota(jnp.int32, (tm, cw), 0)
    at_start = jnp.logical_and(row == 0, t == 0)
    at_end = jnp.logical_and(row == tm - 1, t == tiles_per_seq - 1)
    a_ext = jnp.concatenate([ap_ref[...], a_ref[...], an_ref[...]], axis=0)
    for c0 in range(0, tn, cw):
        acc = jnp.dot(a_ext, b_ref[:, c0:c0 + cw], preferred_element_type=F32)
        up = jnp.where(at_start, 0.0, acc[halo - 1:halo - 1 + tm])
        dn = jnp.where(at_end, 0.0, acc[halo + 1:halo + 1 + tm])
        w = cw_ref[:, c0:c0 + cw]
        o_ref[0, :, c0:c0 + cw] = up * w[0:1] + acc[halo:halo + tm] * w[1:2] + dn * w[2:3] + cb_ref[:, c0:c0 + cw]


def hyena_in(h, w_in, conv_w, conv_b, L, *, tm=1024, tn=1024, halo=16):
    M, K = h.shape
    N = w_in.shape[1]
    D = N // 3
    per = D // tn
    return pl.pallas_call(
        functools.partial(_hyin_kernel, tiles_per_seq=L // tm),
        grid=(M // tm, N // tn),
        in_specs=[pl.BlockSpec((halo, K), lambda i, j: (jnp.maximum(i * (tm // halo) - 1, 0), 0)),
                  pl.BlockSpec((tm, K), lambda i, j: (i, 0)),
                  pl.BlockSpec((halo, K), lambda i, j: (jnp.minimum((i + 1) * (tm // halo), M // halo - 1), 0)),
                  pl.BlockSpec((K, tn), lambda i, j: (0, j)),
                  pl.BlockSpec((3, tn), lambda i, j: (0, j)),
                  pl.BlockSpec((1, tn), lambda i, j: (0, j))],
        out_specs=pl.BlockSpec((1, tm, tn), lambda i, j: (j // per, i, j % per)),
        out_shape=jax.ShapeDtypeStruct((3, M, D), F32),
        compiler_params=_params("parallel", "parallel"),
    )(h, h, h, w_in, conv_w, conv_b.reshape(1, N))


def _taps_kernel(a_ref, w_ref, b_ref, dl_ref, keep_ref, o_ref, ss_ref, *, seq_len):
    i = pl.program_id(1)
    tm = a_ref.shape[0]
    acc = jnp.dot(a_ref[...].astype(BF16), w_ref[...].astype(BF16), preferred_element_type=F32) + b_ref[...]
    row = i * tm + lax.broadcasted_iota(jnp.int32, acc.shape, 0)
    t = row.astype(F32) / seq_len
    hf = acc * jnp.exp(-t * dl_ref[...])
    hf = jnp.where(row == 0, hf * keep_ref[...], hf)
    o_ref[...] = hf.astype(o_ref.dtype)
    part = jnp.sum(hf * hf, axis=0, keepdims=True)

    @pl.when(i == 0)
    def _():
        ss_ref[...] = part

    @pl.when(i > 0)
    def _():
        ss_ref[...] += part


def _hyena_filter_taps(L, w1, b1, w2, b2, w3, b3, freq, *, tm=512, tn=1024):
    D = D_MODEL
    N = 2 * HYENA_ORDER * D
    t = jnp.arange(L, dtype=F32) / L
    w = 2 * math.pi * jnp.arange(L, dtype=F32) / L
    bands = jnp.linspace(1e-4, HYENA_BANDS - 1, HYENA_BANDS, dtype=F32)
    ang = w[:, None] * bands[None, :]
    z = jnp.concatenate([t[:, None], jnp.cos(ang), -jnp.sin(ang)], axis=-1)
    a = jnp.sin(freq * (z @ w1 + b1))
    a = jnp.sin(freq * (a @ w2 + b2))
    max_decay = math.log(HYENA_TARGET) / HYENA_FAST_PCT
    min_decay = math.log(HYENA_TARGET) / HYENA_SLOW_PCT
    decay = jnp.tile(jnp.abs(jnp.linspace(min_decay, max_decay, D, dtype=F32)), 2 * HYENA_ORDER).reshape(1, N)
    keep0 = jnp.where((jnp.arange(N) // D) % 2 == 1, 0.0, 1.0).astype(F32).reshape(1, N)
    K = a.shape[1]
    taps, ss = pl.pallas_call(
        functools.partial(_taps_kernel, seq_len=L),
        grid=(N // tn, L // tm),
        in_specs=[pl.BlockSpec((tm, K), lambda j, i: (i, 0)),
                  pl.BlockSpec((K, tn), lambda j, i: (0, j)),
                  pl.BlockSpec((1, tn), lambda j, i: (0, j)),
                  pl.BlockSpec((1, tn), lambda j, i: (0, j)),
                  pl.BlockSpec((1, tn), lambda j, i: (0, j))],
        out_specs=[pl.BlockSpec((tm, tn), lambda j, i: (i, j)), pl.BlockSpec((1, tn), lambda j, i: (0, j))],
        out_shape=[jax.ShapeDtypeStruct((L, N), BF16), jax.ShapeDtypeStruct((1, N), F32)],
        compiler_params=_params("parallel", "arbitrary"),
    )(a, w3, b3.reshape(1, N), decay, keep0)
    ss = ss.reshape(HYENA_ORDER, 2, D).sum(axis=1, keepdims=True)
    return taps, lax.rsqrt(ss + NORM_EPS)


def _hyena_layer(h, B, L, w_in, conv_w, conv_b, f_w1, f_b1, f_w2, f_b2, f_w3, f_b3, f_freq, skip):
    D = D_MODEL
    n2_in = L // FFT_N1
    f1k, gk, m2, m2t = _fft_tables(L)
    taps, inv = _hyena_filter_taps(L, f_w1, f_b1, f_w2, f_b2, f_w3, f_b3, f_freq)
    af = fft_stage1(taps.reshape(1, n2_in, FFT_N1, taps.shape[1]), f1k, lead_off=0, nb=1)
    kf = fft_filter_spectrum(af[0], m2, inv, D)
    u4 = hyena_in(h, w_in.astype(BF16), conv_w, conv_b, L).reshape(3 * B, n2_in, FFT_N1, D)
    z4 = u4
    for order in range(HYENA_ORDER):
        a = fft_stage1(z4, f1k, lead_off=0, nb=B)
        e = fft_stage2(a, m2, m2t, kf, order)
        last = order == HYENA_ORDER - 1
        z4 = fft_stage3(e, gk, z4, 0, u4, (order + 1) * B, skip[order], out_dtype=BF16 if last else F32)
    return z4.reshape(B * L, D)


def _moe_layer(a, w_proj, x_in, norm_gain, mod, B, T, w_router, w_gate, w_up, w_down, layer, next_g,
               next_shift=None, next_scale=None, *, tj=256, win=64):
    E = N_EXPERTS
    cap = EC_CAPACITY_FACTOR * T // E
    x2, hp, aff = proj_norm_route(a, w_proj.astype(BF16), x_in, mod[:, 2:3], norm_gain, mod[:, 3:4], mod[:, 4:5],
                                  w_router.T.astype(BF16), rows_per_batch=T)
    pos, idx, g = route_select(aff, cap)
    idx3 = idx.reshape(B, E, cap)
    y = moe_ffn(idx3, hp, g, w_gate, w_up, w_down, layer, T)
    nj = T // tj
    edges = jnp.arange(nj + 1, dtype=jnp.int32) * tj
    first = jnp.sum((idx3[..., None] < edges).astype(jnp.int32), axis=2)
    nominal = (first[..., :nj] // 16) * 16
    ntr = jnp.maximum(jnp.max((first[..., 1:] - nominal + win - 1) // win, axis=1), 1)
    starts = nominal.transpose(0, 2, 1).reshape(-1)
    return moe_combine(starts, ntr.reshape(-1), y, pos.transpose(0, 2, 1), x2, mod[:, 5:6], next_g, next_shift,
                       next_scale, tj=tj, win=win)


def kernel(x, c, ctx, c_ctx, ada_w, ada_b, norm_g, final_g, mla_w_in, mla_g_q, mla_g_kv, mla_w_uq, mla_w_ukv, mla_w_o, hy_w_in, hy_conv_w, hy_conv_b, hy_f_w1, hy_f_b1, hy_f_w2, hy_f_b2, hy_f_w3, hy_f_b3, hy_f_freq, hy_skip, hy_w_out, moe_w_router, moe_w_gate, moe_w_up, moe_w_down):
    B, L, D = x.shape
    Lc = ctx.shape[1]
    x2 = x.reshape(B * L, D)
    ctx2 = ctx.reshape(B * Lc, D)

    mod3 = _ada(jnp.concatenate([c, c_ctx[None]], axis=0), ada_w, ada_b, 0)
    mod, mod_c = mod3[:B], jnp.broadcast_to(mod3[B:], (B, 6, D))
    h = norm_mod(x2, norm_g[0, 0], mod[:, 0:1], mod[:, 1:2], rows_per_batch=L)
    h_c = norm_mod(ctx2, norm_g[0, 0], mod_c[:, 0:1], mod_c[:, 1:2], rows_per_batch=Lc)
    o = _mla_layer(h, h_c, B, L, Lc, mla_w_in[0], mla_g_q[0], mla_g_kv[0], mla_w_uq[0], mla_w_ukv[0])
    mod1 = _ada(c, ada_w, ada_b, 1)
    x2, h = _moe_layer(o.reshape(B * L, -1), mla_w_o[0], x2, norm_g[0, 1], mod, B, L, moe_w_router[0], moe_w_gate,
                       moe_w_up, moe_w_down, 0, norm_g[1, 0], mod1[:, 0:1], mod1[:, 1:2])

    mod = mod1
    z = _hyena_layer(h, B, L, hy_w_in[0], hy_conv_w[0], hy_conv_b[0], hy_f_w1[0], hy_f_b1[0], hy_f_w2[0],
                     hy_f_b2[0], hy_f_w3[0], hy_f_b3[0], hy_f_freq[0], hy_skip[0])
    out = _moe_layer(z, hy_w_out[0], x2, norm_g[1, 1], mod, B, L, moe_w_router[1], moe_w_gate, moe_w_up, moe_w_down, 1,
                     final_g)
    return out.reshape(B, L, D)
```
